```python
import jax
import jax.numpy as jnp
from jax import lax
import numpy as np

D_MODEL = 1024
BATCH = 4
SEQ = 4096
DEPTH = 4
DEC_BATCH = 128
DEC_SEQ = 8
PAST_LEN = 2048
PAGE_SIZE = 128

N_NSA_LAYERS = (DEPTH + 1) // 2
N_RWKV_LAYERS = DEPTH // 2
N_VRES = N_RWKV_LAYERS - 1

POOL_DIM = D_MODEL // 2
POOL_WINDOWS = (2, 4, 8, 16)
POOL_GROUPS = len(POOL_WINDOWS)
POOL_GDIM = POOL_DIM // POOL_GROUPS
POOL_HIST = max(POOL_WINDOWS) - 1

HEAD_DIM = 64
NSA_HEADS = (D_MODEL // 2) // HEAD_DIM
NSA_KV_HEADS = 2
NSA_GQ = NSA_HEADS // NSA_KV_HEADS
NSA_DIM = NSA_HEADS * HEAD_DIM
CMP_STRIDE = 16
CMP_LEN = 2 * CMP_STRIDE
SLC_LEN = 64
N_SEL = 16
WINDOW = 512
Q_BLOCK = 128
ROPE_DIM = HEAD_DIM // 4
ROPE_THETA = 500000.0
MIX_DIM = POOL_DIM + NSA_DIM
KV_COLS = 6 * NSA_KV_HEADS * HEAD_DIM
IN_COLS = POOL_DIM + NSA_DIM + KV_COLS + 3 * NSA_HEADS

RWKV_N = 64
RWKV_HEADS = D_MODEL // RWKV_N
LORA_W = 64
LORA_A = 64
LORA_V = 32
LORA_G = 128
GN_EPS = 64e-5

MOE_GROUPS = 4
MOE_EPG = 4
N_EXPERTS = MOE_GROUPS * MOE_EPG
MOE_TOPK = 2
D_FF_E = 256

RMS_EPS = 1e-6
NEG_INF = -1e30
RES_SCALE = (2 * DEPTH) ** -0.5

kernel_name = 'hybrid_pool_nsa_rwkv7_hmoe_step'


def rmsnorm(x, g):
    xf = x.astype(jnp.float32)
    y = xf * lax.rsqrt(jnp.mean(xf * xf, axis=-1, keepdims=True) + RMS_EPS)
    return (y * g.astype(jnp.float32)).astype(x.dtype)


def rope_partial(x, pos):
    half = ROPE_DIM // 2
    inv = ROPE_THETA ** (-jnp.arange(half, dtype=jnp.float32) / half)
    ang = pos.astype(jnp.float32)[:, None] * inv
    cos = jnp.cos(ang)[:, None, :]
    sin = jnp.sin(ang)[:, None, :]
    xf = x.astype(jnp.float32)
    x1 = xf[..., :half]
    x2 = xf[..., half:ROPE_DIM]
    out = jnp.concatenate([x1 * cos - x2 * sin, x2 * cos + x1 * sin, xf[..., ROPE_DIM:]], axis=-1)
    return out.astype(x.dtype)


def masked_softmax(s, mask):
    s = jnp.where(mask, s.astype(jnp.float32), NEG_INF)
    p = jax.nn.softmax(s, axis=-1)
    return jnp.where(mask, p, 0.0)


def pool_mix(u, hist, p0, w_grp, scale):
    b, t, _ = u.shape
    ext = jnp.concatenate([hist.astype(u.dtype), u], axis=1).astype(jnp.float32)
    cs = jnp.pad(jnp.cumsum(ext, axis=1), ((0, 0), (1, 0), (0, 0)))
    cnt_pos = p0 + jnp.arange(t, dtype=jnp.int32) + 1
    means = []
    for gi, w in enumerate(POOL_WINDOWS):
        c = cs[..., gi * POOL_GDIM:(gi + 1) * POOL_GDIM]
        win_sum = c[:, POOL_HIST + 1:POOL_HIST + 1 + t] - c[:, POOL_HIST + 1 - w:POOL_HIST + 1 - w + t]
        cnt = jnp.minimum(cnt_pos, w).astype(jnp.float32)[None, :, None]
        means.append(win_sum / cnt)
    mean = jnp.stack(means, axis=2)
    d = mean - u.reshape(b, t, POOL_GROUPS, POOL_GDIM).astype(jnp.float32)
    y = jnp.einsum('btgc,gcd->btgd', d, w_grp.astype(jnp.float32)).reshape(b, t, POOL_DIM)
    return (y * scale.astype(jnp.float32)).astype(u.dtype)


def ab_features(h, pos, w_in, q_norm, k_norm):
    b, t = h.shape[:2]
    u = h @ w_in
    off_kv = POOL_DIM + NSA_DIM
    pool_in = u[..., :POOL_DIM]
    q = u[..., POOL_DIM:off_kv].reshape(b, t, NSA_HEADS, HEAD_DIM)
    kv = u[..., off_kv:off_kv + KV_COLS].reshape(b, t, 6, NSA_KV_HEADS, HEAD_DIM)
    gl = u[..., off_kv + KV_COLS:].reshape(b, t, NSA_HEADS, 3)
    q = rope_partial(rmsnorm(q, q_norm), pos)
    k_slc = rope_partial(rmsnorm(kv[:, :, 2], k_norm[1]), pos)
    k_win = rope_partial(rmsnorm(kv[:, :, 4], k_norm[2]), pos)
    rows = jnp.stack([kv[:, :, 0], kv[:, :, 1], k_slc, kv[:, :, 3]], axis=2)
    win = jnp.stack([k_win, kv[:, :, 5]], axis=2)
    return pool_in, q, gl, rows, win


def compress_kv(k_rows, v_rows, pos_w, phi, k_gain):
    b, length = k_rows.shape[:2]
    n_chunk = length // CMP_STRIDE

    def weighted_block_mean(rows, w):
        ch = rows.reshape(b, n_chunk, CMP_STRIDE, NSA_KV_HEADS, HEAD_DIM)
        return (jnp.einsum('bnlhd,hl->bnhd', ch[:, :-1], w[:, :CMP_STRIDE])
                + jnp.einsum('bnlhd,hl->bnhd', ch[:, 1:], w[:, CMP_STRIDE:]))

    cmp_end = jnp.arange(n_chunk - 1, dtype=jnp.int32) * CMP_STRIDE + (CMP_LEN - 1)
    kc = jnp.einsum('bnhd,de->bnhe', weighted_block_mean(k_rows, pos_w[0]), phi[0])
    kc = rope_partial(rmsnorm(kc, k_gain), cmp_end)
    vc = jnp.einsum('bnhd,de->bnhe', weighted_block_mean(v_rows, pos_w[1]), phi[1])
    return kc, vc, cmp_end


def nsa_attend(q, pos_q, gl, kc, vc, cmp_end, ks, vs, kw, vw, pos_w):
    f32 = jnp.float32
    b, tq = q.shape[:2]
    qg = q.reshape(b, tq, NSA_KV_HEADS, NSA_GQ, HEAD_DIM)
    scale = HEAD_DIM ** -0.5
    s_c = jnp.einsum('bqhgd,bchd->bhgqc', qg, kc) * scale
    p_c = masked_softmax(s_c, cmp_end[None, :] <= pos_q[:, None])
    o_c = jnp.einsum('bhgqc,bchd->bqhgd', p_c, vc.astype(f32))
    imp = p_c.sum(axis=2)
    imp_chunk = 0.5 * (jnp.pad(imp, ((0, 0), (0, 0), (0, 0), (0, 1)))
                       + jnp.pad(imp, ((0, 0), (0, 0), (0, 0), (1, 0))))
    n_slc = ks.shape[1] // SLC_LEN
    imp_blk = imp_chunk.reshape(b, NSA_KV_HEADS, tq, n_slc, SLC_LEN // CMP_STRIDE).sum(-1)
    blk = jnp.arange(n_slc, dtype=jnp.int32)[None, :]
    cur = (pos_q // SLC_LEN)[:, None]
    forced = (blk == 0) | (blk == cur) | (blk == cur - 1)
    score = jnp.where(blk > cur, -1.0, jnp.where(forced, 1e6, imp_blk))
    n_sel = min(N_SEL, n_slc)
    _, idx = lax.top_k(score, n_sel)
    gather = jax.vmap(jax.vmap(lambda rows, i: rows[i]))
    ksb = ks.reshape(b, n_slc, SLC_LEN, NSA_KV_HEADS, HEAD_DIM).transpose(0, 3, 1, 2, 4)
    vsb = vs.reshape(b, n_slc, SLC_LEN, NSA_KV_HEADS, HEAD_DIM).transpose(0, 3, 1, 2, 4)
    kg = gather(ksb, idx)
    vg = gather(vsb, idx)
    kpos = idx[..., None] * SLC_LEN + jnp.arange(SLC_LEN, dtype=jnp.int32)
    n_keys = n_sel * SLC_LEN
    m_s = (kpos <= pos_q[None, None, :, None, None]).reshape(b, NSA_KV_HEADS, 1, tq, n_keys)
    s_s = jnp.einsum('bqhgd,bhqnld->bhgqnl', qg, kg).reshape(b, NSA_KV_HEADS, NSA_GQ, tq, n_keys) * scale
    p_s = masked_softmax(s_s, m_s)
    o_s = jnp.einsum('bhgqk,bhqkd->bqhgd', p_s,
                     vg.reshape(b, NSA_KV_HEADS, tq, n_keys, HEAD_DIM).astype(f32))
    s_w = jnp.einsum('bqhgd,bkhd->bhgqk', qg, kw) * scale
    dq = pos_q[:, None] - pos_w[None, :]
    m_w = (dq >= 0) & (dq < WINDOW) & (pos_w[None, :] >= 0)
    p_w = masked_softmax(s_w, m_w)
    o_w = jnp.einsum('bhgqk,bkhd->bqhgd', p_w, vw.astype(f32))
    g = jax.nn.sigmoid(gl.astype(f32)).reshape(b, tq, NSA_KV_HEADS, NSA_GQ, 3)
    o = g[..., 0:1] * o_c + g[..., 1:2] * o_s + g[..., 2:3] * o_w
    return o.reshape(b, tq, NSA_DIM)


def nsa_prompt(q, gl, rows, win, cmp_pos_w, cmp_phi, k_gain):
    b, t = q.shape[:2]
    kc, vc, cmp_end = compress_kv(rows[:, :, 0], rows[:, :, 1], cmp_pos_w, cmp_phi, k_gain)
    ks, vs = rows[:, :, 2], rows[:, :, 3]
    win_pad = jnp.pad(win, ((0, 0), (WINDOW, 0), (0, 0), (0, 0), (0, 0)))

    def block(i):
        s0 = i * Q_BLOCK
        qb = lax.dynamic_slice_in_dim(q, s0, Q_BLOCK, axis=1)
        gb = lax.dynamic_slice_in_dim(gl, s0, Q_BLOCK, axis=1)
        wb = lax.dynamic_slice_in_dim(win_pad, s0, WINDOW + Q_BLOCK, axis=1)
        pos_q = s0 + jnp.arange(Q_BLOCK, dtype=jnp.int32)
        pos_w = s0 - WINDOW + jnp.arange(WINDOW + Q_BLOCK, dtype=jnp.int32)
        return nsa_attend(qb, pos_q, gb, kc, vc, cmp_end, ks, vs, wb[:, :, 0], wb[:, :, 1], pos_w)

    out = lax.map(block, jnp.arange(t // Q_BLOCK, dtype=jnp.int32))
    return jnp.moveaxis(out, 0, 1).reshape(b, t, NSA_DIM)


def nsa_sample(q, gl, rows_new, win_new, pool_kv, page_table, win_buf, cmp_pos_w, cmp_phi, k_gain):
    bd, ts = q.shape[:2]
    past_len = page_table.shape[1] * pool_kv.shape[1]
    past = pool_kv[page_table].reshape(bd, past_len, 4, NSA_KV_HEADS, HEAD_DIM)
    rows = jnp.concatenate([past, rows_new.astype(past.dtype)], axis=1)
    pad = (-(past_len + ts)) % SLC_LEN
    rows = jnp.pad(rows, ((0, 0), (0, pad), (0, 0), (0, 0), (0, 0)))
    kc, vc, cmp_end = compress_kv(rows[:, :, 0], rows[:, :, 1], cmp_pos_w, cmp_phi, k_gain)
    lb = win_buf.shape[1]
    win = jnp.concatenate([win_buf, win_new.astype(win_buf.dtype)], axis=1)
    pos_q = past_len + jnp.arange(ts, dtype=jnp.int32)
    pos_w = past_len - lb + jnp.arange(lb + ts, dtype=jnp.int32)
    o = nsa_attend(q, pos_q, gl, kc, vc, cmp_end, rows[:, :, 2], rows[:, :, 3],
                   win[:, :, 0], win[:, :, 1], pos_w)
    keep = min(WINDOW, lb + ts)
    return o, win[:, lb + ts - keep:]


def ab_layer_prompt(h, w_in, w_out, q_norm, k_norm, cmp_pos_w, cmp_phi, pool_w, pool_scale):
    b, t = h.shape[:2]
    pos = jnp.arange(t, dtype=jnp.int32)
    pool_in, q, gl, rows, win = ab_features(h, pos, w_in, q_norm, k_norm)
    hist0 = jnp.zeros((b, POOL_HIST, POOL_DIM), h.dtype)
    pool_out = pool_mix(pool_in, hist0, 0, pool_w, pool_scale)
    nsa_out = nsa_prompt(q, gl, rows, win, cmp_pos_w, cmp_phi, k_norm[0])
    y = jnp.concatenate([pool_out, nsa_out.astype(pool_out.dtype)], axis=-1) @ w_out
    keep = min(WINDOW, t)
    return y, rows, win[:, t - keep:], pool_in[:, t - POOL_HIST:]


def ab_layer_sample(h, pool_kv, page_table, win_buf, pool_hist, w_in, w_out, q_norm, k_norm,
                    cmp_pos_w, cmp_phi, pool_w, pool_scale):
    ts = h.shape[1]
    past_len = page_table.shape[1] * pool_kv.shape[1]
    pos = past_len + jnp.arange(ts, dtype=jnp.int32)
    pool_in, q, gl, rows, win = ab_features(h, pos, w_in, q_norm, k_norm)
    pool_out = pool_mix(pool_in, pool_hist, past_len, pool_w, pool_scale)
    nsa_out, new_win = nsa_sample(q, gl, rows, win, pool_kv, page_table, win_buf,
                                  cmp_pos_w, cmp_phi, k_norm[0])
    y = jnp.concatenate([pool_out, nsa_out.astype(pool_out.dtype)], axis=-1) @ w_out
    new_hist = jnp.concatenate([pool_hist.astype(pool_in.dtype), pool_in], axis=1)[:, -POOL_HIST:]
    return y, rows, new_win, new_hist


def wkv_scan(s0, r, w, k, v, kk, a):
    def step(s, inp):
        r_t, w_t, k_t, v_t, kk_t, a_t = inp
        sa = jnp.einsum('bhij,bhj->bhi', s, -kk_t)
        s = (s * w_t[:, :, None, :] + sa[..., None] * (kk_t * a_t)[:, :, None, :]
             + v_t[..., None] * k_t[:, :, None, :])
        return s, jnp.einsum('bhij,bhj->bhi', s, r_t)

    xs = tuple(jnp.moveaxis(z.astype(jnp.float32), 1, 0) for z in (r, w, k, v, kk, a))
    s, o = lax.scan(step, s0.astype(jnp.float32), xs)
    return s, jnp.moveaxis(o, 0, 1)


def rwkv_layer(h, shift_prev, s0, v_first, vres, mu, wr, wk, wv, wo, w0, w1, w2, a0, a1, a2,
               g1, g2, k_k, k_a, r_k, gn_w, gn_b):
    f32 = jnp.float32
    b, t, d = h.shape
    prev = jnp.concatenate([shift_prev[:, None, :].astype(h.dtype), h[:, :-1]], axis=1)
    xx = prev - h
    xr, xw, xk, xv, xa, xg = [h + xx * mu[j] for j in range(6)]
    r = xr @ wr
    k = xk @ wk
    v = xv @ wv
    w_log = -jax.nn.softplus(-(w0 + jnp.tanh(xw @ w1) @ w2).astype(f32)) - 0.5
    decay = jnp.exp(-jnp.exp(w_log))
    if vres is None:
        v_first = v
    else:
        v0, v1, v2 = vres
        v = v + (v_first - v) * jax.nn.sigmoid(v0 + (xv @ v1) @ v2)
    a = jax.nn.sigmoid((a0 + (xa @ a1) @ a2).astype(f32))
    g = jax.nn.sigmoid(xg @ g1) @ g2

    def heads(z):
        return z.reshape(b, t, RWKV_HEADS, RWKV_N).astype(f32)

    kk = heads(k * k_k)
    kk = kk / jnp.maximum(jnp.sqrt(jnp.sum(kk * kk, axis=-1, keepdims=True)), 1e-12)
    k = k.astype(f32) * (1.0 + (a - 1.0) * k_a.astype(f32))
    rh, kh, vh, ah, dh = heads(r), heads(k), heads(v), heads(a), heads(decay)
    s, o = wkv_scan(s0, rh, dh, kh, vh, kk, ah)
    mean = jnp.mean(o, axis=-1, keepdims=True)
    var = jnp.mean(jnp.square(o - mean), axis=-1, keepdims=True)
    o = ((o - mean) * lax.rsqrt(var + GN_EPS) * gn_w.reshape(RWKV_HEADS, RWKV_N).astype(f32)
         + gn_b.reshape(RWKV_HEADS, RWKV_N).astype(f32))
    o = o + jnp.sum(rh * kh * r_k.astype(f32), axis=-1, keepdims=True) * vh
    y = (o.reshape(b, t, d) * g.astype(f32)).astype(h.dtype) @ wo
    return y, v_first, s, h[:, -1]


def hier_moe(h, wc, bc, wf, bf, wg, wu, wd):
    f32 = jnp.float32
    lc = (h @ wc).astype(f32) + bc.astype(f32)
    g_idx = jnp.argmax(lc, axis=-1)
    g_w = jnp.max(jax.nn.softmax(lc, axis=-1), axis=-1)
    g_hot = jax.nn.one_hot(g_idx, MOE_GROUPS, dtype=f32)
    lf = ((h @ wf).astype(f32) + bf.astype(f32)).reshape(h.shape[:-1] + (MOE_GROUPS, MOE_EPG))
    lf_sel = jnp.einsum('btge,btg->bte', lf, g_hot)
    top_v, top_i = lax.top_k(lf_sel, MOE_TOPK)
    top_w = jax.nn.softmax(top_v, axis=-1) * g_w[..., None]
    e_id = g_idx[..., None] * MOE_EPG + top_i
    gate = jnp.einsum('btke,btk->bte', jax.nn.one_hot(e_id, N_EXPERTS, dtype=f32), top_w)
    hg = jnp.einsum('btd,edf->btef', h, wg)
    hu = jnp.einsum('btd,edf->btef', h, wu)
    act = (jax.nn.silu(hg) * hu * gate[..., None].astype(h.dtype)).astype(h.dtype)
    return jnp.einsum('btef,efd->btd', act, wd)


def setup_inputs(seed: int = 0) -> dict:
    key = jax.random.key(seed)
    keys = iter(jax.random.split(key, 64))
    f32 = jnp.float32

    def nrm(shape, scale=1.0):
        return scale * jax.random.normal(next(keys), shape, f32)

    def gain(shape):
        return 1.0 + nrm(shape, 0.02)

    n_pages = PAST_LEN // PAGE_SIZE
    n_used = DEC_BATCH * n_pages
    n_pool = n_used + max(1, n_used // 4)
    win_buf = min(WINDOW, PAST_LEN)
    na, nr, nv, d = N_NSA_LAYERS, N_RWKV_LAYERS, N_VRES, D_MODEL
    inp = {}
    inp['x_prompt'] = nrm((BATCH, SEQ, d))
    inp['x_sample'] = nrm((DEC_BATCH, DEC_SEQ, d))
    inp['cache_nsa_kv'] = nrm((na, n_pool, PAGE_SIZE, 4, NSA_KV_HEADS, HEAD_DIM))
    inp['cache_win_kv'] = nrm((na, DEC_BATCH, win_buf, 2, NSA_KV_HEADS, HEAD_DIM))
    inp['state_pool'] = nrm((na, DEC_BATCH, POOL_HIST, POOL_DIM))
    inp['state_wkv'] = nrm((nr, DEC_BATCH, RWKV_HEADS, RWKV_N, RWKV_N), 0.3)
    inp['state_shift'] = nrm((nr, DEC_BATCH, d))
    perm = jax.random.permutation(next(keys), n_pool)
    inp['page_table'] = perm[:n_used].reshape(DEC_BATCH, n_pages).astype(jnp.int32)
    inp['norm_mix'] = gain((DEPTH, d))
    inp['norm_ffn'] = gain((DEPTH, d))
    inp['ab_w_in'] = nrm((na, d, IN_COLS), d ** -0.5)
    inp['ab_w_out'] = nrm((na, MIX_DIM, d), RES_SCALE * MIX_DIM ** -0.5)
    inp['ab_q_norm'] = gain((na, HEAD_DIM))
    inp['ab_k_norm'] = gain((na, 3, HEAD_DIM))
    inp['cmp_pos_w'] = 1.0 / CMP_LEN + nrm((na, 2, NSA_KV_HEADS, CMP_LEN), 0.005)
    inp['cmp_phi'] = nrm((na, 2, HEAD_DIM, HEAD_DIM), HEAD_DIM ** -0.5)
    inp['pool_w'] = nrm((na, POOL_GROUPS, POOL_GDIM, POOL_GDIM), POOL_GDIM ** -0.5)
    inp['pool_scale'] = 1.0 + nrm((na, POOL_DIM), 0.1)
    inp['rw_mu'] = jax.random.uniform(next(keys), (nr, 6, d), f32)
    inp['rw_wr'] = nrm((nr, d, d), d ** -0.5)
    inp['rw_wk'] = nrm((nr, d, d), d ** -0.5)
    inp['rw_wv'] = nrm((nr, d, d), d ** -0.5)
    inp['rw_wo'] = nrm((nr, d, d), RES_SCALE * d ** -0.5)
    inp['rw_w0'] = -1.0 + nrm((nr, d), 0.5)
    inp['rw_w1'] = nrm((nr, d, LORA_W), d ** -0.5)
    inp['rw_w2'] = nrm((nr, LORA_W, d), 0.5 * LORA_W ** -0.5)
    inp['rw_a0'] = nrm((nr, d), 0.1)
    inp['rw_a1'] = nrm((nr, d, LORA_A), d ** -0.5)
    inp['rw_a2'] = nrm((nr, LORA_A, d), 0.5 * LORA_A ** -0.5)
    inp['rw_v0'] = nrm((nv, d), 0.1)
    inp['rw_v1'] = nrm((nv, d, LORA_V), d ** -0.5)
    inp['rw_v2'] = nrm((nv, LORA_V, d), 0.5 * LORA_V ** -0.5)
    inp['rw_g1'] = nrm((nr, d, LORA_G), d ** -0.5)
    inp['rw_g2'] = nrm((nr, LORA_G, d), LORA_G ** -0.5)
    inp['rw_kk'] = 0.85 + nrm((nr, d), 0.05)
    inp['rw_ka'] = 1.0 + nrm((nr, d), 0.05)
    inp['rw_rk'] = nrm((nr, RWKV_HEADS, RWKV_N), 0.1)
    inp['rw_gn_w'] = gain((nr, d))
    inp['rw_gn_b'] = nrm((nr, d), 0.01)
    inp['moe_wc'] = nrm((DEPTH, d, MOE_GROUPS), d ** -0.5)
    inp['moe_bc'] = nrm((DEPTH, MOE_GROUPS), 0.01)
    inp['moe_wf'] = nrm((DEPTH, d, N_EXPERTS), d ** -0.5)
    inp['moe_bf'] = nrm((DEPTH, N_EXPERTS), 0.01)
    inp['moe_wg'] = nrm((DEPTH, N_EXPERTS, d, D_FF_E), d ** -0.5)
    inp['moe_wu'] = nrm((DEPTH, N_EXPERTS, d, D_FF_E), d ** -0.5)
    inp['moe_wd'] = nrm((DEPTH, N_EXPERTS, D_FF_E, d), RES_SCALE * D_FF_E ** -0.5)
    return inp


def reference(x_prompt, x_sample, cache_nsa_kv, cache_win_kv, state_pool, state_wkv, state_shift,
              page_table, norm_mix, norm_ffn, ab_w_in, ab_w_out, ab_q_norm, ab_k_norm, cmp_pos_w,
              cmp_phi, pool_w, pool_scale, rw_mu, rw_wr, rw_wk, rw_wv, rw_wo, rw_w0, rw_w1, rw_w2,
              rw_a0, rw_a1, rw_a2, rw_v0, rw_v1, rw_v2, rw_g1, rw_g2, rw_kk, rw_ka, rw_rk, rw_gn_w,
              rw_gn_b, moe_wc, moe_bc, moe_wf, moe_bf, moe_wg, moe_wu, moe_wd):
    xp, xs = x_prompt, x_sample
    vf_p, vf_s = None, None
    nsa_p, nsa_s, win_p, win_s, pool_p, pool_s = [], [], [], [], [], []
    wkv_p, wkv_s, sh_p, sh_s = [], [], [], []
    for l in range(DEPTH):
        if l % 2 == 0:
            i = l // 2
            wts = (ab_w_in[i], ab_w_out[i], ab_q_norm[i], ab_k_norm[i], cmp_pos_w[i], cmp_phi[i],
                   pool_w[i], pool_scale[i])
            yp, r_p, w_p, h_p = ab_layer_prompt(rmsnorm(xp, norm_mix[l]), *wts)
            ys, r_s, w_s, h_s = ab_layer_sample(rmsnorm(xs, norm_mix[l]), cache_nsa_kv[i], page_table,
                                                cache_win_kv[i], state_pool[i], *wts)
            nsa_p.append(r_p)
            nsa_s.append(r_s)
            win_p.append(w_p)
            win_s.append(w_s)
            pool_p.append(h_p)
            pool_s.append(h_s)
        else:
            j = l // 2
            vres = None if j == 0 else (rw_v0[j - 1], rw_v1[j - 1], rw_v2[j - 1])
            wts = (rw_mu[j], rw_wr[j], rw_wk[j], rw_wv[j], rw_wo[j], rw_w0[j], rw_w1[j], rw_w2[j],
                   rw_a0[j], rw_a1[j], rw_a2[j], rw_g1[j], rw_g2[j], rw_kk[j], rw_ka[j], rw_rk[j],
                   rw_gn_w[j], rw_gn_b[j])
            bp = xp.shape[0]
            zero_shift = jnp.zeros((bp, D_MODEL), xp.dtype)
            zero_state = jnp.zeros((bp, RWKV_HEADS, RWKV_N, RWKV_N), jnp.float32)
            yp, vf_p, s_p, shp = rwkv_layer(rmsnorm(xp, norm_mix[l]), zero_shift, zero_state, vf_p, vres, *wts)
            ys, vf_s, s_s, shs = rwkv_layer(rmsnorm(xs, norm_mix[l]), state_shift[j], state_wkv[j], vf_s, vres, *wts)
            wkv_p.append(s_p)
            wkv_s.append(s_s)
            sh_p.append(shp)
            sh_s.append(shs)
        xp = xp + yp.astype(xp.dtype)
        xs = xs + ys.astype(xs.dtype)
        moe_w = (moe_wc[l], moe_bc[l], moe_wf[l], moe_bf[l], moe_wg[l], moe_wu[l], moe_wd[l])
        xp = xp + hier_moe(rmsnorm(xp, norm_ffn[l]), *moe_w).astype(xp.dtype)
        xs = xs + hier_moe(rmsnorm(xs, norm_ffn[l]), *moe_w).astype(xs.dtype)
    new_nsa_p = jnp.stack(nsa_p)
    new_nsa_s = jnp.stack(nsa_s)
    new_win_p = jnp.stack(win_p)
    new_win_s = jnp.stack(win_s)
    new_pool_p = jnp.stack(pool_p)
    new_pool_s = jnp.stack(pool_s)
    new_wkv_p = jnp.stack(wkv_p)
    new_wkv_s = jnp.stack(wkv_s)
    new_shift_p = jnp.stack(sh_p)
    new_shift_s = jnp.stack(sh_s)
    return (xp, xs, new_nsa_p, new_nsa_s, new_win_p, new_win_s, new_pool_p, new_pool_s,
            new_wkv_p, new_wkv_s, new_shift_p, new_shift_s)
```

```python
import functools

import jax
import jax.numpy as jnp
from jax import lax
from jax.experimental import pallas as pl
from jax.experimental.pallas import tpu as pltpu


def _mm_kernel(x_ref, w_ref, o_ref):
    o_ref[...] = jnp.dot(x_ref[...].astype(jnp.bfloat16), w_ref[...],
                         preferred_element_type=jnp.float32)


def _mm(x, w):
    lead = x.shape[:-1]
    k = x.shape[-1]
    n = w.shape[1]
    x2 = x.reshape(-1, k)
    m = x2.shape[0]
    npad = -(-n // 128) * 128
    wb = w.astype(jnp.bfloat16)
    if npad != n:
        wb = jnp.pad(wb, ((0, 0), (0, npad - n)))
    tn = npad
    for cand in (512, 640, 384, 256, 128):
        if npad % cand == 0:
            tn = cand
            break
    tm = 512 if m % 512 == 0 else m
    out = pl.pallas_call(
        _mm_kernel,
        grid=(m // tm, npad // tn),
        in_specs=[pl.BlockSpec((tm, k), lambda i, j: (i, 0)),
                  pl.BlockSpec((k, tn), lambda i, j: (0, j))],
        out_specs=pl.BlockSpec((tm, tn), lambda i, j: (i, j)),
        out_shape=jax.ShapeDtypeStruct((m, npad), jnp.float32),
        name="mm",
    )(x2, wb)
    return out[:, :n].reshape(lead + (n,))


f32 = jnp.float32
bf16 = jnp.bfloat16
WKV_CHUNK = 64
WKV_PAIRS = 8
WKV_PASSES = 3


def _split(x):
    hi = x.astype(bf16)
    lo = (x - hi.astype(f32)).astype(bf16)
    return hi, lo


def _mmul(a, b, passes, nt=False):
    dn = (((1,), (1,)), ((), ())) if nt else (((1,), (0,)), ((), ()))
    d = lambda x, y: lax.dot_general(x, y, dn, preferred_element_type=f32)
    if passes == 1:
        return d(a.astype(bf16), b.astype(bf16))
    ah, al = _split(a)
    bh, bl = _split(b)
    return d(ah, bh) + (d(ah, bl) + d(al, bh))


def _wkv_kernel(r_ref, lw_ref, k_ref, v_ref, kk_ref, a_ref, s0_ref, o_ref, sT_ref, st_scr, *, passes):
    C = WKV_CHUNK
    c = pl.program_id(1)
    nc = pl.num_programs(1)
    row = lax.broadcasted_iota(jnp.int32, (2 * C, 2 * C), 0)
    col = lax.broadcasted_iota(jnp.int32, (2 * C, 2 * C), 1)
    bd = (row < C) == (col < C)
    strict = bd & ((row % C) > (col % C))
    incl = bd & ((row % C) >= (col % C))
    eye = (row == col).astype(f32)
    lane_s = col < C
    m1 = lax.broadcasted_iota(jnp.int32, (C, 2 * C), 1) < C
    tri = (lax.broadcasted_iota(jnp.int32, (C, C), 0)
           >= lax.broadcasted_iota(jnp.int32, (C, C), 1)).astype(bf16)

    @pl.when(c == 0)
    def _():
        z = jnp.zeros((C, C), f32)
        for p in range(WKV_PAIRS):
            s1 = s0_ref[0, 2 * p]
            s2 = s0_ref[0, 2 * p + 1]
            st_scr[p] = jnp.concatenate([jnp.concatenate([s1, z], axis=1),
                                         jnp.concatenate([z, s2], axis=1)], axis=0)

    def stack2(x):
        return jnp.concatenate([jnp.where(m1, x, 0.0), jnp.where(m1, 0.0, x)], axis=0)

    dd = lambda x, y: jnp.dot(x, y, preferred_element_type=f32)
    for p in range(WKV_PAIRS):
        sl = slice(p * 2 * C, (p + 1) * 2 * C)
        r = r_ref[0, :, sl]
        lw = lw_ref[0, :, sl]
        k = k_ref[0, :, sl]
        v = v_ref[0, :, sl]
        kk = kk_ref[0, :, sl]
        a = a_ref[0, :, sl]
        S = st_scr[p]
        h1 = lw.astype(bf16)
        r1 = lw - h1.astype(f32)
        h2 = r1.astype(bf16)
        h3 = (r1 - h2.astype(f32)).astype(bf16)
        cw = dd(tri, h1) + (dd(tri, h2) + dd(tri, h3))
        cwC = cw[C - 1:C, :]
        b = kk * a
        At = -kk * jnp.exp(cw - lw)
        Rt = r * jnp.exp(cw)
        einv = jnp.exp(-cw)
        Bt = b * einv
        Kt = k * einv
        efut = jnp.exp(cwC - cw)
        Bh = b * efut
        Kh = k * efut
        X = jnp.concatenate([stack2(At), stack2(Rt)], axis=0)
        Y = jnp.concatenate([Bt, Kt], axis=0)
        G = _mmul(X, Y, passes, nt=True)
        GA = G[0:2 * C]
        GR = G[2 * C:4 * C]
        GAr = pltpu.roll(GA, C, axis=1)
        GRr = pltpu.roll(GR, C, axis=1)
        L = jnp.where(strict, jnp.where(lane_s, GA, GAr), 0.0)
        Mak = jnp.where(strict, jnp.where(lane_s, GAr, GA), 0.0)
        Mrb = jnp.where(incl, jnp.where(lane_s, GR, GRr), 0.0)
        Mrk = jnp.where(incl, jnp.where(lane_s, GRr, GR), 0.0)
        P = eye + L
        Q = L
        for _ in range(5):
            Q = _mmul(Q, Q, passes)
            P = P + _mmul(Q, P, passes)
        AR = jnp.concatenate([At, Rt], axis=0)
        ARS = _mmul(AR, S, passes, nt=True)
        Vs = stack2(v)
        Xs = stack2(ARS[0:C]) + _mmul(Mak, Vs, passes)
        Us = _mmul(P, Xs, passes)
        Os = _mmul(Mrb, Us, passes) + _mmul(Mrk, Vs, passes)
        o_ref[0, :, sl] = ARS[C:2 * C] + Os[0:C] + Os[C:2 * C]
        U = Us[0:C] + Us[C:2 * C]
        UV = jnp.concatenate([U, v], axis=0)
        BK = jnp.concatenate([Bh, Kh], axis=0)
        dS = _mmul(UV.T, BK, passes)
        Snew = S * jnp.exp(cwC) + jnp.where(bd, dS, 0.0)
        st_scr[p] = Snew

        @pl.when(c == nc - 1)
        def _():
            sT_ref[0, 2 * p] = Snew[0:C, 0:C]
            sT_ref[0, 2 * p + 1] = Snew[C:2 * C, C:2 * C]


def wkv_chunked(r, lw, k, v, kk, a, s0):
    B, T, D = r.shape
    H = D // 64
    C = WKV_CHUNK
    assert T % C == 0 and D == WKV_PAIRS * 2 * C
    blk = pl.BlockSpec((1, C, D), lambda b, c: (b, c, 0))
    sblk = pl.BlockSpec((1, H, 64, 64), lambda b, c: (b, 0, 0, 0))
    return pl.pallas_call(
        functools.partial(_wkv_kernel, passes=WKV_PASSES),
        grid=(B, T // C),
        in_specs=[blk] * 6 + [sblk],
        out_specs=[blk, sblk],
        out_shape=[jax.ShapeDtypeStruct((B, T, D), f32), jax.ShapeDtypeStruct((B, H, 64, 64), f32)],
        scratch_shapes=[pltpu.VMEM((WKV_PAIRS, 2 * C, 2 * C), f32)],
        compiler_params=pltpu.CompilerParams(dimension_semantics=("parallel", "arbitrary")),
        name="wkv7_chunked",
    )(r, lw, k, v, kk, a, s0)


D_MODEL = 1024
BATCH = 4
SEQ = 4096
DEPTH = 4
DEC_BATCH = 128
DEC_SEQ = 8
PAST_LEN = 2048
PAGE_SIZE = 128

N_NSA_LAYERS = (DEPTH + 1) // 2
N_RWKV_LAYERS = DEPTH // 2
N_VRES = N_RWKV_LAYERS - 1

POOL_DIM = D_MODEL // 2
POOL_WINDOWS = (2, 4, 8, 16)
POOL_GROUPS = len(POOL_WINDOWS)
POOL_GDIM = POOL_DIM // POOL_GROUPS
POOL_HIST = max(POOL_WINDOWS) - 1

HEAD_DIM = 64
NSA_HEADS = (D_MODEL // 2) // HEAD_DIM
NSA_KV_HEADS = 2
NSA_GQ = NSA_HEADS // NSA_KV_HEADS
NSA_DIM = NSA_HEADS * HEAD_DIM
CMP_STRIDE = 16
CMP_LEN = 2 * CMP_STRIDE
SLC_LEN = 64
N_SEL = 16
WINDOW = 512
Q_BLOCK = 128
ROPE_DIM = HEAD_DIM // 4
ROPE_THETA = 500000.0
MIX_DIM = POOL_DIM + NSA_DIM
KV_COLS = 6 * NSA_KV_HEADS * HEAD_DIM
IN_COLS = POOL_DIM + NSA_DIM + KV_COLS + 3 * NSA_HEADS

RWKV_N = 64
RWKV_HEADS = D_MODEL // RWKV_N
LORA_W = 64
LORA_A = 64
LORA_V = 32
LORA_G = 128
GN_EPS = 64e-5

MOE_GROUPS = 4
MOE_EPG = 4
N_EXPERTS = MOE_GROUPS * MOE_EPG
MOE_TOPK = 2
D_FF_E = 256

RMS_EPS = 1e-6
NEG_INF = -1e30
RES_SCALE = (2 * DEPTH) ** -0.5

SEL_TK = 512
WIN_TK = 128


def _nsa_prompt_kernel(q_ref, ql_ref, g_ref, kc_ref, kcl_ref, vct_ref, ks_ref, vst_ref, kw_ref, vwt_ref,
                       o_ref, score_scr, sel_scr, *, n_cmp):
    QB = Q_BLOCK
    GQ = NSA_GQ * QB
    i = pl.program_id(2)
    s0 = i * QB
    qT = q_ref[...]
    posq = s0 + lax.broadcasted_iota(jnp.int32, (1, GQ), 1) % QB
    dd = lambda x, y: jnp.dot(x, y, preferred_element_type=f32)

    ncp = kc_ref.shape[0]
    cidx = lax.broadcasted_iota(jnp.int32, (ncp, 1), 0)
    mask_c = (cidx * CMP_STRIDE + (CMP_LEN - 1) <= posq) & (cidx < n_cmp)
    sc = dd(kc_ref[...], qT) + (dd(kc_ref[...], ql_ref[...]) + dd(kcl_ref[...], qT))
    sc = jnp.where(mask_c, sc, NEG_INF)
    pe = jnp.exp(sc - jnp.max(sc, axis=0, keepdims=True))
    pc = jnp.where(mask_c, pe / jnp.sum(pe, axis=0, keepdims=True), 0.0)
    o_c = dd(vct_ref[...], pc.astype(bf16))

    imp = (pc[:, 0:QB] + pc[:, QB:2 * QB]) + (pc[:, 2 * QB:3 * QB] + pc[:, 3 * QB:4 * QB])
    n_slc = score_scr.shape[0]
    per = SLC_LEN // CMP_STRIDE
    nn = lax.broadcasted_iota(jnp.int32, (n_slc, ncp), 0) * per
    cc = lax.broadcasted_iota(jnp.int32, (n_slc, ncp), 1)
    mt = (0.5 * ((cc >= nn) & (cc < nn + per)).astype(f32)
          + 0.5 * ((cc + 1 >= nn) & (cc + 1 < nn + per)).astype(f32)).astype(bf16)
    i1 = imp.astype(bf16)
    r1 = imp - i1.astype(f32)
    i2 = r1.astype(bf16)
    i3 = (r1 - i2.astype(f32)).astype(bf16)
    imp_blk = dd(mt, i1) + (dd(mt, i2) + dd(mt, i3))
    nidx = lax.broadcasted_iota(jnp.int32, (n_slc, 1), 0)
    cur = (s0 + lax.broadcasted_iota(jnp.int32, (1, QB), 1)) // SLC_LEN
    forced = (nidx == 0) | (nidx == cur) | (nidx == cur - 1)
    score = jnp.where(nidx > cur, -1.0, jnp.where(forced, 1e6, imp_blk))
    score_scr[...] = score
    rank = jnp.zeros((n_slc, QB), jnp.int32)
    for m in range(n_slc):
        sm = score_scr[m:m + 1, :]
        beats = (sm > score) | ((sm == score) & (nidx > m))
        rank = rank + beats.astype(jnp.int32)
    sel = (rank < min(N_SEL, n_slc)).astype(f32)
    sel_scr[...] = jnp.concatenate([sel] * NSA_GQ, axis=1)

    def attend(k_ref, vt_ref, t_lo, t_hi, tk, mask_fn):
        def body(kt, carry):
            m, l, acc = carry
            k0 = pl.multiple_of(kt * tk, tk)
            s = dd(k_ref[pl.ds(k0, tk), :], qT)
            kpos = k0 + lax.broadcasted_iota(jnp.int32, (tk, 1), 0)
            mask = mask_fn(kt, kpos)
            s = jnp.where(mask, s, NEG_INF)
            m_new = jnp.maximum(m, jnp.max(s, axis=0, keepdims=True))
            alpha = jnp.exp(m - m_new)
            p = jnp.where(mask, jnp.exp(s - m_new), 0.0)
            l = alpha * l + jnp.sum(p, axis=0, keepdims=True)
            acc = alpha * acc + dd(vt_ref[:, pl.ds(k0, tk)], p.astype(bf16))
            return m_new, l, acc
        init = (jnp.full((1, GQ), NEG_INF, f32), jnp.zeros((1, GQ), f32), jnp.zeros((HEAD_DIM, GQ), f32))
        _, l, acc = lax.fori_loop(t_lo, t_hi, body, init)
        return acc / l

    bpt = SEL_TK // SLC_LEN

    def sel_mask(kt, kpos):
        rows = sel_scr[pl.ds(pl.multiple_of(kt * bpt, bpt), bpt), :]
        selx = jnp.concatenate([jnp.broadcast_to(rows[j:j + 1, :], (SLC_LEN, GQ)) for j in range(bpt)], axis=0)
        return (selx > 0.5) & (kpos <= posq)

    o_s = attend(ks_ref, vst_ref, 0, (s0 + QB + SEL_TK - 1) // SEL_TK, SEL_TK, sel_mask)

    def win_mask(kt, kpos):
        dq = posq - kpos
        return (dq >= 0) & (dq < WINDOW)

    o_w = attend(kw_ref, vwt_ref, jnp.maximum(0, (s0 - WINDOW) // WIN_TK), (s0 + QB) // WIN_TK, WIN_TK, win_mask)

    g = jax.nn.sigmoid(g_ref[...])
    o_ref[...] = g[0:1] * o_c + g[1:2] * o_s + g[2:3] * o_w


def nsa_prompt_pallas(q, gl, rows, win, kc, vc):
    B, T = q.shape[:2]
    KVH, G, D, QB = NSA_KV_HEADS, NSA_GQ, HEAD_DIM, Q_BLOCK
    assert T % SEL_TK == 0 and T % QB == 0
    nqb = T // QB
    n_cmp = kc.shape[1]
    ncp = -(-n_cmp // 128) * 128
    n_slc = T // SLC_LEN
    scale = D ** -0.5
    qT = (q * scale).reshape(B, nqb, QB, KVH, G, D).transpose(0, 3, 1, 5, 4, 2).reshape(B, KVH, nqb, D, G * QB)
    qT, qTl = _split(qT)
    gT = gl.reshape(B, nqb, QB, KVH, G, 3).transpose(0, 3, 1, 5, 4, 2).reshape(B, KVH, nqb, 3, G * QB).astype(f32)
    kcp, kcl = _split(jnp.pad(kc, ((0, 0), (0, ncp - n_cmp), (0, 0), (0, 0))).transpose(0, 2, 1, 3))
    vct = jnp.pad(vc, ((0, 0), (0, ncp - n_cmp), (0, 0), (0, 0))).transpose(0, 2, 3, 1).astype(bf16)
    ks = rows[:, :, 2].transpose(0, 2, 1, 3).astype(bf16)
    vst = rows[:, :, 3].transpose(0, 2, 3, 1).astype(bf16)
    kw = win[:, :, 0].transpose(0, 2, 1, 3).astype(bf16)
    vwt = win[:, :, 1].transpose(0, 2, 3, 1).astype(bf16)
    bh = lambda *shape: pl.BlockSpec((None, None) + shape, lambda b, h, i: (b, h) + (0,) * len(shape))
    bhi = lambda *shape: pl.BlockSpec((None, None, None) + shape, lambda b, h, i: (b, h, i) + (0,) * len(shape))
    oT = pl.pallas_call(
        functools.partial(_nsa_prompt_kernel, n_cmp=n_cmp),
        grid=(B, KVH, nqb),
        in_specs=[bhi(D, G * QB), bhi(D, G * QB), bhi(3, G * QB), bh(ncp, D), bh(ncp, D), bh(D, ncp),
                  bh(T, D), bh(D, T), bh(T, D), bh(D, T)],
        out_specs=bhi(D, G * QB),
        out_shape=jax.ShapeDtypeStruct((B, KVH, nqb, D, G * QB), f32),
        scratch_shapes=[pltpu.VMEM((n_slc, QB), f32), pltpu.VMEM((n_slc, G * QB), f32)],
        compiler_params=pltpu.CompilerParams(dimension_semantics=("parallel", "parallel", "arbitrary"),
                                             vmem_limit_bytes=48 * 1024 * 1024),
        name="nsa_prompt",
    )(qT, qTl, gT, kcp, kcl, vct, ks, vst, kw, vwt)
    return oT.reshape(B, KVH, nqb, D, G, QB).transpose(0, 2, 5, 1, 4, 3).reshape(B, T, KVH * G * D)


def rmsnorm(x, g):
    xf = x.astype(jnp.float32)
    y = xf * lax.rsqrt(jnp.mean(xf * xf, axis=-1, keepdims=True) + RMS_EPS)
    return (y * g.astype(jnp.float32)).astype(x.dtype)


def rope_partial(x, pos):
    half = ROPE_DIM // 2
    inv = ROPE_THETA ** (-jnp.arange(half, dtype=jnp.float32) / half)
    ang = pos.astype(jnp.float32)[:, None] * inv
    cos = jnp.cos(ang)[:, None, :]
    sin = jnp.sin(ang)[:, None, :]
    xf = x.astype(jnp.float32)
    x1 = xf[..., :half]
    x2 = xf[..., half:ROPE_DIM]
    out = jnp.concatenate([x1 * cos - x2 * sin, x2 * cos + x1 * sin, xf[..., ROPE_DIM:]], axis=-1)
    return out.astype(x.dtype)


def masked_softmax(s, mask):
    s = jnp.where(mask, s.astype(jnp.float32), NEG_INF)
    p = jax.nn.softmax(s, axis=-1)
    return jnp.where(mask, p, 0.0)


def pool_mix(u, hist, p0, w_grp, scale):
    b, t, _ = u.shape
    ext = jnp.concatenate([hist.astype(u.dtype), u], axis=1).astype(jnp.float32)
    cs = jnp.pad(jnp.cumsum(ext, axis=1), ((0, 0), (1, 0), (0, 0)))
    cnt_pos = p0 + jnp.arange(t, dtype=jnp.int32) + 1
    means = []
    for gi, w in enumerate(POOL_WINDOWS):
        c = cs[..., gi * POOL_GDIM:(gi + 1) * POOL_GDIM]
        win_sum = c[:, POOL_HIST + 1:POOL_HIST + 1 + t] - c[:, POOL_HIST + 1 - w:POOL_HIST + 1 - w + t]
        cnt = jnp.minimum(cnt_pos, w).astype(jnp.float32)[None, :, None]
        means.append(win_sum / cnt)
    mean = jnp.stack(means, axis=2)
    d = mean - u.reshape(b, t, POOL_GROUPS, POOL_GDIM).astype(jnp.float32)
    y = jnp.einsum('btgc,gcd->btgd', d, w_grp.astype(jnp.float32)).reshape(b, t, POOL_DIM)
    return (y * scale.astype(jnp.float32)).astype(u.dtype)


def ab_features(h, pos, w_in, q_norm, k_norm):
    b, t = h.shape[:2]
    u = _mm(h, w_in)
    off_kv = POOL_DIM + NSA_DIM
    pool_in = u[..., :POOL_DIM]
    q = u[..., POOL_DIM:off_kv].reshape(b, t, NSA_HEADS, HEAD_DIM)
    kv = u[..., off_kv:off_kv + KV_COLS].reshape(b, t, 6, NSA_KV_HEADS, HEAD_DIM)
    gl = u[..., off_kv + KV_COLS:].reshape(b, t, NSA_HEADS, 3)
    q = rope_partial(rmsnorm(q, q_norm), pos)
    k_slc = rope_partial(rmsnorm(kv[:, :, 2], k_norm[1]), pos)
    k_win = rope_partial(rmsnorm(kv[:, :, 4], k_norm[2]), pos)
    rows = jnp.stack([kv[:, :, 0], kv[:, :, 1], k_slc, kv[:, :, 3]], axis=2)
    win = jnp.stack([k_win, kv[:, :, 5]], axis=2)
    return pool_in, q, gl, rows, win


def compress_kv(k_rows, v_rows, pos_w, phi, k_gain):
    b, length = k_rows.shape[:2]
    n_chunk = length // CMP_STRIDE

    def weighted_block_mean(rows, w):
        ch = rows.reshape(b, n_chunk, CMP_STRIDE, NSA_KV_HEADS, HEAD_DIM)
        return (jnp.einsum('bnlhd,hl->bnhd', ch[:, :-1], w[:, :CMP_STRIDE])
                + jnp.einsum('bnlhd,hl->bnhd', ch[:, 1:], w[:, CMP_STRIDE:]))

    cmp_end = jnp.arange(n_chunk - 1, dtype=jnp.int32) * CMP_STRIDE + (CMP_LEN - 1)
    kc = jnp.einsum('bnhd,de->bnhe', weighted_block_mean(k_rows, pos_w[0]), phi[0])
    kc = rope_partial(rmsnorm(kc, k_gain), cmp_end)
    vc = jnp.einsum('bnhd,de->bnhe', weighted_block_mean(v_rows, pos_w[1]), phi[1])
    return kc, vc, cmp_end


def nsa_attend(q, pos_q, gl, kc, vc, cmp_end, ks, vs, kw, vw, pos_w):
    f32 = jnp.float32
    b, tq = q.shape[:2]
    qg = q.reshape(b, tq, NSA_KV_HEADS, NSA_GQ, HEAD_DIM)
    scale = HEAD_DIM ** -0.5
    s_c = jnp.einsum('bqhgd,bchd->bhgqc', qg, kc) * scale
    p_c = masked_softmax(s_c, cmp_end[None, :] <= pos_q[:, None])
    o_c = jnp.einsum('bhgqc,bchd->bqhgd', p_c, vc.astype(f32))
    imp = p_c.sum(axis=2)
    imp_chunk = 0.5 * (jnp.pad(imp, ((0, 0), (0, 0), (0, 0), (0, 1)))
                       + jnp.pad(imp, ((0, 0), (0, 0), (0, 0), (1, 0))))
    n_slc = ks.shape[1] // SLC_LEN
    imp_blk = imp_chunk.reshape(b, NSA_KV_HEADS, tq, n_slc, SLC_LEN // CMP_STRIDE).sum(-1)
    blk = jnp.arange(n_slc, dtype=jnp.int32)[None, :]
    cur = (pos_q // SLC_LEN)[:, None]
    forced = (blk == 0) | (blk == cur) | (blk == cur - 1)
    score = jnp.where(blk > cur, -1.0, jnp.where(forced, 1e6, imp_blk))
    n_sel = min(N_SEL, n_slc)
    _, idx = lax.top_k(score, n_sel)
    gather = jax.vmap(jax.vmap(lambda rows, i: rows[i]))
    ksb = ks.reshape(b, n_slc, SLC_LEN, NSA_KV_HEADS, HEAD_DIM).transpose(0, 3, 1, 2, 4)
    vsb = vs.reshape(b, n_slc, SLC_LEN, NSA_KV_HEADS, HEAD_DIM).transpose(0, 3, 1, 2, 4)
    kg = gather(ksb, idx)
    vg = gather(vsb, idx)
    kpos = idx[..., None] * SLC_LEN + jnp.arange(SLC_LEN, dtype=jnp.int32)
    n_keys = n_sel * SLC_LEN
    m_s = (kpos <= pos_q[None, None, :, None, None]).reshape(b, NSA_KV_HEADS, 1, tq, n_keys)
    s_s = jnp.einsum('bqhgd,bhqnld->bhgqnl', qg, kg).reshape(b, NSA_KV_HEADS, NSA_GQ, tq, n_keys) * scale
    p_s = masked_softmax(s_s, m_s)
    o_s = jnp.einsum('bhgqk,bhqkd->bqhgd', p_s,
                     vg.reshape(b, NSA_KV_HEADS, tq, n_keys, HEAD_DIM).astype(f32))
    s_w = jnp.einsum('bqhgd,bkhd->bhgqk', qg, kw) * scale
    dq = pos_q[:, None] - pos_w[None, :]
    m_w = (dq >= 0) & (dq < WINDOW) & (pos_w[None, :] >= 0)
    p_w = masked_softmax(s_w, m_w)
    o_w = jnp.einsum('bhgqk,bkhd->bqhgd', p_w, vw.astype(f32))
    g = jax.nn.sigmoid(gl.astype(f32)).reshape(b, tq, NSA_KV_HEADS, NSA_GQ, 3)
    o = g[..., 0:1] * o_c + g[..., 1:2] * o_s + g[..., 2:3] * o_w
    return o.reshape(b, tq, NSA_DIM)


def nsa_prompt(q, gl, rows, win, cmp_pos_w, cmp_phi, k_gain):
    b, t = q.shape[:2]
    kc, vc, cmp_end = compress_kv(rows[:, :, 0], rows[:, :, 1], cmp_pos_w, cmp_phi, k_gain)
    ks, vs = rows[:, :, 2], rows[:, :, 3]
    win_pad = jnp.pad(win, ((0, 0), (WINDOW, 0), (0, 0), (0, 0), (0, 0)))

    def block(i):
        s0 = i * Q_BLOCK
        qb = lax.dynamic_slice_in_dim(q, s0, Q_BLOCK, axis=1)
        gb = lax.dynamic_slice_in_dim(gl, s0, Q_BLOCK, axis=1)
        wb = lax.dynamic_slice_in_dim(win_pad, s0, WINDOW + Q_BLOCK, axis=1)
        pos_q = s0 + jnp.arange(Q_BLOCK, dtype=jnp.int32)
        pos_w = s0 - WINDOW + jnp.arange(WINDOW + Q_BLOCK, dtype=jnp.int32)
        return nsa_attend(qb, pos_q, gb, kc, vc, cmp_end, ks, vs, wb[:, :, 0], wb[:, :, 1], pos_w)

    out = lax.map(block, jnp.arange(t // Q_BLOCK, dtype=jnp.int32))
    return jnp.moveaxis(out, 0, 1).reshape(b, t, NSA_DIM)


def nsa_sample(q, gl, rows_new, win_new, pool_kv, page_table, win_buf, cmp_pos_w, cmp_phi, k_gain):
    bd, ts = q.shape[:2]
    past_len = page_table.shape[1] * pool_kv.shape[1]
    past = pool_kv[page_table].reshape(bd, past_len, 4, NSA_KV_HEADS, HEAD_DIM)
    rows = jnp.concatenate([past, rows_new.astype(past.dtype)], axis=1)
    pad = (-(past_len + ts)) % SLC_LEN
    rows = jnp.pad(rows, ((0, 0), (0, pad), (0, 0), (0, 0), (0, 0)))
    kc, vc, cmp_end = compress_kv(rows[:, :, 0], rows[:, :, 1], cmp_pos_w, cmp_phi, k_gain)
    lb = win_buf.shape[1]
    win = jnp.concatenate([win_buf, win_new.astype(win_buf.dtype)], axis=1)
    pos_q = past_len + jnp.arange(ts, dtype=jnp.int32)
    pos_w = past_len - lb + jnp.arange(lb + ts, dtype=jnp.int32)
    o = nsa_attend(q, pos_q, gl, kc, vc, cmp_end, rows[:, :, 2], rows[:, :, 3],
                   win[:, :, 0], win[:, :, 1], pos_w)
    keep = min(WINDOW, lb + ts)
    return o, win[:, lb + ts - keep:]


def ab_layer_prompt(h, w_in, w_out, q_norm, k_norm, cmp_pos_w, cmp_phi, pool_w, pool_scale):
    b, t = h.shape[:2]
    pos = jnp.arange(t, dtype=jnp.int32)
    pool_in, q, gl, rows, win = ab_features(h, pos, w_in, q_norm, k_norm)
    hist0 = jnp.zeros((b, POOL_HIST, POOL_DIM), h.dtype)
    pool_out = pool_mix(pool_in, hist0, 0, pool_w, pool_scale)
    kc, vc, _ = compress_kv(rows[:, :, 0], rows[:, :, 1], cmp_pos_w, cmp_phi, k_norm[0])
    nsa_out = nsa_prompt_pallas(q, gl, rows, win, kc, vc)
    y = _mm(jnp.concatenate([pool_out, nsa_out.astype(pool_out.dtype)], axis=-1), w_out)
    keep = min(WINDOW, t)
    return y, rows, win[:, t - keep:], pool_in[:, t - POOL_HIST:]


def ab_layer_sample(h, pool_kv, page_table, win_buf, pool_hist, w_in, w_out, q_norm, k_norm,
                    cmp_pos_w, cmp_phi, pool_w, pool_scale):
    ts = h.shape[1]
    past_len = page_table.shape[1] * pool_kv.shape[1]
    pos = past_len + jnp.arange(ts, dtype=jnp.int32)
    pool_in, q, gl, rows, win = ab_features(h, pos, w_in, q_norm, k_norm)
    pool_out = pool_mix(pool_in, pool_hist, past_len, pool_w, pool_scale)
    nsa_out, new_win = nsa_sample(q, gl, rows, win, pool_kv, page_table, win_buf,
                                  cmp_pos_w, cmp_phi, k_norm[0])
    y = _mm(jnp.concatenate([pool_out, nsa_out.astype(pool_out.dtype)], axis=-1), w_out)
    new_hist = jnp.concatenate([pool_hist.astype(pool_in.dtype), pool_in], axis=1)[:, -POOL_HIST:]
    return y, rows, new_win, new_hist


def wkv_scan(s0, r, w, k, v, kk, a):
    def step(s, inp):
        r_t, w_t, k_t, v_t, kk_t, a_t = inp
        sa = jnp.einsum('bhij,bhj->bhi', s, -kk_t)
        s = (s * w_t[:, :, None, :] + sa[..., None] * (kk_t * a_t)[:, :, None, :]
             + v_t[..., None] * k_t[:, :, None, :])
        return s, jnp.einsum('bhij,bhj->bhi', s, r_t)

    xs = tuple(jnp.moveaxis(z.astype(jnp.float32), 1, 0) for z in (r, w, k, v, kk, a))
    s, o = lax.scan(step, s0.astype(jnp.float32), xs)
    return s, jnp.moveaxis(o, 0, 1)


def rwkv_layer(h, shift_prev, s0, v_first, vres, mu, wr, wk, wv, wo, w0, w1, w2, a0, a1, a2,
               g1, g2, k_k, k_a, r_k, gn_w, gn_b):
    f32 = jnp.float32
    b, t, d = h.shape
    prev = jnp.concatenate([shift_prev[:, None, :].astype(h.dtype), h[:, :-1]], axis=1)
    xx = prev - h
    xr, xw, xk, xv, xa, xg = [h + xx * mu[j] for j in range(6)]
    r = _mm(xr, wr)
    k = _mm(xk, wk)
    v = _mm(xv, wv)
    w_log = -jax.nn.softplus(-(w0 + jnp.tanh(xw @ w1) @ w2).astype(f32)) - 0.5
    decay = jnp.exp(-jnp.exp(w_log))
    if vres is None:
        v_first = v
    else:
        v0, v1, v2 = vres
        v = v + (v_first - v) * jax.nn.sigmoid(v0 + (xv @ v1) @ v2)
    a = jax.nn.sigmoid((a0 + (xa @ a1) @ a2).astype(f32))
    g = jax.nn.sigmoid(xg @ g1) @ g2

    def heads(z):
        return z.reshape(b, t, RWKV_HEADS, RWKV_N).astype(f32)

    kk = heads(k * k_k)
    kk = kk / jnp.maximum(jnp.sqrt(jnp.sum(kk * kk, axis=-1, keepdims=True)), 1e-12)
    k = k.astype(f32) * (1.0 + (a - 1.0) * k_a.astype(f32))
    rh, kh, vh, ah, dh = heads(r), heads(k), heads(v), heads(a), heads(decay)
    if t % WKV_CHUNK == 0:
        o, s = wkv_chunked(r, -jnp.exp(w_log), k, v, kk.reshape(b, t, d), a, s0.astype(f32))
        o = o.reshape(b, t, RWKV_HEADS, RWKV_N)
    else:
        s, o = wkv_scan(s0, rh, dh, kh, vh, kk, ah)
    mean = jnp.mean(o, axis=-1, keepdims=True)
    var = jnp.mean(jnp.square(o - mean), axis=-1, keepdims=True)
    o = ((o - mean) * lax.rsqrt(var + GN_EPS) * gn_w.reshape(RWKV_HEADS, RWKV_N).astype(f32)
         + gn_b.reshape(RWKV_HEADS, RWKV_N).astype(f32))
    o = o + jnp.sum(rh * kh * r_k.astype(f32), axis=-1, keepdims=True) * vh
    y = _mm((o.reshape(b, t, d) * g.astype(f32)).astype(h.dtype), wo)
    return y, v_first, s, h[:, -1]


def hier_moe(h, wc, bc, wf, bf, wg, wu, wd):
    f32 = jnp.float32
    hp = lax.Precision.HIGHEST
    lc = jnp.dot(h, wc, precision=hp).astype(f32) + bc.astype(f32)
    g_idx = jnp.argmax(lc, axis=-1)
    g_w = jnp.max(jax.nn.softmax(lc, axis=-1), axis=-1)
    g_hot = jax.nn.one_hot(g_idx, MOE_GROUPS, dtype=f32)
    lf = (jnp.dot(h, wf, precision=hp).astype(f32) + bf.astype(f32)).reshape(h.shape[:-1] + (MOE_GROUPS, MOE_EPG))
    lf_sel = jnp.einsum('btge,btg->bte', lf, g_hot)
    top_v, top_i = lax.top_k(lf_sel, MOE_TOPK)
    top_w = jax.nn.softmax(top_v, axis=-1) * g_w[..., None]
    e_id = g_idx[..., None] * MOE_EPG + top_i
    gate = jnp.einsum('btke,btk->bte', jax.nn.one_hot(e_id, N_EXPERTS, dtype=f32), top_w)
    hg = jnp.einsum('btd,edf->btef', h, wg)
    hu = jnp.einsum('btd,edf->btef', h, wu)
    act = (jax.nn.silu(hg) * hu * gate[..., None].astype(h.dtype)).astype(h.dtype)
    return jnp.einsum('btef,efd->btd', act, wd)


def kernel(x_prompt, x_sample, cache_nsa_kv, cache_win_kv, state_pool, state_wkv, state_shift,
           page_table, norm_mix, norm_ffn, ab_w_in, ab_w_out, ab_q_norm, ab_k_norm, cmp_pos_w,
           cmp_phi, pool_w, pool_scale, rw_mu, rw_wr, rw_wk, rw_wv, rw_wo, rw_w0, rw_w1, rw_w2,
           rw_a0, rw_a1, rw_a2, rw_v0, rw_v1, rw_v2, rw_g1, rw_g2, rw_kk, rw_ka, rw_rk, rw_gn_w,
           rw_gn_b, moe_wc, moe_bc, moe_wf, moe_bf, moe_wg, moe_wu, moe_wd):
    xp, xs = x_prompt, x_sample
    vf_p, vf_s = None, None
    nsa_p, nsa_s, win_p, win_s, pool_p, pool_s = [], [], [], [], [], []
    wkv_p, wkv_s, sh_p, sh_s = [], [], [], []
    for l in range(DEPTH):
        if l % 2 == 0:
            i = l // 2
            wts = (ab_w_in[i], ab_w_out[i], ab_q_norm[i], ab_k_norm[i], cmp_pos_w[i], cmp_phi[i],
                   pool_w[i], pool_scale[i])
            yp, r_p, w_p, h_p = ab_layer_prompt(rmsnorm(xp, norm_mix[l]), *wts)
            ys, r_s, w_s, h_s = ab_layer_sample(rmsnorm(xs, norm_mix[l]), cache_nsa_kv[i], page_table,
                                                cache_win_kv[i], state_pool[i], *wts)
            nsa_p.append(r_p)
            nsa_s.append(r_s)
            win_p.append(w_p)
            win_s.append(w_s)
            pool_p.append(h_p)
            pool_s.append(h_s)
        else:
            j = l // 2
            vres = None if j == 0 else (rw_v0[j - 1], rw_v1[j - 1], rw_v2[j - 1])
            wts = (rw_mu[j], rw_wr[j], rw_wk[j], rw_wv[j], rw_wo[j], rw_w0[j], rw_w1[j], rw_w2[j],
                   rw_a0[j], rw_a1[j], rw_a2[j], rw_g1[j], rw_g2[j], rw_kk[j], rw_ka[j], rw_rk[j],
                   rw_gn_w[j], rw_gn_b[j])
            bp = xp.shape[0]
            zero_shift = jnp.zeros((bp, D_MODEL), xp.dtype)
            zero_state = jnp.zeros((bp, RWKV_HEADS, RWKV_N, RWKV_N), jnp.float32)
            yp, vf_p, s_p, shp = rwkv_layer(rmsnorm(xp, norm_mix[l]), zero_shift, zero_state, vf_p, vres, *wts)
            ys, vf_s, s_s, shs = rwkv_layer(rmsnorm(xs, norm_mix[l]), state_shift[j], state_wkv[j], vf_s, vres, *wts)
            wkv_p.append(s_p)
            wkv_s.append(s_s)
            sh_p.append(shp)
            sh_s.append(shs)
        xp = xp + yp.astype(xp.dtype)
        xs = xs + ys.astype(xs.dtype)
        moe_w = (moe_wc[l], moe_bc[l], moe_wf[l], moe_bf[l], moe_wg[l], moe_wu[l], moe_wd[l])
        xp = xp + hier_moe(rmsnorm(xp, norm_ffn[l]), *moe_w).astype(xp.dtype)
        xs = xs + hier_moe(rmsnorm(xs, norm_ffn[l]), *moe_w).astype(xs.dtype)
    return (xp, xs, jnp.stack(nsa_p), jnp.stack(nsa_s), jnp.stack(win_p), jnp.stack(win_s),
            jnp.stack(pool_p), jnp.stack(pool_s), jnp.stack(wkv_p), jnp.stack(wkv_s),
            jnp.stack(sh_p), jnp.stack(sh_s))
```

```python
import functools

import jax
import jax.numpy as jnp
from jax import lax
from jax.experimental import pallas as pl
from jax.experimental.pallas import tpu as pltpu


def _mm_kernel(x_ref, w_ref, o_ref):
    o_ref[...] = jnp.dot(x_ref[...].astype(jnp.bfloat16), w_ref[...],
                         preferred_element_type=jnp.float32)


def _mm(x, w):
    lead = x.shape[:-1]
    k = x.shape[-1]
    n = w.shape[1]
    x2 = x.reshape(-1, k)
    m = x2.shape[0]
    npad = -(-n // 128) * 128
    wb = w.astype(jnp.bfloat16)
    if npad != n:
        wb = jnp.pad(wb, ((0, 0), (0, npad - n)))
    tn = npad
    for cand in (512, 640, 384, 256, 128):
        if npad % cand == 0:
            tn = cand
            break
    tm = 512 if m % 512 == 0 else m
    out = pl.pallas_call(
        _mm_kernel,
        grid=(m // tm, npad // tn),
        in_specs=[pl.BlockSpec((tm, k), lambda i, j: (i, 0)),
                  pl.BlockSpec((k, tn), lambda i, j: (0, j))],
        out_specs=pl.BlockSpec((tm, tn), lambda i, j: (i, j)),
        out_shape=jax.ShapeDtypeStruct((m, npad), jnp.float32),
        name="mm",
    )(x2, wb)
    return out[:, :n].reshape(lead + (n,))


f32 = jnp.float32
bf16 = jnp.bfloat16
WKV_CHUNK = 64
WKV_PAIRS = 8
WKV_PASSES = 1
WKV_GRAM_PASSES = 3


def _split(x):
    hi = x.astype(bf16)
    lo = (x - hi.astype(f32)).astype(bf16)
    return hi, lo


def _mmul(a, b, passes, nt=False):
    dn = (((1,), (1,)), ((), ())) if nt else (((1,), (0,)), ((), ()))
    d = lambda x, y: lax.dot_general(x, y, dn, preferred_element_type=f32)
    if passes == 1:
        return d(a.astype(bf16), b.astype(bf16))
    ah, al = _split(a)
    bh, bl = _split(b)
    return d(ah, bh) + (d(ah, bl) + d(al, bh))


def _wkv_kernel(r_ref, lw_ref, k_ref, v_ref, kk_ref, a_ref, s0_ref, o_ref, sT_ref, st_scr, *, passes):
    C = WKV_CHUNK
    c = pl.program_id(1)
    nc = pl.num_programs(1)
    row = lax.broadcasted_iota(jnp.int32, (2 * C, 2 * C), 0)
    col = lax.broadcasted_iota(jnp.int32, (2 * C, 2 * C), 1)
    bd = (row < C) == (col < C)
    strict = bd & ((row % C) > (col % C))
    incl = bd & ((row % C) >= (col % C))
    eye = (row == col).astype(f32)
    lane_s = col < C
    m1 = lax.broadcasted_iota(jnp.int32, (C, 2 * C), 1) < C
    tri = (lax.broadcasted_iota(jnp.int32, (C, C), 0)
           >= lax.broadcasted_iota(jnp.int32, (C, C), 1)).astype(bf16)

    @pl.when(c == 0)
    def _():
        z = jnp.zeros((C, C), f32)
        for p in range(WKV_PAIRS):
            s1 = s0_ref[0, 2 * p]
            s2 = s0_ref[0, 2 * p + 1]
            st_scr[p] = jnp.concatenate([jnp.concatenate([s1, z], axis=1),
                                         jnp.concatenate([z, s2], axis=1)], axis=0)

    def stack2(x):
        return jnp.concatenate([jnp.where(m1, x, 0.0), jnp.where(m1, 0.0, x)], axis=0)

    dd = lambda x, y: jnp.dot(x, y, preferred_element_type=f32)
    pairs = range(WKV_PAIRS)
    sls = [slice(p * 2 * C, (p + 1) * 2 * C) for p in pairs]

    def prep(p):
        sl = sls[p]
        lw = lw_ref[0, :, sl]
        kk = kk_ref[0, :, sl]
        h1 = lw.astype(bf16)
        r1 = lw - h1.astype(f32)
        h2 = r1.astype(bf16)
        h3 = (r1 - h2.astype(f32)).astype(bf16)
        cw = dd(tri, h1) + (dd(tri, h2) + dd(tri, h3))
        cwC = cw[C - 1:C, :]
        b = kk * a_ref[0, :, sl]
        k = k_ref[0, :, sl]
        At = -kk * jnp.exp(cw - lw)
        Rt = r_ref[0, :, sl] * jnp.exp(cw)
        einv = jnp.exp(-cw)
        efut = jnp.exp(cwC - cw)
        X = jnp.concatenate([stack2(At), stack2(Rt)], axis=0)
        Y = jnp.concatenate([b * einv, k * einv], axis=0)
        AR = jnp.concatenate([At, Rt], axis=0)
        BK = jnp.concatenate([b * efut, k * efut], axis=0)
        return X, Y, AR, BK, jnp.exp(cwC)

    pre = [prep(p) for p in pairs]
    G = [_mmul(pre[p][0], pre[p][1], WKV_GRAM_PASSES, nt=True) for p in pairs]
    ARS = [_mmul(pre[p][2], st_scr[p], passes, nt=True) for p in pairs]
    L, Mak, Mrb, Mrk = [], [], [], []
    for p in pairs:
        GA = G[p][0:2 * C]
        GR = G[p][2 * C:4 * C]
        GAr = pltpu.roll(GA, C, axis=1)
        GRr = pltpu.roll(GR, C, axis=1)
        L.append(jnp.where(strict, jnp.where(lane_s, GA, GAr), 0.0))
        Mak.append(jnp.where(strict, jnp.where(lane_s, GAr, GA), 0.0))
        Mrb.append(jnp.where(incl, jnp.where(lane_s, GR, GRr), 0.0))
        Mrk.append(jnp.where(incl, jnp.where(lane_s, GRr, GR), 0.0))
    Vs = [stack2(v_ref[0, :, sls[p]]) for p in pairs]
    Xs = [stack2(ARS[p][0:C]) + _mmul(Mak[p], Vs[p], passes) for p in pairs]
    OV = [_mmul(Mrk[p], Vs[p], passes) for p in pairs]
    P = [eye + L[p] for p in pairs]
    Q = L
    for _ in range(5):
        Q = [_mmul(Q[p], Q[p], passes) for p in pairs]
        P = [P[p] + _mmul(Q[p], P[p], passes) for p in pairs]
    Us = [_mmul(P[p], Xs[p], passes) for p in pairs]
    Os = [_mmul(Mrb[p], Us[p], passes) + OV[p] for p in pairs]
    for p in pairs:
        o_ref[0, :, sls[p]] = ARS[p][C:2 * C] + Os[p][0:C] + Os[p][C:2 * C]
    for p in pairs:
        U = Us[p][0:C] + Us[p][C:2 * C]
        UV = jnp.concatenate([U, v_ref[0, :, sls[p]]], axis=0)
        dS = _mmul(UV.T, pre[p][3], passes)
        Snew = st_scr[p] * pre[p][4] + jnp.where(bd, dS, 0.0)
        st_scr[p] = Snew

        @pl.when(c == nc - 1)
        def _():
            sT_ref[0, 2 * p] = Snew[0:C, 0:C]
            sT_ref[0, 2 * p + 1] = Snew[C:2 * C, C:2 * C]


def wkv_chunked(r, lw, k, v, kk, a, s0):
    B, T, D = r.shape
    H = D // 64
    C = WKV_CHUNK
    assert T % C == 0 and D == WKV_PAIRS * 2 * C
    blk = pl.BlockSpec((1, C, D), lambda b, c: (b, c, 0))
    sblk = pl.BlockSpec((1, H, 64, 64), lambda b, c: (b, 0, 0, 0))
    return pl.pallas_call(
        functools.partial(_wkv_kernel, passes=WKV_PASSES),
        grid=(B, T // C),
        in_specs=[blk] * 6 + [sblk],
        out_specs=[blk, sblk],
        out_shape=[jax.ShapeDtypeStruct((B, T, D), f32), jax.ShapeDtypeStruct((B, H, 64, 64), f32)],
        scratch_shapes=[pltpu.VMEM((WKV_PAIRS, 2 * C, 2 * C), f32)],
        compiler_params=pltpu.CompilerParams(dimension_semantics=("parallel", "arbitrary")),
        name="wkv7_chunked",
    )(r, lw, k, v, kk, a, s0)


D_MODEL = 1024
BATCH = 4
SEQ = 4096
DEPTH = 4
DEC_BATCH = 128
DEC_SEQ = 8
PAST_LEN = 2048
PAGE_SIZE = 128

N_NSA_LAYERS = (DEPTH + 1) // 2
N_RWKV_LAYERS = DEPTH // 2
N_VRES = N_RWKV_LAYERS - 1

POOL_DIM = D_MODEL // 2
POOL_WINDOWS = (2, 4, 8, 16)
POOL_GROUPS = len(POOL_WINDOWS)
POOL_GDIM = POOL_DIM // POOL_GROUPS
POOL_HIST = max(POOL_WINDOWS) - 1

HEAD_DIM = 64
NSA_HEADS = (D_MODEL // 2) // HEAD_DIM
NSA_KV_HEADS = 2
NSA_GQ = NSA_HEADS // NSA_KV_HEADS
NSA_DIM = NSA_HEADS * HEAD_DIM
CMP_STRIDE = 16
CMP_LEN = 2 * CMP_STRIDE
SLC_LEN = 64
N_SEL = 16
WINDOW = 512
Q_BLOCK = 128
ROPE_DIM = HEAD_DIM // 4
ROPE_THETA = 500000.0
MIX_DIM = POOL_DIM + NSA_DIM
KV_COLS = 6 * NSA_KV_HEADS * HEAD_DIM
IN_COLS = POOL_DIM + NSA_DIM + KV_COLS + 3 * NSA_HEADS

RWKV_N = 64
RWKV_HEADS = D_MODEL // RWKV_N
LORA_W = 64
LORA_A = 64
LORA_V = 32
LORA_G = 128
GN_EPS = 64e-5

MOE_GROUPS = 4
MOE_EPG = 4
N_EXPERTS = MOE_GROUPS * MOE_EPG
MOE_TOPK = 2
D_FF_E = 256

RMS_EPS = 1e-6
NEG_INF = -1e30
RES_SCALE = (2 * DEPTH) ** -0.5

SEL_TK = 512
WIN_TK = 128


def _nsa_prompt_kernel(q_ref, ql_ref, g_ref, kc_ref, kcl_ref, vct_ref, ks_ref, vst_ref, kw_ref, vwt_ref,
                       o_ref, score_scr, sel_scr, *, n_cmp):
    QB = Q_BLOCK
    GQ = NSA_GQ * QB
    i = pl.program_id(2)
    s0 = i * QB
    qT = q_ref[...]
    posq = s0 + lax.broadcasted_iota(jnp.int32, (1, GQ), 1) % QB
    dd = lambda x, y: jnp.dot(x, y, preferred_element_type=f32)

    ncp = kc_ref.shape[0]
    cidx = lax.broadcasted_iota(jnp.int32, (ncp, 1), 0)
    mask_c = (cidx * CMP_STRIDE + (CMP_LEN - 1) <= posq) & (cidx < n_cmp)
    sc = dd(kc_ref[...], qT) + (dd(kc_ref[...], ql_ref[...]) + dd(kcl_ref[...], qT))
    sc = jnp.where(mask_c, sc, NEG_INF)
    pe = jnp.exp(sc - jnp.max(sc, axis=0, keepdims=True))
    pc = jnp.where(mask_c, pe / jnp.sum(pe, axis=0, keepdims=True), 0.0)
    o_c = dd(vct_ref[...], pc.astype(bf16))

    imp = (pc[:, 0:QB] + pc[:, QB:2 * QB]) + (pc[:, 2 * QB:3 * QB] + pc[:, 3 * QB:4 * QB])
    n_slc = score_scr.shape[0]
    per = SLC_LEN // CMP_STRIDE
    nn = lax.broadcasted_iota(jnp.int32, (n_slc, ncp), 0) * per
    cc = lax.broadcasted_iota(jnp.int32, (n_slc, ncp), 1)
    mt = (0.5 * ((cc >= nn) & (cc < nn + per)).astype(f32)
          + 0.5 * ((cc + 1 >= nn) & (cc + 1 < nn + per)).astype(f32)).astype(bf16)
    i1 = imp.astype(bf16)
    r1 = imp - i1.astype(f32)
    i2 = r1.astype(bf16)
    i3 = (r1 - i2.astype(f32)).astype(bf16)
    imp_blk = dd(mt, i1) + (dd(mt, i2) + dd(mt, i3))
    nidx = lax.broadcasted_iota(jnp.int32, (n_slc, 1), 0)
    cur = (s0 + lax.broadcasted_iota(jnp.int32, (1, QB), 1)) // SLC_LEN
    forced = (nidx == 0) | (nidx == cur) | (nidx == cur - 1)
    score = jnp.where(nidx > cur, -1.0, jnp.where(forced, 1e6, imp_blk))
    score_scr[...] = score
    rank = jnp.zeros((n_slc, QB), jnp.int32)
    for m in range(n_slc):
        sm = score_scr[m:m + 1, :]
        beats = (sm > score) | ((sm == score) & (nidx > m))
        rank = rank + beats.astype(jnp.int32)
    sel = (rank < min(N_SEL, n_slc)).astype(f32)
    sel_scr[...] = jnp.concatenate([sel] * NSA_GQ, axis=1)

    def attend(k_ref, vt_ref, t_lo, t_hi, tk, mask_fn):
        def body(kt, carry):
            m, l, acc = carry
            k0 = pl.multiple_of(kt * tk, tk)
            s = dd(k_ref[pl.ds(k0, tk), :], qT)
            kpos = k0 + lax.broadcasted_iota(jnp.int32, (tk, 1), 0)
            mask = mask_fn(kt, kpos)
            s = jnp.where(mask, s, NEG_INF)
            m_new = jnp.maximum(m, jnp.max(s, axis=0, keepdims=True))
            alpha = jnp.exp(m - m_new)
            p = jnp.where(mask, jnp.exp(s - m_new), 0.0)
            l = alpha * l + jnp.sum(p, axis=0, keepdims=True)
            acc = alpha * acc + dd(vt_ref[:, pl.ds(k0, tk)], p.astype(bf16))
            return m_new, l, acc
        init = (jnp.full((1, GQ), NEG_INF, f32), jnp.zeros((1, GQ), f32), jnp.zeros((HEAD_DIM, GQ), f32))
        _, l, acc = lax.fori_loop(t_lo, t_hi, body, init)
        return acc / l

    bpt = SEL_TK // SLC_LEN

    def sel_mask(kt, kpos):
        rows = sel_scr[pl.ds(pl.multiple_of(kt * bpt, bpt), bpt), :]
        selx = jnp.concatenate([jnp.broadcast_to(rows[j:j + 1, :], (SLC_LEN, GQ)) for j in range(bpt)], axis=0)
        return (selx > 0.5) & (kpos <= posq)

    o_s = attend(ks_ref, vst_ref, 0, (s0 + QB + SEL_TK - 1) // SEL_TK, SEL_TK, sel_mask)

    def win_mask(kt, kpos):
        dq = posq - kpos
        return (dq >= 0) & (dq < WINDOW)

    o_w = attend(kw_ref, vwt_ref, jnp.maximum(0, (s0 - WINDOW) // WIN_TK), (s0 + QB) // WIN_TK, WIN_TK, win_mask)

    g = jax.nn.sigmoid(g_ref[...])
    o_ref[...] = g[0:1] * o_c + g[1:2] * o_s + g[2:3] * o_w


def nsa_prompt_pallas(q, gl, rows, win, kc, vc):
    B, T = q.shape[:2]
    KVH, G, D, QB = NSA_KV_HEADS, NSA_GQ, HEAD_DIM, Q_BLOCK
    assert T % SEL_TK == 0 and T % QB == 0
    nqb = T // QB
    n_cmp = kc.shape[1]
    ncp = -(-n_cmp // 128) * 128
    n_slc = T // SLC_LEN
    scale = D ** -0.5
    qT = (q * scale).reshape(B, nqb, QB, KVH, G, D).transpose(0, 3, 1, 5, 4, 2).reshape(B, KVH, nqb, D, G * QB)
    qT, qTl = _split(qT)
    gT = gl.reshape(B, nqb, QB, KVH, G, 3).transpose(0, 3, 1, 5, 4, 2).reshape(B, KVH, nqb, 3, G * QB).astype(f32)
    kcp, kcl = _split(jnp.pad(kc, ((0, 0), (0, ncp - n_cmp), (0, 0), (0, 0))).transpose(0, 2, 1, 3))
    vct = jnp.pad(vc, ((0, 0), (0, ncp - n_cmp), (0, 0), (0, 0))).transpose(0, 2, 3, 1).astype(bf16)
    ks = rows[:, :, 2].transpose(0, 2, 1, 3).astype(bf16)
    vst = rows[:, :, 3].transpose(0, 2, 3, 1).astype(bf16)
    kw = win[:, :, 0].transpose(0, 2, 1, 3).astype(bf16)
    vwt = win[:, :, 1].transpose(0, 2, 3, 1).astype(bf16)
    bh = lambda *shape: pl.BlockSpec((None, None) + shape, lambda b, h, i: (b, h) + (0,) * len(shape))
    bhi = lambda *shape: pl.BlockSpec((None, None, None) + shape, lambda b, h, i: (b, h, i) + (0,) * len(shape))
    oT = pl.pallas_call(
        functools.partial(_nsa_prompt_kernel, n_cmp=n_cmp),
        grid=(B, KVH, nqb),
        in_specs=[bhi(D, G * QB), bhi(D, G * QB), bhi(3, G * QB), bh(ncp, D), bh(ncp, D), bh(D, ncp),
                  bh(T, D), bh(D, T), bh(T, D), bh(D, T)],
        out_specs=bhi(D, G * QB),
        out_shape=jax.ShapeDtypeStruct((B, KVH, nqb, D, G * QB), f32),
        scratch_shapes=[pltpu.VMEM((n_slc, QB), f32), pltpu.VMEM((n_slc, G * QB), f32)],
        compiler_params=pltpu.CompilerParams(dimension_semantics=("parallel", "parallel", "arbitrary"),
                                             vmem_limit_bytes=48 * 1024 * 1024),
        name="nsa_prompt",
    )(qT, qTl, gT, kcp, kcl, vct, ks, vst, kw, vwt)
    return oT.reshape(B, KVH, nqb, D, G, QB).transpose(0, 2, 5, 1, 4, 3).reshape(B, T, KVH * G * D)


MOE_TM = 512
ROUTER_LANES = 128


def _moe_kernel(x_ref, g_ref, wrh_ref, wrl_ref, br_ref, wg_ref, wu_ref, wd_ref, o_ref,
                h_scr, gate_scr, acc_scr):
    grp = pl.program_id(1)
    dd = lambda a, b: jnp.dot(a, b, preferred_element_type=f32)
    tm = x_ref.shape[0]
    lane = lax.broadcasted_iota(jnp.int32, (tm, ROUTER_LANES), 1).astype(f32)
    far = float(ROUTER_LANES)

    @pl.when(grp == 0)
    def _():
        x = x_ref[...]
        h = x * lax.rsqrt(jnp.mean(x * x, axis=-1, keepdims=True) + RMS_EPS) * g_ref[...]
        hh, hl = _split(h)
        h_scr[...] = hh
        logits = dd(hh, wrh_ref[...]) + (dd(hh, wrl_ref[...]) + dd(hl, wrh_ref[...])) + br_ref[...]
        is_c = lane < MOE_GROUPS
        lc = jnp.where(is_c, logits, NEG_INF)
        mc = jnp.max(lc, axis=1, keepdims=True)
        g_idx = jnp.min(jnp.where(lc == mc, lane, far), axis=1, keepdims=True)
        g_w = 1.0 / jnp.sum(jnp.where(is_c, jnp.exp(lc - mc), 0.0), axis=1, keepdims=True)
        lo = MOE_GROUPS + MOE_EPG * g_idx
        lf = jnp.where((lane >= lo) & (lane < lo + MOE_EPG), logits, NEG_INF)
        v1 = jnp.max(lf, axis=1, keepdims=True)
        i1 = jnp.min(jnp.where(lf == v1, lane, far), axis=1, keepdims=True)
        lf2 = jnp.where(lane == i1, NEG_INF, lf)
        v2 = jnp.max(lf2, axis=1, keepdims=True)
        i2 = jnp.min(jnp.where(lf2 == v2, lane, far), axis=1, keepdims=True)
        e21 = jnp.exp(v2 - v1)
        w1 = g_w / (1.0 + e21)
        gate_scr[...] = jnp.where(lane == i1, w1, jnp.where(lane == i2, e21 * w1, 0.0))
        acc_scr[...] = x

    h = h_scr[...]
    hg = dd(h, wg_ref[...])
    hu = dd(h, wu_ref[...])
    gate = gate_scr[...]
    first = (MOE_GROUPS + MOE_EPG * grp).astype(f32)
    cols = []
    for e in range(MOE_EPG):
        ge = jnp.sum(jnp.where(lane == first + e, gate, 0.0), axis=1, keepdims=True)
        sl = slice(e * D_FF_E, (e + 1) * D_FF_E)
        hge = hg[:, sl]
        cols.append((hge * jax.nn.sigmoid(hge) * hu[:, sl] * ge).astype(bf16))
    acc_scr[...] += dd(jnp.concatenate(cols, axis=1), wd_ref[...])

    @pl.when(grp == MOE_GROUPS - 1)
    def _():
        o_ref[...] = acc_scr[...]


def moe_prep(g, wc, bc, wf, bf, wg, wu, wd):
    d = wc.shape[0]
    pad = ROUTER_LANES - MOE_GROUPS - N_EXPERTS
    wr = jnp.pad(jnp.concatenate([wc, wf], axis=1).astype(f32), ((0, 0), (0, pad)))
    wrh, wrl = _split(wr)
    br = jnp.pad(jnp.concatenate([bc, bf]).astype(f32), (0, pad)).reshape(1, ROUTER_LANES)
    regroup = lambda w: (w.reshape(MOE_GROUPS, MOE_EPG, d, D_FF_E).transpose(0, 2, 1, 3)
                         .reshape(MOE_GROUPS, d, MOE_EPG * D_FF_E).astype(bf16))
    wdg = wd.reshape(MOE_GROUPS, MOE_EPG * D_FF_E, d).astype(bf16)
    return g.reshape(1, d).astype(f32), wrh, wrl, br, regroup(wg), regroup(wu), wdg


def moe_residual(x, prep):
    g, wrh, wrl, br, wgg, wug, wdg = prep
    shp = x.shape
    d = shp[-1]
    x2 = x.reshape(-1, d)
    m = x2.shape[0]
    tm = MOE_TM
    assert m % tm == 0
    gf = MOE_EPG * D_FF_E
    full = lambda r, c: pl.BlockSpec((r, c), lambda i, j: (0, 0))
    out = pl.pallas_call(
        _moe_kernel,
        grid=(m // tm, MOE_GROUPS),
        in_specs=[pl.BlockSpec((tm, d), lambda i, j: (i, 0)), full(1, d),
                  full(d, ROUTER_LANES), full(d, ROUTER_LANES), full(1, ROUTER_LANES),
                  pl.BlockSpec((None, d, gf), lambda i, j: (j, 0, 0)),
                  pl.BlockSpec((None, d, gf), lambda i, j: (j, 0, 0)),
                  pl.BlockSpec((None, gf, d), lambda i, j: (j, 0, 0))],
        out_specs=pl.BlockSpec((tm, d), lambda i, j: (i, 0)),
        out_shape=jax.ShapeDtypeStruct((m, d), f32),
        scratch_shapes=[pltpu.VMEM((tm, d), bf16), pltpu.VMEM((tm, ROUTER_LANES), f32),
                        pltpu.VMEM((tm, d), f32)],
        compiler_params=pltpu.CompilerParams(dimension_semantics=("parallel", "arbitrary"),
                                             vmem_limit_bytes=48 * 1024 * 1024),
        name="moe",
    )(x2, g, wrh, wrl, br, wgg, wug, wdg)
    return out.reshape(shp)


def rmsnorm(x, g):
    xf = x.astype(jnp.float32)
    y = xf * lax.rsqrt(jnp.mean(xf * xf, axis=-1, keepdims=True) + RMS_EPS)
    return (y * g.astype(jnp.float32)).astype(x.dtype)


def rope_partial(x, pos):
    half = ROPE_DIM // 2
    inv = ROPE_THETA ** (-jnp.arange(half, dtype=jnp.float32) / half)
    ang = pos.astype(jnp.float32)[:, None] * inv
    cos = jnp.cos(ang)[:, None, :]
    sin = jnp.sin(ang)[:, None, :]
    xf = x.astype(jnp.float32)
    x1 = xf[..., :half]
    x2 = xf[..., half:ROPE_DIM]
    out = jnp.concatenate([x1 * cos - x2 * sin, x2 * cos + x1 * sin, xf[..., ROPE_DIM:]], axis=-1)
    return out.astype(x.dtype)


def masked_softmax(s, mask):
    s = jnp.where(mask, s.astype(jnp.float32), NEG_INF)
    p = jax.nn.softmax(s, axis=-1)
    return jnp.where(mask, p, 0.0)


def pool_mix(u, hist, p0, w_grp, scale):
    b, t, _ = u.shape
    ext = jnp.concatenate([hist.astype(u.dtype), u], axis=1).astype(jnp.float32)
    cs = jnp.pad(jnp.cumsum(ext, axis=1), ((0, 0), (1, 0), (0, 0)))
    cnt_pos = p0 + jnp.arange(t, dtype=jnp.int32) + 1
    means = []
    for gi, w in enumerate(POOL_WINDOWS):
        c = cs[..., gi * POOL_GDIM:(gi + 1) * POOL_GDIM]
        win_sum = c[:, POOL_HIST + 1:POOL_HIST + 1 + t] - c[:, POOL_HIST + 1 - w:POOL_HIST + 1 - w + t]
        cnt = jnp.minimum(cnt_pos, w).astype(jnp.float32)[None, :, None]
        means.append(win_sum / cnt)
    mean = jnp.stack(means, axis=2)
    d = mean - u.reshape(b, t, POOL_GROUPS, POOL_GDIM).astype(jnp.float32)
    y = jnp.einsum('btgc,gcd->btgd', d, w_grp.astype(jnp.float32)).reshape(b, t, POOL_DIM)
    return (y * scale.astype(jnp.float32)).astype(u.dtype)


def ab_features(h, pos, w_in, q_norm, k_norm):
    b, t = h.shape[:2]
    u = _mm(h, w_in)
    off_kv = POOL_DIM + NSA_DIM
    pool_in = u[..., :POOL_DIM]
    q = u[..., POOL_DIM:off_kv].reshape(b, t, NSA_HEADS, HEAD_DIM)
    kv = u[..., off_kv:off_kv + KV_COLS].reshape(b, t, 6, NSA_KV_HEADS, HEAD_DIM)
    gl = u[..., off_kv + KV_COLS:].reshape(b, t, NSA_HEADS, 3)
    q = rope_partial(rmsnorm(q, q_norm), pos)
    k_slc = rope_partial(rmsnorm(kv[:, :, 2], k_norm[1]), pos)
    k_win = rope_partial(rmsnorm(kv[:, :, 4], k_norm[2]), pos)
    rows = jnp.stack([kv[:, :, 0], kv[:, :, 1], k_slc, kv[:, :, 3]], axis=2)
    win = jnp.stack([k_win, kv[:, :, 5]], axis=2)
    return pool_in, q, gl, rows, win


def compress_kv(k_rows, v_rows, pos_w, phi, k_gain):
    b, length = k_rows.shape[:2]
    n_chunk = length // CMP_STRIDE

    def weighted_block_mean(rows, w):
        ch = rows.reshape(b, n_chunk, CMP_STRIDE, NSA_KV_HEADS, HEAD_DIM)
        return (jnp.einsum('bnlhd,hl->bnhd', ch[:, :-1], w[:, :CMP_STRIDE])
                + jnp.einsum('bnlhd,hl->bnhd', ch[:, 1:], w[:, CMP_STRIDE:]))

    cmp_end = jnp.arange(n_chunk - 1, dtype=jnp.int32) * CMP_STRIDE + (CMP_LEN - 1)
    kc = jnp.einsum('bnhd,de->bnhe', weighted_block_mean(k_rows, pos_w[0]), phi[0])
    kc = rope_partial(rmsnorm(kc, k_gain), cmp_end)
    vc = jnp.einsum('bnhd,de->bnhe', weighted_block_mean(v_rows, pos_w[1]), phi[1])
    return kc, vc, cmp_end


def nsa_attend(q, pos_q, gl, kc, vc, cmp_end, ks, vs, kw, vw, pos_w):
    f32 = jnp.float32
    b, tq = q.shape[:2]
    qg = q.reshape(b, tq, NSA_KV_HEADS, NSA_GQ, HEAD_DIM)
    scale = HEAD_DIM ** -0.5
    s_c = jnp.einsum('bqhgd,bchd->bhgqc', qg, kc) * scale
    p_c = masked_softmax(s_c, cmp_end[None, :] <= pos_q[:, None])
    o_c = jnp.einsum('bhgqc,bchd->bqhgd', p_c, vc.astype(f32))
    imp = p_c.sum(axis=2)
    imp_chunk = 0.5 * (jnp.pad(imp, ((0, 0), (0, 0), (0, 0), (0, 1)))
                       + jnp.pad(imp, ((0, 0), (0, 0), (0, 0), (1, 0))))
    n_slc = ks.shape[1] // SLC_LEN
    imp_blk = imp_chunk.reshape(b, NSA_KV_HEADS, tq, n_slc, SLC_LEN // CMP_STRIDE).sum(-1)
    blk = jnp.arange(n_slc, dtype=jnp.int32)[None, :]
    cur = (pos_q // SLC_LEN)[:, None]
    forced = (blk == 0) | (blk == cur) | (blk == cur - 1)
    score = jnp.where(blk > cur, -1.0, jnp.where(forced, 1e6, imp_blk))
    n_sel = min(N_SEL, n_slc)
    _, idx = lax.top_k(score, n_sel)
    gather = jax.vmap(jax.vmap(lambda rows, i: rows[i]))
    ksb = ks.reshape(b, n_slc, SLC_LEN, NSA_KV_HEADS, HEAD_DIM).transpose(0, 3, 1, 2, 4)
    vsb = vs.reshape(b, n_slc, SLC_LEN, NSA_KV_HEADS, HEAD_DIM).transpose(0, 3, 1, 2, 4)
    kg = gather(ksb, idx)
    vg = gather(vsb, idx)
    kpos = idx[..., None] * SLC_LEN + jnp.arange(SLC_LEN, dtype=jnp.int32)
    n_keys = n_sel * SLC_LEN
    m_s = (kpos <= pos_q[None, None, :, None, None]).reshape(b, NSA_KV_HEADS, 1, tq, n_keys)
    s_s = jnp.einsum('bqhgd,bhqnld->bhgqnl', qg, kg).reshape(b, NSA_KV_HEADS, NSA_GQ, tq, n_keys) * scale
    p_s = masked_softmax(s_s, m_s)
    o_s = jnp.einsum('bhgqk,bhqkd->bqhgd', p_s,
                     vg.reshape(b, NSA_KV_HEADS, tq, n_keys, HEAD_DIM).astype(f32))
    s_w = jnp.einsum('bqhgd,bkhd->bhgqk', qg, kw) * scale
    dq = pos_q[:, None] - pos_w[None, :]
    m_w = (dq >= 0) & (dq < WINDOW) & (pos_w[None, :] >= 0)
    p_w = masked_softmax(s_w, m_w)
    o_w = jnp.einsum('bhgqk,bkhd->bqhgd', p_w, vw.astype(f32))
    g = jax.nn.sigmoid(gl.astype(f32)).reshape(b, tq, NSA_KV_HEADS, NSA_GQ, 3)
    o = g[..., 0:1] * o_c + g[..., 1:2] * o_s + g[..., 2:3] * o_w
    return o.reshape(b, tq, NSA_DIM)


def nsa_prompt(q, gl, rows, win, cmp_pos_w, cmp_phi, k_gain):
    b, t = q.shape[:2]
    kc, vc, cmp_end = compress_kv(rows[:, :, 0], rows[:, :, 1], cmp_pos_w, cmp_phi, k_gain)
    ks, vs = rows[:, :, 2], rows[:, :, 3]
    win_pad = jnp.pad(win, ((0, 0), (WINDOW, 0), (0, 0), (0, 0), (0, 0)))

    def block(i):
        s0 = i * Q_BLOCK
        qb = lax.dynamic_slice_in_dim(q, s0, Q_BLOCK, axis=1)
        gb = lax.dynamic_slice_in_dim(gl, s0, Q_BLOCK, axis=1)
        wb = lax.dynamic_slice_in_dim(win_pad, s0, WINDOW + Q_BLOCK, axis=1)
        pos_q = s0 + jnp.arange(Q_BLOCK, dtype=jnp.int32)
        pos_w = s0 - WINDOW + jnp.arange(WINDOW + Q_BLOCK, dtype=jnp.int32)
        return nsa_attend(qb, pos_q, gb, kc, vc, cmp_end, ks, vs, wb[:, :, 0], wb[:, :, 1], pos_w)

    out = lax.map(block, jnp.arange(t // Q_BLOCK, dtype=jnp.int32))
    return jnp.moveaxis(out, 0, 1).reshape(b, t, NSA_DIM)


def nsa_sample(q, gl, rows_new, win_new, pool_kv, page_table, win_buf, cmp_pos_w, cmp_phi, k_gain):
    bd, ts = q.shape[:2]
    past_len = page_table.shape[1] * pool_kv.shape[1]
    past = pool_kv[page_table].reshape(bd, past_len, 4, NSA_KV_HEADS, HEAD_DIM)
    rows = jnp.concatenate([past, rows_new.astype(past.dtype)], axis=1)
    pad = (-(past_len + ts)) % SLC_LEN
    rows = jnp.pad(rows, ((0, 0), (0, pad), (0, 0), (0, 0), (0, 0)))
    kc, vc, cmp_end = compress_kv(rows[:, :, 0], rows[:, :, 1], cmp_pos_w, cmp_phi, k_gain)
    lb = win_buf.shape[1]
    win = jnp.concatenate([win_buf, win_new.astype(win_buf.dtype)], axis=1)
    pos_q = past_len + jnp.arange(ts, dtype=jnp.int32)
    pos_w = past_len - lb + jnp.arange(lb + ts, dtype=jnp.int32)
    o = nsa_attend(q, pos_q, gl, kc, vc, cmp_end, rows[:, :, 2], rows[:, :, 3],
                   win[:, :, 0], win[:, :, 1], pos_w)
    keep = min(WINDOW, lb + ts)
    return o, win[:, lb + ts - keep:]


def ab_layer_prompt(h, w_in, w_out, q_norm, k_norm, cmp_pos_w, cmp_phi, pool_w, pool_scale):
    b, t = h.shape[:2]
    pos = jnp.arange(t, dtype=jnp.int32)
    pool_in, q, gl, rows, win = ab_features(h, pos, w_in, q_norm, k_norm)
    hist0 = jnp.zeros((b, POOL_HIST, POOL_DIM), h.dtype)
    pool_out = pool_mix(pool_in, hist0, 0, pool_w, pool_scale)
    kc, vc, _ = compress_kv(rows[:, :, 0], rows[:, :, 1], cmp_pos_w, cmp_phi, k_norm[0])
    nsa_out = nsa_prompt_pallas(q, gl, rows, win, kc, vc)
    y = _mm(jnp.concatenate([pool_out, nsa_out.astype(pool_out.dtype)], axis=-1), w_out)
    keep = min(WINDOW, t)
    return y, rows, win[:, t - keep:], pool_in[:, t - POOL_HIST:]


def ab_layer_sample(h, pool_kv, page_table, win_buf, pool_hist, w_in, w_out, q_norm, k_norm,
                    cmp_pos_w, cmp_phi, pool_w, pool_scale):
    ts = h.shape[1]
    past_len = page_table.shape[1] * pool_kv.shape[1]
    pos = past_len + jnp.arange(ts, dtype=jnp.int32)
    pool_in, q, gl, rows, win = ab_features(h, pos, w_in, q_norm, k_norm)
    pool_out = pool_mix(pool_in, pool_hist, past_len, pool_w, pool_scale)
    nsa_out, new_win = nsa_sample(q, gl, rows, win, pool_kv, page_table, win_buf,
                                  cmp_pos_w, cmp_phi, k_norm[0])
    y = _mm(jnp.concatenate([pool_out, nsa_out.astype(pool_out.dtype)], axis=-1), w_out)
    new_hist = jnp.concatenate([pool_hist.astype(pool_in.dtype), pool_in], axis=1)[:, -POOL_HIST:]
    return y, rows, new_win, new_hist


def wkv_scan(s0, r, w, k, v, kk, a):
    def step(s, inp):
        r_t, w_t, k_t, v_t, kk_t, a_t = inp
        sa = jnp.einsum('bhij,bhj->bhi', s, -kk_t)
        s = (s * w_t[:, :, None, :] + sa[..., None] * (kk_t * a_t)[:, :, None, :]
             + v_t[..., None] * k_t[:, :, None, :])
        return s, jnp.einsum('bhij,bhj->bhi', s, r_t)

    xs = tuple(jnp.moveaxis(z.astype(jnp.float32), 1, 0) for z in (r, w, k, v, kk, a))
    s, o = lax.scan(step, s0.astype(jnp.float32), xs)
    return s, jnp.moveaxis(o, 0, 1)


def rwkv_layer(h, shift_prev, s0, v_first, vres, mu, wr, wk, wv, wo, w0, w1, w2, a0, a1, a2,
               g1, g2, k_k, k_a, r_k, gn_w, gn_b):
    f32 = jnp.float32
    b, t, d = h.shape
    prev = jnp.concatenate([shift_prev[:, None, :].astype(h.dtype), h[:, :-1]], axis=1)
    xx = prev - h
    xr, xw, xk, xv, xa, xg = [h + xx * mu[j] for j in range(6)]
    r = _mm(xr, wr)
    k = _mm(xk, wk)
    v = _mm(xv, wv)
    w_log = -jax.nn.softplus(-(w0 + jnp.tanh(xw @ w1) @ w2).astype(f32)) - 0.5
    decay = jnp.exp(-jnp.exp(w_log))
    if vres is None:
        v_first = v
    else:
        v0, v1, v2 = vres
        v = v + (v_first - v) * jax.nn.sigmoid(v0 + (xv @ v1) @ v2)
    a = jax.nn.sigmoid((a0 + (xa @ a1) @ a2).astype(f32))
    g = jax.nn.sigmoid(xg @ g1) @ g2

    def heads(z):
        return z.reshape(b, t, RWKV_HEADS, RWKV_N).astype(f32)

    kk = heads(k * k_k)
    kk = kk / jnp.maximum(jnp.sqrt(jnp.sum(kk * kk, axis=-1, keepdims=True)), 1e-12)
    k = k.astype(f32) * (1.0 + (a - 1.0) * k_a.astype(f32))
    rh, kh, vh, ah, dh = heads(r), heads(k), heads(v), heads(a), heads(decay)
    if t % WKV_CHUNK == 0:
        o, s = wkv_chunked(r, -jnp.exp(w_log), k, v, kk.reshape(b, t, d), a, s0.astype(f32))
        o = o.reshape(b, t, RWKV_HEADS, RWKV_N)
    else:
        s, o = wkv_scan(s0, rh, dh, kh, vh, kk, ah)
    mean = jnp.mean(o, axis=-1, keepdims=True)
    var = jnp.mean(jnp.square(o - mean), axis=-1, keepdims=True)
    o = ((o - mean) * lax.rsqrt(var + GN_EPS) * gn_w.reshape(RWKV_HEADS, RWKV_N).astype(f32)
         + gn_b.reshape(RWKV_HEADS, RWKV_N).astype(f32))
    o = o + jnp.sum(rh * kh * r_k.astype(f32), axis=-1, keepdims=True) * vh
    y = _mm((o.reshape(b, t, d) * g.astype(f32)).astype(h.dtype), wo)
    return y, v_first, s, h[:, -1]


def hier_moe(h, wc, bc, wf, bf, wg, wu, wd):
    f32 = jnp.float32
    hp = lax.Precision.HIGHEST
    lc = jnp.dot(h, wc, precision=hp).astype(f32) + bc.astype(f32)
    g_idx = jnp.argmax(lc, axis=-1)
    g_w = jnp.max(jax.nn.softmax(lc, axis=-1), axis=-1)
    g_hot = jax.nn.one_hot(g_idx, MOE_GROUPS, dtype=f32)
    lf = (jnp.dot(h, wf, precision=hp).astype(f32) + bf.astype(f32)).reshape(h.shape[:-1] + (MOE_GROUPS, MOE_EPG))
    lf_sel = jnp.einsum('btge,btg->bte', lf, g_hot)
    top_v, top_i = lax.top_k(lf_sel, MOE_TOPK)
    top_w = jax.nn.softmax(top_v, axis=-1) * g_w[..., None]
    e_id = g_idx[..., None] * MOE_EPG + top_i
    gate = jnp.einsum('btke,btk->bte', jax.nn.one_hot(e_id, N_EXPERTS, dtype=f32), top_w)
    hg = jnp.einsum('btd,edf->btef', h, wg)
    hu = jnp.einsum('btd,edf->btef', h, wu)
    act = (jax.nn.silu(hg) * hu * gate[..., None].astype(h.dtype)).astype(h.dtype)
    return jnp.einsum('btef,efd->btd', act, wd)


def kernel(x_prompt, x_sample, cache_nsa_kv, cache_win_kv, state_pool, state_wkv, state_shift,
           page_table, norm_mix, norm_ffn, ab_w_in, ab_w_out, ab_q_norm, ab_k_norm, cmp_pos_w,
           cmp_phi, pool_w, pool_scale, rw_mu, rw_wr, rw_wk, rw_wv, rw_wo, rw_w0, rw_w1, rw_w2,
           rw_a0, rw_a1, rw_a2, rw_v0, rw_v1, rw_v2, rw_g1, rw_g2, rw_kk, rw_ka, rw_rk, rw_gn_w,
           rw_gn_b, moe_wc, moe_bc, moe_wf, moe_bf, moe_wg, moe_wu, moe_wd):
    xp, xs = x_prompt, x_sample
    vf_p, vf_s = None, None
    nsa_p, nsa_s, win_p, win_s, pool_p, pool_s = [], [], [], [], [], []
    wkv_p, wkv_s, sh_p, sh_s = [], [], [], []
    for l in range(DEPTH):
        if l % 2 == 0:
            i = l // 2
            wts = (ab_w_in[i], ab_w_out[i], ab_q_norm[i], ab_k_norm[i], cmp_pos_w[i], cmp_phi[i],
                   pool_w[i], pool_scale[i])
            yp, r_p, w_p, h_p = ab_layer_prompt(rmsnorm(xp, norm_mix[l]), *wts)
            ys, r_s, w_s, h_s = ab_layer_sample(rmsnorm(xs, norm_mix[l]), cache_nsa_kv[i], page_table,
                                                cache_win_kv[i], state_pool[i], *wts)
            nsa_p.append(r_p)
            nsa_s.append(r_s)
            win_p.append(w_p)
            win_s.append(w_s)
            pool_p.append(h_p)
            pool_s.append(h_s)
        else:
            j = l // 2
            vres = None if j == 0 else (rw_v0[j - 1], rw_v1[j - 1], rw_v2[j - 1])
            wts = (rw_mu[j], rw_wr[j], rw_wk[j], rw_wv[j], rw_wo[j], rw_w0[j], rw_w1[j], rw_w2[j],
                   rw_a0[j], rw_a1[j], rw_a2[j], rw_g1[j], rw_g2[j], rw_kk[j], rw_ka[j], rw_rk[j],
                   rw_gn_w[j], rw_gn_b[j])
            bp = xp.shape[0]
            zero_shift = jnp.zeros((bp, D_MODEL), xp.dtype)
            zero_state = jnp.zeros((bp, RWKV_HEADS, RWKV_N, RWKV_N), jnp.float32)
            yp, vf_p, s_p, shp = rwkv_layer(rmsnorm(xp, norm_mix[l]), zero_shift, zero_state, vf_p, vres, *wts)
            ys, vf_s, s_s, shs = rwkv_layer(rmsnorm(xs, norm_mix[l]), state_shift[j], state_wkv[j], vf_s, vres, *wts)
            wkv_p.append(s_p)
            wkv_s.append(s_s)
            sh_p.append(shp)
            sh_s.append(shs)
        xp = xp + yp.astype(xp.dtype)
        xs = xs + ys.astype(xs.dtype)
        prep = moe_prep(norm_ffn[l], moe_wc[l], moe_bc[l], moe_wf[l], moe_bf[l], moe_wg[l], moe_wu[l], moe_wd[l])
        xp = moe_residual(xp, prep)
        xs = moe_residual(xs, prep)
    return (xp, xs, jnp.stack(nsa_p), jnp.stack(nsa_s), jnp.stack(win_p), jnp.stack(win_s),
            jnp.stack(pool_p), jnp.stack(pool_s), jnp.stack(wkv_p), jnp.stack(wkv_s),
            jnp.stack(sh_p), jnp.stack(sh_s))
```

```python
import functools

import jax
import jax.numpy as jnp
from jax import lax
from jax.experimental import pallas as pl
from jax.experimental.pallas import tpu as pltpu


def _mm_kernel(x_ref, w_ref, o_ref):
    o_ref[...] = jnp.dot(x_ref[...].astype(jnp.bfloat16), w_ref[...],
                         preferred_element_type=jnp.float32)


def _mm(x, w):
    lead = x.shape[:-1]
    k = x.shape[-1]
    n = w.shape[1]
    x2 = x.reshape(-1, k)
    m = x2.shape[0]
    npad = -(-n // 128) * 128
    wb = w.astype(jnp.bfloat16)
    if npad != n:
        wb = jnp.pad(wb, ((0, 0), (0, npad - n)))
    tn = npad
    for cand in (512, 640, 384, 256, 128):
        if npad % cand == 0:
            tn = cand
            break
    tm = 512 if m % 512 == 0 else m
    out = pl.pallas_call(
        _mm_kernel,
        grid=(m // tm, npad // tn),
        in_specs=[pl.BlockSpec((tm, k), lambda i, j: (i, 0)),
                  pl.BlockSpec((k, tn), lambda i, j: (0, j))],
        out_specs=pl.BlockSpec((tm, tn), lambda i, j: (i, j)),
        out_shape=jax.ShapeDtypeStruct((m, npad), jnp.float32),
        name="mm",
    )(x2, wb)
    return out[:, :n].reshape(lead + (n,))


f32 = jnp.float32
bf16 = jnp.bfloat16
WKV_CHUNK = 64
WKV_PAIRS = 8
WKV_PASSES = 1
WKV_GRAM_PASSES = 3


def _split(x):
    hi = x.astype(bf16)
    lo = (x - hi.astype(f32)).astype(bf16)
    return hi, lo


def _mmul(a, b, passes, nt=False):
    dn = (((1,), (1,)), ((), ())) if nt else (((1,), (0,)), ((), ()))
    d = lambda x, y: lax.dot_general(x, y, dn, preferred_element_type=f32)
    if passes == 1:
        return d(a.astype(bf16), b.astype(bf16))
    ah, al = _split(a)
    bh, bl = _split(b)
    return d(ah, bh) + (d(ah, bl) + d(al, bh))


def _wkv_kernel(r_ref, lw_ref, k_ref, v_ref, kk_ref, a_ref, s0_ref, o_ref, sT_ref, st_scr, *, passes):
    C = WKV_CHUNK
    c = pl.program_id(1)
    nc = pl.num_programs(1)
    row = lax.broadcasted_iota(jnp.int32, (2 * C, 2 * C), 0)
    col = lax.broadcasted_iota(jnp.int32, (2 * C, 2 * C), 1)
    bd = (row < C) == (col < C)
    strict = bd & ((row % C) > (col % C))
    incl = bd & ((row % C) >= (col % C))
    eye = (row == col).astype(f32)
    lane_s = col < C
    m1 = lax.broadcasted_iota(jnp.int32, (C, 2 * C), 1) < C
    tri = (lax.broadcasted_iota(jnp.int32, (C, C), 0)
           >= lax.broadcasted_iota(jnp.int32, (C, C), 1)).astype(bf16)

    @pl.when(c == 0)
    def _():
        z = jnp.zeros((C, C), f32)
        for p in range(WKV_PAIRS):
            s1 = s0_ref[0, 2 * p]
            s2 = s0_ref[0, 2 * p + 1]
            st_scr[p] = jnp.concatenate([jnp.concatenate([s1, z], axis=1),
                                         jnp.concatenate([z, s2], axis=1)], axis=0)

    def stack2(x):
        return jnp.concatenate([jnp.where(m1, x, 0.0), jnp.where(m1, 0.0, x)], axis=0)

    dd = lambda x, y: jnp.dot(x, y, preferred_element_type=f32)
    pairs = range(WKV_PAIRS)
    sls = [slice(p * 2 * C, (p + 1) * 2 * C) for p in pairs]

    def prep(p):
        sl = sls[p]
        lw = lw_ref[0, :, sl]
        kk = kk_ref[0, :, sl]
        h1 = lw.astype(bf16)
        r1 = lw - h1.astype(f32)
        h2 = r1.astype(bf16)
        h3 = (r1 - h2.astype(f32)).astype(bf16)
        cw = dd(tri, h1) + (dd(tri, h2) + dd(tri, h3))
        cwC = cw[C - 1:C, :]
        b = kk * a_ref[0, :, sl]
        k = k_ref[0, :, sl]
        At = -kk * jnp.exp(cw - lw)
        Rt = r_ref[0, :, sl] * jnp.exp(cw)
        einv = jnp.exp(-cw)
        efut = jnp.exp(cwC - cw)
        X = jnp.concatenate([stack2(At), stack2(Rt)], axis=0)
        Y = jnp.concatenate([b * einv, k * einv], axis=0)
        AR = jnp.concatenate([At, Rt], axis=0)
        BK = jnp.concatenate([b * efut, k * efut], axis=0)
        return X, Y, AR, BK, jnp.exp(cwC)

    pre = [prep(p) for p in pairs]
    G = [_mmul(pre[p][0], pre[p][1], WKV_GRAM_PASSES, nt=True) for p in pairs]
    ARS = [_mmul(pre[p][2], st_scr[p], passes, nt=True) for p in pairs]
    L, Mak, Mrb, Mrk = [], [], [], []
    for p in pairs:
        GA = G[p][0:2 * C]
        GR = G[p][2 * C:4 * C]
        GAr = pltpu.roll(GA, C, axis=1)
        GRr = pltpu.roll(GR, C, axis=1)
        L.append(jnp.where(strict, jnp.where(lane_s, GA, GAr), 0.0))
        Mak.append(jnp.where(strict, jnp.where(lane_s, GAr, GA), 0.0))
        Mrb.append(jnp.where(incl, jnp.where(lane_s, GR, GRr), 0.0))
        Mrk.append(jnp.where(incl, jnp.where(lane_s, GRr, GR), 0.0))
    Vs = [stack2(v_ref[0, :, sls[p]]) for p in pairs]
    Xs = [stack2(ARS[p][0:C]) + _mmul(Mak[p], Vs[p], passes) for p in pairs]
    OV = [_mmul(Mrk[p], Vs[p], passes) for p in pairs]
    P = [eye + L[p] for p in pairs]
    Q = L
    for _ in range(5):
        Q = [_mmul(Q[p], Q[p], passes) for p in pairs]
        P = [P[p] + _mmul(Q[p], P[p], passes) for p in pairs]
    Us = [_mmul(P[p], Xs[p], passes) for p in pairs]
    Os = [_mmul(Mrb[p], Us[p], passes) + OV[p] for p in pairs]
    for p in pairs:
        o_ref[0, :, sls[p]] = ARS[p][C:2 * C] + Os[p][0:C] + Os[p][C:2 * C]
    for p in pairs:
        U = Us[p][0:C] + Us[p][C:2 * C]
        UV = jnp.concatenate([U, v_ref[0, :, sls[p]]], axis=0)
        dS = _mmul(UV.T, pre[p][3], passes)
        Snew = st_scr[p] * pre[p][4] + jnp.where(bd, dS, 0.0)
        st_scr[p] = Snew

        @pl.when(c == nc - 1)
        def _():
            sT_ref[0, 2 * p] = Snew[0:C, 0:C]
            sT_ref[0, 2 * p + 1] = Snew[C:2 * C, C:2 * C]


def wkv_chunked(r, lw, k, v, kk, a, s0):
    B, T, D = r.shape
    H = D // 64
    C = WKV_CHUNK
    assert T % C == 0 and D == WKV_PAIRS * 2 * C
    blk = pl.BlockSpec((1, C, D), lambda b, c: (b, c, 0))
    sblk = pl.BlockSpec((1, H, 64, 64), lambda b, c: (b, 0, 0, 0))
    return pl.pallas_call(
        functools.partial(_wkv_kernel, passes=WKV_PASSES),
        grid=(B, T // C),
        in_specs=[blk] * 6 + [sblk],
        out_specs=[blk, sblk],
        out_shape=[jax.ShapeDtypeStruct((B, T, D), f32), jax.ShapeDtypeStruct((B, H, 64, 64), f32)],
        scratch_shapes=[pltpu.VMEM((WKV_PAIRS, 2 * C, 2 * C), f32)],
        compiler_params=pltpu.CompilerParams(dimension_semantics=("parallel", "arbitrary")),
        name="wkv7_chunked",
    )(r, lw, k, v, kk, a, s0)


D_MODEL = 1024
BATCH = 4
SEQ = 4096
DEPTH = 4
DEC_BATCH = 128
DEC_SEQ = 8
PAST_LEN = 2048
PAGE_SIZE = 128

N_NSA_LAYERS = (DEPTH + 1) // 2
N_RWKV_LAYERS = DEPTH // 2
N_VRES = N_RWKV_LAYERS - 1

POOL_DIM = D_MODEL // 2
POOL_WINDOWS = (2, 4, 8, 16)
POOL_GROUPS = len(POOL_WINDOWS)
POOL_GDIM = POOL_DIM // POOL_GROUPS
POOL_HIST = max(POOL_WINDOWS) - 1

HEAD_DIM = 64
NSA_HEADS = (D_MODEL // 2) // HEAD_DIM
NSA_KV_HEADS = 2
NSA_GQ = NSA_HEADS // NSA_KV_HEADS
NSA_DIM = NSA_HEADS * HEAD_DIM
CMP_STRIDE = 16
CMP_LEN = 2 * CMP_STRIDE
SLC_LEN = 64
N_SEL = 16
WINDOW = 512
Q_BLOCK = 128
ROPE_DIM = HEAD_DIM // 4
ROPE_THETA = 500000.0
MIX_DIM = POOL_DIM + NSA_DIM
KV_COLS = 6 * NSA_KV_HEADS * HEAD_DIM
IN_COLS = POOL_DIM + NSA_DIM + KV_COLS + 3 * NSA_HEADS

RWKV_N = 64
RWKV_HEADS = D_MODEL // RWKV_N
LORA_W = 64
LORA_A = 64
LORA_V = 32
LORA_G = 128
GN_EPS = 64e-5

MOE_GROUPS = 4
MOE_EPG = 4
N_EXPERTS = MOE_GROUPS * MOE_EPG
MOE_TOPK = 2
D_FF_E = 256

RMS_EPS = 1e-6
NEG_INF = -1e30
RES_SCALE = (2 * DEPTH) ** -0.5

SEL_TK = 512
WIN_TK = 128


def _nsa_prompt_kernel(q_ref, ql_ref, g_ref, kc_ref, kcl_ref, vct_ref, ks_ref, vst_ref, kw_ref, vwt_ref,
                       o_ref, score_scr, sel_scr, *, n_cmp):
    QB = Q_BLOCK
    GQ = NSA_GQ * QB
    i = pl.program_id(2)
    s0 = i * QB
    qT = q_ref[...]
    posq = s0 + lax.broadcasted_iota(jnp.int32, (1, GQ), 1) % QB
    dd = lambda x, y: jnp.dot(x, y, preferred_element_type=f32)

    ncp = kc_ref.shape[0]
    cidx = lax.broadcasted_iota(jnp.int32, (ncp, 1), 0)
    mask_c = (cidx * CMP_STRIDE + (CMP_LEN - 1) <= posq) & (cidx < n_cmp)
    sc = dd(kc_ref[...], qT) + (dd(kc_ref[...], ql_ref[...]) + dd(kcl_ref[...], qT))
    sc = jnp.where(mask_c, sc, NEG_INF)
    pe = jnp.exp(sc - jnp.max(sc, axis=0, keepdims=True))
    pc = jnp.where(mask_c, pe / jnp.sum(pe, axis=0, keepdims=True), 0.0)
    o_c = dd(vct_ref[...], pc.astype(bf16))

    imp = (pc[:, 0:QB] + pc[:, QB:2 * QB]) + (pc[:, 2 * QB:3 * QB] + pc[:, 3 * QB:4 * QB])
    n_slc = score_scr.shape[0]
    per = SLC_LEN // CMP_STRIDE
    nn = lax.broadcasted_iota(jnp.int32, (n_slc, ncp), 0) * per
    cc = lax.broadcasted_iota(jnp.int32, (n_slc, ncp), 1)
    mt = (0.5 * ((cc >= nn) & (cc < nn + per)).astype(f32)
          + 0.5 * ((cc + 1 >= nn) & (cc + 1 < nn + per)).astype(f32)).astype(bf16)
    i1 = imp.astype(bf16)
    r1 = imp - i1.astype(f32)
    i2 = r1.astype(bf16)
    i3 = (r1 - i2.astype(f32)).astype(bf16)
    imp_blk = dd(mt, i1) + (dd(mt, i2) + dd(mt, i3))
    nidx = lax.broadcasted_iota(jnp.int32, (n_slc, 1), 0)
    cur = (s0 + lax.broadcasted_iota(jnp.int32, (1, QB), 1)) // SLC_LEN
    forced = (nidx == 0) | (nidx == cur) | (nidx == cur - 1)
    score = jnp.where(nidx > cur, -1.0, jnp.where(forced, 1e6, imp_blk))
    score_scr[...] = score
    rank = jnp.zeros((n_slc, QB), jnp.int32)
    for m in range(n_slc):
        sm = score_scr[m:m + 1, :]
        beats = (sm > score) | ((sm == score) & (nidx > m))
        rank = rank + beats.astype(jnp.int32)
    sel = (rank < min(N_SEL, n_slc)).astype(f32)
    sel_scr[...] = jnp.concatenate([sel] * NSA_GQ, axis=1)

    def attend(k_ref, vt_ref, t_lo, t_hi, tk, mask_fn):
        def body(kt, carry):
            m, l, acc = carry
            k0 = pl.multiple_of(kt * tk, tk)
            s = dd(k_ref[pl.ds(k0, tk), :], qT)
            kpos = k0 + lax.broadcasted_iota(jnp.int32, (tk, 1), 0)
            mask = mask_fn(kt, kpos)
            s = jnp.where(mask, s, NEG_INF)
            m_new = jnp.maximum(m, jnp.max(s, axis=0, keepdims=True))
            alpha = jnp.exp(m - m_new)
            p = jnp.where(mask, jnp.exp(s - m_new), 0.0)
            l = alpha * l + jnp.sum(p, axis=0, keepdims=True)
            acc = alpha * acc + dd(vt_ref[:, pl.ds(k0, tk)], p.astype(bf16))
            return m_new, l, acc
        init = (jnp.full((1, GQ), NEG_INF, f32), jnp.zeros((1, GQ), f32), jnp.zeros((HEAD_DIM, GQ), f32))
        _, l, acc = lax.fori_loop(t_lo, t_hi, body, init)
        return acc / l

    bpt = SEL_TK // SLC_LEN

    def sel_mask(kt, kpos):
        rows = sel_scr[pl.ds(pl.multiple_of(kt * bpt, bpt), bpt), :]
        selx = jnp.concatenate([jnp.broadcast_to(rows[j:j + 1, :], (SLC_LEN, GQ)) for j in range(bpt)], axis=0)
        return (selx > 0.5) & (kpos <= posq)

    o_s = attend(ks_ref, vst_ref, 0, (s0 + QB + SEL_TK - 1) // SEL_TK, SEL_TK, sel_mask)

    def win_mask(kt, kpos):
        dq = posq - kpos
        return (dq >= 0) & (dq < WINDOW)

    o_w = attend(kw_ref, vwt_ref, jnp.maximum(0, (s0 - WINDOW) // WIN_TK), (s0 + QB) // WIN_TK, WIN_TK, win_mask)

    g = jax.nn.sigmoid(g_ref[...])
    o_ref[...] = g[0:1] * o_c + g[1:2] * o_s + g[2:3] * o_w


def nsa_prompt_pallas(q, gl, rows, win, kc, vc):
    B, T = q.shape[:2]
    KVH, G, D, QB = NSA_KV_HEADS, NSA_GQ, HEAD_DIM, Q_BLOCK
    assert T % SEL_TK == 0 and T % QB == 0
    nqb = T // QB
    n_cmp = kc.shape[1]
    ncp = -(-n_cmp // 128) * 128
    n_slc = T // SLC_LEN
    scale = D ** -0.5
    qT = (q * scale).reshape(B, nqb, QB, KVH, G, D).transpose(0, 3, 1, 5, 4, 2).reshape(B, KVH, nqb, D, G * QB)
    qT, qTl = _split(qT)
    gT = gl.reshape(B, nqb, QB, KVH, G, 3).transpose(0, 3, 1, 5, 4, 2).reshape(B, KVH, nqb, 3, G * QB).astype(f32)
    kcp, kcl = _split(jnp.pad(kc, ((0, 0), (0, ncp - n_cmp), (0, 0), (0, 0))).transpose(0, 2, 1, 3))
    vct = jnp.pad(vc, ((0, 0), (0, ncp - n_cmp), (0, 0), (0, 0))).transpose(0, 2, 3, 1).astype(bf16)
    ks = rows[:, :, 2].transpose(0, 2, 1, 3).astype(bf16)
    vst = rows[:, :, 3].transpose(0, 2, 3, 1).astype(bf16)
    kw = win[:, :, 0].transpose(0, 2, 1, 3).astype(bf16)
    vwt = win[:, :, 1].transpose(0, 2, 3, 1).astype(bf16)
    bh = lambda *shape: pl.BlockSpec((None, None) + shape, lambda b, h, i: (b, h) + (0,) * len(shape))
    bhi = lambda *shape: pl.BlockSpec((None, None, None) + shape, lambda b, h, i: (b, h, i) + (0,) * len(shape))
    oT = pl.pallas_call(
        functools.partial(_nsa_prompt_kernel, n_cmp=n_cmp),
        grid=(B, KVH, nqb),
        in_specs=[bhi(D, G * QB), bhi(D, G * QB), bhi(3, G * QB), bh(ncp, D), bh(ncp, D), bh(D, ncp),
                  bh(T, D), bh(D, T), bh(T, D), bh(D, T)],
        out_specs=bhi(D, G * QB),
        out_shape=jax.ShapeDtypeStruct((B, KVH, nqb, D, G * QB), f32),
        scratch_shapes=[pltpu.VMEM((n_slc, QB), f32), pltpu.VMEM((n_slc, G * QB), f32)],
        compiler_params=pltpu.CompilerParams(dimension_semantics=("parallel", "parallel", "arbitrary"),
                                             vmem_limit_bytes=48 * 1024 * 1024),
        name="nsa_prompt",
    )(qT, qTl, gT, kcp, kcl, vct, ks, vst, kw, vwt)
    return oT.reshape(B, KVH, nqb, D, G, QB).transpose(0, 2, 5, 1, 4, 3).reshape(B, T, KVH * G * D)


def _split3(x):
    h1 = x.astype(bf16)
    r1 = x - h1.astype(f32)
    h2 = r1.astype(bf16)
    return h1, h2, (r1 - h2.astype(f32)).astype(bf16)


def _dot_exact_rhs(x, m):
    d = lambda a: jnp.dot(a, m, preferred_element_type=f32)
    h1, h2, h3 = _split3(x)
    return d(h1) + (d(h2) + d(h3))


def _dot_exact_lhs(m, x):
    d = lambda a: jnp.dot(m, a, preferred_element_type=f32)
    h1, h2, h3 = _split3(x)
    return d(h1) + (d(h2) + d(h3))


def _nsa_sample_kernel(pt_ref, *refs, n_pages, ts, past_len):
    pages = refs[:n_pages]
    (new_ref, wbuf_ref, wnew_ref, qh_ref, ql_ref, gate_ref, wa_ref, wb_ref, phik_ref, phiv_ref,
     gain_ref, cos_ref, sin_ref, o_ref, ssel_scr, a_scr, b_scr, score_scr) = refs[n_pages:]
    P = PAGE_SIZE
    KV = NSA_KV_HEADS * HEAD_DIM
    NCOL = NSA_KV_HEADS * NSA_GQ * ts
    NQ = NSA_KV_HEADS * ts
    cpp = P // CMP_STRIDE
    n_chunk = (past_len + SLC_LEN) // CMP_STRIDE
    n_cmp = n_chunk - 1
    ncp = a_scr.shape[0]
    n_slc = (past_len + SLC_LEN) // SLC_LEN
    nsp = score_scr.shape[0]
    dd = lambda x, y: jnp.dot(x, y, preferred_element_type=f32)
    qh = qh_ref[...]
    ql = ql_ref[...]
    col = lax.broadcasted_iota(jnp.int32, (1, NCOL), 1)
    t_col = col % ts
    zpad = jnp.zeros((P - ts, 4 * KV), f32)
    new_tile = jnp.concatenate([new_ref[...], zpad], axis=0)

    wa = wa_ref[...]
    wb = wb_ref[...]
    a_scr[...] = jnp.zeros(a_scr.shape, f32)
    b_scr[...] = jnp.zeros(b_scr.shape, f32)
    for j in range(n_pages + 1):
        tile = pages[j][...] if j < n_pages else new_tile
        xc = tile[:, 0:2 * KV]
        a_scr[j * cpp:(j + 1) * cpp, :] = (xc * wa).reshape(cpp, CMP_STRIDE, 2 * KV).sum(axis=1)
        b_scr[j * cpp:(j + 1) * cpp, :] = (xc * wb).reshape(cpp, CMP_STRIDE, 2 * KV).sum(axis=1)
        ssel_scr[j * P:(j + 1) * P, :] = dd(tile[:, 2 * KV:3 * KV].astype(bf16), qh)

    mean = a_scr[...] + pltpu.roll(b_scr[...], ncp - 1, axis=0)
    kc = _mmul(mean[:, 0:KV], phik_ref[...], 3)
    vc = _mmul(mean[:, KV:2 * KV], phiv_ref[...], 3)
    r_i = lax.broadcasted_iota(jnp.int32, (KV, KV), 0)
    c_i = lax.broadcasted_iota(jnp.int32, (KV, KV), 1)
    same_head = (r_i // HEAD_DIM) == (c_i // HEAD_DIM)
    mavg = jnp.where(same_head, 1.0 / HEAD_DIM, 0.0).astype(bf16)
    kc = kc * lax.rsqrt(_dot_exact_rhs(kc * kc, mavg) + RMS_EPS) * gain_ref[...]
    half = ROPE_DIM // 2
    rd, cd = r_i % HEAD_DIM, c_i % HEAD_DIM
    rot = jnp.where(same_head & (cd < half) & (rd == cd + half), -1.0,
                    jnp.where(same_head & (cd >= half) & (cd < ROPE_DIM) & (rd == cd - half), 1.0, 0.0)).astype(bf16)
    kc = kc * cos_ref[...] + _dot_exact_rhs(kc, rot) * sin_ref[...]

    kch, kcl = _split(kc)
    sc = dd(kch, qh) + (dd(kch, ql) + dd(kcl, qh))
    cidx = lax.broadcasted_iota(jnp.int32, (ncp, 1), 0)
    mask_c = (cidx * CMP_STRIDE + (CMP_LEN - 1) <= past_len + t_col) & (cidx < n_cmp)
    sc = jnp.where(mask_c, sc, NEG_INF)
    pe = jnp.exp(sc - jnp.max(sc, axis=0, keepdims=True))
    pc = jnp.where(mask_c, pe / jnp.sum(pe, axis=0, keepdims=True), 0.0)
    o_c = dd(pc.T.astype(bf16), vc.astype(bf16))

    gr = lax.broadcasted_iota(jnp.int32, (NCOL, NQ), 0)
    gc = lax.broadcasted_iota(jnp.int32, (NCOL, NQ), 1)
    gsum = ((gr // (NSA_GQ * ts) == gc // ts) & (gr % ts == gc % ts)).astype(bf16)
    imp = _dot_exact_rhs(pc, gsum)
    per = SLC_LEN // CMP_STRIDE
    nn = lax.broadcasted_iota(jnp.int32, (nsp, ncp), 0) * per
    cc = lax.broadcasted_iota(jnp.int32, (nsp, ncp), 1)
    mt = (0.5 * ((cc >= nn) & (cc < nn + per)).astype(f32)
          + 0.5 * ((cc + 1 >= nn) & (cc + 1 < nn + per)).astype(f32)).astype(bf16)
    imp_blk = _dot_exact_lhs(mt, imp)
    nidx = lax.broadcasted_iota(jnp.int32, (nsp, 1), 0)
    cur = (past_len + lax.broadcasted_iota(jnp.int32, (1, NQ), 1) % ts) // SLC_LEN
    forced = (nidx == 0) | (nidx == cur) | (nidx == cur - 1)
    score = jnp.where(nidx >= n_slc, -2.0, jnp.where(nidx > cur, -1.0, jnp.where(forced, 1e6, imp_blk)))
    score_scr[...] = score
    rank = jnp.zeros((nsp, NQ), jnp.int32)
    for m in range(n_slc):
        sm = score_scr[m:m + 1, :]
        beats = (sm > score) | ((sm == score) & (nidx > m))
        rank = rank + beats.astype(jnp.int32)
    sel = (rank < min(N_SEL, n_slc)).astype(bf16)
    gr2 = lax.broadcasted_iota(jnp.int32, (NQ, NCOL), 0)
    gc2 = lax.broadcasted_iota(jnp.int32, (NQ, NCOL), 1)
    gexp = ((gc2 // (NSA_GQ * ts) == gr2 // ts) & (gc2 % ts == gr2 % ts)).astype(bf16)
    sel_c = dd(sel, gexp)

    def two_pass(n_tiles, score_tile, mask_tile, v_tile):
        m = jnp.full((1, NCOL), NEG_INF, f32)
        for j in range(n_tiles):
            m = jnp.maximum(m, jnp.max(jnp.where(mask_tile(j), score_tile(j), NEG_INF), axis=0, keepdims=True))
        num = jnp.zeros((NCOL, KV), f32)
        den = jnp.zeros((NCOL, KV), f32)
        ones = jnp.ones((P, KV), bf16)
        for j in range(n_tiles):
            p = jnp.where(mask_tile(j), jnp.exp(score_tile(j) - m), 0.0).T.astype(bf16)
            num = num + dd(p, v_tile(j))
            den = den + dd(p, ones)
        return num, den

    bpp = P // SLC_LEN
    row = lax.broadcasted_iota(jnp.int32, (P, 1), 0)

    def sel_mask(j):
        blk = jnp.concatenate([jnp.broadcast_to(sel_c[j * bpp + i:j * bpp + i + 1, :], (SLC_LEN, NCOL))
                               for i in range(bpp)], axis=0)
        return (blk > 0.5) & (j * P + row <= past_len + t_col)

    num_s, den_s = two_pass(
        n_pages + 1, lambda j: ssel_scr[j * P:(j + 1) * P, :], sel_mask,
        lambda j: (pages[j][:, 3 * KV:4 * KV] if j < n_pages else new_tile[:, 3 * KV:4 * KV]).astype(bf16))

    lb = wbuf_ref.shape[0]
    nwt = lb // P
    wnew = jnp.concatenate([wnew_ref[...], jnp.zeros((P - ts, 2 * KV), f32)], axis=0)

    def w_tile(j):
        return wbuf_ref[j * P:(j + 1) * P, :] if j < nwt else wnew

    def win_mask(j):
        pos_w = (past_len - lb + j * P + row) if j < nwt else (past_len + row)
        dq = past_len + t_col - pos_w
        return (dq >= 0) & (dq < WINDOW) & (pos_w >= 0) & ((row < ts) | (j < nwt))

    num_w, den_w = two_pass(
        nwt + 1, lambda j: dd(w_tile(j)[:, 0:KV].astype(bf16), qh), win_mask,
        lambda j: w_tile(j)[:, KV:2 * KV].astype(bf16))

    g = jax.nn.sigmoid(gate_ref[...])
    o_ref[...] = g[0] * o_c + g[1] * (num_s / den_s) + g[2] * (num_w / den_w)


def nsa_sample_pallas(q, gl, rows_new, win_new, pool_kv, page_table, win_buf, kc_w, phi, k_gain):
    B, ts = q.shape[:2]
    KVH, G, D, P = NSA_KV_HEADS, NSA_GQ, HEAD_DIM, PAGE_SIZE
    KV = KVH * D
    n_pages = page_table.shape[1]
    past_len = n_pages * P
    assert pool_kv.shape[1] == P and ts <= SLC_LEN and P % SLC_LEN == 0
    lb = win_buf.shape[1]
    assert lb % P == 0
    NCOL = KVH * G * ts
    n_chunk = (past_len + SLC_LEN) // CMP_STRIDE
    ncp = -(-n_chunk // 8) * 8
    n_slc = (past_len + SLC_LEN) // SLC_LEN
    nsp = -(-n_slc // 8) * 8
    qs = (q * D ** -0.5).reshape(B, ts, KVH, G, D).transpose(0, 2, 4, 3, 1).reshape(B, KVH, D, G * ts)
    z = jnp.zeros_like(qs[:, 0])
    qbd = jnp.concatenate([jnp.concatenate([qs[:, 0], z], axis=2), jnp.concatenate([z, qs[:, 1]], axis=2)], axis=1)
    qh, ql = _split(qbd)
    gate = gl.reshape(B, ts, KVH, G, 3).transpose(0, 4, 2, 3, 1).reshape(B, 3, NCOL, 1)
    gate = jnp.broadcast_to(gate, (B, 3, NCOL, KV)).astype(f32)
    w_lane = jnp.repeat(kc_w.reshape(2 * KVH, CMP_LEN), D, axis=0)
    reps = P // CMP_STRIDE
    wa = jnp.tile(w_lane[:, :CMP_STRIDE].T, (reps, 1)).astype(f32)
    wb = jnp.tile(w_lane[:, CMP_STRIDE:].T, (reps, 1)).astype(f32)
    zz = jnp.zeros((D, D), f32)
    bdiag = lambda m: jnp.concatenate([jnp.concatenate([m, zz], axis=1), jnp.concatenate([zz, m], axis=1)], axis=0)
    phik, phiv = bdiag(phi[0].astype(f32)), bdiag(phi[1].astype(f32))
    gain = jnp.tile(k_gain.astype(f32), KVH).reshape(1, KV)
    half = ROPE_DIM // 2
    inv = ROPE_THETA ** (-jnp.arange(half, dtype=f32) / half)
    cmp_end = (jnp.arange(ncp, dtype=jnp.int32) * CMP_STRIDE + (CMP_LEN - 1)).astype(f32)
    ang = cmp_end[:, None] * inv
    cos_h = jnp.concatenate([jnp.cos(ang), jnp.cos(ang), jnp.ones((ncp, D - ROPE_DIM), f32)], axis=1)
    sin_h = jnp.concatenate([jnp.sin(ang), jnp.sin(ang), jnp.zeros((ncp, D - ROPE_DIM), f32)], axis=1)
    cos_t, sin_t = jnp.tile(cos_h, (1, KVH)), jnp.tile(sin_h, (1, KVH))
    pool2 = pool_kv.reshape(pool_kv.shape[0], P, 4 * KV)
    new2 = rows_new.reshape(B, ts, 4 * KV).astype(f32)
    wbuf2 = win_buf.reshape(B, lb, 2 * KV)
    wnew2 = win_new.reshape(B, ts, 2 * KV).astype(f32)
    page_spec = lambda j: pl.BlockSpec((None, P, 4 * KV), lambda b, pt, j=j: (pt[b, j], 0, 0))
    per_b = lambda *s: pl.BlockSpec((None,) + s, lambda b, pt: (b,) + (0,) * len(s))
    const = lambda *s: pl.BlockSpec(s, lambda b, pt: (0,) * len(s))
    grid_spec = pltpu.PrefetchScalarGridSpec(
        num_scalar_prefetch=1, grid=(B,),
        in_specs=[page_spec(j) for j in range(n_pages)] + [
            per_b(ts, 4 * KV), per_b(lb, 2 * KV), per_b(ts, 2 * KV), per_b(KV, NCOL), per_b(KV, NCOL),
            per_b(3, NCOL, KV), const(P, 2 * KV), const(P, 2 * KV), const(KV, KV), const(KV, KV),
            const(1, KV), const(ncp, KV), const(ncp, KV)],
        out_specs=per_b(NCOL, KV),
        scratch_shapes=[pltpu.VMEM(((n_pages + 1) * P, NCOL), f32), pltpu.VMEM((ncp, 2 * KV), f32),
                        pltpu.VMEM((ncp, 2 * KV), f32), pltpu.VMEM((nsp, KVH * ts), f32)])
    out = pl.pallas_call(
        functools.partial(_nsa_sample_kernel, n_pages=n_pages, ts=ts, past_len=past_len),
        grid_spec=grid_spec,
        out_shape=jax.ShapeDtypeStruct((B, NCOL, KV), f32),
        compiler_params=pltpu.CompilerParams(dimension_semantics=("arbitrary",),
                                             vmem_limit_bytes=48 * 1024 * 1024),
        name="nsa_sample",
    )(page_table, *([pool2] * n_pages), new2, wbuf2, wnew2, qh, ql, gate, wa, wb, phik, phiv, gain, cos_t, sin_t)
    o4 = out.reshape(B, KVH, G, ts, KVH, D)
    o = jnp.stack([o4[:, 0, :, :, 0], o4[:, 1, :, :, 1]], axis=1)
    return o.transpose(0, 3, 1, 2, 4).reshape(B, ts, KVH * G * D)


MOE_TM = 512
ROUTER_LANES = 128


def _moe_kernel(x_ref, g_ref, wrh_ref, wrl_ref, br_ref, wg_ref, wu_ref, wd_ref, o_ref,
                h_scr, gate_scr, acc_scr):
    grp = pl.program_id(1)
    dd = lambda a, b: jnp.dot(a, b, preferred_element_type=f32)
    tm = x_ref.shape[0]
    lane = lax.broadcasted_iota(jnp.int32, (tm, ROUTER_LANES), 1).astype(f32)
    far = float(ROUTER_LANES)

    @pl.when(grp == 0)
    def _():
        x = x_ref[...]
        h = x * lax.rsqrt(jnp.mean(x * x, axis=-1, keepdims=True) + RMS_EPS) * g_ref[...]
        hh, hl = _split(h)
        h_scr[...] = hh
        logits = dd(hh, wrh_ref[...]) + (dd(hh, wrl_ref[...]) + dd(hl, wrh_ref[...])) + br_ref[...]
        is_c = lane < MOE_GROUPS
        lc = jnp.where(is_c, logits, NEG_INF)
        mc = jnp.max(lc, axis=1, keepdims=True)
        g_idx = jnp.min(jnp.where(lc == mc, lane, far), axis=1, keepdims=True)
        g_w = 1.0 / jnp.sum(jnp.where(is_c, jnp.exp(lc - mc), 0.0), axis=1, keepdims=True)
        lo = MOE_GROUPS + MOE_EPG * g_idx
        lf = jnp.where((lane >= lo) & (lane < lo + MOE_EPG), logits, NEG_INF)
        v1 = jnp.max(lf, axis=1, keepdims=True)
        i1 = jnp.min(jnp.where(lf == v1, lane, far), axis=1, keepdims=True)
        lf2 = jnp.where(lane == i1, NEG_INF, lf)
        v2 = jnp.max(lf2, axis=1, keepdims=True)
        i2 = jnp.min(jnp.where(lf2 == v2, lane, far), axis=1, keepdims=True)
        e21 = jnp.exp(v2 - v1)
        w1 = g_w / (1.0 + e21)
        gate_scr[...] = jnp.where(lane == i1, w1, jnp.where(lane == i2, e21 * w1, 0.0))
        acc_scr[...] = x

    h = h_scr[...]
    hg = dd(h, wg_ref[...])
    hu = dd(h, wu_ref[...])
    gate = gate_scr[...]
    first = (MOE_GROUPS + MOE_EPG * grp).astype(f32)
    cols = []
    for e in range(MOE_EPG):
        ge = jnp.sum(jnp.where(lane == first + e, gate, 0.0), axis=1, keepdims=True)
        sl = slice(e * D_FF_E, (e + 1) * D_FF_E)
        hge = hg[:, sl]
        cols.append((hge * jax.nn.sigmoid(hge) * hu[:, sl] * ge).astype(bf16))
    acc_scr[...] += dd(jnp.concatenate(cols, axis=1), wd_ref[...])

    @pl.when(grp == MOE_GROUPS - 1)
    def _():
        o_ref[...] = acc_scr[...]


def moe_prep(g, wc, bc, wf, bf, wg, wu, wd):
    d = wc.shape[0]
    pad = ROUTER_LANES - MOE_GROUPS - N_EXPERTS
    wr = jnp.pad(jnp.concatenate([wc, wf], axis=1).astype(f32), ((0, 0), (0, pad)))
    wrh, wrl = _split(wr)
    br = jnp.pad(jnp.concatenate([bc, bf]).astype(f32), (0, pad)).reshape(1, ROUTER_LANES)
    regroup = lambda w: (w.reshape(MOE_GROUPS, MOE_EPG, d, D_FF_E).transpose(0, 2, 1, 3)
                         .reshape(MOE_GROUPS, d, MOE_EPG * D_FF_E).astype(bf16))
    wdg = wd.reshape(MOE_GROUPS, MOE_EPG * D_FF_E, d).astype(bf16)
    return g.reshape(1, d).astype(f32), wrh, wrl, br, regroup(wg), regroup(wu), wdg


def moe_residual(x, prep):
    g, wrh, wrl, br, wgg, wug, wdg = prep
    shp = x.shape
    d = shp[-1]
    x2 = x.reshape(-1, d)
    m = x2.shape[0]
    tm = MOE_TM
    assert m % tm == 0
    gf = MOE_EPG * D_FF_E
    full = lambda r, c: pl.BlockSpec((r, c), lambda i, j: (0, 0))
    out = pl.pallas_call(
        _moe_kernel,
        grid=(m // tm, MOE_GROUPS),
        in_specs=[pl.BlockSpec((tm, d), lambda i, j: (i, 0)), full(1, d),
                  full(d, ROUTER_LANES), full(d, ROUTER_LANES), full(1, ROUTER_LANES),
                  pl.BlockSpec((None, d, gf), lambda i, j: (j, 0, 0)),
                  pl.BlockSpec((None, d, gf), lambda i, j: (j, 0, 0)),
                  pl.BlockSpec((None, gf, d), lambda i, j: (j, 0, 0))],
        out_specs=pl.BlockSpec((tm, d), lambda i, j: (i, 0)),
        out_shape=jax.ShapeDtypeStruct((m, d), f32),
        scratch_shapes=[pltpu.VMEM((tm, d), bf16), pltpu.VMEM((tm, ROUTER_LANES), f32),
                        pltpu.VMEM((tm, d), f32)],
        compiler_params=pltpu.CompilerParams(dimension_semantics=("parallel", "arbitrary"),
                                             vmem_limit_bytes=48 * 1024 * 1024),
        name="moe",
    )(x2, g, wrh, wrl, br, wgg, wug, wdg)
    return out.reshape(shp)


def rmsnorm(x, g):
    xf = x.astype(jnp.float32)
    y = xf * lax.rsqrt(jnp.mean(xf * xf, axis=-1, keepdims=True) + RMS_EPS)
    return (y * g.astype(jnp.float32)).astype(x.dtype)


def rope_partial(x, pos):
    half = ROPE_DIM // 2
    inv = ROPE_THETA ** (-jnp.arange(half, dtype=jnp.float32) / half)
    ang = pos.astype(jnp.float32)[:, None] * inv
    cos = jnp.cos(ang)[:, None, :]
    sin = jnp.sin(ang)[:, None, :]
    xf = x.astype(jnp.float32)
    x1 = xf[..., :half]
    x2 = xf[..., half:ROPE_DIM]
    out = jnp.concatenate([x1 * cos - x2 * sin, x2 * cos + x1 * sin, xf[..., ROPE_DIM:]], axis=-1)
    return out.astype(x.dtype)


def masked_softmax(s, mask):
    s = jnp.where(mask, s.astype(jnp.float32), NEG_INF)
    p = jax.nn.softmax(s, axis=-1)
    return jnp.where(mask, p, 0.0)


def pool_mix(u, hist, p0, w_grp, scale):
    b, t, _ = u.shape
    ext = jnp.concatenate([hist.astype(u.dtype), u], axis=1).astype(jnp.float32)
    cs = jnp.pad(jnp.cumsum(ext, axis=1), ((0, 0), (1, 0), (0, 0)))
    cnt_pos = p0 + jnp.arange(t, dtype=jnp.int32) + 1
    means = []
    for gi, w in enumerate(POOL_WINDOWS):
        c = cs[..., gi * POOL_GDIM:(gi + 1) * POOL_GDIM]
        win_sum = c[:, POOL_HIST + 1:POOL_HIST + 1 + t] - c[:, POOL_HIST + 1 - w:POOL_HIST + 1 - w + t]
        cnt = jnp.minimum(cnt_pos, w).astype(jnp.float32)[None, :, None]
        means.append(win_sum / cnt)
    mean = jnp.stack(means, axis=2)
    d = mean - u.reshape(b, t, POOL_GROUPS, POOL_GDIM).astype(jnp.float32)
    y = jnp.einsum('btgc,gcd->btgd', d, w_grp.astype(jnp.float32)).reshape(b, t, POOL_DIM)
    return (y * scale.astype(jnp.float32)).astype(u.dtype)


def ab_features(h, pos, w_in, q_norm, k_norm):
    b, t = h.shape[:2]
    u = _mm(h, w_in)
    off_kv = POOL_DIM + NSA_DIM
    pool_in = u[..., :POOL_DIM]
    q = u[..., POOL_DIM:off_kv].reshape(b, t, NSA_HEADS, HEAD_DIM)
    kv = u[..., off_kv:off_kv + KV_COLS].reshape(b, t, 6, NSA_KV_HEADS, HEAD_DIM)
    gl = u[..., off_kv + KV_COLS:].reshape(b, t, NSA_HEADS, 3)
    q = rope_partial(rmsnorm(q, q_norm), pos)
    k_slc = rope_partial(rmsnorm(kv[:, :, 2], k_norm[1]), pos)
    k_win = rope_partial(rmsnorm(kv[:, :, 4], k_norm[2]), pos)
    rows = jnp.stack([kv[:, :, 0], kv[:, :, 1], k_slc, kv[:, :, 3]], axis=2)
    win = jnp.stack([k_win, kv[:, :, 5]], axis=2)
    return pool_in, q, gl, rows, win


def compress_kv(k_rows, v_rows, pos_w, phi, k_gain):
    b, length = k_rows.shape[:2]
    n_chunk = length // CMP_STRIDE

    def weighted_block_mean(rows, w):
        ch = rows.reshape(b, n_chunk, CMP_STRIDE, NSA_KV_HEADS, HEAD_DIM)
        return (jnp.einsum('bnlhd,hl->bnhd', ch[:, :-1], w[:, :CMP_STRIDE])
                + jnp.einsum('bnlhd,hl->bnhd', ch[:, 1:], w[:, CMP_STRIDE:]))

    cmp_end = jnp.arange(n_chunk - 1, dtype=jnp.int32) * CMP_STRIDE + (CMP_LEN - 1)
    kc = jnp.einsum('bnhd,de->bnhe', weighted_block_mean(k_rows, pos_w[0]), phi[0])
    kc = rope_partial(rmsnorm(kc, k_gain), cmp_end)
    vc = jnp.einsum('bnhd,de->bnhe', weighted_block_mean(v_rows, pos_w[1]), phi[1])
    return kc, vc, cmp_end


def nsa_attend(q, pos_q, gl, kc, vc, cmp_end, ks, vs, kw, vw, pos_w):
    f32 = jnp.float32
    b, tq = q.shape[:2]
    qg = q.reshape(b, tq, NSA_KV_HEADS, NSA_GQ, HEAD_DIM)
    scale = HEAD_DIM ** -0.5
    s_c = jnp.einsum('bqhgd,bchd->bhgqc', qg, kc) * scale
    p_c = masked_softmax(s_c, cmp_end[None, :] <= pos_q[:, None])
    o_c = jnp.einsum('bhgqc,bchd->bqhgd', p_c, vc.astype(f32))
    imp = p_c.sum(axis=2)
    imp_chunk = 0.5 * (jnp.pad(imp, ((0, 0), (0, 0), (0, 0), (0, 1)))
                       + jnp.pad(imp, ((0, 0), (0, 0), (0, 0), (1, 0))))
    n_slc = ks.shape[1] // SLC_LEN
    imp_blk = imp_chunk.reshape(b, NSA_KV_HEADS, tq, n_slc, SLC_LEN // CMP_STRIDE).sum(-1)
    blk = jnp.arange(n_slc, dtype=jnp.int32)[None, :]
    cur = (pos_q // SLC_LEN)[:, None]
    forced = (blk == 0) | (blk == cur) | (blk == cur - 1)
    score = jnp.where(blk > cur, -1.0, jnp.where(forced, 1e6, imp_blk))
    n_sel = min(N_SEL, n_slc)
    _, idx = lax.top_k(score, n_sel)
    gather = jax.vmap(jax.vmap(lambda rows, i: rows[i]))
    ksb = ks.reshape(b, n_slc, SLC_LEN, NSA_KV_HEADS, HEAD_DIM).transpose(0, 3, 1, 2, 4)
    vsb = vs.reshape(b, n_slc, SLC_LEN, NSA_KV_HEADS, HEAD_DIM).transpose(0, 3, 1, 2, 4)
    kg = gather(ksb, idx)
    vg = gather(vsb, idx)
    kpos = idx[..., None] * SLC_LEN + jnp.arange(SLC_LEN, dtype=jnp.int32)
    n_keys = n_sel * SLC_LEN
    m_s = (kpos <= pos_q[None, None, :, None, None]).reshape(b, NSA_KV_HEADS, 1, tq, n_keys)
    s_s = jnp.einsum('bqhgd,bhqnld->bhgqnl', qg, kg).reshape(b, NSA_KV_HEADS, NSA_GQ, tq, n_keys) * scale
    p_s = masked_softmax(s_s, m_s)
    o_s = jnp.einsum('bhgqk,bhqkd->bqhgd', p_s,
                     vg.reshape(b, NSA_KV_HEADS, tq, n_keys, HEAD_DIM).astype(f32))
    s_w = jnp.einsum('bqhgd,bkhd->bhgqk', qg, kw) * scale
    dq = pos_q[:, None] - pos_w[None, :]
    m_w = (dq >= 0) & (dq < WINDOW) & (pos_w[None, :] >= 0)
    p_w = masked_softmax(s_w, m_w)
    o_w = jnp.einsum('bhgqk,bkhd->bqhgd', p_w, vw.astype(f32))
    g = jax.nn.sigmoid(gl.astype(f32)).reshape(b, tq, NSA_KV_HEADS, NSA_GQ, 3)
    o = g[..., 0:1] * o_c + g[..., 1:2] * o_s + g[..., 2:3] * o_w
    return o.reshape(b, tq, NSA_DIM)


def nsa_prompt(q, gl, rows, win, cmp_pos_w, cmp_phi, k_gain):
    b, t = q.shape[:2]
    kc, vc, cmp_end = compress_kv(rows[:, :, 0], rows[:, :, 1], cmp_pos_w, cmp_phi, k_gain)
    ks, vs = rows[:, :, 2], rows[:, :, 3]
    win_pad = jnp.pad(win, ((0, 0), (WINDOW, 0), (0, 0), (0, 0), (0, 0)))

    def block(i):
        s0 = i * Q_BLOCK
        qb = lax.dynamic_slice_in_dim(q, s0, Q_BLOCK, axis=1)
        gb = lax.dynamic_slice_in_dim(gl, s0, Q_BLOCK, axis=1)
        wb = lax.dynamic_slice_in_dim(win_pad, s0, WINDOW + Q_BLOCK, axis=1)
        pos_q = s0 + jnp.arange(Q_BLOCK, dtype=jnp.int32)
        pos_w = s0 - WINDOW + jnp.arange(WINDOW + Q_BLOCK, dtype=jnp.int32)
        return nsa_attend(qb, pos_q, gb, kc, vc, cmp_end, ks, vs, wb[:, :, 0], wb[:, :, 1], pos_w)

    out = lax.map(block, jnp.arange(t // Q_BLOCK, dtype=jnp.int32))
    return jnp.moveaxis(out, 0, 1).reshape(b, t, NSA_DIM)


def nsa_sample(q, gl, rows_new, win_new, pool_kv, page_table, win_buf, cmp_pos_w, cmp_phi, k_gain):
    bd, ts = q.shape[:2]
    past_len = page_table.shape[1] * pool_kv.shape[1]
    past = pool_kv[page_table].reshape(bd, past_len, 4, NSA_KV_HEADS, HEAD_DIM)
    rows = jnp.concatenate([past, rows_new.astype(past.dtype)], axis=1)
    pad = (-(past_len + ts)) % SLC_LEN
    rows = jnp.pad(rows, ((0, 0), (0, pad), (0, 0), (0, 0), (0, 0)))
    kc, vc, cmp_end = compress_kv(rows[:, :, 0], rows[:, :, 1], cmp_pos_w, cmp_phi, k_gain)
    lb = win_buf.shape[1]
    win = jnp.concatenate([win_buf, win_new.astype(win_buf.dtype)], axis=1)
    pos_q = past_len + jnp.arange(ts, dtype=jnp.int32)
    pos_w = past_len - lb + jnp.arange(lb + ts, dtype=jnp.int32)
    o = nsa_attend(q, pos_q, gl, kc, vc, cmp_end, rows[:, :, 2], rows[:, :, 3],
                   win[:, :, 0], win[:, :, 1], pos_w)
    keep = min(WINDOW, lb + ts)
    return o, win[:, lb + ts - keep:]


def ab_layer_prompt(h, w_in, w_out, q_norm, k_norm, cmp_pos_w, cmp_phi, pool_w, pool_scale):
    b, t = h.shape[:2]
    pos = jnp.arange(t, dtype=jnp.int32)
    pool_in, q, gl, rows, win = ab_features(h, pos, w_in, q_norm, k_norm)
    hist0 = jnp.zeros((b, POOL_HIST, POOL_DIM), h.dtype)
    pool_out = pool_mix(pool_in, hist0, 0, pool_w, pool_scale)
    kc, vc, _ = compress_kv(rows[:, :, 0], rows[:, :, 1], cmp_pos_w, cmp_phi, k_norm[0])
    nsa_out = nsa_prompt_pallas(q, gl, rows, win, kc, vc)
    y = _mm(jnp.concatenate([pool_out, nsa_out.astype(pool_out.dtype)], axis=-1), w_out)
    keep = min(WINDOW, t)
    return y, rows, win[:, t - keep:], pool_in[:, t - POOL_HIST:]


def ab_layer_sample(h, pool_kv, page_table, win_buf, pool_hist, w_in, w_out, q_norm, k_norm,
                    cmp_pos_w, cmp_phi, pool_w, pool_scale):
    ts = h.shape[1]
    past_len = page_table.shape[1] * pool_kv.shape[1]
    pos = past_len + jnp.arange(ts, dtype=jnp.int32)
    pool_in, q, gl, rows, win = ab_features(h, pos, w_in, q_norm, k_norm)
    pool_out = pool_mix(pool_in, pool_hist, past_len, pool_w, pool_scale)
    nsa_out = nsa_sample_pallas(q, gl, rows, win, pool_kv, page_table, win_buf,
                                cmp_pos_w, cmp_phi, k_norm[0])
    lb = win_buf.shape[1]
    keep = min(WINDOW, lb + ts)
    new_win = jnp.concatenate([win_buf, win.astype(win_buf.dtype)], axis=1)[:, lb + ts - keep:]
    y = _mm(jnp.concatenate([pool_out, nsa_out.astype(pool_out.dtype)], axis=-1), w_out)
    new_hist = jnp.concatenate([pool_hist.astype(pool_in.dtype), pool_in], axis=1)[:, -POOL_HIST:]
    return y, rows, new_win, new_hist


def wkv_scan(s0, r, w, k, v, kk, a):
    def step(s, inp):
        r_t, w_t, k_t, v_t, kk_t, a_t = inp
        sa = jnp.einsum('bhij,bhj->bhi', s, -kk_t)
        s = (s * w_t[:, :, None, :] + sa[..., None] * (kk_t * a_t)[:, :, None, :]
             + v_t[..., None] * k_t[:, :, None, :])
        return s, jnp.einsum('bhij,bhj->bhi', s, r_t)

    xs = tuple(jnp.moveaxis(z.astype(jnp.float32), 1, 0) for z in (r, w, k, v, kk, a))
    s, o = lax.scan(step, s0.astype(jnp.float32), xs)
    return s, jnp.moveaxis(o, 0, 1)


def rwkv_layer(h, shift_prev, s0, v_first, vres, mu, wr, wk, wv, wo, w0, w1, w2, a0, a1, a2,
               g1, g2, k_k, k_a, r_k, gn_w, gn_b):
    f32 = jnp.float32
    b, t, d = h.shape
    prev = jnp.concatenate([shift_prev[:, None, :].astype(h.dtype), h[:, :-1]], axis=1)
    xx = prev - h
    xr, xw, xk, xv, xa, xg = [h + xx * mu[j] for j in range(6)]
    r = _mm(xr, wr)
    k = _mm(xk, wk)
    v = _mm(xv, wv)
    w_log = -jax.nn.softplus(-(w0 + jnp.tanh(xw @ w1) @ w2).astype(f32)) - 0.5
    decay = jnp.exp(-jnp.exp(w_log))
    if vres is None:
        v_first = v
    else:
        v0, v1, v2 = vres
        v = v + (v_first - v) * jax.nn.sigmoid(v0 + (xv @ v1) @ v2)
    a = jax.nn.sigmoid((a0 + (xa @ a1) @ a2).astype(f32))
    g = jax.nn.sigmoid(xg @ g1) @ g2

    def heads(z):
        return z.reshape(b, t, RWKV_HEADS, RWKV_N).astype(f32)

    kk = heads(k * k_k)
    kk = kk / jnp.maximum(jnp.sqrt(jnp.sum(kk * kk, axis=-1, keepdims=True)), 1e-12)
    k = k.astype(f32) * (1.0 + (a - 1.0) * k_a.astype(f32))
    rh, kh, vh, ah, dh = heads(r), heads(k), heads(v), heads(a), heads(decay)
    tp = -(-t // WKV_CHUNK) * WKV_CHUNK
    padt = lambda z: jnp.pad(z, ((0, 0), (0, tp - t), (0, 0)))
    o, s = wkv_chunked(padt(r), padt(-jnp.exp(w_log)), padt(k), padt(v), padt(kk.reshape(b, t, d)),
                       padt(a), s0.astype(f32))
    o = o[:, :t].reshape(b, t, RWKV_HEADS, RWKV_N)
    mean = jnp.mean(o, axis=-1, keepdims=True)
    var = jnp.mean(jnp.square(o - mean), axis=-1, keepdims=True)
    o = ((o - mean) * lax.rsqrt(var + GN_EPS) * gn_w.reshape(RWKV_HEADS, RWKV_N).astype(f32)
         + gn_b.reshape(RWKV_HEADS, RWKV_N).astype(f32))
    o = o + jnp.sum(rh * kh * r_k.astype(f32), axis=-1, keepdims=True) * vh
    y = _mm((o.reshape(b, t, d) * g.astype(f32)).astype(h.dtype), wo)
    return y, v_first, s, h[:, -1]


def hier_moe(h, wc, bc, wf, bf, wg, wu, wd):
    f32 = jnp.float32
    hp = lax.Precision.HIGHEST
    lc = jnp.dot(h, wc, precision=hp).astype(f32) + bc.astype(f32)
    g_idx = jnp.argmax(lc, axis=-1)
    g_w = jnp.max(jax.nn.softmax(lc, axis=-1), axis=-1)
    g_hot = jax.nn.one_hot(g_idx, MOE_GROUPS, dtype=f32)
    lf = (jnp.dot(h, wf, precision=hp).astype(f32) + bf.astype(f32)).reshape(h.shape[:-1] + (MOE_GROUPS, MOE_EPG))
    lf_sel = jnp.einsum('btge,btg->bte', lf, g_hot)
    top_v, top_i = lax.top_k(lf_sel, MOE_TOPK)
    top_w = jax.nn.softmax(top_v, axis=-1) * g_w[..., None]
    e_id = g_idx[..., None] * MOE_EPG + top_i
    gate = jnp.einsum('btke,btk->bte', jax.nn.one_hot(e_id, N_EXPERTS, dtype=f32), top_w)
    hg = jnp.einsum('btd,edf->btef', h, wg)
    hu = jnp.einsum('btd,edf->btef', h, wu)
    act = (jax.nn.silu(hg) * hu * gate[..., None].astype(h.dtype)).astype(h.dtype)
    return jnp.einsum('btef,efd->btd', act, wd)


def kernel(x_prompt, x_sample, cache_nsa_kv, cache_win_kv, state_pool, state_wkv, state_shift,
           page_table, norm_mix, norm_ffn, ab_w_in, ab_w_out, ab_q_norm, ab_k_norm, cmp_pos_w,
           cmp_phi, pool_w, pool_scale, rw_mu, rw_wr, rw_wk, rw_wv, rw_wo, rw_w0, rw_w1, rw_w2,
           rw_a0, rw_a1, rw_a2, rw_v0, rw_v1, rw_v2, rw_g1, rw_g2, rw_kk, rw_ka, rw_rk, rw_gn_w,
           rw_gn_b, moe_wc, moe_bc, moe_wf, moe_bf, moe_wg, moe_wu, moe_wd):
    xp, xs = x_prompt, x_sample
    vf_p, vf_s = None, None
    nsa_p, nsa_s, win_p, win_s, pool_p, pool_s = [], [], [], [], [], []
    wkv_p, wkv_s, sh_p, sh_s = [], [], [], []
    for l in range(DEPTH):
        if l % 2 == 0:
            i = l // 2
            wts = (ab_w_in[i], ab_w_out[i], ab_q_norm[i], ab_k_norm[i], cmp_pos_w[i], cmp_phi[i],
                   pool_w[i], pool_scale[i])
            yp, r_p, w_p, h_p = ab_layer_prompt(rmsnorm(xp, norm_mix[l]), *wts)
            ys, r_s, w_s, h_s = ab_layer_sample(rmsnorm(xs, norm_mix[l]), cache_nsa_kv[i], page_table,
                                                cache_win_kv[i], state_pool[i], *wts)
            nsa_p.append(r_p)
            nsa_s.append(r_s)
            win_p.append(w_p)
            win_s.append(w_s)
            pool_p.append(h_p)
            pool_s.append(h_s)
        else:
            j = l // 2
            vres = None if j == 0 else (rw_v0[j - 1], rw_v1[j - 1], rw_v2[j - 1])
            wts = (rw_mu[j], rw_wr[j], rw_wk[j], rw_wv[j], rw_wo[j], rw_w0[j], rw_w1[j], rw_w2[j],
                   rw_a0[j], rw_a1[j], rw_a2[j], rw_g1[j], rw_g2[j], rw_kk[j], rw_ka[j], rw_rk[j],
                   rw_gn_w[j], rw_gn_b[j])
            bp = xp.shape[0]
            zero_shift = jnp.zeros((bp, D_MODEL), xp.dtype)
            zero_state = jnp.zeros((bp, RWKV_HEADS, RWKV_N, RWKV_N), jnp.float32)
            yp, vf_p, s_p, shp = rwkv_layer(rmsnorm(xp, norm_mix[l]), zero_shift, zero_state, vf_p, vres, *wts)
            ys, vf_s, s_s, shs = rwkv_layer(rmsnorm(xs, norm_mix[l]), state_shift[j], state_wkv[j], vf_s, vres, *wts)
            wkv_p.append(s_p)
            wkv_s.append(s_s)
            sh_p.append(shp)
            sh_s.append(shs)
        xp = xp + yp.astype(xp.dtype)
        xs = xs + ys.astype(xs.dtype)
        prep = moe_prep(norm_ffn[l], moe_wc[l], moe_bc[l], moe_wf[l], moe_bf[l], moe_wg[l], moe_wu[l], moe_wd[l])
        xp = moe_residual(xp, prep)
        xs = moe_residual(xs, prep)
    return (xp, xs, jnp.stack(nsa_p), jnp.stack(nsa_s), jnp.stack(win_p), jnp.stack(win_s),
            jnp.stack(pool_p), jnp.stack(pool_s), jnp.stack(wkv_p), jnp.stack(wkv_s),
            jnp.stack(sh_p), jnp.stack(sh_s))
```

```python
import functools

import jax
import jax.numpy as jnp
from jax import lax
from jax.experimental import pallas as pl
from jax.experimental.pallas import tpu as pltpu


def _mm_kernel(x_ref, w_ref, o_ref):
    o_ref[...] = jnp.dot(x_ref[...].astype(jnp.bfloat16), w_ref[...],
                         preferred_element_type=jnp.float32)


def _mm(x, w):
    lead = x.shape[:-1]
    k = x.shape[-1]
    n = w.shape[1]
    x2 = x.reshape(-1, k)
    m = x2.shape[0]
    npad = -(-n // 128) * 128
    wb = w.astype(jnp.bfloat16)
    if npad != n:
        wb = jnp.pad(wb, ((0, 0), (0, npad - n)))
    tn = npad
    for cand in (512, 640, 384, 256, 128):
        if npad % cand == 0:
            tn = cand
            break
    tm = 512 if m % 512 == 0 else m
    out = pl.pallas_call(
        _mm_kernel,
        grid=(m // tm, npad // tn),
        in_specs=[pl.BlockSpec((tm, k), lambda i, j: (i, 0)),
                  pl.BlockSpec((k, tn), lambda i, j: (0, j))],
        out_specs=pl.BlockSpec((tm, tn), lambda i, j: (i, j)),
        out_shape=jax.ShapeDtypeStruct((m, npad), jnp.float32),
        name="mm",
    )(x2, wb)
    return out[:, :n].reshape(lead + (n,))


f32 = jnp.float32
bf16 = jnp.bfloat16
WKV_CHUNK = 64
WKV_PAIRS = 8
WKV_PASSES = 1
WKV_GRAM_PASSES = 3


def _split(x):
    hi = x.astype(bf16)
    lo = (x - hi.astype(f32)).astype(bf16)
    return hi, lo


def _mmul(a, b, passes, nt=False):
    dn = (((1,), (1,)), ((), ())) if nt else (((1,), (0,)), ((), ()))
    d = lambda x, y: lax.dot_general(x, y, dn, preferred_element_type=f32)
    if passes == 1:
        return d(a.astype(bf16), b.astype(bf16))
    ah, al = _split(a)
    bh, bl = _split(b)
    return d(ah, bh) + (d(ah, bl) + d(al, bh))


def _wkv_kernel(r_ref, lw_ref, k_ref, v_ref, kk_ref, a_ref, s0_ref, o_ref, sT_ref, st_scr, *, passes):
    C = WKV_CHUNK
    c = pl.program_id(1)
    nc = pl.num_programs(1)
    row = lax.broadcasted_iota(jnp.int32, (2 * C, 2 * C), 0)
    col = lax.broadcasted_iota(jnp.int32, (2 * C, 2 * C), 1)
    bd = (row < C) == (col < C)
    strict = bd & ((row % C) > (col % C))
    incl = bd & ((row % C) >= (col % C))
    eye = (row == col).astype(f32)
    lane_s = col < C
    m1 = lax.broadcasted_iota(jnp.int32, (C, 2 * C), 1) < C
    tri = (lax.broadcasted_iota(jnp.int32, (C, C), 0)
           >= lax.broadcasted_iota(jnp.int32, (C, C), 1)).astype(bf16)

    @pl.when(c == 0)
    def _():
        z = jnp.zeros((C, C), f32)
        for p in range(WKV_PAIRS):
            s1 = s0_ref[0, 2 * p]
            s2 = s0_ref[0, 2 * p + 1]
            st_scr[p] = jnp.concatenate([jnp.concatenate([s1, z], axis=1),
                                         jnp.concatenate([z, s2], axis=1)], axis=0)

    def stack2(x):
        return jnp.concatenate([jnp.where(m1, x, 0.0), jnp.where(m1, 0.0, x)], axis=0)

    dd = lambda x, y: jnp.dot(x, y, preferred_element_type=f32)
    pairs = range(WKV_PAIRS)
    sls = [slice(p * 2 * C, (p + 1) * 2 * C) for p in pairs]

    def prep(p):
        sl = sls[p]
        lw = lw_ref[0, :, sl]
        kk = kk_ref[0, :, sl]
        h1 = lw.astype(bf16)
        r1 = lw - h1.astype(f32)
        h2 = r1.astype(bf16)
        h3 = (r1 - h2.astype(f32)).astype(bf16)
        cw = dd(tri, h1) + (dd(tri, h2) + dd(tri, h3))
        cwC = cw[C - 1:C, :]
        b = kk * a_ref[0, :, sl]
        k = k_ref[0, :, sl]
        At = -kk * jnp.exp(cw - lw)
        Rt = r_ref[0, :, sl] * jnp.exp(cw)
        einv = jnp.exp(-cw)
        efut = jnp.exp(cwC - cw)
        X = jnp.concatenate([stack2(At), stack2(Rt)], axis=0)
        Y = jnp.concatenate([b * einv, k * einv], axis=0)
        AR = jnp.concatenate([At, Rt], axis=0)
        BK = jnp.concatenate([b * efut, k * efut], axis=0)
        return X, Y, AR, BK, jnp.exp(cwC)

    pre = [prep(p) for p in pairs]
    G = [_mmul(pre[p][0], pre[p][1], WKV_GRAM_PASSES, nt=True) for p in pairs]
    ARS = [_mmul(pre[p][2], st_scr[p], passes, nt=True) for p in pairs]
    L, Mak, Mrb, Mrk = [], [], [], []
    for p in pairs:
        GA = G[p][0:2 * C]
        GR = G[p][2 * C:4 * C]
        GAr = pltpu.roll(GA, C, axis=1)
        GRr = pltpu.roll(GR, C, axis=1)
        L.append(jnp.where(strict, jnp.where(lane_s, GA, GAr), 0.0))
        Mak.append(jnp.where(strict, jnp.where(lane_s, GAr, GA), 0.0))
        Mrb.append(jnp.where(incl, jnp.where(lane_s, GR, GRr), 0.0))
        Mrk.append(jnp.where(incl, jnp.where(lane_s, GRr, GR), 0.0))
    Vs = [stack2(v_ref[0, :, sls[p]]) for p in pairs]
    Xs = [stack2(ARS[p][0:C]) + _mmul(Mak[p], Vs[p], passes) for p in pairs]
    OV = [_mmul(Mrk[p], Vs[p], passes) for p in pairs]
    P = [eye + L[p] for p in pairs]
    Q = L
    for _ in range(5):
        Q = [_mmul(Q[p], Q[p], passes) for p in pairs]
        P = [P[p] + _mmul(Q[p], P[p], passes) for p in pairs]
    Us = [_mmul(P[p], Xs[p], passes) for p in pairs]
    Os = [_mmul(Mrb[p], Us[p], passes) + OV[p] for p in pairs]
    for p in pairs:
        o_ref[0, :, sls[p]] = ARS[p][C:2 * C] + Os[p][0:C] + Os[p][C:2 * C]
    for p in pairs:
        U = Us[p][0:C] + Us[p][C:2 * C]
        UV = jnp.concatenate([U, v_ref[0, :, sls[p]]], axis=0)
        dS = _mmul(UV.T, pre[p][3], passes)
        Snew = st_scr[p] * pre[p][4] + jnp.where(bd, dS, 0.0)
        st_scr[p] = Snew

        @pl.when(c == nc - 1)
        def _():
            sT_ref[0, 2 * p] = Snew[0:C, 0:C]
            sT_ref[0, 2 * p + 1] = Snew[C:2 * C, C:2 * C]


def wkv_chunked(r, lw, k, v, kk, a, s0):
    B, T, D = r.shape
    H = D // 64
    C = WKV_CHUNK
    assert T % C == 0 and D == WKV_PAIRS * 2 * C
    blk = pl.BlockSpec((1, C, D), lambda b, c: (b, c, 0))
    sblk = pl.BlockSpec((1, H, 64, 64), lambda b, c: (b, 0, 0, 0))
    return pl.pallas_call(
        functools.partial(_wkv_kernel, passes=WKV_PASSES),
        grid=(B, T // C),
        in_specs=[blk] * 6 + [sblk],
        out_specs=[blk, sblk],
        out_shape=[jax.ShapeDtypeStruct((B, T, D), f32), jax.ShapeDtypeStruct((B, H, 64, 64), f32)],
        scratch_shapes=[pltpu.VMEM((WKV_PAIRS, 2 * C, 2 * C), f32)],
        compiler_params=pltpu.CompilerParams(dimension_semantics=("parallel", "arbitrary")),
        name="wkv7_chunked",
    )(r, lw, k, v, kk, a, s0)


D_MODEL = 1024
BATCH = 4
SEQ = 4096
DEPTH = 4
DEC_BATCH = 128
DEC_SEQ = 8
PAST_LEN = 2048
PAGE_SIZE = 128

N_NSA_LAYERS = (DEPTH + 1) // 2
N_RWKV_LAYERS = DEPTH // 2
N_VRES = N_RWKV_LAYERS - 1

POOL_DIM = D_MODEL // 2
POOL_WINDOWS = (2, 4, 8, 16)
POOL_GROUPS = len(POOL_WINDOWS)
POOL_GDIM = POOL_DIM // POOL_GROUPS
POOL_HIST = max(POOL_WINDOWS) - 1

HEAD_DIM = 64
NSA_HEADS = (D_MODEL // 2) // HEAD_DIM
NSA_KV_HEADS = 2
NSA_GQ = NSA_HEADS // NSA_KV_HEADS
NSA_DIM = NSA_HEADS * HEAD_DIM
CMP_STRIDE = 16
CMP_LEN = 2 * CMP_STRIDE
SLC_LEN = 64
N_SEL = 16
WINDOW = 512
Q_BLOCK = 128
ROPE_DIM = HEAD_DIM // 4
ROPE_THETA = 500000.0
MIX_DIM = POOL_DIM + NSA_DIM
KV_COLS = 6 * NSA_KV_HEADS * HEAD_DIM
IN_COLS = POOL_DIM + NSA_DIM + KV_COLS + 3 * NSA_HEADS

RWKV_N = 64
RWKV_HEADS = D_MODEL // RWKV_N
LORA_W = 64
LORA_A = 64
LORA_V = 32
LORA_G = 128
GN_EPS = 64e-5

MOE_GROUPS = 4
MOE_EPG = 4
N_EXPERTS = MOE_GROUPS * MOE_EPG
MOE_TOPK = 2
D_FF_E = 256

RMS_EPS = 1e-6
NEG_INF = -1e30
RES_SCALE = (2 * DEPTH) ** -0.5

SEL_TK = 512
WIN_TK = 128


def _nsa_prompt_kernel(q_ref, ql_ref, g_ref, kc_ref, kcl_ref, vct_ref, ks_ref, vst_ref, kw_ref, vwt_ref,
                       o_ref, score_scr, sel_scr, *, n_cmp):
    QB = Q_BLOCK
    GQ = NSA_GQ * QB
    i = pl.program_id(2)
    s0 = i * QB
    qT = q_ref[...]
    posq = s0 + lax.broadcasted_iota(jnp.int32, (1, GQ), 1) % QB
    dd = lambda x, y: jnp.dot(x, y, preferred_element_type=f32)

    ncp = kc_ref.shape[0]
    cidx = lax.broadcasted_iota(jnp.int32, (ncp, 1), 0)
    mask_c = (cidx * CMP_STRIDE + (CMP_LEN - 1) <= posq) & (cidx < n_cmp)
    sc = dd(kc_ref[...], qT) + (dd(kc_ref[...], ql_ref[...]) + dd(kcl_ref[...], qT))
    sc = jnp.where(mask_c, sc, NEG_INF)
    pe = jnp.exp(sc - jnp.max(sc, axis=0, keepdims=True))
    pc = jnp.where(mask_c, pe / jnp.sum(pe, axis=0, keepdims=True), 0.0)
    o_c = dd(vct_ref[...], pc.astype(bf16))

    imp = (pc[:, 0:QB] + pc[:, QB:2 * QB]) + (pc[:, 2 * QB:3 * QB] + pc[:, 3 * QB:4 * QB])
    n_slc = score_scr.shape[0]
    per = SLC_LEN // CMP_STRIDE
    nn = lax.broadcasted_iota(jnp.int32, (n_slc, ncp), 0) * per
    cc = lax.broadcasted_iota(jnp.int32, (n_slc, ncp), 1)
    mt = (0.5 * ((cc >= nn) & (cc < nn + per)).astype(f32)
          + 0.5 * ((cc + 1 >= nn) & (cc + 1 < nn + per)).astype(f32)).astype(bf16)
    i1 = imp.astype(bf16)
    r1 = imp - i1.astype(f32)
    i2 = r1.astype(bf16)
    i3 = (r1 - i2.astype(f32)).astype(bf16)
    imp_blk = dd(mt, i1) + (dd(mt, i2) + dd(mt, i3))
    nidx = lax.broadcasted_iota(jnp.int32, (n_slc, 1), 0)
    cur = (s0 + lax.broadcasted_iota(jnp.int32, (1, QB), 1)) // SLC_LEN
    forced = (nidx == 0) | (nidx == cur) | (nidx == cur - 1)
    score = jnp.where(nidx > cur, -1.0, jnp.where(forced, 1e6, imp_blk))
    score_scr[...] = score
    rank = jnp.zeros((n_slc, QB), jnp.int32)
    for m in range(n_slc):
        sm = score_scr[m:m + 1, :]
        beats = (sm > score) | ((sm == score) & (nidx > m))
        rank = rank + beats.astype(jnp.int32)
    sel = (rank < min(N_SEL, n_slc)).astype(f32)
    sel_scr[...] = jnp.concatenate([sel] * NSA_GQ, axis=1)

    def attend(k_ref, vt_ref, t_lo, t_hi, tk, mask_fn):
        def body(kt, carry):
            m, l, acc = carry
            k0 = pl.multiple_of(kt * tk, tk)
            s = dd(k_ref[pl.ds(k0, tk), :], qT)
            kpos = k0 + lax.broadcasted_iota(jnp.int32, (tk, 1), 0)
            mask = mask_fn(kt, kpos)
            s = jnp.where(mask, s, NEG_INF)
            m_new = jnp.maximum(m, jnp.max(s, axis=0, keepdims=True))
            alpha = jnp.exp(m - m_new)
            p = jnp.where(mask, jnp.exp(s - m_new), 0.0)
            l = alpha * l + jnp.sum(p, axis=0, keepdims=True)
            acc = alpha * acc + dd(vt_ref[:, pl.ds(k0, tk)], p.astype(bf16))
            return m_new, l, acc
        init = (jnp.full((1, GQ), NEG_INF, f32), jnp.zeros((1, GQ), f32), jnp.zeros((HEAD_DIM, GQ), f32))
        _, l, acc = lax.fori_loop(t_lo, t_hi, body, init)
        return acc / l

    bpt = SEL_TK // SLC_LEN

    def sel_mask(kt, kpos):
        rows = sel_scr[pl.ds(pl.multiple_of(kt * bpt, bpt), bpt), :]
        selx = jnp.concatenate([jnp.broadcast_to(rows[j:j + 1, :], (SLC_LEN, GQ)) for j in range(bpt)], axis=0)
        return (selx > 0.5) & (kpos <= posq)

    o_s = attend(ks_ref, vst_ref, 0, (s0 + QB + SEL_TK - 1) // SEL_TK, SEL_TK, sel_mask)

    def win_mask(kt, kpos):
        dq = posq - kpos
        return (dq >= 0) & (dq < WINDOW)

    o_w = attend(kw_ref, vwt_ref, jnp.maximum(0, (s0 - WINDOW) // WIN_TK), (s0 + QB) // WIN_TK, WIN_TK, win_mask)

    g = jax.nn.sigmoid(g_ref[...])
    o_ref[...] = g[0:1] * o_c + g[1:2] * o_s + g[2:3] * o_w


def nsa_prompt_pallas(q, gl, rows, win, kc, vc):
    B, T = q.shape[:2]
    KVH, G, D, QB = NSA_KV_HEADS, NSA_GQ, HEAD_DIM, Q_BLOCK
    assert T % SEL_TK == 0 and T % QB == 0
    nqb = T // QB
    n_cmp = kc.shape[1]
    ncp = -(-n_cmp // 128) * 128
    n_slc = T // SLC_LEN
    scale = D ** -0.5
    qT = (q * scale).reshape(B, nqb, QB, KVH, G, D).transpose(0, 3, 1, 5, 4, 2).reshape(B, KVH, nqb, D, G * QB)
    qT, qTl = _split(qT)
    gT = gl.reshape(B, nqb, QB, KVH, G, 3).transpose(0, 3, 1, 5, 4, 2).reshape(B, KVH, nqb, 3, G * QB).astype(f32)
    kcp, kcl = _split(jnp.pad(kc, ((0, 0), (0, ncp - n_cmp), (0, 0), (0, 0))).transpose(0, 2, 1, 3))
    vct = jnp.pad(vc, ((0, 0), (0, ncp - n_cmp), (0, 0), (0, 0))).transpose(0, 2, 3, 1).astype(bf16)
    ks = rows[:, :, 2].transpose(0, 2, 1, 3).astype(bf16)
    vst = rows[:, :, 3].transpose(0, 2, 3, 1).astype(bf16)
    kw = win[:, :, 0].transpose(0, 2, 1, 3).astype(bf16)
    vwt = win[:, :, 1].transpose(0, 2, 3, 1).astype(bf16)
    bh = lambda *shape: pl.BlockSpec((None, None) + shape, lambda b, h, i: (b, h) + (0,) * len(shape))
    bhi = lambda *shape: pl.BlockSpec((None, None, None) + shape, lambda b, h, i: (b, h, i) + (0,) * len(shape))
    oT = pl.pallas_call(
        functools.partial(_nsa_prompt_kernel, n_cmp=n_cmp),
        grid=(B, KVH, nqb),
        in_specs=[bhi(D, G * QB), bhi(D, G * QB), bhi(3, G * QB), bh(ncp, D), bh(ncp, D), bh(D, ncp),
                  bh(T, D), bh(D, T), bh(T, D), bh(D, T)],
        out_specs=bhi(D, G * QB),
        out_shape=jax.ShapeDtypeStruct((B, KVH, nqb, D, G * QB), f32),
        scratch_shapes=[pltpu.VMEM((n_slc, QB), f32), pltpu.VMEM((n_slc, G * QB), f32)],
        compiler_params=pltpu.CompilerParams(dimension_semantics=("parallel", "parallel", "arbitrary"),
                                             vmem_limit_bytes=48 * 1024 * 1024),
        name="nsa_prompt",
    )(qT, qTl, gT, kcp, kcl, vct, ks, vst, kw, vwt)
    return oT.reshape(B, KVH, nqb, D, G, QB).transpose(0, 2, 5, 1, 4, 3).reshape(B, T, KVH * G * D)


def _split3(x):
    h1 = x.astype(bf16)
    r1 = x - h1.astype(f32)
    h2 = r1.astype(bf16)
    return h1, h2, (r1 - h2.astype(f32)).astype(bf16)


def _dot_exact_rhs(x, m):
    d = lambda a: jnp.dot(a, m, preferred_element_type=f32)
    h1, h2, h3 = _split3(x)
    return d(h1) + (d(h2) + d(h3))


def _dot_exact_lhs(m, x):
    d = lambda a: jnp.dot(m, a, preferred_element_type=f32)
    h1, h2, h3 = _split3(x)
    return d(h1) + (d(h2) + d(h3))


def _nsa_sample_kernel(pt_ref, *refs, n_pages, ts, past_len):
    pages = refs[:n_pages]
    (new_ref, wbuf_ref, wnew_ref, qh_ref, ql_ref, gate_ref, wa_ref, wb_ref, phik_ref, phiv_ref,
     gain_ref, cos_ref, sin_ref, o_ref, ssel_scr, a_scr, b_scr, score_scr) = refs[n_pages:]
    P = PAGE_SIZE
    KV = NSA_KV_HEADS * HEAD_DIM
    NCOL = NSA_KV_HEADS * NSA_GQ * ts
    NQ = NSA_KV_HEADS * ts
    cpp = P // CMP_STRIDE
    n_chunk = (past_len + SLC_LEN) // CMP_STRIDE
    n_cmp = n_chunk - 1
    ncp = a_scr.shape[0]
    n_slc = (past_len + SLC_LEN) // SLC_LEN
    nsp = score_scr.shape[0]
    dd = lambda x, y: jnp.dot(x, y, preferred_element_type=f32)
    qh = qh_ref[...]
    ql = ql_ref[...]
    col = lax.broadcasted_iota(jnp.int32, (1, NCOL), 1)
    t_col = col % ts
    zpad = jnp.zeros((P - ts, 4 * KV), f32)
    new_tile = jnp.concatenate([new_ref[...], zpad], axis=0)

    wa = wa_ref[...]
    wb = wb_ref[...]
    a_scr[...] = jnp.zeros(a_scr.shape, f32)
    b_scr[...] = jnp.zeros(b_scr.shape, f32)
    for j in range(n_pages + 1):
        tile = pages[j][...] if j < n_pages else new_tile
        xc = tile[:, 0:2 * KV]
        a_scr[j * cpp:(j + 1) * cpp, :] = (xc * wa).reshape(cpp, CMP_STRIDE, 2 * KV).sum(axis=1)
        b_scr[j * cpp:(j + 1) * cpp, :] = (xc * wb).reshape(cpp, CMP_STRIDE, 2 * KV).sum(axis=1)
        ssel_scr[j * P:(j + 1) * P, :] = dd(tile[:, 2 * KV:3 * KV].astype(bf16), qh)

    mean = a_scr[...] + pltpu.roll(b_scr[...], ncp - 1, axis=0)
    kc = _mmul(mean[:, 0:KV], phik_ref[...], 3)
    vc = _mmul(mean[:, KV:2 * KV], phiv_ref[...], 3)
    r_i = lax.broadcasted_iota(jnp.int32, (KV, KV), 0)
    c_i = lax.broadcasted_iota(jnp.int32, (KV, KV), 1)
    same_head = (r_i // HEAD_DIM) == (c_i // HEAD_DIM)
    mavg = jnp.where(same_head, 1.0 / HEAD_DIM, 0.0).astype(bf16)
    kc = kc * lax.rsqrt(_dot_exact_rhs(kc * kc, mavg) + RMS_EPS) * gain_ref[...]
    half = ROPE_DIM // 2
    rd, cd = r_i % HEAD_DIM, c_i % HEAD_DIM
    rot = jnp.where(same_head & (cd < half) & (rd == cd + half), -1.0,
                    jnp.where(same_head & (cd >= half) & (cd < ROPE_DIM) & (rd == cd - half), 1.0, 0.0)).astype(bf16)
    kc = kc * cos_ref[...] + _dot_exact_rhs(kc, rot) * sin_ref[...]

    kch, kcl = _split(kc)
    sc = dd(kch, qh) + (dd(kch, ql) + dd(kcl, qh))
    cidx = lax.broadcasted_iota(jnp.int32, (ncp, 1), 0)
    mask_c = (cidx * CMP_STRIDE + (CMP_LEN - 1) <= past_len + t_col) & (cidx < n_cmp)
    sc = jnp.where(mask_c, sc, NEG_INF)
    pe = jnp.exp(sc - jnp.max(sc, axis=0, keepdims=True))
    pc = jnp.where(mask_c, pe / jnp.sum(pe, axis=0, keepdims=True), 0.0)
    o_c = dd(pc.T.astype(bf16), vc.astype(bf16))

    gr = lax.broadcasted_iota(jnp.int32, (NCOL, NQ), 0)
    gc = lax.broadcasted_iota(jnp.int32, (NCOL, NQ), 1)
    gsum = ((gr // (NSA_GQ * ts) == gc // ts) & (gr % ts == gc % ts)).astype(bf16)
    imp = _dot_exact_rhs(pc, gsum)
    per = SLC_LEN // CMP_STRIDE
    nn = lax.broadcasted_iota(jnp.int32, (nsp, ncp), 0) * per
    cc = lax.broadcasted_iota(jnp.int32, (nsp, ncp), 1)
    mt = (0.5 * ((cc >= nn) & (cc < nn + per)).astype(f32)
          + 0.5 * ((cc + 1 >= nn) & (cc + 1 < nn + per)).astype(f32)).astype(bf16)
    imp_blk = _dot_exact_lhs(mt, imp)
    nidx = lax.broadcasted_iota(jnp.int32, (nsp, 1), 0)
    cur = (past_len + lax.broadcasted_iota(jnp.int32, (1, NQ), 1) % ts) // SLC_LEN
    forced = (nidx == 0) | (nidx == cur) | (nidx == cur - 1)
    score = jnp.where(nidx >= n_slc, -2.0, jnp.where(nidx > cur, -1.0, jnp.where(forced, 1e6, imp_blk)))
    score_scr[...] = score
    rank = jnp.zeros((nsp, NQ), jnp.int32)
    for m in range(n_slc):
        sm = score_scr[m:m + 1, :]
        beats = (sm > score) | ((sm == score) & (nidx > m))
        rank = rank + beats.astype(jnp.int32)
    sel = (rank < min(N_SEL, n_slc)).astype(bf16)
    gr2 = lax.broadcasted_iota(jnp.int32, (NQ, NCOL), 0)
    gc2 = lax.broadcasted_iota(jnp.int32, (NQ, NCOL), 1)
    gexp = ((gc2 // (NSA_GQ * ts) == gr2 // ts) & (gc2 % ts == gr2 % ts)).astype(bf16)
    sel_c = dd(sel, gexp)

    def two_pass(n_tiles, score_tile, mask_tile, v_tile):
        m = jnp.full((1, NCOL), NEG_INF, f32)
        for j in range(n_tiles):
            m = jnp.maximum(m, jnp.max(jnp.where(mask_tile(j), score_tile(j), NEG_INF), axis=0, keepdims=True))
        num = jnp.zeros((NCOL, KV), f32)
        den = jnp.zeros((NCOL, KV), f32)
        ones = jnp.ones((P, KV), bf16)
        for j in range(n_tiles):
            p = jnp.where(mask_tile(j), jnp.exp(score_tile(j) - m), 0.0).T.astype(bf16)
            num = num + dd(p, v_tile(j))
            den = den + dd(p, ones)
        return num, den

    bpp = P // SLC_LEN
    row = lax.broadcasted_iota(jnp.int32, (P, 1), 0)

    def sel_mask(j):
        blk = jnp.concatenate([jnp.broadcast_to(sel_c[j * bpp + i:j * bpp + i + 1, :], (SLC_LEN, NCOL))
                               for i in range(bpp)], axis=0)
        return (blk > 0.5) & (j * P + row <= past_len + t_col)

    num_s, den_s = two_pass(
        n_pages + 1, lambda j: ssel_scr[j * P:(j + 1) * P, :], sel_mask,
        lambda j: (pages[j][:, 3 * KV:4 * KV] if j < n_pages else new_tile[:, 3 * KV:4 * KV]).astype(bf16))

    lb = wbuf_ref.shape[0]
    nwt = lb // P
    wnew = jnp.concatenate([wnew_ref[...], jnp.zeros((P - ts, 2 * KV), f32)], axis=0)

    def w_tile(j):
        return wbuf_ref[j * P:(j + 1) * P, :] if j < nwt else wnew

    def win_mask(j):
        pos_w = (past_len - lb + j * P + row) if j < nwt else (past_len + row)
        dq = past_len + t_col - pos_w
        return (dq >= 0) & (dq < WINDOW) & (pos_w >= 0) & ((row < ts) | (j < nwt))

    num_w, den_w = two_pass(
        nwt + 1, lambda j: dd(w_tile(j)[:, 0:KV].astype(bf16), qh), win_mask,
        lambda j: w_tile(j)[:, KV:2 * KV].astype(bf16))

    g = jax.nn.sigmoid(gate_ref[...])
    o_ref[...] = g[0] * o_c + g[1] * (num_s / den_s) + g[2] * (num_w / den_w)


def nsa_sample_pallas(q, gl, rows_new, win_new, pool_kv, page_table, win_buf, kc_w, phi, k_gain):
    B, ts = q.shape[:2]
    KVH, G, D, P = NSA_KV_HEADS, NSA_GQ, HEAD_DIM, PAGE_SIZE
    KV = KVH * D
    n_pages = page_table.shape[1]
    past_len = n_pages * P
    assert pool_kv.shape[1] == P and ts <= SLC_LEN and P % SLC_LEN == 0
    lb = win_buf.shape[1]
    assert lb % P == 0
    NCOL = KVH * G * ts
    n_chunk = (past_len + SLC_LEN) // CMP_STRIDE
    ncp = -(-n_chunk // 8) * 8
    n_slc = (past_len + SLC_LEN) // SLC_LEN
    nsp = -(-n_slc // 8) * 8
    qs = (q * D ** -0.5).reshape(B, ts, KVH, G, D).transpose(0, 2, 4, 3, 1).reshape(B, KVH, D, G * ts)
    z = jnp.zeros_like(qs[:, 0])
    qbd = jnp.concatenate([jnp.concatenate([qs[:, 0], z], axis=2), jnp.concatenate([z, qs[:, 1]], axis=2)], axis=1)
    qh, ql = _split(qbd)
    gate = gl.reshape(B, ts, KVH, G, 3).transpose(0, 4, 2, 3, 1).reshape(B, 3, NCOL, 1)
    gate = jnp.broadcast_to(gate, (B, 3, NCOL, KV)).astype(f32)
    w_lane = jnp.repeat(kc_w.reshape(2 * KVH, CMP_LEN), D, axis=0)
    reps = P // CMP_STRIDE
    wa = jnp.tile(w_lane[:, :CMP_STRIDE].T, (reps, 1)).astype(f32)
    wb = jnp.tile(w_lane[:, CMP_STRIDE:].T, (reps, 1)).astype(f32)
    zz = jnp.zeros((D, D), f32)
    bdiag = lambda m: jnp.concatenate([jnp.concatenate([m, zz], axis=1), jnp.concatenate([zz, m], axis=1)], axis=0)
    phik, phiv = bdiag(phi[0].astype(f32)), bdiag(phi[1].astype(f32))
    gain = jnp.tile(k_gain.astype(f32), KVH).reshape(1, KV)
    half = ROPE_DIM // 2
    inv = ROPE_THETA ** (-jnp.arange(half, dtype=f32) / half)
    cmp_end = (jnp.arange(ncp, dtype=jnp.int32) * CMP_STRIDE + (CMP_LEN - 1)).astype(f32)
    ang = cmp_end[:, None] * inv
    cos_h = jnp.concatenate([jnp.cos(ang), jnp.cos(ang), jnp.ones((ncp, D - ROPE_DIM), f32)], axis=1)
    sin_h = jnp.concatenate([jnp.sin(ang), jnp.sin(ang), jnp.zeros((ncp, D - ROPE_DIM), f32)], axis=1)
    cos_t, sin_t = jnp.tile(cos_h, (1, KVH)), jnp.tile(sin_h, (1, KVH))
    pool2 = pool_kv.reshape(pool_kv.shape[0], P, 4 * KV)
    new2 = rows_new.reshape(B, ts, 4 * KV).astype(f32)
    wbuf2 = win_buf.reshape(B, lb, 2 * KV)
    wnew2 = win_new.reshape(B, ts, 2 * KV).astype(f32)
    page_spec = lambda j: pl.BlockSpec((None, P, 4 * KV), lambda b, pt, j=j: (pt[b, j], 0, 0))
    per_b = lambda *s: pl.BlockSpec((None,) + s, lambda b, pt: (b,) + (0,) * len(s))
    const = lambda *s: pl.BlockSpec(s, lambda b, pt: (0,) * len(s))
    grid_spec = pltpu.PrefetchScalarGridSpec(
        num_scalar_prefetch=1, grid=(B,),
        in_specs=[page_spec(j) for j in range(n_pages)] + [
            per_b(ts, 4 * KV), per_b(lb, 2 * KV), per_b(ts, 2 * KV), per_b(KV, NCOL), per_b(KV, NCOL),
            per_b(3, NCOL, KV), const(P, 2 * KV), const(P, 2 * KV), const(KV, KV), const(KV, KV),
            const(1, KV), const(ncp, KV), const(ncp, KV)],
        out_specs=per_b(NCOL, KV),
        scratch_shapes=[pltpu.VMEM(((n_pages + 1) * P, NCOL), f32), pltpu.VMEM((ncp, 2 * KV), f32),
                        pltpu.VMEM((ncp, 2 * KV), f32), pltpu.VMEM((nsp, KVH * ts), f32)])
    out = pl.pallas_call(
        functools.partial(_nsa_sample_kernel, n_pages=n_pages, ts=ts, past_len=past_len),
        grid_spec=grid_spec,
        out_shape=jax.ShapeDtypeStruct((B, NCOL, KV), f32),
        compiler_params=pltpu.CompilerParams(dimension_semantics=("arbitrary",),
                                             vmem_limit_bytes=48 * 1024 * 1024),
        name="nsa_sample",
    )(page_table, *([pool2] * n_pages), new2, wbuf2, wnew2, qh, ql, gate, wa, wb, phik, phiv, gain, cos_t, sin_t)
    o4 = out.reshape(B, KVH, G, ts, KVH, D)
    o = jnp.stack([o4[:, 0, :, :, 0], o4[:, 1, :, :, 1]], axis=1)
    return o.transpose(0, 3, 1, 2, 4).reshape(B, ts, KVH * G * D)


MOE_TM = 512
ROUTER_LANES = 128


def _moe_kernel(x_ref, g_ref, wrh_ref, wrl_ref, br_ref, wg_ref, wu_ref, wd_ref, o_ref,
                h_scr, gate_scr, acc_scr):
    grp = pl.program_id(1)
    dd = lambda a, b: jnp.dot(a, b, preferred_element_type=f32)
    tm = x_ref.shape[0]
    lane = lax.broadcasted_iota(jnp.int32, (tm, ROUTER_LANES), 1).astype(f32)
    far = float(ROUTER_LANES)

    @pl.when(grp == 0)
    def _():
        x = x_ref[...]
        h = x * lax.rsqrt(jnp.mean(x * x, axis=-1, keepdims=True) + RMS_EPS) * g_ref[...]
        hh, hl = _split(h)
        h_scr[...] = hh
        logits = dd(hh, wrh_ref[...]) + (dd(hh, wrl_ref[...]) + dd(hl, wrh_ref[...])) + br_ref[...]
        is_c = lane < MOE_GROUPS
        lc = jnp.where(is_c, logits, NEG_INF)
        mc = jnp.max(lc, axis=1, keepdims=True)
        g_idx = jnp.min(jnp.where(lc == mc, lane, far), axis=1, keepdims=True)
        g_w = 1.0 / jnp.sum(jnp.where(is_c, jnp.exp(lc - mc), 0.0), axis=1, keepdims=True)
        lo = MOE_GROUPS + MOE_EPG * g_idx
        lf = jnp.where((lane >= lo) & (lane < lo + MOE_EPG), logits, NEG_INF)
        v1 = jnp.max(lf, axis=1, keepdims=True)
        i1 = jnp.min(jnp.where(lf == v1, lane, far), axis=1, keepdims=True)
        lf2 = jnp.where(lane == i1, NEG_INF, lf)
        v2 = jnp.max(lf2, axis=1, keepdims=True)
        i2 = jnp.min(jnp.where(lf2 == v2, lane, far), axis=1, keepdims=True)
        e21 = jnp.exp(v2 - v1)
        w1 = g_w / (1.0 + e21)
        gate_scr[...] = jnp.where(lane == i1, w1, jnp.where(lane == i2, e21 * w1, 0.0))
        acc_scr[...] = x

    h = h_scr[...]
    hg = dd(h, wg_ref[...])
    hu = dd(h, wu_ref[...])
    gate = gate_scr[...]
    first = (MOE_GROUPS + MOE_EPG * grp).astype(f32)
    cols = []
    for e in range(MOE_EPG):
        ge = jnp.sum(jnp.where(lane == first + e, gate, 0.0), axis=1, keepdims=True)
        sl = slice(e * D_FF_E, (e + 1) * D_FF_E)
        hge = hg[:, sl]
        cols.append((hge * jax.nn.sigmoid(hge) * hu[:, sl] * ge).astype(bf16))
    acc_scr[...] += dd(jnp.concatenate(cols, axis=1), wd_ref[...])

    @pl.when(grp == MOE_GROUPS - 1)
    def _():
        o_ref[...] = acc_scr[...]


def moe_prep(g, wc, bc, wf, bf, wg, wu, wd):
    d = wc.shape[0]
    pad = ROUTER_LANES - MOE_GROUPS - N_EXPERTS
    wr = jnp.pad(jnp.concatenate([wc, wf], axis=1).astype(f32), ((0, 0), (0, pad)))
    wrh, wrl = _split(wr)
    br = jnp.pad(jnp.concatenate([bc, bf]).astype(f32), (0, pad)).reshape(1, ROUTER_LANES)
    regroup = lambda w: (w.reshape(MOE_GROUPS, MOE_EPG, d, D_FF_E).transpose(0, 2, 1, 3)
                         .reshape(MOE_GROUPS, d, MOE_EPG * D_FF_E).astype(bf16))
    wdg = wd.reshape(MOE_GROUPS, MOE_EPG * D_FF_E, d).astype(bf16)
    return g.reshape(1, d).astype(f32), wrh, wrl, br, regroup(wg), regroup(wu), wdg


def moe_residual(x, prep):
    g, wrh, wrl, br, wgg, wug, wdg = prep
    shp = x.shape
    d = shp[-1]
    x2 = x.reshape(-1, d)
    m = x2.shape[0]
    tm = MOE_TM
    assert m % tm == 0
    gf = MOE_EPG * D_FF_E
    full = lambda r, c: pl.BlockSpec((r, c), lambda i, j: (0, 0))
    out = pl.pallas_call(
        _moe_kernel,
        grid=(m // tm, MOE_GROUPS),
        in_specs=[pl.BlockSpec((tm, d), lambda i, j: (i, 0)), full(1, d),
                  full(d, ROUTER_LANES), full(d, ROUTER_LANES), full(1, ROUTER_LANES),
                  pl.BlockSpec((None, d, gf), lambda i, j: (j, 0, 0)),
                  pl.BlockSpec((None, d, gf), lambda i, j: (j, 0, 0)),
                  pl.BlockSpec((None, gf, d), lambda i, j: (j, 0, 0))],
        out_specs=pl.BlockSpec((tm, d), lambda i, j: (i, 0)),
        out_shape=jax.ShapeDtypeStruct((m, d), f32),
        scratch_shapes=[pltpu.VMEM((tm, d), bf16), pltpu.VMEM((tm, ROUTER_LANES), f32),
                        pltpu.VMEM((tm, d), f32)],
        compiler_params=pltpu.CompilerParams(dimension_semantics=("parallel", "arbitrary"),
                                             vmem_limit_bytes=48 * 1024 * 1024),
        name="moe",
    )(x2, g, wrh, wrl, br, wgg, wug, wdg)
    return out.reshape(shp)


def rmsnorm(x, g):
    xf = x.astype(jnp.float32)
    y = xf * lax.rsqrt(jnp.mean(xf * xf, axis=-1, keepdims=True) + RMS_EPS)
    return (y * g.astype(jnp.float32)).astype(x.dtype)


def rope_partial(x, pos):
    half = ROPE_DIM // 2
    inv = ROPE_THETA ** (-jnp.arange(half, dtype=jnp.float32) / half)
    ang = pos.astype(jnp.float32)[:, None] * inv
    cos = jnp.cos(ang)[:, None, :]
    sin = jnp.sin(ang)[:, None, :]
    xf = x.astype(jnp.float32)
    x1 = xf[..., :half]
    x2 = xf[..., half:ROPE_DIM]
    out = jnp.concatenate([x1 * cos - x2 * sin, x2 * cos + x1 * sin, xf[..., ROPE_DIM:]], axis=-1)
    return out.astype(x.dtype)


def masked_softmax(s, mask):
    s = jnp.where(mask, s.astype(jnp.float32), NEG_INF)
    p = jax.nn.softmax(s, axis=-1)
    return jnp.where(mask, p, 0.0)


def pool_mix(u, hist, p0, w_grp, scale):
    b, t, _ = u.shape
    ext = jnp.concatenate([hist.astype(u.dtype), u], axis=1).astype(jnp.float32)
    cs = jnp.pad(jnp.cumsum(ext, axis=1), ((0, 0), (1, 0), (0, 0)))
    cnt_pos = p0 + jnp.arange(t, dtype=jnp.int32) + 1
    means = []
    for gi, w in enumerate(POOL_WINDOWS):
        c = cs[..., gi * POOL_GDIM:(gi + 1) * POOL_GDIM]
        win_sum = c[:, POOL_HIST + 1:POOL_HIST + 1 + t] - c[:, POOL_HIST + 1 - w:POOL_HIST + 1 - w + t]
        cnt = jnp.minimum(cnt_pos, w).astype(jnp.float32)[None, :, None]
        means.append(win_sum / cnt)
    mean = jnp.stack(means, axis=2)
    d = mean - u.reshape(b, t, POOL_GROUPS, POOL_GDIM).astype(jnp.float32)
    y = jnp.einsum('btgc,gcd->btgd', d, w_grp.astype(jnp.float32)).reshape(b, t, POOL_DIM)
    return (y * scale.astype(jnp.float32)).astype(u.dtype)


PROJ_TM = 512


def _norm_mm_kernel(x_ref, g_ref, w_ref, o_ref):
    x = x_ref[...]
    h = x * lax.rsqrt(jnp.mean(x * x, axis=-1, keepdims=True) + RMS_EPS) * g_ref[...]
    o_ref[...] = jnp.dot(h.astype(bf16), w_ref[...], preferred_element_type=f32)


def _norm_mm(x, g, w):
    lead, d = x.shape[:-1], x.shape[-1]
    n = w.shape[1]
    x2 = x.reshape(-1, d)
    m = x2.shape[0]
    npad = -(-n // 128) * 128
    wb = jnp.pad(w.astype(bf16), ((0, 0), (0, npad - n)))
    tm = PROJ_TM
    assert m % tm == 0
    out = pl.pallas_call(
        _norm_mm_kernel, grid=(m // tm,),
        in_specs=[pl.BlockSpec((tm, d), lambda i: (i, 0)), pl.BlockSpec((1, d), lambda i: (0, 0)),
                  pl.BlockSpec((d, npad), lambda i: (0, 0))],
        out_specs=pl.BlockSpec((tm, npad), lambda i: (i, 0)),
        out_shape=jax.ShapeDtypeStruct((m, npad), f32),
        compiler_params=pltpu.CompilerParams(dimension_semantics=("parallel",),
                                             vmem_limit_bytes=48 * 1024 * 1024),
        name="norm_mm",
    )(x2, g.reshape(1, d).astype(f32), wb)
    return out[:, :n].reshape(lead + (n,))


def _mix_out_kernel(a_ref, b_ref, wa_ref, wb_ref, x_ref, o_ref):
    dd = lambda p, q: jnp.dot(p.astype(bf16), q, preferred_element_type=f32)
    o_ref[...] = x_ref[...] + (dd(a_ref[...], wa_ref[...]) + dd(b_ref[...], wb_ref[...]))


def _mix_out(a, b, w, x):
    d = x.shape[-1]
    ka, kb = a.shape[-1], b.shape[-1]
    x2 = x.reshape(-1, d)
    m = x2.shape[0]
    tm = PROJ_TM
    assert m % tm == 0
    wbf = w.astype(bf16)
    tile = lambda c: pl.BlockSpec((tm, c), lambda i: (i, 0))
    full = lambda r, c: pl.BlockSpec((r, c), lambda i: (0, 0))
    out = pl.pallas_call(
        _mix_out_kernel, grid=(m // tm,),
        in_specs=[tile(ka), tile(kb), full(ka, d), full(kb, d), tile(d)], out_specs=tile(d),
        out_shape=jax.ShapeDtypeStruct((m, d), f32),
        compiler_params=pltpu.CompilerParams(dimension_semantics=("parallel",)),
        name="mix_out",
    )(a.reshape(m, ka), b.reshape(m, kb), wbf[:ka], wbf[ka:], x2)
    return out.reshape(x.shape)


def ab_features(x, norm_g, pos, w_in, q_norm, k_norm):
    b, t = x.shape[:2]
    u = _norm_mm(x, norm_g, w_in)
    off_kv = POOL_DIM + NSA_DIM
    pool_in = u[..., :POOL_DIM]
    q = u[..., POOL_DIM:off_kv].reshape(b, t, NSA_HEADS, HEAD_DIM)
    kv = u[..., off_kv:off_kv + KV_COLS].reshape(b, t, 6, NSA_KV_HEADS, HEAD_DIM)
    gl = u[..., off_kv + KV_COLS:].reshape(b, t, NSA_HEADS, 3)
    q = rope_partial(rmsnorm(q, q_norm), pos)
    k_slc = rope_partial(rmsnorm(kv[:, :, 2], k_norm[1]), pos)
    k_win = rope_partial(rmsnorm(kv[:, :, 4], k_norm[2]), pos)
    rows = jnp.stack([kv[:, :, 0], kv[:, :, 1], k_slc, kv[:, :, 3]], axis=2)
    win = jnp.stack([k_win, kv[:, :, 5]], axis=2)
    return pool_in, q, gl, rows, win


def compress_kv(k_rows, v_rows, pos_w, phi, k_gain):
    b, length = k_rows.shape[:2]
    n_chunk = length // CMP_STRIDE

    def weighted_block_mean(rows, w):
        ch = rows.reshape(b, n_chunk, CMP_STRIDE, NSA_KV_HEADS, HEAD_DIM)
        return (jnp.einsum('bnlhd,hl->bnhd', ch[:, :-1], w[:, :CMP_STRIDE])
                + jnp.einsum('bnlhd,hl->bnhd', ch[:, 1:], w[:, CMP_STRIDE:]))

    cmp_end = jnp.arange(n_chunk - 1, dtype=jnp.int32) * CMP_STRIDE + (CMP_LEN - 1)
    kc = jnp.einsum('bnhd,de->bnhe', weighted_block_mean(k_rows, pos_w[0]), phi[0])
    kc = rope_partial(rmsnorm(kc, k_gain), cmp_end)
    vc = jnp.einsum('bnhd,de->bnhe', weighted_block_mean(v_rows, pos_w[1]), phi[1])
    return kc, vc, cmp_end


def nsa_attend(q, pos_q, gl, kc, vc, cmp_end, ks, vs, kw, vw, pos_w):
    f32 = jnp.float32
    b, tq = q.shape[:2]
    qg = q.reshape(b, tq, NSA_KV_HEADS, NSA_GQ, HEAD_DIM)
    scale = HEAD_DIM ** -0.5
    s_c = jnp.einsum('bqhgd,bchd->bhgqc', qg, kc) * scale
    p_c = masked_softmax(s_c, cmp_end[None, :] <= pos_q[:, None])
    o_c = jnp.einsum('bhgqc,bchd->bqhgd', p_c, vc.astype(f32))
    imp = p_c.sum(axis=2)
    imp_chunk = 0.5 * (jnp.pad(imp, ((0, 0), (0, 0), (0, 0), (0, 1)))
                       + jnp.pad(imp, ((0, 0), (0, 0), (0, 0), (1, 0))))
    n_slc = ks.shape[1] // SLC_LEN
    imp_blk = imp_chunk.reshape(b, NSA_KV_HEADS, tq, n_slc, SLC_LEN // CMP_STRIDE).sum(-1)
    blk = jnp.arange(n_slc, dtype=jnp.int32)[None, :]
    cur = (pos_q // SLC_LEN)[:, None]
    forced = (blk == 0) | (blk == cur) | (blk == cur - 1)
    score = jnp.where(blk > cur, -1.0, jnp.where(forced, 1e6, imp_blk))
    n_sel = min(N_SEL, n_slc)
    _, idx = lax.top_k(score, n_sel)
    gather = jax.vmap(jax.vmap(lambda rows, i: rows[i]))
    ksb = ks.reshape(b, n_slc, SLC_LEN, NSA_KV_HEADS, HEAD_DIM).transpose(0, 3, 1, 2, 4)
    vsb = vs.reshape(b, n_slc, SLC_LEN, NSA_KV_HEADS, HEAD_DIM).transpose(0, 3, 1, 2, 4)
    kg = gather(ksb, idx)
    vg = gather(vsb, idx)
    kpos = idx[..., None] * SLC_LEN + jnp.arange(SLC_LEN, dtype=jnp.int32)
    n_keys = n_sel * SLC_LEN
    m_s = (kpos <= pos_q[None, None, :, None, None]).reshape(b, NSA_KV_HEADS, 1, tq, n_keys)
    s_s = jnp.einsum('bqhgd,bhqnld->bhgqnl', qg, kg).reshape(b, NSA_KV_HEADS, NSA_GQ, tq, n_keys) * scale
    p_s = masked_softmax(s_s, m_s)
    o_s = jnp.einsum('bhgqk,bhqkd->bqhgd', p_s,
                     vg.reshape(b, NSA_KV_HEADS, tq, n_keys, HEAD_DIM).astype(f32))
    s_w = jnp.einsum('bqhgd,bkhd->bhgqk', qg, kw) * scale
    dq = pos_q[:, None] - pos_w[None, :]
    m_w = (dq >= 0) & (dq < WINDOW) & (pos_w[None, :] >= 0)
    p_w = masked_softmax(s_w, m_w)
    o_w = jnp.einsum('bhgqk,bkhd->bqhgd', p_w, vw.astype(f32))
    g = jax.nn.sigmoid(gl.astype(f32)).reshape(b, tq, NSA_KV_HEADS, NSA_GQ, 3)
    o = g[..., 0:1] * o_c + g[..., 1:2] * o_s + g[..., 2:3] * o_w
    return o.reshape(b, tq, NSA_DIM)


def nsa_prompt(q, gl, rows, win, cmp_pos_w, cmp_phi, k_gain):
    b, t = q.shape[:2]
    kc, vc, cmp_end = compress_kv(rows[:, :, 0], rows[:, :, 1], cmp_pos_w, cmp_phi, k_gain)
    ks, vs = rows[:, :, 2], rows[:, :, 3]
    win_pad = jnp.pad(win, ((0, 0), (WINDOW, 0), (0, 0), (0, 0), (0, 0)))

    def block(i):
        s0 = i * Q_BLOCK
        qb = lax.dynamic_slice_in_dim(q, s0, Q_BLOCK, axis=1)
        gb = lax.dynamic_slice_in_dim(gl, s0, Q_BLOCK, axis=1)
        wb = lax.dynamic_slice_in_dim(win_pad, s0, WINDOW + Q_BLOCK, axis=1)
        pos_q = s0 + jnp.arange(Q_BLOCK, dtype=jnp.int32)
        pos_w = s0 - WINDOW + jnp.arange(WINDOW + Q_BLOCK, dtype=jnp.int32)
        return nsa_attend(qb, pos_q, gb, kc, vc, cmp_end, ks, vs, wb[:, :, 0], wb[:, :, 1], pos_w)

    out = lax.map(block, jnp.arange(t // Q_BLOCK, dtype=jnp.int32))
    return jnp.moveaxis(out, 0, 1).reshape(b, t, NSA_DIM)


def nsa_sample(q, gl, rows_new, win_new, pool_kv, page_table, win_buf, cmp_pos_w, cmp_phi, k_gain):
    bd, ts = q.shape[:2]
    past_len = page_table.shape[1] * pool_kv.shape[1]
    past = pool_kv[page_table].reshape(bd, past_len, 4, NSA_KV_HEADS, HEAD_DIM)
    rows = jnp.concatenate([past, rows_new.astype(past.dtype)], axis=1)
    pad = (-(past_len + ts)) % SLC_LEN
    rows = jnp.pad(rows, ((0, 0), (0, pad), (0, 0), (0, 0), (0, 0)))
    kc, vc, cmp_end = compress_kv(rows[:, :, 0], rows[:, :, 1], cmp_pos_w, cmp_phi, k_gain)
    lb = win_buf.shape[1]
    win = jnp.concatenate([win_buf, win_new.astype(win_buf.dtype)], axis=1)
    pos_q = past_len + jnp.arange(ts, dtype=jnp.int32)
    pos_w = past_len - lb + jnp.arange(lb + ts, dtype=jnp.int32)
    o = nsa_attend(q, pos_q, gl, kc, vc, cmp_end, rows[:, :, 2], rows[:, :, 3],
                   win[:, :, 0], win[:, :, 1], pos_w)
    keep = min(WINDOW, lb + ts)
    return o, win[:, lb + ts - keep:]


def ab_layer_prompt(x, norm_g, w_in, w_out, q_norm, k_norm, cmp_pos_w, cmp_phi, pool_w, pool_scale):
    b, t = x.shape[:2]
    pos = jnp.arange(t, dtype=jnp.int32)
    pool_in, q, gl, rows, win = ab_features(x, norm_g, pos, w_in, q_norm, k_norm)
    hist0 = jnp.zeros((b, POOL_HIST, POOL_DIM), x.dtype)
    pool_out = pool_mix(pool_in, hist0, 0, pool_w, pool_scale)
    kc, vc, _ = compress_kv(rows[:, :, 0], rows[:, :, 1], cmp_pos_w, cmp_phi, k_norm[0])
    nsa_out = nsa_prompt_pallas(q, gl, rows, win, kc, vc)
    x_new = _mix_out(pool_out, nsa_out, w_out, x)
    keep = min(WINDOW, t)
    return x_new, rows, win[:, t - keep:], pool_in[:, t - POOL_HIST:]


def ab_layer_sample(x, norm_g, pool_kv, page_table, win_buf, pool_hist, w_in, w_out, q_norm, k_norm,
                    cmp_pos_w, cmp_phi, pool_w, pool_scale):
    ts = x.shape[1]
    past_len = page_table.shape[1] * pool_kv.shape[1]
    pos = past_len + jnp.arange(ts, dtype=jnp.int32)
    pool_in, q, gl, rows, win = ab_features(x, norm_g, pos, w_in, q_norm, k_norm)
    pool_out = pool_mix(pool_in, pool_hist, past_len, pool_w, pool_scale)
    nsa_out = nsa_sample_pallas(q, gl, rows, win, pool_kv, page_table, win_buf,
                                cmp_pos_w, cmp_phi, k_norm[0])
    lb = win_buf.shape[1]
    keep = min(WINDOW, lb + ts)
    new_win = jnp.concatenate([win_buf, win.astype(win_buf.dtype)], axis=1)[:, lb + ts - keep:]
    x_new = _mix_out(pool_out, nsa_out, w_out, x)
    new_hist = jnp.concatenate([pool_hist.astype(pool_in.dtype), pool_in], axis=1)[:, -POOL_HIST:]
    return x_new, rows, new_win, new_hist


def wkv_scan(s0, r, w, k, v, kk, a):
    def step(s, inp):
        r_t, w_t, k_t, v_t, kk_t, a_t = inp
        sa = jnp.einsum('bhij,bhj->bhi', s, -kk_t)
        s = (s * w_t[:, :, None, :] + sa[..., None] * (kk_t * a_t)[:, :, None, :]
             + v_t[..., None] * k_t[:, :, None, :])
        return s, jnp.einsum('bhij,bhj->bhi', s, r_t)

    xs = tuple(jnp.moveaxis(z.astype(jnp.float32), 1, 0) for z in (r, w, k, v, kk, a))
    s, o = lax.scan(step, s0.astype(jnp.float32), xs)
    return s, jnp.moveaxis(o, 0, 1)


RW_TM = 256
HEAD_LANES = 128


def _rwkv_pre_kernel(*refs, seq_len, has_vres):
    it = iter(refs)
    x_ref, xprev_ref, fp_ref, g_ref, mu_ref = [next(it) for _ in range(5)]
    wr_ref, wk_ref, wv_ref = [next(it) for _ in range(3)]
    w0_ref, w1_ref, w2_ref, a0_ref, a1_ref, a2_ref, g1_ref, g2_ref = [next(it) for _ in range(8)]
    kkw_ref, kaw_ref, hsum_ref, hexp_ref = [next(it) for _ in range(4)]
    if has_vres:
        v0_ref, v1_ref, v2_ref, vf_ref = [next(it) for _ in range(4)]
    r_ref, k_ref, v_ref, lw_ref, kk_ref, a_ref, gg_ref = [next(it) for _ in range(7)]
    dd = lambda a, b: jnp.dot(a, b, preferred_element_type=f32)
    tm = x_ref.shape[0]
    norm = lambda z: z * lax.rsqrt(jnp.mean(z * z, axis=-1, keepdims=True) + RMS_EPS) * g_ref[...]
    h = norm(x_ref[...])
    row = lax.broadcasted_iota(jnp.int32, (tm, 1), 0)
    rolled = pltpu.roll(h, 1, axis=0)
    if seq_len % tm == 0:
        first = (pl.program_id(0) % (seq_len // tm)) == 0
        last_prev = norm(xprev_ref[...])[xprev_ref.shape[0] - 1:, :]
        prev = jnp.where(row == 0, jnp.where(first, fp_ref[...], last_prev), rolled)
    else:
        nseq = tm // seq_len
        sel = (lax.broadcasted_iota(jnp.int32, (tm, nseq), 0)
               == seq_len * lax.broadcasted_iota(jnp.int32, (tm, nseq), 1)).astype(bf16)
        prev = jnp.where(row % seq_len == 0, _dot_exact_lhs(sel, fp_ref[...]), rolled)
    xx = prev - h
    mix = lambda j: (h + xx * mu_ref[j:j + 1, :]).astype(bf16)
    xr, xw, xk, xv, xa, xg = [mix(j) for j in range(6)]
    k = dd(xk, wk_ref[...])
    v = dd(xv, wv_ref[...])
    r_ref[...] = dd(xr, wr_ref[...])
    z = -(w0_ref[...] + dd(jnp.tanh(dd(xw, w1_ref[...])).astype(bf16), w2_ref[...]))
    softplus = jnp.maximum(z, 0.0) + jnp.log(1.0 + jnp.exp(-jnp.abs(z)))
    lw_ref[...] = -jnp.exp(-softplus - 0.5)
    a = jax.nn.sigmoid(a0_ref[...] + dd(dd(xa, a1_ref[...]).astype(bf16), a2_ref[...]))
    a_ref[...] = a
    gg_ref[...] = dd(jax.nn.sigmoid(dd(xg, g1_ref[...])).astype(bf16), g2_ref[...])
    if has_vres:
        v = v + (vf_ref[...] - v) * jax.nn.sigmoid(
            v0_ref[...] + dd(dd(xv, v1_ref[...]).astype(bf16), v2_ref[...]))
    v_ref[...] = v
    kk = k * kkw_ref[...]
    nrm = jnp.maximum(jnp.sqrt(_dot_exact_rhs(kk * kk, hsum_ref[...])), 1e-12)
    kk_ref[...] = kk * _dot_exact_rhs(1.0 / nrm, hexp_ref[...])
    k_ref[...] = k * (1.0 + (a - 1.0) * kaw_ref[...])


def _rwkv_post_kernel(o_ref, r_ref, k_ref, v_ref, gg_ref, x_ref, gnw_ref, gnb_ref, rk_ref, hsum_ref,
                      hexp_ref, wo_ref, out_ref):
    hs = hsum_ref[...]
    he = hexp_ref[...]
    head_sum = lambda z: _dot_exact_rhs(_dot_exact_rhs(z, hs), he)
    o = o_ref[...]
    v = v_ref[...]
    d = o - head_sum(o) * (1.0 / RWKV_N)
    var = head_sum(d * d) * (1.0 / RWKV_N)
    on = d * lax.rsqrt(var + GN_EPS) * gnw_ref[...] + gnb_ref[...]
    on = on + head_sum(r_ref[...] * k_ref[...] * rk_ref[...]) * v
    y = (on * gg_ref[...]).astype(bf16)
    out_ref[...] = x_ref[...] + jnp.dot(y, wo_ref[...], preferred_element_type=f32)


def rwkv_layer_fused(x, norm_g, shift_prev, s0, v_first, vres, mu, wr, wk, wv, wo, w0, w1, w2, a0, a1, a2,
                     g1, g2, k_k, k_a, r_k, gn_w, gn_b):
    n, t, d = x.shape
    m = n * t
    tm = RW_TM
    assert m % tm == 0 and (t % tm == 0 or tm % t == 0)
    x2 = x.reshape(m, d)
    row = lambda z: z.reshape(1, d).astype(f32)
    cb = lambda z: z.astype(bf16)
    hd = lax.broadcasted_iota(jnp.int32, (d, HEAD_LANES), 0) // RWKV_N
    hsum = (hd == lax.broadcasted_iota(jnp.int32, (d, HEAD_LANES), 1)).astype(bf16)
    hexp = hsum.T
    tile = pl.BlockSpec((tm, d), lambda i: (i, 0))
    full = lambda z: pl.BlockSpec(z.shape, lambda i: (0,) * z.ndim)
    if t % tm == 0:
        tps = t // tm
        fp = shift_prev.reshape(n, 1, d).astype(f32)
        fp_spec = pl.BlockSpec((None, 1, d), lambda i: (i // tps, 0, 0))
    else:
        fp = shift_prev.astype(f32)
        fp_spec = pl.BlockSpec((tm // t, d), lambda i: (i, 0))
    xprev_spec = pl.BlockSpec((8, d), lambda i: (jnp.maximum(i * (tm // 8) - 1, 0), 0))
    mu8 = jnp.pad(mu.astype(f32), ((0, 2), (0, 0)))
    consts = [row(norm_g), mu8, cb(wr), cb(wk), cb(wv), row(w0), cb(w1), cb(w2), row(a0), cb(a1), cb(a2),
              cb(g1), cb(g2), row(k_k), row(k_a), hsum, hexp]
    args = [x2, x2, fp] + consts
    specs = [tile, xprev_spec, fp_spec] + [full(c) for c in consts]
    if vres is not None:
        v0, v1, v2 = vres
        extra = [row(v0), cb(v1), cb(v2)]
        args += extra + [v_first.reshape(m, d)]
        specs += [full(c) for c in extra] + [tile]
    cp = pltpu.CompilerParams(dimension_semantics=("parallel",), vmem_limit_bytes=56 * 1024 * 1024)
    r, k, v, lw, kk, a, gg = pl.pallas_call(
        functools.partial(_rwkv_pre_kernel, seq_len=t, has_vres=vres is not None),
        grid=(m // tm,), in_specs=specs, out_specs=[tile] * 7,
        out_shape=[jax.ShapeDtypeStruct((m, d), f32)] * 7, compiler_params=cp, name="rwkv_pre",
    )(*args)
    if vres is None:
        v_first = v.reshape(n, t, d)
    tp = -(-t // WKV_CHUNK) * WKV_CHUNK
    padt = lambda z: jnp.pad(z.reshape(n, t, d), ((0, 0), (0, tp - t), (0, 0)))
    o, s = wkv_chunked(padt(r), padt(lw), padt(k), padt(v), padt(kk), padt(a), s0.astype(f32))
    o = o[:, :t].reshape(m, d)
    post_consts = [row(gn_w), row(gn_b), r_k.reshape(1, d).astype(f32), hsum, hexp, cb(wo)]
    x_new = pl.pallas_call(
        _rwkv_post_kernel, grid=(m // tm,),
        in_specs=[tile] * 6 + [full(c) for c in post_consts], out_specs=tile,
        out_shape=jax.ShapeDtypeStruct((m, d), f32), compiler_params=cp, name="rwkv_post",
    )(o, r, k, v, gg, x2, *post_consts)
    h_last = rmsnorm(x[:, -1], norm_g)
    return x_new.reshape(n, t, d), v_first, s, h_last


def rwkv_layer(h, shift_prev, s0, v_first, vres, mu, wr, wk, wv, wo, w0, w1, w2, a0, a1, a2,
               g1, g2, k_k, k_a, r_k, gn_w, gn_b):
    f32 = jnp.float32
    b, t, d = h.shape
    prev = jnp.concatenate([shift_prev[:, None, :].astype(h.dtype), h[:, :-1]], axis=1)
    xx = prev - h
    xr, xw, xk, xv, xa, xg = [h + xx * mu[j] for j in range(6)]
    r = _mm(xr, wr)
    k = _mm(xk, wk)
    v = _mm(xv, wv)
    w_log = -jax.nn.softplus(-(w0 + jnp.tanh(xw @ w1) @ w2).astype(f32)) - 0.5
    decay = jnp.exp(-jnp.exp(w_log))
    if vres is None:
        v_first = v
    else:
        v0, v1, v2 = vres
        v = v + (v_first - v) * jax.nn.sigmoid(v0 + (xv @ v1) @ v2)
    a = jax.nn.sigmoid((a0 + (xa @ a1) @ a2).astype(f32))
    g = jax.nn.sigmoid(xg @ g1) @ g2

    def heads(z):
        return z.reshape(b, t, RWKV_HEADS, RWKV_N).astype(f32)

    kk = heads(k * k_k)
    kk = kk / jnp.maximum(jnp.sqrt(jnp.sum(kk * kk, axis=-1, keepdims=True)), 1e-12)
    k = k.astype(f32) * (1.0 + (a - 1.0) * k_a.astype(f32))
    rh, kh, vh, ah, dh = heads(r), heads(k), heads(v), heads(a), heads(decay)
    tp = -(-t // WKV_CHUNK) * WKV_CHUNK
    padt = lambda z: jnp.pad(z, ((0, 0), (0, tp - t), (0, 0)))
    o, s = wkv_chunked(padt(r), padt(-jnp.exp(w_log)), padt(k), padt(v), padt(kk.reshape(b, t, d)),
                       padt(a), s0.astype(f32))
    o = o[:, :t].reshape(b, t, RWKV_HEADS, RWKV_N)
    mean = jnp.mean(o, axis=-1, keepdims=True)
    var = jnp.mean(jnp.square(o - mean), axis=-1, keepdims=True)
    o = ((o - mean) * lax.rsqrt(var + GN_EPS) * gn_w.reshape(RWKV_HEADS, RWKV_N).astype(f32)
         + gn_b.reshape(RWKV_HEADS, RWKV_N).astype(f32))
    o = o + jnp.sum(rh * kh * r_k.astype(f32), axis=-1, keepdims=True) * vh
    y = _mm((o.reshape(b, t, d) * g.astype(f32)).astype(h.dtype), wo)
    return y, v_first, s, h[:, -1]


def hier_moe(h, wc, bc, wf, bf, wg, wu, wd):
    f32 = jnp.float32
    hp = lax.Precision.HIGHEST
    lc = jnp.dot(h, wc, precision=hp).astype(f32) + bc.astype(f32)
    g_idx = jnp.argmax(lc, axis=-1)
    g_w = jnp.max(jax.nn.softmax(lc, axis=-1), axis=-1)
    g_hot = jax.nn.one_hot(g_idx, MOE_GROUPS, dtype=f32)
    lf = (jnp.dot(h, wf, precision=hp).astype(f32) + bf.astype(f32)).reshape(h.shape[:-1] + (MOE_GROUPS, MOE_EPG))
    lf_sel = jnp.einsum('btge,btg->bte', lf, g_hot)
    top_v, top_i = lax.top_k(lf_sel, MOE_TOPK)
    top_w = jax.nn.softmax(top_v, axis=-1) * g_w[..., None]
    e_id = g_idx[..., None] * MOE_EPG + top_i
    gate = jnp.einsum('btke,btk->bte', jax.nn.one_hot(e_id, N_EXPERTS, dtype=f32), top_w)
    hg = jnp.einsum('btd,edf->btef', h, wg)
    hu = jnp.einsum('btd,edf->btef', h, wu)
    act = (jax.nn.silu(hg) * hu * gate[..., None].astype(h.dtype)).astype(h.dtype)
    return jnp.einsum('btef,efd->btd', act, wd)


def kernel(x_prompt, x_sample, cache_nsa_kv, cache_win_kv, state_pool, state_wkv, state_shift,
           page_table, norm_mix, norm_ffn, ab_w_in, ab_w_out, ab_q_norm, ab_k_norm, cmp_pos_w,
           cmp_phi, pool_w, pool_scale, rw_mu, rw_wr, rw_wk, rw_wv, rw_wo, rw_w0, rw_w1, rw_w2,
           rw_a0, rw_a1, rw_a2, rw_v0, rw_v1, rw_v2, rw_g1, rw_g2, rw_kk, rw_ka, rw_rk, rw_gn_w,
           rw_gn_b, moe_wc, moe_bc, moe_wf, moe_bf, moe_wg, moe_wu, moe_wd):
    xp, xs = x_prompt, x_sample
    vf_p, vf_s = None, None
    nsa_p, nsa_s, win_p, win_s, pool_p, pool_s = [], [], [], [], [], []
    wkv_p, wkv_s, sh_p, sh_s = [], [], [], []
    for l in range(DEPTH):
        if l % 2 == 0:
            i = l // 2
            wts = (ab_w_in[i], ab_w_out[i], ab_q_norm[i], ab_k_norm[i], cmp_pos_w[i], cmp_phi[i],
                   pool_w[i], pool_scale[i])
            xp, r_p, w_p, h_p = ab_layer_prompt(xp, norm_mix[l], *wts)
            xs, r_s, w_s, h_s = ab_layer_sample(xs, norm_mix[l], cache_nsa_kv[i], page_table,
                                                cache_win_kv[i], state_pool[i], *wts)
            nsa_p.append(r_p)
            nsa_s.append(r_s)
            win_p.append(w_p)
            win_s.append(w_s)
            pool_p.append(h_p)
            pool_s.append(h_s)
        else:
            j = l // 2
            vres = None if j == 0 else (rw_v0[j - 1], rw_v1[j - 1], rw_v2[j - 1])
            wts = (rw_mu[j], rw_wr[j], rw_wk[j], rw_wv[j], rw_wo[j], rw_w0[j], rw_w1[j], rw_w2[j],
                   rw_a0[j], rw_a1[j], rw_a2[j], rw_g1[j], rw_g2[j], rw_kk[j], rw_ka[j], rw_rk[j],
                   rw_gn_w[j], rw_gn_b[j])
            bp = xp.shape[0]
            zero_shift = jnp.zeros((bp, D_MODEL), xp.dtype)
            zero_state = jnp.zeros((bp, RWKV_HEADS, RWKV_N, RWKV_N), jnp.float32)
            xp, vf_p, s_p, shp = rwkv_layer_fused(xp, norm_mix[l], zero_shift, zero_state, vf_p, vres, *wts)
            xs, vf_s, s_s, shs = rwkv_layer_fused(xs, norm_mix[l], state_shift[j], state_wkv[j], vf_s, vres, *wts)
            wkv_p.append(s_p)
            wkv_s.append(s_s)
            sh_p.append(shp)
            sh_s.append(shs)
        prep =moe_prep(norm_ffn[l], moe_wc[l], moe_bc[l], moe_wf[l], moe_bf[l], moe_wg[l], moe_wu[l], moe_wd[l])
        xp = moe_residual(xp, prep)
        xs = moe_residual(xs, prep)
    return (xp, xs, jnp.stack(nsa_p), jnp.stack(nsa_s), jnp.stack(win_p), jnp.stack(win_s),
            jnp.stack(pool_p), jnp.stack(pool_s), jnp.stack(wkv_p), jnp.stack(wkv_s),
            jnp.stack(sh_p), jnp.stack(sh_s))
```

```python
import functools

import jax
import jax.numpy as jnp
from jax import lax
from jax.experimental import pallas as pl
from jax.experimental.pallas import tpu as pltpu


def _mm_kernel(x_ref, w_ref, o_ref):
    o_ref[...] = jnp.dot(x_ref[...].astype(jnp.bfloat16), w_ref[...],
                         preferred_element_type=jnp.float32)


def _mm(x, w):
    lead = x.shape[:-1]
    k = x.shape[-1]
    n = w.shape[1]
    x2 = x.reshape(-1, k)
    m = x2.shape[0]
    npad = -(-n // 128) * 128
    wb = w.astype(jnp.bfloat16)
    if npad != n:
        wb = jnp.pad(wb, ((0, 0), (0, npad - n)))
    tn = npad
    for cand in (512, 640, 384, 256, 128):
        if npad % cand == 0:
            tn = cand
            break
    tm = 512 if m % 512 == 0 else m
    out = pl.pallas_call(
        _mm_kernel,
        grid=(m // tm, npad // tn),
        in_specs=[pl.BlockSpec((tm, k), lambda i, j: (i, 0)),
                  pl.BlockSpec((k, tn), lambda i, j: (0, j))],
        out_specs=pl.BlockSpec((tm, tn), lambda i, j: (i, j)),
        out_shape=jax.ShapeDtypeStruct((m, npad), jnp.float32),
        name="mm",
    )(x2, wb)
    return out[:, :n].reshape(lead + (n,))


f32 = jnp.float32
bf16 = jnp.bfloat16
WKV_CHUNK = 64
WKV_PAIRS = 8
WKV_SEQS = 2
WKV_PASSES = 1
WKV_GRAM_PASSES = 3


def _split(x):
    hi = x.astype(bf16)
    lo = (x - hi.astype(f32)).astype(bf16)
    return hi, lo


def _mmul(a, b, passes, nt=False):
    dn = (((1,), (1,)), ((), ())) if nt else (((1,), (0,)), ((), ()))
    d = lambda x, y: lax.dot_general(x, y, dn, preferred_element_type=f32)
    if passes == 1:
        return d(a.astype(bf16), b.astype(bf16))
    ah, al = _split(a)
    bh, bl = _split(b)
    return d(ah, bh) + (d(ah, bl) + d(al, bh))


def _wkv_kernel(r_ref, lw_ref, k_ref, v_ref, kk_ref, a_ref, s0_ref, o_ref, sT_ref, st_scr, *, passes):
    C = WKV_CHUNK
    nb = r_ref.shape[0]
    c = pl.program_id(1)
    nc = pl.num_programs(1)
    row = lax.broadcasted_iota(jnp.int32, (2 * C, 2 * C), 0)
    col = lax.broadcasted_iota(jnp.int32, (2 * C, 2 * C), 1)
    bd = (row < C) == (col < C)
    strict = bd & ((row % C) > (col % C))
    incl = bd & ((row % C) >= (col % C))
    eye = (row == col).astype(f32)
    lane_s = col < C
    m1 = lax.broadcasted_iota(jnp.int32, (C, 2 * C), 1) < C
    tri = (lax.broadcasted_iota(jnp.int32, (C, C), 0)
           >= lax.broadcasted_iota(jnp.int32, (C, C), 1)).astype(bf16)

    @pl.when(c == 0)
    def _():
        z = jnp.zeros((C, C), f32)
        for p in range(nb * WKV_PAIRS):
            s1 = s0_ref[p // WKV_PAIRS, 2 * (p % WKV_PAIRS)]
            s2 = s0_ref[p // WKV_PAIRS, 2 * (p % WKV_PAIRS) + 1]
            st_scr[p] =jnp.concatenate([jnp.concatenate([s1, z], axis=1),
                                         jnp.concatenate([z, s2], axis=1)], axis=0)

    def stack2(x):
        return jnp.concatenate([jnp.where(m1, x, 0.0), jnp.where(m1, 0.0, x)], axis=0)

    dd = lambda x, y: jnp.dot(x, y, preferred_element_type=f32)
    pairs = range(nb * WKV_PAIRS)
    sq = [p // WKV_PAIRS for p in pairs]
    sls = [slice((p % WKV_PAIRS) * 2 * C, (p % WKV_PAIRS + 1) * 2 * C) for p in pairs]

    def prep(p):
        sl = sls[p]
        lw = lw_ref[sq[p], :, sl]
        kk = kk_ref[sq[p], :, sl]
        h1 = lw.astype(bf16)
        r1 = lw - h1.astype(f32)
        h2 = r1.astype(bf16)
        h3 = (r1 - h2.astype(f32)).astype(bf16)
        cw = dd(tri, h1) + (dd(tri, h2) + dd(tri, h3))
        cwC = cw[C - 1:C, :]
        b = kk * a_ref[sq[p], :, sl]
        k = k_ref[sq[p], :, sl]
        At = -kk * jnp.exp(cw - lw)
        Rt = r_ref[sq[p], :, sl] * jnp.exp(cw)
        einv = jnp.exp(-cw)
        efut = jnp.exp(cwC - cw)
        X = jnp.concatenate([stack2(At), stack2(Rt)], axis=0)
        Y = jnp.concatenate([b * einv, k * einv], axis=0)
        AR = jnp.concatenate([At, Rt], axis=0)
        BK = jnp.concatenate([b * efut, k * efut], axis=0)
        return X, Y, AR, BK, jnp.exp(cwC)

    pre = [prep(p) for p in pairs]
    G = [_mmul(pre[p][0], pre[p][1], WKV_GRAM_PASSES, nt=True) for p in pairs]
    ARS = [_mmul(pre[p][2], st_scr[p], passes, nt=True) for p in pairs]
    L, Mak, Mrb, Mrk = [], [], [], []
    for p in pairs:
        GA = G[p][0:2 * C]
        GR = G[p][2 * C:4 * C]
        GAr = pltpu.roll(GA, C, axis=1)
        GRr = pltpu.roll(GR, C, axis=1)
        L.append(jnp.where(strict, jnp.where(lane_s, GA, GAr), 0.0))
        Mak.append(jnp.where(strict, jnp.where(lane_s, GAr, GA), 0.0))
        Mrb.append(jnp.where(incl, jnp.where(lane_s, GR, GRr), 0.0))
        Mrk.append(jnp.where(incl, jnp.where(lane_s, GRr, GR), 0.0))
    Vs = [stack2(v_ref[sq[p], :, sls[p]]) for p in pairs]
    Xs = [stack2(ARS[p][0:C]) + _mmul(Mak[p], Vs[p], passes) for p in pairs]
    OV = [_mmul(Mrk[p], Vs[p], passes) for p in pairs]
    P = [eye + L[p] for p in pairs]
    Q = L
    for _ in range(5):
        Q = [_mmul(Q[p], Q[p], passes) for p in pairs]
        P = [P[p] + _mmul(Q[p], P[p], passes) for p in pairs]
    Us = [_mmul(P[p], Xs[p], passes) for p in pairs]
    Os = [_mmul(Mrb[p], Us[p], passes) + OV[p] for p in pairs]
    for p in pairs:
        o_ref[sq[p], :, sls[p]] = ARS[p][C:2 * C] + Os[p][0:C] + Os[p][C:2 * C]
    for p in pairs:
        U = Us[p][0:C] + Us[p][C:2 * C]
        UV = jnp.concatenate([U, v_ref[sq[p], :, sls[p]]], axis=0)
        dS = _mmul(UV.T, pre[p][3], passes)
        st_scr[p] = st_scr[p] * pre[p][4] + jnp.where(bd, dS, 0.0)

    @pl.when(c == nc - 1)
    def _():
        for p in pairs:
            Snew = st_scr[p]
            sT_ref[sq[p], 2 * (p % WKV_PAIRS)] = Snew[0:C, 0:C]
            sT_ref[sq[p], 2 * (p % WKV_PAIRS) + 1] = Snew[C:2 * C, C:2 * C]


def wkv_chunked(r, lw, k, v, kk, a, s0):
    B, T, D = r.shape
    H = D // 64
    C = WKV_CHUNK
    nb = WKV_SEQS
    assert T % C == 0 and D == WKV_PAIRS * 2 * C and B % nb == 0
    blk = pl.BlockSpec((nb, C, D), lambda b, c: (b, c, 0))
    sblk = pl.BlockSpec((nb, H, 64, 64), lambda b, c: (b, 0, 0, 0))
    return pl.pallas_call(
        functools.partial(_wkv_kernel, passes=WKV_PASSES),
        grid=(B // nb, T // C),
        in_specs=[blk] * 6 + [sblk],
        out_specs=[blk, sblk],
        out_shape=[jax.ShapeDtypeStruct((B, T, D), f32), jax.ShapeDtypeStruct((B, H, 64, 64), f32)],
        scratch_shapes=[pltpu.VMEM((nb * WKV_PAIRS, 2 * C, 2 * C), f32)],
        compiler_params=pltpu.CompilerParams(dimension_semantics=("parallel", "arbitrary")),
        name="wkv7_chunked",
    )(r, lw, k, v, kk, a, s0)


D_MODEL = 1024
BATCH = 4
SEQ = 4096
DEPTH = 4
DEC_BATCH = 128
DEC_SEQ = 8
PAST_LEN = 2048
PAGE_SIZE = 128

N_NSA_LAYERS = (DEPTH + 1) // 2
N_RWKV_LAYERS = DEPTH // 2
N_VRES = N_RWKV_LAYERS - 1

POOL_DIM = D_MODEL // 2
POOL_WINDOWS = (2, 4, 8, 16)
POOL_GROUPS = len(POOL_WINDOWS)
POOL_GDIM = POOL_DIM // POOL_GROUPS
POOL_HIST = max(POOL_WINDOWS) - 1

HEAD_DIM = 64
NSA_HEADS = (D_MODEL // 2) // HEAD_DIM
NSA_KV_HEADS = 2
NSA_GQ = NSA_HEADS // NSA_KV_HEADS
NSA_DIM = NSA_HEADS * HEAD_DIM
CMP_STRIDE = 16
CMP_LEN = 2 * CMP_STRIDE
SLC_LEN = 64
N_SEL = 16
WINDOW = 512
Q_BLOCK = 128
ROPE_DIM = HEAD_DIM // 4
ROPE_THETA = 500000.0
MIX_DIM = POOL_DIM + NSA_DIM
KV_COLS = 6 * NSA_KV_HEADS * HEAD_DIM
IN_COLS = POOL_DIM + NSA_DIM + KV_COLS + 3 * NSA_HEADS

RWKV_N = 64
RWKV_HEADS = D_MODEL // RWKV_N
LORA_W = 64
LORA_A = 64
LORA_V = 32
LORA_G = 128
GN_EPS = 64e-5

MOE_GROUPS = 4
MOE_EPG = 4
N_EXPERTS = MOE_GROUPS * MOE_EPG
MOE_TOPK = 2
D_FF_E = 256

RMS_EPS = 1e-6
NEG_INF = -1e30
RES_SCALE = (2 * DEPTH) ** -0.5

SEL_TK = 512


def _nsa_prompt_kernel(q_ref, ql_ref, g_ref, kc_ref, kcl_ref, vct_ref, ks_ref, vst_ref, kw_ref, vwt_ref,
                       o_ref, score_scr, sel_scr, *, n_cmp):
    QB = Q_BLOCK
    GQ = NSA_GQ * QB
    i = pl.program_id(2)
    s0 = i * QB
    qT = q_ref[...]
    posq = s0 + lax.broadcasted_iota(jnp.int32, (1, GQ), 1) % QB
    dd = lambda x, y: jnp.dot(x, y, preferred_element_type=f32)

    ncp = kc_ref.shape[0]
    cidx = lax.broadcasted_iota(jnp.int32, (ncp, 1), 0)
    mask_c = cidx * CMP_STRIDE + (CMP_LEN - 1) <= posq
    sc = dd(kc_ref[...], qT) + (dd(kc_ref[...], ql_ref[...]) + dd(kcl_ref[...], qT))
    sc = jnp.where(mask_c, sc, NEG_INF)
    pe = jnp.exp(sc - jnp.max(sc, axis=0, keepdims=True))
    pc = jnp.where(mask_c, pe / jnp.sum(pe, axis=0, keepdims=True), 0.0)
    o_c = dd(vct_ref[...], pc.astype(bf16))

    imp = (pc[:, 0:QB] + pc[:, QB:2 * QB]) + (pc[:, 2 * QB:3 * QB] + pc[:, 3 * QB:4 * QB])
    n_slc = score_scr.shape[0]
    per = SLC_LEN // CMP_STRIDE
    nn = lax.broadcasted_iota(jnp.int32, (n_slc, ncp), 0) * per
    cc = lax.broadcasted_iota(jnp.int32, (n_slc, ncp), 1)
    mt = (0.5 * ((cc >= nn) & (cc < nn + per)).astype(f32)
          + 0.5 * ((cc + 1 >= nn) & (cc + 1 < nn + per)).astype(f32)).astype(bf16)
    i1 = imp.astype(bf16)
    r1 = imp - i1.astype(f32)
    i2 = r1.astype(bf16)
    i3 = (r1 - i2.astype(f32)).astype(bf16)
    imp_blk = dd(mt, i1) + (dd(mt, i2) + dd(mt, i3))
    nidx = lax.broadcasted_iota(jnp.int32, (n_slc, 1), 0)
    cur = (s0 + lax.broadcasted_iota(jnp.int32, (1, QB), 1)) // SLC_LEN
    forced = (nidx == 0) | (nidx == cur) | (nidx == cur - 1)
    score = jnp.where(nidx > cur, -1.0, jnp.where(forced, 1e6, imp_blk))
    score_scr[...] = score
    rank = jnp.zeros((n_slc, QB), jnp.int32)
    for m in range(n_slc):
        sm = score_scr[m:m + 1, :]
        tie = (nidx > m).astype(jnp.int32)
        rank = rank + jnp.where(sm > score, 1, jnp.where(sm == score, tie, 0))
    bias = jnp.where(rank < min(N_SEL, n_slc), 0.0, NEG_INF)
    sel_scr[...] = jnp.concatenate([bias] * NSA_GQ, axis=1)

    def online(carry, s, vt_blk):
        m, l, acc = carry
        m_new = jnp.maximum(m, jnp.max(s, axis=0, keepdims=True))
        alpha = jnp.exp(m - m_new)
        p = jnp.exp(s - m_new)
        l = alpha * l + jnp.sum(p, axis=0, keepdims=True)
        acc = alpha * acc + dd(vt_blk, p.astype(bf16))
        return m_new, l, acc

    init = (jnp.full((1, GQ), NEG_INF, f32), jnp.zeros((1, GQ), f32), jnp.zeros((HEAD_DIM, GQ), f32))

    bpt = SEL_TK // SLC_LEN

    def sel_scores(kt):
        k0 = pl.multiple_of(kt * SEL_TK, SEL_TK)
        rows = sel_scr[pl.ds(pl.multiple_of(kt * bpt, bpt), bpt), :]
        blk = jnp.concatenate([jnp.broadcast_to(rows[j:j + 1, :], (SLC_LEN, GQ)) for j in range(bpt)], axis=0)
        return k0, dd(ks_ref[pl.ds(k0, SEL_TK), :], qT) + blk

    def sel_body(kt, carry):
        k0, s = sel_scores(kt)
        return online(carry, s, vst_ref[:, pl.ds(k0, SEL_TK)])

    n_full = s0 // SEL_TK
    carry = lax.fori_loop(0, n_full, sel_body, init)
    k0, s = sel_scores(n_full)
    kpos = k0 + lax.broadcasted_iota(jnp.int32, (SEL_TK, 1), 0)
    _, l_s, acc_s = online(carry, jnp.where(kpos <= posq, s, NEG_INF), vst_ref[:, pl.ds(k0, SEL_TK)])

    nwt = WINDOW // QB
    carry = init
    for j in range(nwt, -1, -1):
        k0 = s0 - WINDOW + j * QB
        k0c = pl.multiple_of(jnp.maximum(k0, 0), QB)
        s = dd(kw_ref[pl.ds(k0c, QB), :], qT)
        kpos = k0c + lax.broadcasted_iota(jnp.int32, (QB, 1), 0)
        if j == nwt:
            s = jnp.where(kpos <= posq, s, NEG_INF)
        else:
            if j == 0:
                s = jnp.where(posq - kpos < WINDOW, s, NEG_INF)
            s = s + jnp.where(k0 >= 0, 0.0, NEG_INF)
        carry = online(carry, s, vwt_ref[:, pl.ds(k0c, QB)])
    _, l_w, acc_w = carry

    g = jax.nn.sigmoid(g_ref[...])
    o_ref[...] = g[0:1] * o_c + g[1:2] * (acc_s / l_s) + g[2:3] * (acc_w / l_w)


def nsa_prompt_pallas(q, gl, rows, win, kc, vc):
    B, T = q.shape[:2]
    KVH, G, D, QB = NSA_KV_HEADS, NSA_GQ, HEAD_DIM, Q_BLOCK
    assert T % SEL_TK == 0 and T % QB == 0
    nqb = T // QB
    n_cmp = kc.shape[1]
    ncp = -(-n_cmp // 128) * 128
    n_slc = T // SLC_LEN
    scale = D ** -0.5
    qT = (q * scale).reshape(B, nqb, QB, KVH, G, D).transpose(0, 3, 1, 5, 4, 2).reshape(B, KVH, nqb, D, G * QB)
    qT, qTl = _split(qT)
    gT = gl.reshape(B, nqb, QB, KVH, G, 3).transpose(0, 3, 1, 5, 4, 2).reshape(B, KVH, nqb, 3, G * QB).astype(f32)
    kcp, kcl = _split(jnp.pad(kc, ((0, 0), (0, ncp - n_cmp), (0, 0), (0, 0))).transpose(0, 2, 1, 3))
    vct = jnp.pad(vc, ((0, 0), (0, ncp - n_cmp), (0, 0), (0, 0))).transpose(0, 2, 3, 1).astype(bf16)
    ks = rows[:, :, 2].transpose(0, 2, 1, 3).astype(bf16)
    vst = rows[:, :, 3].transpose(0, 2, 3, 1).astype(bf16)
    kw = win[:, :, 0].transpose(0, 2, 1, 3).astype(bf16)
    vwt = win[:, :, 1].transpose(0, 2, 3, 1).astype(bf16)
    bh = lambda *shape: pl.BlockSpec((None, None) + shape, lambda b, h, i: (b, h) + (0,) * len(shape))
    bhi = lambda *shape: pl.BlockSpec((None, None, None) + shape, lambda b, h, i: (b, h, i) + (0,) * len(shape))
    oT = pl.pallas_call(
        functools.partial(_nsa_prompt_kernel, n_cmp=n_cmp),
        grid=(B, KVH, nqb),
        in_specs=[bhi(D, G * QB), bhi(D, G * QB), bhi(3, G * QB), bh(ncp, D), bh(ncp, D), bh(D, ncp),
                  bh(T, D), bh(D, T), bh(T, D), bh(D, T)],
        out_specs=bhi(D, G * QB),
        out_shape=jax.ShapeDtypeStruct((B, KVH, nqb, D, G * QB), f32),
        scratch_shapes=[pltpu.VMEM((n_slc, QB), f32), pltpu.VMEM((n_slc, G * QB), f32)],
        compiler_params=pltpu.CompilerParams(dimension_semantics=("parallel", "parallel", "arbitrary"),
                                             vmem_limit_bytes=48 * 1024 * 1024),
        name="nsa_prompt",
    )(qT, qTl, gT, kcp, kcl, vct, ks, vst, kw, vwt)
    return oT.reshape(B, KVH, nqb, D, G, QB).transpose(0, 2, 5, 1, 4, 3).reshape(B, T, KVH * G * D)


def _split3(x):
    h1 = x.astype(bf16)
    r1 = x - h1.astype(f32)
    h2 = r1.astype(bf16)
    return h1, h2, (r1 - h2.astype(f32)).astype(bf16)


def _dot_exact_rhs(x, m):
    d = lambda a: jnp.dot(a, m, preferred_element_type=f32)
    h1, h2, h3 = _split3(x)
    return d(h1) + (d(h2) + d(h3))


def _dot_exact_lhs(m, x):
    d = lambda a: jnp.dot(m, a, preferred_element_type=f32)
    h1, h2, h3 = _split3(x)
    return d(h1) + (d(h2) + d(h3))


def _nsa_sample_kernel(pt_ref, *refs, n_pages, ts, past_len):
    pages = refs[:n_pages]
    (new_ref, wbuf_ref, wnew_ref, qh_ref, ql_ref, gate_ref, wa_ref, wb_ref, phik_ref, phiv_ref,
     gain_ref, cos_ref, sin_ref, o_ref, ssel_scr, a_scr, b_scr, score_scr) = refs[n_pages:]
    P = PAGE_SIZE
    KV = NSA_KV_HEADS * HEAD_DIM
    NCOL = NSA_KV_HEADS * NSA_GQ * ts
    NQ = NSA_KV_HEADS * ts
    cpp = P // CMP_STRIDE
    n_chunk = (past_len + SLC_LEN) // CMP_STRIDE
    n_cmp = n_chunk - 1
    ncp = a_scr.shape[0]
    n_slc = (past_len + SLC_LEN) // SLC_LEN
    nsp = score_scr.shape[0]
    dd = lambda x, y: jnp.dot(x, y, preferred_element_type=f32)
    qh = qh_ref[...]
    ql = ql_ref[...]
    col = lax.broadcasted_iota(jnp.int32, (1, NCOL), 1)
    t_col = col % ts
    zpad = jnp.zeros((P - ts, 4 * KV), f32)
    new_tile = jnp.concatenate([new_ref[...], zpad], axis=0)

    wa = wa_ref[...]
    wb = wb_ref[...]
    a_scr[...] = jnp.zeros(a_scr.shape, f32)
    b_scr[...] = jnp.zeros(b_scr.shape, f32)
    for j in range(n_pages + 1):
        tile = pages[j][...] if j < n_pages else new_tile
        xc = tile[:, 0:2 * KV]
        a_scr[j * cpp:(j + 1) * cpp, :] = (xc * wa).reshape(cpp, CMP_STRIDE, 2 * KV).sum(axis=1)
        b_scr[j * cpp:(j + 1) * cpp, :] = (xc * wb).reshape(cpp, CMP_STRIDE, 2 * KV).sum(axis=1)
        ssel_scr[j * P:(j + 1) * P, :] = dd(tile[:, 2 * KV:3 * KV].astype(bf16), qh)

    mean = a_scr[...] + pltpu.roll(b_scr[...], ncp - 1, axis=0)
    kc = _mmul(mean[:, 0:KV], phik_ref[...], 3)
    vc = _mmul(mean[:, KV:2 * KV], phiv_ref[...], 3)
    r_i = lax.broadcasted_iota(jnp.int32, (KV, KV), 0)
    c_i = lax.broadcasted_iota(jnp.int32, (KV, KV), 1)
    same_head = (r_i // HEAD_DIM) == (c_i // HEAD_DIM)
    mavg = jnp.where(same_head, 1.0 / HEAD_DIM, 0.0).astype(bf16)
    kc = kc * lax.rsqrt(_dot_exact_rhs(kc * kc, mavg) + RMS_EPS) * gain_ref[...]
    half = ROPE_DIM // 2
    rd, cd = r_i % HEAD_DIM, c_i % HEAD_DIM
    rot = jnp.where(same_head & (cd < half) & (rd == cd + half), -1.0,
                    jnp.where(same_head & (cd >= half) & (cd < ROPE_DIM) & (rd == cd - half), 1.0, 0.0)).astype(bf16)
    kc = kc * cos_ref[...] + _dot_exact_rhs(kc, rot) * sin_ref[...]

    kch, kcl = _split(kc)
    sc = dd(kch, qh) + (dd(kch, ql) + dd(kcl, qh))
    cidx = lax.broadcasted_iota(jnp.int32, (ncp, 1), 0)
    mask_c = (cidx * CMP_STRIDE + (CMP_LEN - 1) <= past_len + t_col) & (cidx < n_cmp)
    sc = jnp.where(mask_c, sc, NEG_INF)
    pe = jnp.exp(sc - jnp.max(sc, axis=0, keepdims=True))
    pc = jnp.where(mask_c, pe / jnp.sum(pe, axis=0, keepdims=True), 0.0)
    o_c = dd(pc.T.astype(bf16), vc.astype(bf16))

    gr = lax.broadcasted_iota(jnp.int32, (NCOL, NQ), 0)
    gc = lax.broadcasted_iota(jnp.int32, (NCOL, NQ), 1)
    gsum = ((gr // (NSA_GQ * ts) == gc // ts) & (gr % ts == gc % ts)).astype(bf16)
    imp = _dot_exact_rhs(pc, gsum)
    per = SLC_LEN // CMP_STRIDE
    nn = lax.broadcasted_iota(jnp.int32, (nsp, ncp), 0) * per
    cc = lax.broadcasted_iota(jnp.int32, (nsp, ncp), 1)
    mt = (0.5 * ((cc >= nn) & (cc < nn + per)).astype(f32)
          + 0.5 * ((cc + 1 >= nn) & (cc + 1 < nn + per)).astype(f32)).astype(bf16)
    imp_blk = _dot_exact_lhs(mt, imp)
    nidx = lax.broadcasted_iota(jnp.int32, (nsp, 1), 0)
    cur = (past_len + lax.broadcasted_iota(jnp.int32, (1, NQ), 1) % ts) // SLC_LEN
    forced = (nidx == 0) | (nidx == cur) | (nidx == cur - 1)
    score = jnp.where(nidx >= n_slc, -2.0, jnp.where(nidx > cur, -1.0, jnp.where(forced, 1e6, imp_blk)))
    score_scr[...] = score
    rank = jnp.zeros((nsp, NQ), jnp.int32)
    for m in range(n_slc):
        sm = score_scr[m:m + 1, :]
        beats = (sm > score) | ((sm == score) & (nidx > m))
        rank = rank + beats.astype(jnp.int32)
    sel = (rank < min(N_SEL, n_slc)).astype(bf16)
    gr2 = lax.broadcasted_iota(jnp.int32, (NQ, NCOL), 0)
    gc2 = lax.broadcasted_iota(jnp.int32, (NQ, NCOL), 1)
    gexp = ((gc2 // (NSA_GQ * ts) == gr2 // ts) & (gc2 % ts == gr2 % ts)).astype(bf16)
    sel_c = dd(sel, gexp)

    def two_pass(n_tiles, score_tile, mask_tile, v_tile):
        m = jnp.full((1, NCOL), NEG_INF, f32)
        for j in range(n_tiles):
            m = jnp.maximum(m, jnp.max(jnp.where(mask_tile(j), score_tile(j), NEG_INF), axis=0, keepdims=True))
        num = jnp.zeros((NCOL, KV), f32)
        den = jnp.zeros((NCOL, KV), f32)
        ones = jnp.ones((P, KV), bf16)
        for j in range(n_tiles):
            p = jnp.where(mask_tile(j), jnp.exp(score_tile(j) - m), 0.0).T.astype(bf16)
            num = num + dd(p, v_tile(j))
            den = den + dd(p, ones)
        return num, den

    bpp = P // SLC_LEN
    row = lax.broadcasted_iota(jnp.int32, (P, 1), 0)

    def sel_mask(j):
        blk = jnp.concatenate([jnp.broadcast_to(sel_c[j * bpp + i:j * bpp + i + 1, :], (SLC_LEN, NCOL))
                               for i in range(bpp)], axis=0)
        return (blk > 0.5) & (j * P + row <= past_len + t_col)

    num_s, den_s = two_pass(
        n_pages + 1, lambda j: ssel_scr[j * P:(j + 1) * P, :], sel_mask,
        lambda j: (pages[j][:, 3 * KV:4 * KV] if j < n_pages else new_tile[:, 3 * KV:4 * KV]).astype(bf16))

    lb = wbuf_ref.shape[0]
    nwt = lb // P
    wnew = jnp.concatenate([wnew_ref[...], jnp.zeros((P - ts, 2 * KV), f32)], axis=0)

    def w_tile(j):
        return wbuf_ref[j * P:(j + 1) * P, :] if j < nwt else wnew

    def win_mask(j):
        pos_w = (past_len - lb + j * P + row) if j < nwt else (past_len + row)
        dq = past_len + t_col - pos_w
        return (dq >= 0) & (dq < WINDOW) & (pos_w >= 0) & ((row < ts) | (j < nwt))

    num_w, den_w = two_pass(
        nwt + 1, lambda j: dd(w_tile(j)[:, 0:KV].astype(bf16), qh), win_mask,
        lambda j: w_tile(j)[:, KV:2 * KV].astype(bf16))

    g = jax.nn.sigmoid(gate_ref[...])
    o_ref[...] = g[0] * o_c + g[1] * (num_s / den_s) + g[2] * (num_w / den_w)


def nsa_sample_pallas(q, gl, rows_new, win_new, pool_kv, page_table, win_buf, kc_w, phi, k_gain):
    B, ts = q.shape[:2]
    KVH, G, D, P = NSA_KV_HEADS, NSA_GQ, HEAD_DIM, PAGE_SIZE
    KV = KVH * D
    n_pages = page_table.shape[1]
    past_len = n_pages * P
    assert pool_kv.shape[1] == P and ts <= SLC_LEN and P % SLC_LEN == 0
    lb = win_buf.shape[1]
    assert lb % P == 0
    NCOL = KVH * G * ts
    n_chunk = (past_len + SLC_LEN) // CMP_STRIDE
    ncp = -(-n_chunk // 8) * 8
    n_slc = (past_len + SLC_LEN) // SLC_LEN
    nsp = -(-n_slc // 8) * 8
    qs = (q * D ** -0.5).reshape(B, ts, KVH, G, D).transpose(0, 2, 4, 3, 1).reshape(B, KVH, D, G * ts)
    z = jnp.zeros_like(qs[:, 0])
    qbd = jnp.concatenate([jnp.concatenate([qs[:, 0], z], axis=2), jnp.concatenate([z, qs[:, 1]], axis=2)], axis=1)
    qh, ql = _split(qbd)
    gate = gl.reshape(B, ts, KVH, G, 3).transpose(0, 4, 2, 3, 1).reshape(B, 3, NCOL, 1)
    gate = jnp.broadcast_to(gate, (B, 3, NCOL, KV)).astype(f32)
    w_lane = jnp.repeat(kc_w.reshape(2 * KVH, CMP_LEN), D, axis=0)
    reps = P // CMP_STRIDE
    wa = jnp.tile(w_lane[:, :CMP_STRIDE].T, (reps, 1)).astype(f32)
    wb = jnp.tile(w_lane[:, CMP_STRIDE:].T, (reps, 1)).astype(f32)
    zz = jnp.zeros((D, D), f32)
    bdiag = lambda m: jnp.concatenate([jnp.concatenate([m, zz], axis=1), jnp.concatenate([zz, m], axis=1)], axis=0)
    phik, phiv = bdiag(phi[0].astype(f32)), bdiag(phi[1].astype(f32))
    gain = jnp.tile(k_gain.astype(f32), KVH).reshape(1, KV)
    half = ROPE_DIM // 2
    inv = ROPE_THETA ** (-jnp.arange(half, dtype=f32) / half)
    cmp_end = (jnp.arange(ncp, dtype=jnp.int32) * CMP_STRIDE + (CMP_LEN - 1)).astype(f32)
    ang = cmp_end[:, None] * inv
    cos_h = jnp.concatenate([jnp.cos(ang), jnp.cos(ang), jnp.ones((ncp, D - ROPE_DIM), f32)], axis=1)
    sin_h = jnp.concatenate([jnp.sin(ang), jnp.sin(ang), jnp.zeros((ncp, D - ROPE_DIM), f32)], axis=1)
    cos_t, sin_t = jnp.tile(cos_h, (1, KVH)), jnp.tile(sin_h, (1, KVH))
    pool2 = pool_kv.reshape(pool_kv.shape[0], P, 4 * KV)
    new2 = rows_new.reshape(B, ts, 4 * KV).astype(f32)
    wbuf2 = win_buf.reshape(B, lb, 2 * KV)
    wnew2 = win_new.reshape(B, ts, 2 * KV).astype(f32)
    page_spec = lambda j: pl.BlockSpec((None, P, 4 * KV), lambda b, pt, j=j: (pt[b, j], 0, 0))
    per_b = lambda *s: pl.BlockSpec((None,) + s, lambda b, pt: (b,) + (0,) * len(s))
    const = lambda *s: pl.BlockSpec(s, lambda b, pt: (0,) * len(s))
    grid_spec = pltpu.PrefetchScalarGridSpec(
        num_scalar_prefetch=1, grid=(B,),
        in_specs=[page_spec(j) for j in range(n_pages)] + [
            per_b(ts, 4 * KV), per_b(lb, 2 * KV), per_b(ts, 2 * KV), per_b(KV, NCOL), per_b(KV, NCOL),
            per_b(3, NCOL, KV), const(P, 2 * KV), const(P, 2 * KV), const(KV, KV), const(KV, KV),
            const(1, KV), const(ncp, KV), const(ncp, KV)],
        out_specs=per_b(NCOL, KV),
        scratch_shapes=[pltpu.VMEM(((n_pages + 1) * P, NCOL), f32), pltpu.VMEM((ncp, 2 * KV), f32),
                        pltpu.VMEM((ncp, 2 * KV), f32), pltpu.VMEM((nsp, KVH * ts), f32)])
    out = pl.pallas_call(
        functools.partial(_nsa_sample_kernel, n_pages=n_pages, ts=ts, past_len=past_len),
        grid_spec=grid_spec,
        out_shape=jax.ShapeDtypeStruct((B, NCOL, KV), f32),
        compiler_params=pltpu.CompilerParams(dimension_semantics=("arbitrary",),
                                             vmem_limit_bytes=48 * 1024 * 1024),
        name="nsa_sample",
    )(page_table, *([pool2] * n_pages), new2, wbuf2, wnew2, qh, ql, gate, wa, wb, phik, phiv, gain, cos_t, sin_t)
    o4 = out.reshape(B, KVH, G, ts, KVH, D)
    o = jnp.stack([o4[:, 0, :, :, 0], o4[:, 1, :, :, 1]], axis=1)
    return o.transpose(0, 3, 1, 2, 4).reshape(B, ts, KVH * G * D)


MOE_TM = 512
ROUTER_LANES = 128


def _moe_kernel(x_ref, g_ref, wrh_ref, wrl_ref, br_ref, wg_ref, wu_ref, wd_ref, o_ref,
                h_scr, gate_scr, acc_scr):
    grp = pl.program_id(1)
    dd = lambda a, b: jnp.dot(a, b, preferred_element_type=f32)
    tm = x_ref.shape[0]
    lane = lax.broadcasted_iota(jnp.int32, (tm, ROUTER_LANES), 1).astype(f32)
    far = float(ROUTER_LANES)

    @pl.when(grp == 0)
    def _():
        x = x_ref[...]
        h = x * lax.rsqrt(jnp.mean(x * x, axis=-1, keepdims=True) + RMS_EPS) * g_ref[...]
        hh, hl = _split(h)
        h_scr[...] = hh
        logits = dd(hh, wrh_ref[...]) + (dd(hh, wrl_ref[...]) + dd(hl, wrh_ref[...])) + br_ref[...]
        is_c = lane < MOE_GROUPS
        lc = jnp.where(is_c, logits, NEG_INF)
        mc = jnp.max(lc, axis=1, keepdims=True)
        g_idx = jnp.min(jnp.where(lc == mc, lane, far), axis=1, keepdims=True)
        g_w = 1.0 / jnp.sum(jnp.where(is_c, jnp.exp(lc - mc), 0.0), axis=1, keepdims=True)
        lo = MOE_GROUPS + MOE_EPG * g_idx
        lf = jnp.where((lane >= lo) & (lane < lo + MOE_EPG), logits, NEG_INF)
        v1 = jnp.max(lf, axis=1, keepdims=True)
        i1 = jnp.min(jnp.where(lf == v1, lane, far), axis=1, keepdims=True)
        lf2 = jnp.where(lane == i1, NEG_INF, lf)
        v2 = jnp.max(lf2, axis=1, keepdims=True)
        i2 = jnp.min(jnp.where(lf2 == v2, lane, far), axis=1, keepdims=True)
        e21 = jnp.exp(v2 - v1)
        w1 = g_w / (1.0 + e21)
        gate_scr[...] = jnp.where(lane == i1, w1, jnp.where(lane == i2, e21 * w1, 0.0))
        acc_scr[...] = x

    h = h_scr[...]
    hg = dd(h, wg_ref[...])
    hu = dd(h, wu_ref[...])
    gate = gate_scr[...]
    first = (MOE_GROUPS + MOE_EPG * grp).astype(f32)
    cols = []
    for e in range(MOE_EPG):
        ge = jnp.sum(jnp.where(lane == first + e, gate, 0.0), axis=1, keepdims=True)
        sl = slice(e * D_FF_E, (e + 1) * D_FF_E)
        hge = hg[:, sl]
        cols.append((hge * jax.nn.sigmoid(hge) * hu[:, sl] * ge).astype(bf16))
    acc_scr[...] += dd(jnp.concatenate(cols, axis=1), wd_ref[...])

    @pl.when(grp == MOE_GROUPS - 1)
    def _():
        o_ref[...] = acc_scr[...]


def moe_prep(g, wc, bc, wf, bf, wg, wu, wd):
    d = wc.shape[0]
    pad = ROUTER_LANES - MOE_GROUPS - N_EXPERTS
    wr = jnp.pad(jnp.concatenate([wc, wf], axis=1).astype(f32), ((0, 0), (0, pad)))
    wrh, wrl = _split(wr)
    br = jnp.pad(jnp.concatenate([bc, bf]).astype(f32), (0, pad)).reshape(1, ROUTER_LANES)
    regroup = lambda w: (w.reshape(MOE_GROUPS, MOE_EPG, d, D_FF_E).transpose(0, 2, 1, 3)
                         .reshape(MOE_GROUPS, d, MOE_EPG * D_FF_E).astype(bf16))
    wdg = wd.reshape(MOE_GROUPS, MOE_EPG * D_FF_E, d).astype(bf16)
    return g.reshape(1, d).astype(f32), wrh, wrl, br, regroup(wg), regroup(wu), wdg


def moe_residual(x, prep):
    g, wrh, wrl, br, wgg, wug, wdg = prep
    shp = x.shape
    d = shp[-1]
    x2 = x.reshape(-1, d)
    m = x2.shape[0]
    tm = MOE_TM
    assert m % tm == 0
    gf = MOE_EPG * D_FF_E
    full = lambda r, c: pl.BlockSpec((r, c), lambda i, j: (0, 0))
    out = pl.pallas_call(
        _moe_kernel,
        grid=(m // tm, MOE_GROUPS),
        in_specs=[pl.BlockSpec((tm, d), lambda i, j: (i, 0)), full(1, d),
                  full(d, ROUTER_LANES), full(d, ROUTER_LANES), full(1, ROUTER_LANES),
                  pl.BlockSpec((None, d, gf), lambda i, j: (j, 0, 0)),
                  pl.BlockSpec((None, d, gf), lambda i, j: (j, 0, 0)),
                  pl.BlockSpec((None, gf, d), lambda i, j: (j, 0, 0))],
        out_specs=pl.BlockSpec((tm, d), lambda i, j: (i, 0)),
        out_shape=jax.ShapeDtypeStruct((m, d), f32),
        scratch_shapes=[pltpu.VMEM((tm, d), bf16), pltpu.VMEM((tm, ROUTER_LANES), f32),
                        pltpu.VMEM((tm, d), f32)],
        compiler_params=pltpu.CompilerParams(dimension_semantics=("parallel", "arbitrary"),
                                             vmem_limit_bytes=48 * 1024 * 1024),
        name="moe",
    )(x2, g, wrh, wrl, br, wgg, wug, wdg)
    return out.reshape(shp)


def rmsnorm(x, g):
    xf = x.astype(jnp.float32)
    y = xf * lax.rsqrt(jnp.mean(xf * xf, axis=-1, keepdims=True) + RMS_EPS)
    return (y * g.astype(jnp.float32)).astype(x.dtype)


def rope_partial(x, pos):
    half = ROPE_DIM // 2
    inv = ROPE_THETA ** (-jnp.arange(half, dtype=jnp.float32) / half)
    ang = pos.astype(jnp.float32)[:, None] * inv
    cos = jnp.cos(ang)[:, None, :]
    sin = jnp.sin(ang)[:, None, :]
    xf = x.astype(jnp.float32)
    x1 = xf[..., :half]
    x2 = xf[..., half:ROPE_DIM]
    out = jnp.concatenate([x1 * cos - x2 * sin, x2 * cos + x1 * sin, xf[..., ROPE_DIM:]], axis=-1)
    return out.astype(x.dtype)


def masked_softmax(s, mask):
    s = jnp.where(mask, s.astype(jnp.float32), NEG_INF)
    p = jax.nn.softmax(s, axis=-1)
    return jnp.where(mask, p, 0.0)


def pool_mix(u, hist, p0, w_grp, scale):
    b, t, _ = u.shape
    ext = jnp.concatenate([hist.astype(u.dtype), u], axis=1).astype(jnp.float32)
    cs = jnp.pad(jnp.cumsum(ext, axis=1), ((0, 0), (1, 0), (0, 0)))
    cnt_pos = p0 + jnp.arange(t, dtype=jnp.int32) + 1
    means = []
    for gi, w in enumerate(POOL_WINDOWS):
        c = cs[..., gi * POOL_GDIM:(gi + 1) * POOL_GDIM]
        win_sum = c[:, POOL_HIST + 1:POOL_HIST + 1 + t] - c[:, POOL_HIST + 1 - w:POOL_HIST + 1 - w + t]
        cnt = jnp.minimum(cnt_pos, w).astype(jnp.float32)[None, :, None]
        means.append(win_sum / cnt)
    mean = jnp.stack(means, axis=2)
    d = mean - u.reshape(b, t, POOL_GROUPS, POOL_GDIM).astype(jnp.float32)
    y = jnp.einsum('btgc,gcd->btgd', d, w_grp.astype(jnp.float32)).reshape(b, t, POOL_DIM)
    return (y * scale.astype(jnp.float32)).astype(u.dtype)


PROJ_TM = 512


def _norm_mm_kernel(x_ref, g_ref, w_ref, o_ref):
    x = x_ref[...]
    h = x * lax.rsqrt(jnp.mean(x * x, axis=-1, keepdims=True) + RMS_EPS) * g_ref[...]
    o_ref[...] = jnp.dot(h.astype(bf16), w_ref[...], preferred_element_type=f32)


def _norm_mm(x, g, w):
    lead, d = x.shape[:-1], x.shape[-1]
    n = w.shape[1]
    x2 = x.reshape(-1, d)
    m = x2.shape[0]
    npad = -(-n // 128) * 128
    wb = jnp.pad(w.astype(bf16), ((0, 0), (0, npad - n)))
    tm = PROJ_TM
    assert m % tm == 0
    out = pl.pallas_call(
        _norm_mm_kernel, grid=(m // tm,),
        in_specs=[pl.BlockSpec((tm, d), lambda i: (i, 0)), pl.BlockSpec((1, d), lambda i: (0, 0)),
                  pl.BlockSpec((d, npad), lambda i: (0, 0))],
        out_specs=pl.BlockSpec((tm, npad), lambda i: (i, 0)),
        out_shape=jax.ShapeDtypeStruct((m, npad), f32),
        compiler_params=pltpu.CompilerParams(dimension_semantics=("parallel",),
                                             vmem_limit_bytes=48 * 1024 * 1024),
        name="norm_mm",
    )(x2, g.reshape(1, d).astype(f32), wb)
    return out[:, :n].reshape(lead + (n,))


def _mix_out_kernel(a_ref, b_ref, wa_ref, wb_ref, x_ref, o_ref):
    dd = lambda p, q: jnp.dot(p.astype(bf16), q, preferred_element_type=f32)
    o_ref[...] = x_ref[...] + (dd(a_ref[...], wa_ref[...]) + dd(b_ref[...], wb_ref[...]))


def _mix_out(a, b, w, x):
    d = x.shape[-1]
    ka, kb = a.shape[-1], b.shape[-1]
    x2 = x.reshape(-1, d)
    m = x2.shape[0]
    tm = PROJ_TM
    assert m % tm == 0
    wbf = w.astype(bf16)
    tile = lambda c: pl.BlockSpec((tm, c), lambda i: (i, 0))
    full = lambda r, c: pl.BlockSpec((r, c), lambda i: (0, 0))
    out = pl.pallas_call(
        _mix_out_kernel, grid=(m // tm,),
        in_specs=[tile(ka), tile(kb), full(ka, d), full(kb, d), tile(d)], out_specs=tile(d),
        out_shape=jax.ShapeDtypeStruct((m, d), f32),
        compiler_params=pltpu.CompilerParams(dimension_semantics=("parallel",)),
        name="mix_out",
    )(a.reshape(m, ka), b.reshape(m, kb), wbf[:ka], wbf[ka:], x2)
    return out.reshape(x.shape)


def ab_features(x, norm_g, pos, w_in, q_norm, k_norm):
    b, t = x.shape[:2]
    u = _norm_mm(x, norm_g, w_in)
    off_kv = POOL_DIM + NSA_DIM
    pool_in = u[..., :POOL_DIM]
    q = u[..., POOL_DIM:off_kv].reshape(b, t, NSA_HEADS, HEAD_DIM)
    kv = u[..., off_kv:off_kv + KV_COLS].reshape(b, t, 6, NSA_KV_HEADS, HEAD_DIM)
    gl = u[..., off_kv + KV_COLS:].reshape(b, t, NSA_HEADS, 3)
    q = rope_partial(rmsnorm(q, q_norm), pos)
    k_slc = rope_partial(rmsnorm(kv[:, :, 2], k_norm[1]), pos)
    k_win = rope_partial(rmsnorm(kv[:, :, 4], k_norm[2]), pos)
    rows = jnp.stack([kv[:, :, 0], kv[:, :, 1], k_slc, kv[:, :, 3]], axis=2)
    win = jnp.stack([k_win, kv[:, :, 5]], axis=2)
    return pool_in, q, gl, rows, win


def compress_kv(k_rows, v_rows, pos_w, phi, k_gain):
    b, length = k_rows.shape[:2]
    n_chunk = length // CMP_STRIDE

    def weighted_block_mean(rows, w):
        ch = rows.reshape(b, n_chunk, CMP_STRIDE, NSA_KV_HEADS, HEAD_DIM)
        return (jnp.einsum('bnlhd,hl->bnhd', ch[:, :-1], w[:, :CMP_STRIDE])
                + jnp.einsum('bnlhd,hl->bnhd', ch[:, 1:], w[:, CMP_STRIDE:]))

    cmp_end = jnp.arange(n_chunk - 1, dtype=jnp.int32) * CMP_STRIDE + (CMP_LEN - 1)
    kc = jnp.einsum('bnhd,de->bnhe', weighted_block_mean(k_rows, pos_w[0]), phi[0])
    kc = rope_partial(rmsnorm(kc, k_gain), cmp_end)
    vc = jnp.einsum('bnhd,de->bnhe', weighted_block_mean(v_rows, pos_w[1]), phi[1])
    return kc, vc, cmp_end


def nsa_attend(q, pos_q, gl, kc, vc, cmp_end, ks, vs, kw, vw, pos_w):
    f32 = jnp.float32
    b, tq = q.shape[:2]
    qg = q.reshape(b, tq, NSA_KV_HEADS, NSA_GQ, HEAD_DIM)
    scale = HEAD_DIM ** -0.5
    s_c = jnp.einsum('bqhgd,bchd->bhgqc', qg, kc) * scale
    p_c = masked_softmax(s_c, cmp_end[None, :] <= pos_q[:, None])
    o_c = jnp.einsum('bhgqc,bchd->bqhgd', p_c, vc.astype(f32))
    imp = p_c.sum(axis=2)
    imp_chunk = 0.5 * (jnp.pad(imp, ((0, 0), (0, 0), (0, 0), (0, 1)))
                       + jnp.pad(imp, ((0, 0), (0, 0), (0, 0), (1, 0))))
    n_slc = ks.shape[1] // SLC_LEN
    imp_blk = imp_chunk.reshape(b, NSA_KV_HEADS, tq, n_slc, SLC_LEN // CMP_STRIDE).sum(-1)
    blk = jnp.arange(n_slc, dtype=jnp.int32)[None, :]
    cur = (pos_q // SLC_LEN)[:, None]
    forced = (blk == 0) | (blk == cur) | (blk == cur - 1)
    score = jnp.where(blk > cur, -1.0, jnp.where(forced, 1e6, imp_blk))
    n_sel = min(N_SEL, n_slc)
    _, idx = lax.top_k(score, n_sel)
    gather = jax.vmap(jax.vmap(lambda rows, i: rows[i]))
    ksb = ks.reshape(b, n_slc, SLC_LEN, NSA_KV_HEADS, HEAD_DIM).transpose(0, 3, 1, 2, 4)
    vsb = vs.reshape(b, n_slc, SLC_LEN, NSA_KV_HEADS, HEAD_DIM).transpose(0, 3, 1, 2, 4)
    kg = gather(ksb, idx)
    vg = gather(vsb, idx)
    kpos = idx[..., None] * SLC_LEN + jnp.arange(SLC_LEN, dtype=jnp.int32)
    n_keys = n_sel * SLC_LEN
    m_s = (kpos <= pos_q[None, None, :, None, None]).reshape(b, NSA_KV_HEADS, 1, tq, n_keys)
    s_s = jnp.einsum('bqhgd,bhqnld->bhgqnl', qg, kg).reshape(b, NSA_KV_HEADS, NSA_GQ, tq, n_keys) * scale
    p_s = masked_softmax(s_s, m_s)
    o_s = jnp.einsum('bhgqk,bhqkd->bqhgd', p_s,
                     vg.reshape(b, NSA_KV_HEADS, tq, n_keys, HEAD_DIM).astype(f32))
    s_w = jnp.einsum('bqhgd,bkhd->bhgqk', qg, kw) * scale
    dq = pos_q[:, None] - pos_w[None, :]
    m_w = (dq >= 0) & (dq < WINDOW) & (pos_w[None, :] >= 0)
    p_w = masked_softmax(s_w, m_w)
    o_w = jnp.einsum('bhgqk,bkhd->bqhgd', p_w, vw.astype(f32))
    g = jax.nn.sigmoid(gl.astype(f32)).reshape(b, tq, NSA_KV_HEADS, NSA_GQ, 3)
    o = g[..., 0:1] * o_c + g[..., 1:2] * o_s + g[..., 2:3] * o_w
    return o.reshape(b, tq, NSA_DIM)


def nsa_prompt(q, gl, rows, win, cmp_pos_w, cmp_phi, k_gain):
    b, t = q.shape[:2]
    kc, vc, cmp_end = compress_kv(rows[:, :, 0], rows[:, :, 1], cmp_pos_w, cmp_phi, k_gain)
    ks, vs = rows[:, :, 2], rows[:, :, 3]
    win_pad = jnp.pad(win, ((0, 0), (WINDOW, 0), (0, 0), (0, 0), (0, 0)))

    def block(i):
        s0 = i * Q_BLOCK
        qb = lax.dynamic_slice_in_dim(q, s0, Q_BLOCK, axis=1)
        gb = lax.dynamic_slice_in_dim(gl, s0, Q_BLOCK, axis=1)
        wb = lax.dynamic_slice_in_dim(win_pad, s0, WINDOW + Q_BLOCK, axis=1)
        pos_q = s0 + jnp.arange(Q_BLOCK, dtype=jnp.int32)
        pos_w = s0 - WINDOW + jnp.arange(WINDOW + Q_BLOCK, dtype=jnp.int32)
        return nsa_attend(qb, pos_q, gb, kc, vc, cmp_end, ks, vs, wb[:, :, 0], wb[:, :, 1], pos_w)

    out = lax.map(block, jnp.arange(t // Q_BLOCK, dtype=jnp.int32))
    return jnp.moveaxis(out, 0, 1).reshape(b, t, NSA_DIM)


def nsa_sample(q, gl, rows_new, win_new, pool_kv, page_table, win_buf, cmp_pos_w, cmp_phi, k_gain):
    bd, ts = q.shape[:2]
    past_len = page_table.shape[1] * pool_kv.shape[1]
    past = pool_kv[page_table].reshape(bd, past_len, 4, NSA_KV_HEADS, HEAD_DIM)
    rows = jnp.concatenate([past, rows_new.astype(past.dtype)], axis=1)
    pad = (-(past_len + ts)) % SLC_LEN
    rows = jnp.pad(rows, ((0, 0), (0, pad), (0, 0), (0, 0), (0, 0)))
    kc, vc, cmp_end = compress_kv(rows[:, :, 0], rows[:, :, 1], cmp_pos_w, cmp_phi, k_gain)
    lb = win_buf.shape[1]
    win = jnp.concatenate([win_buf, win_new.astype(win_buf.dtype)], axis=1)
    pos_q = past_len + jnp.arange(ts, dtype=jnp.int32)
    pos_w = past_len - lb + jnp.arange(lb + ts, dtype=jnp.int32)
    o = nsa_attend(q, pos_q, gl, kc, vc, cmp_end, rows[:, :, 2], rows[:, :, 3],
                   win[:, :, 0], win[:, :, 1], pos_w)
    keep = min(WINDOW, lb + ts)
    return o, win[:, lb + ts - keep:]


def ab_layer_prompt(x, norm_g, w_in, w_out, q_norm, k_norm, cmp_pos_w, cmp_phi, pool_w, pool_scale):
    b, t = x.shape[:2]
    pos = jnp.arange(t, dtype=jnp.int32)
    pool_in, q, gl, rows, win = ab_features(x, norm_g, pos, w_in, q_norm, k_norm)
    hist0 = jnp.zeros((b, POOL_HIST, POOL_DIM), x.dtype)
    pool_out = pool_mix(pool_in, hist0, 0, pool_w, pool_scale)
    kc, vc, _ = compress_kv(rows[:, :, 0], rows[:, :, 1], cmp_pos_w, cmp_phi, k_norm[0])
    nsa_out = nsa_prompt_pallas(q, gl, rows, win, kc, vc)
    x_new = _mix_out(pool_out, nsa_out, w_out, x)
    keep = min(WINDOW, t)
    return x_new, rows, win[:, t - keep:], pool_in[:, t - POOL_HIST:]


def ab_layer_sample(x, norm_g, pool_kv, page_table, win_buf, pool_hist, w_in, w_out, q_norm, k_norm,
                    cmp_pos_w, cmp_phi, pool_w, pool_scale):
    ts = x.shape[1]
    past_len = page_table.shape[1] * pool_kv.shape[1]
    pos = past_len + jnp.arange(ts, dtype=jnp.int32)
    pool_in, q, gl, rows, win = ab_features(x, norm_g, pos, w_in, q_norm, k_norm)
    pool_out = pool_mix(pool_in, pool_hist, past_len, pool_w, pool_scale)
    nsa_out = nsa_sample_pallas(q, gl, rows, win, pool_kv, page_table, win_buf,
                                cmp_pos_w, cmp_phi, k_norm[0])
    lb = win_buf.shape[1]
    keep = min(WINDOW, lb + ts)
    new_win = jnp.concatenate([win_buf, win.astype(win_buf.dtype)], axis=1)[:, lb + ts - keep:]
    x_new = _mix_out(pool_out, nsa_out, w_out, x)
    new_hist = jnp.concatenate([pool_hist.astype(pool_in.dtype), pool_in], axis=1)[:, -POOL_HIST:]
    return x_new, rows, new_win, new_hist


def wkv_scan(s0, r, w, k, v, kk, a):
    def step(s, inp):
        r_t, w_t, k_t, v_t, kk_t, a_t = inp
        sa = jnp.einsum('bhij,bhj->bhi', s, -kk_t)
        s = (s * w_t[:, :, None, :] + sa[..., None] * (kk_t * a_t)[:, :, None, :]
             + v_t[..., None] * k_t[:, :, None, :])
        return s, jnp.einsum('bhij,bhj->bhi', s, r_t)

    xs = tuple(jnp.moveaxis(z.astype(jnp.float32), 1, 0) for z in (r, w, k, v, kk, a))
    s, o = lax.scan(step, s0.astype(jnp.float32), xs)
    return s, jnp.moveaxis(o, 0, 1)


RW_TM = 256
HEAD_LANES = 128


def _rwkv_pre_kernel(*refs, seq_len, has_vres):
    it = iter(refs)
    x_ref, xprev_ref, fp_ref, g_ref, mu_ref = [next(it) for _ in range(5)]
    wr_ref, wk_ref, wv_ref = [next(it) for _ in range(3)]
    w0_ref, w1_ref, w2_ref, a0_ref, a1_ref, a2_ref, g1_ref, g2_ref = [next(it) for _ in range(8)]
    kkw_ref, kaw_ref, hsum_ref, hexp_ref = [next(it) for _ in range(4)]
    if has_vres:
        v0_ref, v1_ref, v2_ref, vf_ref = [next(it) for _ in range(4)]
    r_ref, k_ref, v_ref, lw_ref, kk_ref, a_ref, gg_ref = [next(it) for _ in range(7)]
    dd = lambda a, b: jnp.dot(a, b, preferred_element_type=f32)
    tm = x_ref.shape[0]
    norm = lambda z: z * lax.rsqrt(jnp.mean(z * z, axis=-1, keepdims=True) + RMS_EPS) * g_ref[...]
    h = norm(x_ref[...])
    row = lax.broadcasted_iota(jnp.int32, (tm, 1), 0)
    rolled = pltpu.roll(h, 1, axis=0)
    if seq_len % tm == 0:
        first = (pl.program_id(0) % (seq_len // tm)) == 0
        last_prev = norm(xprev_ref[...])[xprev_ref.shape[0] - 1:, :]
        prev = jnp.where(row == 0, jnp.where(first, fp_ref[...], last_prev), rolled)
    else:
        nseq = tm // seq_len
        sel = (lax.broadcasted_iota(jnp.int32, (tm, nseq), 0)
               == seq_len * lax.broadcasted_iota(jnp.int32, (tm, nseq), 1)).astype(bf16)
        prev = jnp.where(row % seq_len == 0, _dot_exact_lhs(sel, fp_ref[...]), rolled)
    xx = prev - h
    mix = lambda j: (h + xx * mu_ref[j:j + 1, :]).astype(bf16)
    xr, xw, xk, xv, xa, xg = [mix(j) for j in range(6)]
    k = dd(xk, wk_ref[...])
    v = dd(xv, wv_ref[...])
    r_ref[...] = dd(xr, wr_ref[...])
    z = -(w0_ref[...] + dd(jnp.tanh(dd(xw, w1_ref[...])).astype(bf16), w2_ref[...]))
    softplus = jnp.maximum(z, 0.0) + jnp.log(1.0 + jnp.exp(-jnp.abs(z)))
    lw_ref[...] = -jnp.exp(-softplus - 0.5)
    a = jax.nn.sigmoid(a0_ref[...] + dd(dd(xa, a1_ref[...]).astype(bf16), a2_ref[...]))
    a_ref[...] = a
    gg_ref[...] = dd(jax.nn.sigmoid(dd(xg, g1_ref[...])).astype(bf16), g2_ref[...])
    if has_vres:
        v = v + (vf_ref[...] - v) * jax.nn.sigmoid(
            v0_ref[...] + dd(dd(xv, v1_ref[...]).astype(bf16), v2_ref[...]))
    v_ref[...] = v
    kk = k * kkw_ref[...]
    nrm = jnp.maximum(jnp.sqrt(_dot_exact_rhs(kk * kk, hsum_ref[...])), 1e-12)
    kk_ref[...] = kk * _dot_exact_rhs(1.0 / nrm, hexp_ref[...])
    k_ref[...] = k * (1.0 + (a - 1.0) * kaw_ref[...])


def _rwkv_post_kernel(o_ref, r_ref, k_ref, v_ref, gg_ref, x_ref, gnw_ref, gnb_ref, rk_ref, hsum_ref,
                      hexp_ref, wo_ref, out_ref):
    hs = hsum_ref[...]
    he = hexp_ref[...]
    head_sum = lambda z: _dot_exact_rhs(_dot_exact_rhs(z, hs), he)
    o = o_ref[...]
    v = v_ref[...]
    d = o - head_sum(o) * (1.0 / RWKV_N)
    var = head_sum(d * d) * (1.0 / RWKV_N)
    on = d * lax.rsqrt(var + GN_EPS) * gnw_ref[...] + gnb_ref[...]
    on = on + head_sum(r_ref[...] * k_ref[...] * rk_ref[...]) * v
    y = (on * gg_ref[...]).astype(bf16)
    out_ref[...] = x_ref[...] + jnp.dot(y, wo_ref[...], preferred_element_type=f32)


def rwkv_layer_fused(x, norm_g, shift_prev, s0, v_first, vres, mu, wr, wk, wv, wo, w0, w1, w2, a0, a1, a2,
                     g1, g2, k_k, k_a, r_k, gn_w, gn_b):
    n, t, d = x.shape
    m = n * t
    tm = RW_TM
    assert m % tm == 0 and (t % tm == 0 or tm % t == 0)
    x2 = x.reshape(m, d)
    row = lambda z: z.reshape(1, d).astype(f32)
    cb = lambda z: z.astype(bf16)
    hd = lax.broadcasted_iota(jnp.int32, (d, HEAD_LANES), 0) // RWKV_N
    hsum = (hd == lax.broadcasted_iota(jnp.int32, (d, HEAD_LANES), 1)).astype(bf16)
    hexp = hsum.T
    tile = pl.BlockSpec((tm, d), lambda i: (i, 0))
    full = lambda z: pl.BlockSpec(z.shape, lambda i: (0,) * z.ndim)
    if t % tm == 0:
        tps = t // tm
        fp = shift_prev.reshape(n, 1, d).astype(f32)
        fp_spec = pl.BlockSpec((None, 1, d), lambda i: (i // tps, 0, 0))
    else:
        fp = shift_prev.astype(f32)
        fp_spec = pl.BlockSpec((tm // t, d), lambda i: (i, 0))
    xprev_spec = pl.BlockSpec((8, d), lambda i: (jnp.maximum(i * (tm // 8) - 1, 0), 0))
    mu8 = jnp.pad(mu.astype(f32), ((0, 2), (0, 0)))
    consts = [row(norm_g), mu8, cb(wr), cb(wk), cb(wv), row(w0), cb(w1), cb(w2), row(a0), cb(a1), cb(a2),
              cb(g1), cb(g2), row(k_k), row(k_a), hsum, hexp]
    args = [x2, x2, fp] + consts
    specs = [tile, xprev_spec, fp_spec] + [full(c) for c in consts]
    if vres is not None:
        v0, v1, v2 = vres
        extra = [row(v0), cb(v1), cb(v2)]
        args += extra + [v_first.reshape(m, d)]
        specs += [full(c) for c in extra] + [tile]
    cp = pltpu.CompilerParams(dimension_semantics=("parallel",), vmem_limit_bytes=56 * 1024 * 1024)
    r, k, v, lw, kk, a, gg = pl.pallas_call(
        functools.partial(_rwkv_pre_kernel, seq_len=t, has_vres=vres is not None),
        grid=(m // tm,), in_specs=specs, out_specs=[tile] * 7,
        out_shape=[jax.ShapeDtypeStruct((m, d), f32)] * 7, compiler_params=cp, name="rwkv_pre",
    )(*args)
    if vres is None:
        v_first = v.reshape(n, t, d)
    tp = -(-t // WKV_CHUNK) * WKV_CHUNK
    padt = lambda z: jnp.pad(z.reshape(n, t, d), ((0, 0), (0, tp - t), (0, 0)))
    o, s = wkv_chunked(padt(r), padt(lw), padt(k), padt(v), padt(kk), padt(a), s0.astype(f32))
    o = o[:, :t].reshape(m, d)
    post_consts = [row(gn_w), row(gn_b), r_k.reshape(1, d).astype(f32), hsum, hexp, cb(wo)]
    x_new = pl.pallas_call(
        _rwkv_post_kernel, grid=(m // tm,),
        in_specs=[tile] * 6 + [full(c) for c in post_consts], out_specs=tile,
        out_shape=jax.ShapeDtypeStruct((m, d), f32), compiler_params=cp, name="rwkv_post",
    )(o, r, k, v, gg, x2, *post_consts)
    h_last = rmsnorm(x[:, -1], norm_g)
    return x_new.reshape(n, t, d), v_first, s, h_last


def rwkv_layer(h, shift_prev, s0, v_first, vres, mu, wr, wk, wv, wo, w0, w1, w2, a0, a1, a2,
               g1, g2, k_k, k_a, r_k, gn_w, gn_b):
    f32 = jnp.float32
    b, t, d = h.shape
    prev = jnp.concatenate([shift_prev[:, None, :].astype(h.dtype), h[:, :-1]], axis=1)
    xx = prev - h
    xr, xw, xk, xv, xa, xg = [h + xx * mu[j] for j in range(6)]
    r = _mm(xr, wr)
    k = _mm(xk, wk)
    v = _mm(xv, wv)
    w_log = -jax.nn.softplus(-(w0 + jnp.tanh(xw @ w1) @ w2).astype(f32)) - 0.5
    decay = jnp.exp(-jnp.exp(w_log))
    if vres is None:
        v_first = v
    else:
        v0, v1, v2 = vres
        v = v + (v_first - v) * jax.nn.sigmoid(v0 + (xv @ v1) @ v2)
    a = jax.nn.sigmoid((a0 + (xa @ a1) @ a2).astype(f32))
    g = jax.nn.sigmoid(xg @ g1) @ g2

    def heads(z):
        return z.reshape(b, t, RWKV_HEADS, RWKV_N).astype(f32)

    kk = heads(k * k_k)
    kk = kk / jnp.maximum(jnp.sqrt(jnp.sum(kk * kk, axis=-1, keepdims=True)), 1e-12)
    k = k.astype(f32) * (1.0 + (a - 1.0) * k_a.astype(f32))
    rh, kh, vh, ah, dh = heads(r), heads(k), heads(v), heads(a), heads(decay)
    tp = -(-t // WKV_CHUNK) * WKV_CHUNK
    padt = lambda z: jnp.pad(z, ((0, 0), (0, tp - t), (0, 0)))
    o, s = wkv_chunked(padt(r), padt(-jnp.exp(w_log)), padt(k), padt(v), padt(kk.reshape(b, t, d)),
                       padt(a), s0.astype(f32))
    o = o[:, :t].reshape(b, t, RWKV_HEADS, RWKV_N)
    mean = jnp.mean(o, axis=-1, keepdims=True)
    var = jnp.mean(jnp.square(o - mean), axis=-1, keepdims=True)
    o = ((o - mean) * lax.rsqrt(var + GN_EPS) * gn_w.reshape(RWKV_HEADS, RWKV_N).astype(f32)
         + gn_b.reshape(RWKV_HEADS, RWKV_N).astype(f32))
    o = o + jnp.sum(rh * kh * r_k.astype(f32), axis=-1, keepdims=True) * vh
    y = _mm((o.reshape(b, t, d) * g.astype(f32)).astype(h.dtype), wo)
    return y, v_first, s, h[:, -1]


def hier_moe(h, wc, bc, wf, bf, wg, wu, wd):
    f32 = jnp.float32
    hp = lax.Precision.HIGHEST
    lc = jnp.dot(h, wc, precision=hp).astype(f32) + bc.astype(f32)
    g_idx = jnp.argmax(lc, axis=-1)
    g_w = jnp.max(jax.nn.softmax(lc, axis=-1), axis=-1)
    g_hot = jax.nn.one_hot(g_idx, MOE_GROUPS, dtype=f32)
    lf = (jnp.dot(h, wf, precision=hp).astype(f32) + bf.astype(f32)).reshape(h.shape[:-1] + (MOE_GROUPS, MOE_EPG))
    lf_sel = jnp.einsum('btge,btg->bte', lf, g_hot)
    top_v, top_i = lax.top_k(lf_sel, MOE_TOPK)
    top_w = jax.nn.softmax(top_v, axis=-1) * g_w[..., None]
    e_id = g_idx[..., None] * MOE_EPG + top_i
    gate = jnp.einsum('btke,btk->bte', jax.nn.one_hot(e_id, N_EXPERTS, dtype=f32), top_w)
    hg = jnp.einsum('btd,edf->btef', h, wg)
    hu = jnp.einsum('btd,edf->btef', h, wu)
    act = (jax.nn.silu(hg) * hu * gate[..., None].astype(h.dtype)).astype(h.dtype)
    return jnp.einsum('btef,efd->btd', act, wd)


def kernel(x_prompt, x_sample, cache_nsa_kv, cache_win_kv, state_pool, state_wkv, state_shift,
           page_table, norm_mix, norm_ffn, ab_w_in, ab_w_out, ab_q_norm, ab_k_norm, cmp_pos_w,
           cmp_phi, pool_w, pool_scale, rw_mu, rw_wr, rw_wk, rw_wv, rw_wo, rw_w0, rw_w1, rw_w2,
           rw_a0, rw_a1, rw_a2, rw_v0, rw_v1, rw_v2, rw_g1, rw_g2, rw_kk, rw_ka, rw_rk, rw_gn_w,
           rw_gn_b, moe_wc, moe_bc, moe_wf, moe_bf, moe_wg, moe_wu, moe_wd):
    xp, xs = x_prompt, x_sample
    vf_p, vf_s = None, None
    nsa_p, nsa_s, win_p, win_s, pool_p, pool_s = [], [], [], [], [], []
    wkv_p, wkv_s, sh_p, sh_s = [], [], [], []
    for l in range(DEPTH):
        if l % 2 == 0:
            i = l // 2
            wts = (ab_w_in[i], ab_w_out[i], ab_q_norm[i], ab_k_norm[i], cmp_pos_w[i], cmp_phi[i],
                   pool_w[i], pool_scale[i])
            xp, r_p, w_p, h_p = ab_layer_prompt(xp, norm_mix[l], *wts)
            xs, r_s, w_s, h_s = ab_layer_sample(xs, norm_mix[l], cache_nsa_kv[i], page_table,
                                                cache_win_kv[i], state_pool[i], *wts)
            nsa_p.append(r_p)
            nsa_s.append(r_s)
            win_p.append(w_p)
            win_s.append(w_s)
            pool_p.append(h_p)
            pool_s.append(h_s)
        else:
            j = l // 2
            vres = None if j == 0 else (rw_v0[j - 1], rw_v1[j - 1], rw_v2[j - 1])
            wts = (rw_mu[j], rw_wr[j], rw_wk[j], rw_wv[j], rw_wo[j], rw_w0[j], rw_w1[j], rw_w2[j],
                   rw_a0[j], rw_a1[j], rw_a2[j], rw_g1[j], rw_g2[j], rw_kk[j], rw_ka[j], rw_rk[j],
                   rw_gn_w[j], rw_gn_b[j])
            bp = xp.shape[0]
            zero_shift = jnp.zeros((bp, D_MODEL), xp.dtype)
            zero_state = jnp.zeros((bp, RWKV_HEADS, RWKV_N, RWKV_N), jnp.float32)
            xp, vf_p, s_p, shp = rwkv_layer_fused(xp, norm_mix[l], zero_shift, zero_state, vf_p, vres, *wts)
            xs, vf_s, s_s, shs = rwkv_layer_fused(xs, norm_mix[l], state_shift[j], state_wkv[j], vf_s, vres, *wts)
            wkv_p.append(s_p)
            wkv_s.append(s_s)
            sh_p.append(shp)
            sh_s.append(shs)
        prep =moe_prep(norm_ffn[l], moe_wc[l], moe_bc[l], moe_wf[l], moe_bf[l], moe_wg[l], moe_wu[l], moe_wd[l])
        xp = moe_residual(xp, prep)
        xs = moe_residual(xs, prep)
    return (xp, xs, jnp.stack(nsa_p), jnp.stack(nsa_s), jnp.stack(win_p), jnp.stack(win_s),
            jnp.stack(pool_p), jnp.stack(pool_s), jnp.stack(wkv_p), jnp.stack(wkv_s),
            jnp.stack(sh_p), jnp.stack(sh_s))
```

```python
import functools

import jax
import jax.numpy as jnp
from jax import lax
from jax.experimental import pallas as pl
from jax.experimental.pallas import tpu as pltpu


def _mm_kernel(x_ref, w_ref, o_ref):
    o_ref[...] = jnp.dot(x_ref[...].astype(jnp.bfloat16), w_ref[...],
                         preferred_element_type=jnp.float32)


def _mm(x, w):
    lead = x.shape[:-1]
    k = x.shape[-1]
    n = w.shape[1]
    x2 = x.reshape(-1, k)
    m = x2.shape[0]
    npad = -(-n // 128) * 128
    wb = w.astype(jnp.bfloat16)
    if npad != n:
        wb = jnp.pad(wb, ((0, 0), (0, npad - n)))
    tn = npad
    for cand in (512, 640, 384, 256, 128):
        if npad % cand == 0:
            tn = cand
            break
    tm = 512 if m % 512 == 0 else m
    out = pl.pallas_call(
        _mm_kernel,
        grid=(m // tm, npad // tn),
        in_specs=[pl.BlockSpec((tm, k), lambda i, j: (i, 0)),
                  pl.BlockSpec((k, tn), lambda i, j: (0, j))],
        out_specs=pl.BlockSpec((tm, tn), lambda i, j: (i, j)),
        out_shape=jax.ShapeDtypeStruct((m, npad), jnp.float32),
        name="mm",
    )(x2, wb)
    return out[:, :n].reshape(lead + (n,))


f32 = jnp.float32
bf16 = jnp.bfloat16
WKV_CHUNK = 64
WKV_PAIRS = 8
WKV_SEQS = 2
WKV_PASSES = 1
WKV_GRAM_PASSES = 3


def _split(x):
    hi = x.astype(bf16)
    lo = (x - hi.astype(f32)).astype(bf16)
    return hi, lo


def _mmul(a, b, passes, nt=False):
    dn = (((1,), (1,)), ((), ())) if nt else (((1,), (0,)), ((), ()))
    d = lambda x, y: lax.dot_general(x, y, dn, preferred_element_type=f32)
    if passes == 1:
        return d(a.astype(bf16), b.astype(bf16))
    ah, al = _split(a)
    bh, bl = _split(b)
    return d(ah, bh) + (d(ah, bl) + d(al, bh))


def _wkv_kernel(r_ref, lw_ref, k_ref, v_ref, kk_ref, a_ref, s0_ref, o_ref, sT_ref, st_scr, *, passes):
    C = WKV_CHUNK
    nb = r_ref.shape[0]
    c = pl.program_id(1)
    nc = pl.num_programs(1)
    row = lax.broadcasted_iota(jnp.int32, (2 * C, 2 * C), 0)
    col = lax.broadcasted_iota(jnp.int32, (2 * C, 2 * C), 1)
    bd = (row < C) == (col < C)
    strict = bd & ((row % C) > (col % C))
    incl = bd & ((row % C) >= (col % C))
    eye = (row == col).astype(f32)
    lane_s = col < C
    m1 = lax.broadcasted_iota(jnp.int32, (C, 2 * C), 1) < C
    tri = (lax.broadcasted_iota(jnp.int32, (C, C), 0)
           >= lax.broadcasted_iota(jnp.int32, (C, C), 1)).astype(bf16)

    @pl.when(c == 0)
    def _():
        z = jnp.zeros((C, C), f32)
        for p in range(nb * WKV_PAIRS):
            s1 = s0_ref[p // WKV_PAIRS, 2 * (p % WKV_PAIRS)]
            s2 = s0_ref[p // WKV_PAIRS, 2 * (p % WKV_PAIRS) + 1]
            st_scr[p] =jnp.concatenate([jnp.concatenate([s1, z], axis=1),
                                         jnp.concatenate([z, s2], axis=1)], axis=0)

    def stack2(x):
        return jnp.concatenate([jnp.where(m1, x, 0.0), jnp.where(m1, 0.0, x)], axis=0)

    dd = lambda x, y: jnp.dot(x, y, preferred_element_type=f32)
    pairs = range(nb * WKV_PAIRS)
    sq = [p // WKV_PAIRS for p in pairs]
    sls = [slice((p % WKV_PAIRS) * 2 * C, (p % WKV_PAIRS + 1) * 2 * C) for p in pairs]
    tv = r_ref.shape[1]

    def ld(ref, p):
        x = ref[sq[p], :, sls[p]]
        return x if tv == C else jnp.concatenate([x, jnp.zeros((C - tv, 2 * C), f32)], axis=0)

    def prep(p):
        sl = sls[p]
        lw = ld(lw_ref, p)
        kk = ld(kk_ref, p)
        h1 = lw.astype(bf16)
        r1 = lw - h1.astype(f32)
        h2 = r1.astype(bf16)
        h3 = (r1 - h2.astype(f32)).astype(bf16)
        cw = dd(tri, h1) + (dd(tri, h2) + dd(tri, h3))
        cwC = cw[C - 1:C, :]
        b = kk * ld(a_ref, p)
        k = ld(k_ref, p)
        At = -kk * jnp.exp(cw - lw)
        Rt = ld(r_ref, p) * jnp.exp(cw)
        einv = jnp.exp(-cw)
        efut = jnp.exp(cwC - cw)
        X = jnp.concatenate([stack2(At), stack2(Rt)], axis=0)
        Y = jnp.concatenate([b * einv, k * einv], axis=0)
        AR = jnp.concatenate([At, Rt], axis=0)
        BK = jnp.concatenate([b * efut, k * efut], axis=0)
        return X, Y, AR, BK, jnp.exp(cwC)

    pre = [prep(p) for p in pairs]
    G = [_mmul(pre[p][0], pre[p][1], WKV_GRAM_PASSES, nt=True) for p in pairs]
    ARS = [_mmul(pre[p][2], st_scr[p], passes, nt=True) for p in pairs]
    L, Mak, Mrb, Mrk = [], [], [], []
    for p in pairs:
        GA = G[p][0:2 * C]
        GR = G[p][2 * C:4 * C]
        GAr = pltpu.roll(GA, C, axis=1)
        GRr = pltpu.roll(GR, C, axis=1)
        L.append(jnp.where(strict, jnp.where(lane_s, GA, GAr), 0.0))
        Mak.append(jnp.where(strict, jnp.where(lane_s, GAr, GA), 0.0))
        Mrb.append(jnp.where(incl, jnp.where(lane_s, GR, GRr), 0.0))
        Mrk.append(jnp.where(incl, jnp.where(lane_s, GRr, GR), 0.0))
    V = [ld(v_ref, p) for p in pairs]
    Vs = [stack2(V[p]) for p in pairs]
    Xs = [stack2(ARS[p][0:C]) + _mmul(Mak[p], Vs[p], passes) for p in pairs]
    OV = [_mmul(Mrk[p], Vs[p], passes) for p in pairs]
    P = [eye + L[p] for p in pairs]
    Q = L
    for _ in range(5):
        Q = [_mmul(Q[p], Q[p], passes) for p in pairs]
        P = [P[p] + _mmul(Q[p], P[p], passes) for p in pairs]
    Us = [_mmul(P[p], Xs[p], passes) for p in pairs]
    Os = [_mmul(Mrb[p], Us[p], passes) + OV[p] for p in pairs]
    for p in pairs:
        o_ref[sq[p], :, sls[p]] = (ARS[p][C:2 * C] + Os[p][0:C] + Os[p][C:2 * C])[0:tv]
    for p in pairs:
        U = Us[p][0:C] + Us[p][C:2 * C]
        UV = jnp.concatenate([U, V[p]], axis=0)
        dS = _mmul(UV.T, pre[p][3], passes)
        st_scr[p] = st_scr[p] * pre[p][4] + jnp.where(bd, dS, 0.0)

    @pl.when(c == nc - 1)
    def _():
        for p in pairs:
            Snew = st_scr[p]
            sT_ref[sq[p], 2 * (p % WKV_PAIRS)] = Snew[0:C, 0:C]
            sT_ref[sq[p], 2 * (p % WKV_PAIRS) + 1] = Snew[C:2 * C, C:2 * C]


def wkv_chunked(r, lw, k, v, kk, a, s0):
    B, T, D = r.shape
    H = D // 64
    C = WKV_CHUNK
    nb = WKV_SEQS
    tb = min(T, C)
    assert (T % C == 0 or T < C) and tb % 8 == 0 and D == WKV_PAIRS * 2 * C and B % nb == 0
    blk = pl.BlockSpec((nb, tb, D), lambda b, c: (b, c, 0))
    sblk = pl.BlockSpec((nb, H, 64, 64), lambda b, c: (b, 0, 0, 0))
    return pl.pallas_call(
        functools.partial(_wkv_kernel, passes=WKV_PASSES),
        grid=(B // nb, T // tb),
        in_specs=[blk] * 6 + [sblk],
        out_specs=[blk, sblk],
        out_shape=[jax.ShapeDtypeStruct((B, T, D), f32), jax.ShapeDtypeStruct((B, H, 64, 64), f32)],
        scratch_shapes=[pltpu.VMEM((nb * WKV_PAIRS, 2 * C, 2 * C), f32)],
        compiler_params=pltpu.CompilerParams(dimension_semantics=("parallel", "arbitrary")),
        name="wkv7_chunked",
    )(r, lw, k, v, kk, a, s0)


D_MODEL = 1024
BATCH = 4
SEQ = 4096
DEPTH = 4
DEC_BATCH = 128
DEC_SEQ = 8
PAST_LEN = 2048
PAGE_SIZE = 128

N_NSA_LAYERS = (DEPTH + 1) // 2
N_RWKV_LAYERS = DEPTH // 2
N_VRES = N_RWKV_LAYERS - 1

POOL_DIM = D_MODEL // 2
POOL_WINDOWS = (2, 4, 8, 16)
POOL_GROUPS = len(POOL_WINDOWS)
POOL_GDIM = POOL_DIM // POOL_GROUPS
POOL_HIST = max(POOL_WINDOWS) - 1

HEAD_DIM = 64
NSA_HEADS = (D_MODEL // 2) // HEAD_DIM
NSA_KV_HEADS = 2
NSA_GQ = NSA_HEADS // NSA_KV_HEADS
NSA_DIM = NSA_HEADS * HEAD_DIM
CMP_STRIDE = 16
CMP_LEN = 2 * CMP_STRIDE
SLC_LEN = 64
N_SEL = 16
WINDOW = 512
Q_BLOCK = 128
ROPE_DIM = HEAD_DIM // 4
ROPE_THETA = 500000.0
MIX_DIM = POOL_DIM + NSA_DIM
KV_COLS = 6 * NSA_KV_HEADS * HEAD_DIM
IN_COLS = POOL_DIM + NSA_DIM + KV_COLS + 3 * NSA_HEADS

RWKV_N = 64
RWKV_HEADS = D_MODEL // RWKV_N
LORA_W = 64
LORA_A = 64
LORA_V = 32
LORA_G = 128
GN_EPS = 64e-5

MOE_GROUPS = 4
MOE_EPG = 4
N_EXPERTS = MOE_GROUPS * MOE_EPG
MOE_TOPK = 2
D_FF_E = 256

RMS_EPS = 1e-6
NEG_INF = -1e30
RES_SCALE = (2 * DEPTH) ** -0.5

SEL_TK = 512


def _nsa_prompt_kernel(q_ref, ql_ref, g_ref, kc_ref, kcl_ref, vct_ref, ks_ref, vst_ref, kw_ref, vwt_ref,
                       o_ref, score_scr, sel_scr, *, n_cmp):
    QB = Q_BLOCK
    GQ = NSA_GQ * QB
    i = pl.program_id(2)
    s0 = i * QB
    qT = q_ref[...]
    posq = s0 + lax.broadcasted_iota(jnp.int32, (1, GQ), 1) % QB
    dd = lambda x, y: jnp.dot(x, y, preferred_element_type=f32)

    ncp = kc_ref.shape[0]
    cidx = lax.broadcasted_iota(jnp.int32, (ncp, 1), 0)
    mask_c = cidx * CMP_STRIDE + (CMP_LEN - 1) <= posq
    sc = dd(kc_ref[...], qT) + (dd(kc_ref[...], ql_ref[...]) + dd(kcl_ref[...], qT))
    sc = jnp.where(mask_c, sc, NEG_INF)
    pe = jnp.exp(sc - jnp.max(sc, axis=0, keepdims=True))
    pc = jnp.where(mask_c, pe / jnp.sum(pe, axis=0, keepdims=True), 0.0)
    o_c = dd(vct_ref[...], pc.astype(bf16))

    imp = (pc[:, 0:QB] + pc[:, QB:2 * QB]) + (pc[:, 2 * QB:3 * QB] + pc[:, 3 * QB:4 * QB])
    n_slc = score_scr.shape[0]
    per = SLC_LEN // CMP_STRIDE
    nn = lax.broadcasted_iota(jnp.int32, (n_slc, ncp), 0) * per
    cc = lax.broadcasted_iota(jnp.int32, (n_slc, ncp), 1)
    mt = (0.5 * ((cc >= nn) & (cc < nn + per)).astype(f32)
          + 0.5 * ((cc + 1 >= nn) & (cc + 1 < nn + per)).astype(f32)).astype(bf16)
    i1 = imp.astype(bf16)
    r1 = imp - i1.astype(f32)
    i2 = r1.astype(bf16)
    i3 = (r1 - i2.astype(f32)).astype(bf16)
    imp_blk = dd(mt, i1) + (dd(mt, i2) + dd(mt, i3))
    nidx = lax.broadcasted_iota(jnp.int32, (n_slc, 1), 0)
    cur = (s0 + lax.broadcasted_iota(jnp.int32, (1, QB), 1)) // SLC_LEN
    forced = (nidx == 0) | (nidx == cur) | (nidx == cur - 1)
    score = jnp.where(nidx > cur, -1.0, jnp.where(forced, 1e6, imp_blk))
    score_scr[...] = score
    rank = jnp.zeros((n_slc, QB), jnp.int32)
    for m in range(n_slc):
        sm = score_scr[m:m + 1, :]
        tie = (nidx > m).astype(jnp.int32)
        rank = rank + jnp.where(sm > score, 1, jnp.where(sm == score, tie, 0))
    bias = jnp.where(rank < min(N_SEL, n_slc), 0.0, NEG_INF)
    sel_scr[...] = jnp.concatenate([bias] * NSA_GQ, axis=1)

    def online(carry, s, vt_blk):
        m, l, acc = carry
        m_new = jnp.maximum(m, jnp.max(s, axis=0, keepdims=True))
        alpha = jnp.exp(m - m_new)
        p = jnp.exp(s - m_new)
        l = alpha * l + jnp.sum(p, axis=0, keepdims=True)
        acc = alpha * acc + dd(vt_blk, p.astype(bf16))
        return m_new, l, acc

    init = (jnp.full((1, GQ), NEG_INF, f32), jnp.zeros((1, GQ), f32), jnp.zeros((HEAD_DIM, GQ), f32))

    bpt = SEL_TK // SLC_LEN

    def sel_scores(kt):
        k0 = pl.multiple_of(kt * SEL_TK, SEL_TK)
        rows = sel_scr[pl.ds(pl.multiple_of(kt * bpt, bpt), bpt), :]
        blk = jnp.concatenate([jnp.broadcast_to(rows[j:j + 1, :], (SLC_LEN, GQ)) for j in range(bpt)], axis=0)
        return k0, dd(ks_ref[pl.ds(k0, SEL_TK), :], qT) + blk

    def sel_body(kt, carry):
        k0, s = sel_scores(kt)
        return online(carry, s, vst_ref[:, pl.ds(k0, SEL_TK)])

    n_full = s0 // SEL_TK
    carry = lax.fori_loop(0, n_full, sel_body, init)
    k0, s = sel_scores(n_full)
    kpos = k0 + lax.broadcasted_iota(jnp.int32, (SEL_TK, 1), 0)
    _, l_s, acc_s = online(carry, jnp.where(kpos <= posq, s, NEG_INF), vst_ref[:, pl.ds(k0, SEL_TK)])

    nwt = WINDOW // QB
    carry = init
    for j in range(nwt, -1, -1):
        k0 = s0 - WINDOW + j * QB
        k0c = pl.multiple_of(jnp.maximum(k0, 0), QB)
        s = dd(kw_ref[pl.ds(k0c, QB), :], qT)
        kpos = k0c + lax.broadcasted_iota(jnp.int32, (QB, 1), 0)
        if j == nwt:
            s = jnp.where(kpos <= posq, s, NEG_INF)
        else:
            if j == 0:
                s = jnp.where(posq - kpos < WINDOW, s, NEG_INF)
            s = s + jnp.where(k0 >= 0, 0.0, NEG_INF)
        carry = online(carry, s, vwt_ref[:, pl.ds(k0c, QB)])
    _, l_w, acc_w = carry

    g = jax.nn.sigmoid(g_ref[...])
    o = g[0:1] * o_c + g[1:2] * (acc_s / l_s) + g[2:3] * (acc_w / l_w)
    o_ref[...] = jnp.concatenate([o[:, j * QB:(j + 1) * QB].T for j in range(NSA_GQ)], axis=1)


def nsa_prompt_pallas(ops, kc, vc):
    qT, qTl, gT, ks, vst, kw, vwt = ops
    B, KVH, T, D = ks.shape
    G, QB = NSA_GQ, Q_BLOCK
    assert T % SEL_TK == 0 and T % QB == 0
    nqb = T // QB
    n_cmp = kc.shape[1]
    ncp = -(-n_cmp // 128) * 128
    n_slc = T // SLC_LEN
    kcp, kcl = _split(jnp.pad(kc, ((0, 0), (0, ncp - n_cmp), (0, 0), (0, 0))).transpose(0, 2, 1, 3))
    vct = jnp.pad(vc, ((0, 0), (0, ncp - n_cmp), (0, 0), (0, 0))).transpose(0, 2, 3, 1).astype(bf16)
    bh = lambda *shape: pl.BlockSpec((None, None) + shape, lambda b, h, i: (b, h) + (0,) * len(shape))
    bhi = lambda *shape: pl.BlockSpec((None, None, None) + shape, lambda b, h, i: (b, h, i) + (0,) * len(shape))
    return pl.pallas_call(
        functools.partial(_nsa_prompt_kernel, n_cmp=n_cmp),
        grid=(B, KVH, nqb),
        in_specs=[bhi(D, G * QB), bhi(D, G * QB), bhi(3, G * QB), bh(ncp, D), bh(ncp, D), bh(D, ncp),
                  bh(T, D), bh(D, T), bh(T, D), bh(D, T)],
        out_specs=pl.BlockSpec((None, QB, G * D), lambda b, h, i: (b, i, h)),
        out_shape=jax.ShapeDtypeStruct((B, T, KVH * G * D), f32),
        scratch_shapes=[pltpu.VMEM((n_slc, QB), f32), pltpu.VMEM((n_slc, G * QB), f32)],
        compiler_params=pltpu.CompilerParams(dimension_semantics=("parallel", "parallel", "arbitrary"),
                                             vmem_limit_bytes=48 * 1024 * 1024),
        name="nsa_prompt",
    )(qT, qTl, gT, kcp, kcl, vct, ks, vst, kw, vwt)


def _split3(x):
    h1 = x.astype(bf16)
    r1 = x - h1.astype(f32)
    h2 = r1.astype(bf16)
    return h1, h2, (r1 - h2.astype(f32)).astype(bf16)


def _dot_exact_rhs(x, m):
    d = lambda a: jnp.dot(a, m, preferred_element_type=f32)
    h1, h2, h3 = _split3(x)
    return d(h1) + (d(h2) + d(h3))


def _dot_exact_lhs(m, x):
    d = lambda a: jnp.dot(m, a, preferred_element_type=f32)
    h1, h2, h3 = _split3(x)
    return d(h1) + (d(h2) + d(h3))


def _nsa_sample_kernel(pt_ref, *refs, n_pages, ts, past_len):
    pages = refs[:n_pages]
    (new_ref, wbuf_ref, wnew_ref, qh_ref, ql_ref, gate_ref, wa_ref, wb_ref, phik_ref, phiv_ref,
     gain_ref, cos_ref, sin_ref, o_ref, ssel_scr, a_scr, b_scr, score_scr) = refs[n_pages:]
    P = PAGE_SIZE
    KV = NSA_KV_HEADS * HEAD_DIM
    NCOL = NSA_KV_HEADS * NSA_GQ * ts
    NQ = NSA_KV_HEADS * ts
    cpp = P // CMP_STRIDE
    n_chunk = (past_len + SLC_LEN) // CMP_STRIDE
    n_cmp = n_chunk - 1
    ncp = a_scr.shape[0]
    n_slc = (past_len + SLC_LEN) // SLC_LEN
    nsp = score_scr.shape[0]
    dd = lambda x, y: jnp.dot(x, y, preferred_element_type=f32)
    qh = qh_ref[...]
    ql = ql_ref[...]
    col = lax.broadcasted_iota(jnp.int32, (1, NCOL), 1)
    t_col = col % ts
    zpad = jnp.zeros((P - ts, 4 * KV), f32)
    new_tile = jnp.concatenate([new_ref[...], zpad], axis=0)

    wa = wa_ref[...]
    wb = wb_ref[...]
    a_scr[...] = jnp.zeros(a_scr.shape, f32)
    b_scr[...] = jnp.zeros(b_scr.shape, f32)
    for j in range(n_pages + 1):
        tile = pages[j][...] if j < n_pages else new_tile
        xc = tile[:, 0:2 * KV]
        a_scr[j * cpp:(j + 1) * cpp, :] = (xc * wa).reshape(cpp, CMP_STRIDE, 2 * KV).sum(axis=1)
        b_scr[j * cpp:(j + 1) * cpp, :] = (xc * wb).reshape(cpp, CMP_STRIDE, 2 * KV).sum(axis=1)
        ssel_scr[j * P:(j + 1) * P, :] = dd(tile[:, 2 * KV:3 * KV].astype(bf16), qh)

    mean = a_scr[...] + pltpu.roll(b_scr[...], ncp - 1, axis=0)
    kc = _mmul(mean[:, 0:KV], phik_ref[...], 3)
    vc = _mmul(mean[:, KV:2 * KV], phiv_ref[...], 3)
    r_i = lax.broadcasted_iota(jnp.int32, (KV, KV), 0)
    c_i = lax.broadcasted_iota(jnp.int32, (KV, KV), 1)
    same_head = (r_i // HEAD_DIM) == (c_i // HEAD_DIM)
    mavg = jnp.where(same_head, 1.0 / HEAD_DIM, 0.0).astype(bf16)
    kc = kc * lax.rsqrt(_dot_exact_rhs(kc * kc, mavg) + RMS_EPS) * gain_ref[...]
    half = ROPE_DIM // 2
    rd, cd = r_i % HEAD_DIM, c_i % HEAD_DIM
    rot = jnp.where(same_head & (cd < half) & (rd == cd + half), -1.0,
                    jnp.where(same_head & (cd >= half) & (cd < ROPE_DIM) & (rd == cd - half), 1.0, 0.0)).astype(bf16)
    kc = kc * cos_ref[...] + _dot_exact_rhs(kc, rot) * sin_ref[...]

    kch, kcl = _split(kc)
    sc = dd(kch, qh) + (dd(kch, ql) + dd(kcl, qh))
    cidx = lax.broadcasted_iota(jnp.int32, (ncp, 1), 0)
    mask_c = (cidx * CMP_STRIDE + (CMP_LEN - 1) <= past_len + t_col) & (cidx < n_cmp)
    sc = jnp.where(mask_c, sc, NEG_INF)
    pe = jnp.exp(sc - jnp.max(sc, axis=0, keepdims=True))
    pc = jnp.where(mask_c, pe / jnp.sum(pe, axis=0, keepdims=True), 0.0)
    o_c = dd(pc.T.astype(bf16), vc.astype(bf16))

    gr = lax.broadcasted_iota(jnp.int32, (NCOL, NQ), 0)
    gc = lax.broadcasted_iota(jnp.int32, (NCOL, NQ), 1)
    gsum = ((gr // (NSA_GQ * ts) == gc // ts) & (gr % ts == gc % ts)).astype(bf16)
    imp = _dot_exact_rhs(pc, gsum)
    per = SLC_LEN // CMP_STRIDE
    nn = lax.broadcasted_iota(jnp.int32, (nsp, ncp), 0) * per
    cc = lax.broadcasted_iota(jnp.int32, (nsp, ncp), 1)
    mt = (0.5 * ((cc >= nn) & (cc < nn + per)).astype(f32)
          + 0.5 * ((cc + 1 >= nn) & (cc + 1 < nn + per)).astype(f32)).astype(bf16)
    imp_blk = _dot_exact_lhs(mt, imp)
    nidx = lax.broadcasted_iota(jnp.int32, (nsp, 1), 0)
    cur = (past_len + lax.broadcasted_iota(jnp.int32, (1, NQ), 1) % ts) // SLC_LEN
    forced = (nidx == 0) | (nidx == cur) | (nidx == cur - 1)
    score = jnp.where(nidx >= n_slc, -2.0, jnp.where(nidx > cur, -1.0, jnp.where(forced, 1e6, imp_blk)))
    score_scr[...] = score
    rank = jnp.zeros((nsp, NQ), jnp.int32)
    for m in range(n_slc):
        sm = score_scr[m:m + 1, :]
        beats = (sm > score) | ((sm == score) & (nidx > m))
        rank = rank + beats.astype(jnp.int32)
    sel = (rank < min(N_SEL, n_slc)).astype(bf16)
    gr2 = lax.broadcasted_iota(jnp.int32, (NQ, NCOL), 0)
    gc2 = lax.broadcasted_iota(jnp.int32, (NQ, NCOL), 1)
    gexp = ((gc2 // (NSA_GQ * ts) == gr2 // ts) & (gc2 % ts == gr2 % ts)).astype(bf16)
    sel_c = dd(sel, gexp)

    def two_pass(n_tiles, score_tile, mask_tile, v_tile):
        m = jnp.full((1, NCOL), NEG_INF, f32)
        for j in range(n_tiles):
            m = jnp.maximum(m, jnp.max(jnp.where(mask_tile(j), score_tile(j), NEG_INF), axis=0, keepdims=True))
        num = jnp.zeros((NCOL, KV), f32)
        den = jnp.zeros((NCOL, KV), f32)
        ones = jnp.ones((P, KV), bf16)
        for j in range(n_tiles):
            p = jnp.where(mask_tile(j), jnp.exp(score_tile(j) - m), 0.0).T.astype(bf16)
            num = num + dd(p, v_tile(j))
            den = den + dd(p, ones)
        return num, den

    bpp = P // SLC_LEN
    row = lax.broadcasted_iota(jnp.int32, (P, 1), 0)

    def sel_mask(j):
        blk = jnp.concatenate([jnp.broadcast_to(sel_c[j * bpp + i:j * bpp + i + 1, :], (SLC_LEN, NCOL))
                               for i in range(bpp)], axis=0)
        return (blk > 0.5) & (j * P + row <= past_len + t_col)

    num_s, den_s = two_pass(
        n_pages + 1, lambda j: ssel_scr[j * P:(j + 1) * P, :], sel_mask,
        lambda j: (pages[j][:, 3 * KV:4 * KV] if j < n_pages else new_tile[:, 3 * KV:4 * KV]).astype(bf16))

    lb = wbuf_ref.shape[0]
    nwt = lb // P
    wnew = jnp.concatenate([wnew_ref[...], jnp.zeros((P - ts, 2 * KV), f32)], axis=0)

    def w_tile(j):
        return wbuf_ref[j * P:(j + 1) * P, :] if j < nwt else wnew

    def win_mask(j):
        pos_w = (past_len - lb + j * P + row) if j < nwt else (past_len + row)
        dq = past_len + t_col - pos_w
        return (dq >= 0) & (dq < WINDOW) & (pos_w >= 0) & ((row < ts) | (j < nwt))

    num_w, den_w = two_pass(
        nwt + 1, lambda j: dd(w_tile(j)[:, 0:KV].astype(bf16), qh), win_mask,
        lambda j: w_tile(j)[:, KV:2 * KV].astype(bf16))

    g = jax.nn.sigmoid(gate_ref[...])
    o_ref[...] = g[0] * o_c + g[1] * (num_s / den_s) + g[2] * (num_w / den_w)


def nsa_sample_pallas(q, gl, rows_new, win_new, pools, page_table, win_bufs, layer, kc_w, phi, k_gain):
    B, ts = q.shape[:2]
    KVH, G, D, P = NSA_KV_HEADS, NSA_GQ, HEAD_DIM, PAGE_SIZE
    KV = KVH * D
    n_pages = page_table.shape[1]
    past_len = n_pages * P
    n_pool = pools.shape[1]
    assert pools.shape[2] == P and ts <= SLC_LEN and P % SLC_LEN == 0
    lb = win_bufs.shape[2]
    assert lb % P == 0
    NCOL = KVH * G * ts
    n_chunk = (past_len + SLC_LEN) // CMP_STRIDE
    ncp = -(-n_chunk // 8) * 8
    n_slc = (past_len + SLC_LEN) // SLC_LEN
    nsp = -(-n_slc // 8) * 8
    qs = (q * D ** -0.5).reshape(B, ts, KVH, G, D).transpose(0, 2, 4, 3, 1).reshape(B, KVH, D, G * ts)
    z = jnp.zeros_like(qs[:, 0])
    qbd = jnp.concatenate([jnp.concatenate([qs[:, 0], z], axis=2), jnp.concatenate([z, qs[:, 1]], axis=2)], axis=1)
    qh, ql = _split(qbd)
    gate = gl.reshape(B, ts, KVH, G, 3).transpose(0, 4, 2, 3, 1).reshape(B, 3, NCOL, 1)
    gate = jnp.broadcast_to(gate, (B, 3, NCOL, KV)).astype(f32)
    w_lane = jnp.repeat(kc_w.reshape(2 * KVH, CMP_LEN), D, axis=0)
    reps = P // CMP_STRIDE
    wa = jnp.tile(w_lane[:, :CMP_STRIDE].T, (reps, 1)).astype(f32)
    wb = jnp.tile(w_lane[:, CMP_STRIDE:].T, (reps, 1)).astype(f32)
    zz = jnp.zeros((D, D), f32)
    bdiag = lambda m: jnp.concatenate([jnp.concatenate([m, zz], axis=1), jnp.concatenate([zz, m], axis=1)], axis=0)
    phik, phiv = bdiag(phi[0].astype(f32)), bdiag(phi[1].astype(f32))
    gain = jnp.tile(k_gain.astype(f32), KVH).reshape(1, KV)
    half = ROPE_DIM // 2
    inv = ROPE_THETA ** (-jnp.arange(half, dtype=f32) / half)
    cmp_end = (jnp.arange(ncp, dtype=jnp.int32) * CMP_STRIDE + (CMP_LEN - 1)).astype(f32)
    ang = cmp_end[:, None] * inv
    cos_h = jnp.concatenate([jnp.cos(ang), jnp.cos(ang), jnp.ones((ncp, D - ROPE_DIM), f32)], axis=1)
    sin_h = jnp.concatenate([jnp.sin(ang), jnp.sin(ang), jnp.zeros((ncp, D - ROPE_DIM), f32)], axis=1)
    cos_t, sin_t = jnp.tile(cos_h, (1, KVH)), jnp.tile(sin_h, (1, KVH))
    pool2 = pools.reshape(pools.shape[0] * n_pool, P, 4 * KV)
    new2 = rows_new.reshape(B, ts, 4 * KV).astype(f32)
    wbuf2 = win_bufs.reshape(win_bufs.shape[0] * B, lb, 2 * KV)
    wnew2 = win_new.reshape(B, ts, 2 * KV).astype(f32)
    page_spec = lambda j: pl.BlockSpec((None, P, 4 * KV),
                                       lambda b, pt, j=j: (layer * n_pool + pt[b, j], 0, 0))
    per_b = lambda *s: pl.BlockSpec((None,) + s, lambda b, pt: (b,) + (0,) * len(s))
    wbuf_spec = pl.BlockSpec((None, lb, 2 * KV), lambda b, pt: (layer * B + b, 0, 0))
    const = lambda *s: pl.BlockSpec(s, lambda b, pt: (0,) * len(s))
    grid_spec = pltpu.PrefetchScalarGridSpec(
        num_scalar_prefetch=1, grid=(B,),
        in_specs=[page_spec(j) for j in range(n_pages)] + [
            per_b(ts, 4 * KV), wbuf_spec, per_b(ts, 2 * KV), per_b(KV, NCOL), per_b(KV, NCOL),
            per_b(3, NCOL, KV), const(P, 2 * KV), const(P, 2 * KV), const(KV, KV), const(KV, KV),
            const(1, KV), const(ncp, KV), const(ncp, KV)],
        out_specs=per_b(NCOL, KV),
        scratch_shapes=[pltpu.VMEM(((n_pages + 1) * P, NCOL), f32), pltpu.VMEM((ncp, 2 * KV), f32),
                        pltpu.VMEM((ncp, 2 * KV), f32), pltpu.VMEM((nsp, KVH * ts), f32)])
    out = pl.pallas_call(
        functools.partial(_nsa_sample_kernel, n_pages=n_pages, ts=ts, past_len=past_len),
        grid_spec=grid_spec,
        out_shape=jax.ShapeDtypeStruct((B, NCOL, KV), f32),
        compiler_params=pltpu.CompilerParams(dimension_semantics=("arbitrary",),
                                             vmem_limit_bytes=48 * 1024 * 1024),
        name="nsa_sample",
    )(page_table, *([pool2] * n_pages), new2, wbuf2, wnew2, qh, ql, gate, wa, wb, phik, phiv, gain, cos_t, sin_t)
    o4 = out.reshape(B, KVH, G, ts, KVH, D)
    o = jnp.stack([o4[:, 0, :, :, 0], o4[:, 1, :, :, 1]], axis=1)
    return o.transpose(0, 3, 1, 2, 4).reshape(B, ts, KVH * G * D)


MOE_TM = 512
ROUTER_LANES = 128


def _moe_kernel(x_ref, g_ref, wrh_ref, wrl_ref, br_ref, wg_ref, wu_ref, wd_ref, o_ref,
                h_scr, gate_scr, acc_scr):
    grp = pl.program_id(1)
    dd = lambda a, b: jnp.dot(a, b, preferred_element_type=f32)
    tm = x_ref.shape[0]
    lane = lax.broadcasted_iota(jnp.int32, (tm, ROUTER_LANES), 1).astype(f32)
    far = float(ROUTER_LANES)

    @pl.when(grp == 0)
    def _():
        x = x_ref[...]
        h = x * lax.rsqrt(jnp.mean(x * x, axis=-1, keepdims=True) + RMS_EPS) * g_ref[...]
        hh, hl = _split(h)
        h_scr[...] = hh
        logits = dd(hh, wrh_ref[...]) + (dd(hh, wrl_ref[...]) + dd(hl, wrh_ref[...])) + br_ref[...]
        is_c = lane < MOE_GROUPS
        lc = jnp.where(is_c, logits, NEG_INF)
        mc = jnp.max(lc, axis=1, keepdims=True)
        g_idx = jnp.min(jnp.where(lc == mc, lane, far), axis=1, keepdims=True)
        g_w = 1.0 / jnp.sum(jnp.where(is_c, jnp.exp(lc - mc), 0.0), axis=1, keepdims=True)
        lo = MOE_GROUPS + MOE_EPG * g_idx
        lf = jnp.where((lane >= lo) & (lane < lo + MOE_EPG), logits, NEG_INF)
        v1 = jnp.max(lf, axis=1, keepdims=True)
        i1 = jnp.min(jnp.where(lf == v1, lane, far), axis=1, keepdims=True)
        lf2 = jnp.where(lane == i1, NEG_INF, lf)
        v2 = jnp.max(lf2, axis=1, keepdims=True)
        i2 = jnp.min(jnp.where(lf2 == v2, lane, far), axis=1, keepdims=True)
        e21 = jnp.exp(v2 - v1)
        w1 = g_w / (1.0 + e21)
        gate_scr[...] = jnp.where(lane == i1, w1, jnp.where(lane == i2, e21 * w1, 0.0))
        acc_scr[...] = x

    h = h_scr[...]
    hg = dd(h, wg_ref[...])
    hu = dd(h, wu_ref[...])
    gate = gate_scr[...]
    first = (MOE_GROUPS + MOE_EPG * grp).astype(f32)
    cols = []
    for e in range(MOE_EPG):
        ge = jnp.sum(jnp.where(lane == first + e, gate, 0.0), axis=1, keepdims=True)
        sl = slice(e * D_FF_E, (e + 1) * D_FF_E)
        hge = hg[:, sl]
        cols.append((hge * jax.nn.sigmoid(hge) * hu[:, sl] * ge).astype(bf16))
    acc_scr[...] += dd(jnp.concatenate(cols, axis=1), wd_ref[...])

    @pl.when(grp == MOE_GROUPS - 1)
    def _():
        o_ref[...] = acc_scr[...]


def moe_prep(g, wc, bc, wf, bf, wg, wu, wd):
    d = wc.shape[0]
    pad = ROUTER_LANES - MOE_GROUPS - N_EXPERTS
    wr = jnp.pad(jnp.concatenate([wc, wf], axis=1).astype(f32), ((0, 0), (0, pad)))
    wrh, wrl = _split(wr)
    br = jnp.pad(jnp.concatenate([bc, bf]).astype(f32), (0, pad)).reshape(1, ROUTER_LANES)
    regroup = lambda w: (w.reshape(MOE_GROUPS, MOE_EPG, d, D_FF_E).transpose(0, 2, 1, 3)
                         .reshape(MOE_GROUPS, d, MOE_EPG * D_FF_E).astype(bf16))
    wdg = wd.reshape(MOE_GROUPS, MOE_EPG * D_FF_E, d).astype(bf16)
    return g.reshape(1, d).astype(f32), wrh, wrl, br, regroup(wg), regroup(wu), wdg


def moe_residual(x, prep):
    g, wrh, wrl, br, wgg, wug, wdg = prep
    shp = x.shape
    d = shp[-1]
    x2 = x.reshape(-1, d)
    m = x2.shape[0]
    tm = MOE_TM
    assert m % tm == 0
    gf = MOE_EPG * D_FF_E
    full = lambda r, c: pl.BlockSpec((r, c), lambda i, j: (0, 0))
    out = pl.pallas_call(
        _moe_kernel,
        grid=(m // tm, MOE_GROUPS),
        in_specs=[pl.BlockSpec((tm, d), lambda i, j: (i, 0)), full(1, d),
                  full(d, ROUTER_LANES), full(d, ROUTER_LANES), full(1, ROUTER_LANES),
                  pl.BlockSpec((None, d, gf), lambda i, j: (j, 0, 0)),
                  pl.BlockSpec((None, d, gf), lambda i, j: (j, 0, 0)),
                  pl.BlockSpec((None, gf, d), lambda i, j: (j, 0, 0))],
        out_specs=pl.BlockSpec((tm, d), lambda i, j: (i, 0)),
        out_shape=jax.ShapeDtypeStruct((m, d), f32),
        scratch_shapes=[pltpu.VMEM((tm, d), bf16), pltpu.VMEM((tm, ROUTER_LANES), f32),
                        pltpu.VMEM((tm, d), f32)],
        compiler_params=pltpu.CompilerParams(dimension_semantics=("parallel", "arbitrary"),
                                             vmem_limit_bytes=48 * 1024 * 1024),
        name="moe",
    )(x2, g, wrh, wrl, br, wgg, wug, wdg)
    return out.reshape(shp)


def rmsnorm(x, g):
    xf = x.astype(jnp.float32)
    y = xf * lax.rsqrt(jnp.mean(xf * xf, axis=-1, keepdims=True) + RMS_EPS)
    return (y * g.astype(jnp.float32)).astype(x.dtype)


def rope_partial(x, pos):
    half = ROPE_DIM // 2
    inv = ROPE_THETA ** (-jnp.arange(half, dtype=jnp.float32) / half)
    ang = pos.astype(jnp.float32)[:, None] * inv
    cos = jnp.cos(ang)[:, None, :]
    sin = jnp.sin(ang)[:, None, :]
    xf = x.astype(jnp.float32)
    x1 = xf[..., :half]
    x2 = xf[..., half:ROPE_DIM]
    out = jnp.concatenate([x1 * cos - x2 * sin, x2 * cos + x1 * sin, xf[..., ROPE_DIM:]], axis=-1)
    return out.astype(x.dtype)


def masked_softmax(s, mask):
    s = jnp.where(mask, s.astype(jnp.float32), NEG_INF)
    p = jax.nn.softmax(s, axis=-1)
    return jnp.where(mask, p, 0.0)


def pool_mix(u, hist, p0, w_grp, scale):
    b, t, _ = u.shape
    ext = jnp.concatenate([hist.astype(u.dtype), u], axis=1).astype(jnp.float32)
    cs = jnp.pad(jnp.cumsum(ext, axis=1), ((0, 0), (1, 0), (0, 0)))
    cnt_pos = p0 + jnp.arange(t, dtype=jnp.int32) + 1
    means = []
    for gi, w in enumerate(POOL_WINDOWS):
        c = cs[..., gi * POOL_GDIM:(gi + 1) * POOL_GDIM]
        win_sum = c[:, POOL_HIST + 1:POOL_HIST + 1 + t] - c[:, POOL_HIST + 1 - w:POOL_HIST + 1 - w + t]
        cnt = jnp.minimum(cnt_pos, w).astype(jnp.float32)[None, :, None]
        means.append(win_sum / cnt)
    mean = jnp.stack(means, axis=2)
    d = mean - u.reshape(b, t, POOL_GROUPS, POOL_GDIM).astype(jnp.float32)
    y = jnp.einsum('btgc,gcd->btgd', d, w_grp.astype(jnp.float32)).reshape(b, t, POOL_DIM)
    return (y * scale.astype(jnp.float32)).astype(u.dtype)


PROJ_TM = 512


def _norm_mm_kernel(x_ref, g_ref, w_ref, o_ref):
    x = x_ref[...]
    h = x * lax.rsqrt(jnp.mean(x * x, axis=-1, keepdims=True) + RMS_EPS) * g_ref[...]
    o_ref[...] = jnp.dot(h.astype(bf16), w_ref[...], preferred_element_type=f32)


def _norm_mm(x, g, w):
    lead, d = x.shape[:-1], x.shape[-1]
    n = w.shape[1]
    x2 = x.reshape(-1, d)
    m = x2.shape[0]
    npad = -(-n // 128) * 128
    wb = jnp.pad(w.astype(bf16), ((0, 0), (0, npad - n)))
    tm = PROJ_TM
    assert m % tm == 0
    out = pl.pallas_call(
        _norm_mm_kernel, grid=(m // tm,),
        in_specs=[pl.BlockSpec((tm, d), lambda i: (i, 0)), pl.BlockSpec((1, d), lambda i: (0, 0)),
                  pl.BlockSpec((d, npad), lambda i: (0, 0))],
        out_specs=pl.BlockSpec((tm, npad), lambda i: (i, 0)),
        out_shape=jax.ShapeDtypeStruct((m, npad), f32),
        compiler_params=pltpu.CompilerParams(dimension_semantics=("parallel",),
                                             vmem_limit_bytes=48 * 1024 * 1024),
        name="norm_mm",
    )(x2, g.reshape(1, d).astype(f32), wb)
    return out[:, :n].reshape(lead + (n,))


def _mix_out_kernel(a_ref, b_ref, wa_ref, wb_ref, x_ref, o_ref):
    dd = lambda p, q: jnp.dot(p.astype(bf16), q, preferred_element_type=f32)
    o_ref[...] = x_ref[...] + (dd(a_ref[...], wa_ref[...]) + dd(b_ref[...], wb_ref[...]))


def _mix_out(a, b, w, x):
    d = x.shape[-1]
    ka, kb = a.shape[-1], b.shape[-1]
    x2 = x.reshape(-1, d)
    m = x2.shape[0]
    tm = PROJ_TM
    assert m % tm == 0
    wbf = w.astype(bf16)
    tile = lambda c: pl.BlockSpec((tm, c), lambda i: (i, 0))
    full = lambda r, c: pl.BlockSpec((r, c), lambda i: (0, 0))
    out = pl.pallas_call(
        _mix_out_kernel, grid=(m // tm,),
        in_specs=[tile(ka), tile(kb), full(ka, d), full(kb, d), tile(d)], out_specs=tile(d),
        out_shape=jax.ShapeDtypeStruct((m, d), f32),
        compiler_params=pltpu.CompilerParams(dimension_semantics=("parallel",)),
        name="mix_out",
    )(a.reshape(m, ka), b.reshape(m, kb), wbf[:ka], wbf[ka:], x2)
    return out.reshape(x.shape)


def _ab_feat_kernel(x_ref, g_ref, w_ref, qg_ref, kg_ref, cos_ref, sin_ref, hs_ref, he_ref, rot_ref,
                    pool_ref, rows_ref, win_ref, qh_ref, ql_ref, gt_ref, ks_ref, kw_ref, vst_ref, vwt_ref):
    D, KV = HEAD_DIM, NSA_KV_HEADS * HEAD_DIM
    x = x_ref[...]
    h = x * lax.rsqrt(jnp.mean(x * x, axis=-1, keepdims=True) + RMS_EPS) * g_ref[...]
    u = jnp.dot(h.astype(bf16), w_ref[...], preferred_element_type=f32)
    off_kv = POOL_DIM + NSA_DIM
    pool_ref[...] = u[:, :POOL_DIM]
    cos, sin = cos_ref[...], sin_ref[...]

    def norm_rope(z, gain):
        n = z.shape[1] // D
        hs, he, rot = hs_ref[0:n * D, :], he_ref[:, 0:n * D], rot_ref[0:n * D, 0:n * D]
        ms = _dot_exact_rhs(_dot_exact_rhs(z * z, hs), he) * (1.0 / D)
        zn = z * lax.rsqrt(ms + RMS_EPS) * gain
        wide = lambda t: jnp.concatenate([t] * (n // 2), axis=1)
        return zn * wide(cos) + _dot_exact_rhs(zn, rot) * wide(sin)

    q = norm_rope(u[:, POOL_DIM:off_kv], qg_ref[...]) * (D ** -0.5)
    for kvh in range(NSA_KV_HEADS):
        qt = jnp.concatenate([q[:, (kvh * NSA_GQ + g) * D:(kvh * NSA_GQ + g + 1) * D].T
                              for g in range(NSA_GQ)], axis=1)
        hi, lo = _split(qt)
        qh_ref[kvh] = hi
        ql_ref[kvh] = lo
    kv = u[:, off_kv:off_kv + KV_COLS]
    slab = lambda i: kv[:, i * KV:(i + 1) * KV]
    kr = norm_rope(jnp.concatenate([slab(2), slab(4)], axis=1), kg_ref[...])
    k_slc, k_win = kr[:, 0:KV], kr[:, KV:2 * KV]
    rows_ref[...] = jnp.concatenate([slab(0), slab(1), k_slc, slab(3)], axis=1)
    win_ref[...] = jnp.concatenate([k_win, slab(5)], axis=1)
    for kvh in range(NSA_KV_HEADS):
        hsl = slice(kvh * D, (kvh + 1) * D)
        ks_ref[kvh] = k_slc[:, hsl].astype(bf16)
        kw_ref[kvh] = k_win[:, hsl].astype(bf16)
        vst_ref[kvh] = slab(3)[:, hsl].T.astype(bf16)
        vwt_ref[kvh] = slab(5)[:, hsl].T.astype(bf16)
    gt_ref[...] = u[:, off_kv + KV_COLS:].T


def ab_features_prompt(x, norm_g, w_in, q_norm, k_norm):
    B, T, d = x.shape
    QB, D, KVH, G = Q_BLOCK, HEAD_DIM, NSA_KV_HEADS, NSA_GQ
    KV = KVH * D
    assert T % QB == 0
    nqb = T // QB
    npad = -(-IN_COLS // 128) * 128
    assert npad - (POOL_DIM + NSA_DIM + KV_COLS) == 128
    wb = jnp.pad(w_in.astype(bf16), ((0, 0), (0, npad - IN_COLS)))
    half = ROPE_DIM // 2
    inv = ROPE_THETA ** (-jnp.arange(half, dtype=f32) / half)
    ang = jnp.arange(T, dtype=jnp.int32).astype(f32)[:, None] * inv
    cos_h = jnp.concatenate([jnp.cos(ang), jnp.cos(ang), jnp.ones((T, D - ROPE_DIM), f32)], axis=1)
    sin_h = jnp.concatenate([jnp.sin(ang), jnp.sin(ang), jnp.zeros((T, D - ROPE_DIM), f32)], axis=1)
    cos_t, sin_t = jnp.tile(cos_h, (1, 2)), jnp.tile(sin_h, (1, 2))
    r_i = lax.broadcasted_iota(jnp.int32, (NSA_DIM, NSA_DIM), 0)
    c_i = lax.broadcasted_iota(jnp.int32, (NSA_DIM, NSA_DIM), 1)
    same = (r_i // D) == (c_i // D)
    rd, cd = r_i % D, c_i % D
    rot = jnp.where(same & (cd < half) & (rd == cd + half), -1.0,
                    jnp.where(same & (cd >= half) & (cd < ROPE_DIM) & (rd == cd - half), 1.0, 0.0)).astype(bf16)
    hs = (lax.broadcasted_iota(jnp.int32, (NSA_DIM, 128), 0) // D
          == lax.broadcasted_iota(jnp.int32, (NSA_DIM, 128), 1)).astype(bf16)
    qg = jnp.tile(q_norm.astype(f32), NSA_HEADS).reshape(1, NSA_DIM)
    kg = jnp.concatenate([jnp.tile(k_norm[1].astype(f32), KVH), jnp.tile(k_norm[2].astype(f32), KVH)]).reshape(1, 2 * KV)
    full = lambda a: pl.BlockSpec(a.shape, lambda b, i: (0,) * a.ndim)
    tok = lambda c: pl.BlockSpec((None, QB, c), lambda b, i: (b, i, 0))
    f = jax.ShapeDtypeStruct
    outs = pl.pallas_call(
        _ab_feat_kernel, grid=(B, nqb),
        in_specs=[tok(d), pl.BlockSpec((1, d), lambda b, i: (0, 0)), full(wb), full(qg), full(kg),
                  pl.BlockSpec((QB, 2 * D), lambda b, i: (i, 0)), pl.BlockSpec((QB, 2 * D), lambda b, i: (i, 0)),
                  full(hs), pl.BlockSpec((128, NSA_DIM), lambda b, i: (0, 0)), full(rot)],
        out_specs=[tok(POOL_DIM), tok(4 * KV), tok(2 * KV),
                   pl.BlockSpec((None, KVH, None, D, G * QB), lambda b, i: (b, 0, i, 0, 0)),
                   pl.BlockSpec((None, KVH, None, D, G * QB), lambda b, i: (b, 0, i, 0, 0)),
                   pl.BlockSpec((None, None, 128, QB), lambda b, i: (b, i, 0, 0)),
                   pl.BlockSpec((None, KVH, QB, D), lambda b, i: (b, 0, i, 0)),
                   pl.BlockSpec((None, KVH, QB, D), lambda b, i: (b, 0, i, 0)),
                   pl.BlockSpec((None, KVH, D, QB), lambda b, i: (b, 0, 0, i)),
                   pl.BlockSpec((None, KVH, D, QB), lambda b, i: (b, 0, 0, i))],
        out_shape=[f((B, T, POOL_DIM), f32), f((B, T, 4 * KV), f32), f((B, T, 2 * KV), f32),
                   f((B, KVH, nqb, D, G * QB), bf16), f((B, KVH, nqb, D, G * QB), bf16),
                   f((B, nqb, 128, QB), f32), f((B, KVH, T, D), bf16), f((B, KVH, T, D), bf16),
                   f((B, KVH, D, T), bf16), f((B, KVH, D, T), bf16)],
        compiler_params=pltpu.CompilerParams(dimension_semantics=("parallel", "parallel"),
                                             vmem_limit_bytes=48 * 1024 * 1024),
        name="ab_feat",
    )(x, norm_g.reshape(1, d).astype(f32), wb, qg, kg, cos_t, sin_t, hs, hs.T, rot)
    pool_in, rows, win, qh, ql, gt, ks, kw, vst, vwt = outs
    gT = (gt[:, :, :NSA_HEADS * 3].reshape(B, nqb, KVH, G, 3, QB).transpose(0, 2, 1, 4, 3, 5)
          .reshape(B, KVH, nqb, 3, G * QB))
    return pool_in, rows, win, (qh, ql, gT, ks, vst, kw, vwt)


def ab_features(x, norm_g, pos, w_in, q_norm, k_norm):
    b, t = x.shape[:2]
    u = _norm_mm(x, norm_g, w_in)
    off_kv = POOL_DIM + NSA_DIM
    pool_in = u[..., :POOL_DIM]
    q = u[..., POOL_DIM:off_kv].reshape(b, t, NSA_HEADS, HEAD_DIM)
    kv = u[..., off_kv:off_kv + KV_COLS].reshape(b, t, 6, NSA_KV_HEADS, HEAD_DIM)
    gl = u[..., off_kv + KV_COLS:].reshape(b, t, NSA_HEADS, 3)
    q = rope_partial(rmsnorm(q, q_norm), pos)
    k_slc = rope_partial(rmsnorm(kv[:, :, 2], k_norm[1]), pos)
    k_win = rope_partial(rmsnorm(kv[:, :, 4], k_norm[2]), pos)
    rows = jnp.stack([kv[:, :, 0], kv[:, :, 1], k_slc, kv[:, :, 3]], axis=2)
    win = jnp.stack([k_win, kv[:, :, 5]], axis=2)
    return pool_in, q, gl, rows, win


def compress_kv(k_rows, v_rows, pos_w, phi, k_gain):
    b, length = k_rows.shape[:2]
    n_chunk = length // CMP_STRIDE

    def weighted_block_mean(rows, w):
        ch = rows.reshape(b, n_chunk, CMP_STRIDE, NSA_KV_HEADS, HEAD_DIM)
        return (jnp.einsum('bnlhd,hl->bnhd', ch[:, :-1], w[:, :CMP_STRIDE])
                + jnp.einsum('bnlhd,hl->bnhd', ch[:, 1:], w[:, CMP_STRIDE:]))

    cmp_end = jnp.arange(n_chunk - 1, dtype=jnp.int32) * CMP_STRIDE + (CMP_LEN - 1)
    kc = jnp.einsum('bnhd,de->bnhe', weighted_block_mean(k_rows, pos_w[0]), phi[0])
    kc = rope_partial(rmsnorm(kc, k_gain), cmp_end)
    vc = jnp.einsum('bnhd,de->bnhe', weighted_block_mean(v_rows, pos_w[1]), phi[1])
    return kc, vc, cmp_end


def nsa_attend(q, pos_q, gl, kc, vc, cmp_end, ks, vs, kw, vw, pos_w):
    f32 = jnp.float32
    b, tq = q.shape[:2]
    qg = q.reshape(b, tq, NSA_KV_HEADS, NSA_GQ, HEAD_DIM)
    scale = HEAD_DIM ** -0.5
    s_c = jnp.einsum('bqhgd,bchd->bhgqc', qg, kc) * scale
    p_c = masked_softmax(s_c, cmp_end[None, :] <= pos_q[:, None])
    o_c = jnp.einsum('bhgqc,bchd->bqhgd', p_c, vc.astype(f32))
    imp = p_c.sum(axis=2)
    imp_chunk = 0.5 * (jnp.pad(imp, ((0, 0), (0, 0), (0, 0), (0, 1)))
                       + jnp.pad(imp, ((0, 0), (0, 0), (0, 0), (1, 0))))
    n_slc = ks.shape[1] // SLC_LEN
    imp_blk = imp_chunk.reshape(b, NSA_KV_HEADS, tq, n_slc, SLC_LEN // CMP_STRIDE).sum(-1)
    blk = jnp.arange(n_slc, dtype=jnp.int32)[None, :]
    cur = (pos_q // SLC_LEN)[:, None]
    forced = (blk == 0) | (blk == cur) | (blk == cur - 1)
    score = jnp.where(blk > cur, -1.0, jnp.where(forced, 1e6, imp_blk))
    n_sel = min(N_SEL, n_slc)
    _, idx = lax.top_k(score, n_sel)
    gather = jax.vmap(jax.vmap(lambda rows, i: rows[i]))
    ksb = ks.reshape(b, n_slc, SLC_LEN, NSA_KV_HEADS, HEAD_DIM).transpose(0, 3, 1, 2, 4)
    vsb = vs.reshape(b, n_slc, SLC_LEN, NSA_KV_HEADS, HEAD_DIM).transpose(0, 3, 1, 2, 4)
    kg = gather(ksb, idx)
    vg = gather(vsb, idx)
    kpos = idx[..., None] * SLC_LEN + jnp.arange(SLC_LEN, dtype=jnp.int32)
    n_keys = n_sel * SLC_LEN
    m_s = (kpos <= pos_q[None, None, :, None, None]).reshape(b, NSA_KV_HEADS, 1, tq, n_keys)
    s_s = jnp.einsum('bqhgd,bhqnld->bhgqnl', qg, kg).reshape(b, NSA_KV_HEADS, NSA_GQ, tq, n_keys) * scale
    p_s = masked_softmax(s_s, m_s)
    o_s = jnp.einsum('bhgqk,bhqkd->bqhgd', p_s,
                     vg.reshape(b, NSA_KV_HEADS, tq, n_keys, HEAD_DIM).astype(f32))
    s_w = jnp.einsum('bqhgd,bkhd->bhgqk', qg, kw) * scale
    dq = pos_q[:, None] - pos_w[None, :]
    m_w = (dq >= 0) & (dq < WINDOW) & (pos_w[None, :] >= 0)
    p_w = masked_softmax(s_w, m_w)
    o_w = jnp.einsum('bhgqk,bkhd->bqhgd', p_w, vw.astype(f32))
    g = jax.nn.sigmoid(gl.astype(f32)).reshape(b, tq, NSA_KV_HEADS, NSA_GQ, 3)
    o = g[..., 0:1] * o_c + g[..., 1:2] * o_s + g[..., 2:3] * o_w
    return o.reshape(b, tq, NSA_DIM)


def nsa_prompt(q, gl, rows, win, cmp_pos_w, cmp_phi, k_gain):
    b, t = q.shape[:2]
    kc, vc, cmp_end = compress_kv(rows[:, :, 0], rows[:, :, 1], cmp_pos_w, cmp_phi, k_gain)
    ks, vs = rows[:, :, 2], rows[:, :, 3]
    win_pad = jnp.pad(win, ((0, 0), (WINDOW, 0), (0, 0), (0, 0), (0, 0)))

    def block(i):
        s0 = i * Q_BLOCK
        qb = lax.dynamic_slice_in_dim(q, s0, Q_BLOCK, axis=1)
        gb = lax.dynamic_slice_in_dim(gl, s0, Q_BLOCK, axis=1)
        wb = lax.dynamic_slice_in_dim(win_pad, s0, WINDOW + Q_BLOCK, axis=1)
        pos_q = s0 + jnp.arange(Q_BLOCK, dtype=jnp.int32)
        pos_w = s0 - WINDOW + jnp.arange(WINDOW + Q_BLOCK, dtype=jnp.int32)
        return nsa_attend(qb, pos_q, gb, kc, vc, cmp_end, ks, vs, wb[:, :, 0], wb[:, :, 1], pos_w)

    out = lax.map(block, jnp.arange(t // Q_BLOCK, dtype=jnp.int32))
    return jnp.moveaxis(out, 0, 1).reshape(b, t, NSA_DIM)


def nsa_sample(q, gl, rows_new, win_new, pool_kv, page_table, win_buf, cmp_pos_w, cmp_phi, k_gain):
    bd, ts = q.shape[:2]
    past_len = page_table.shape[1] * pool_kv.shape[1]
    past = pool_kv[page_table].reshape(bd, past_len, 4, NSA_KV_HEADS, HEAD_DIM)
    rows = jnp.concatenate([past, rows_new.astype(past.dtype)], axis=1)
    pad = (-(past_len + ts)) % SLC_LEN
    rows = jnp.pad(rows, ((0, 0), (0, pad), (0, 0), (0, 0), (0, 0)))
    kc, vc, cmp_end = compress_kv(rows[:, :, 0], rows[:, :, 1], cmp_pos_w, cmp_phi, k_gain)
    lb = win_buf.shape[1]
    win = jnp.concatenate([win_buf, win_new.astype(win_buf.dtype)], axis=1)
    pos_q = past_len + jnp.arange(ts, dtype=jnp.int32)
    pos_w = past_len - lb + jnp.arange(lb + ts, dtype=jnp.int32)
    o = nsa_attend(q, pos_q, gl, kc, vc, cmp_end, rows[:, :, 2], rows[:, :, 3],
                   win[:, :, 0], win[:, :, 1], pos_w)
    keep = min(WINDOW, lb + ts)
    return o, win[:, lb + ts - keep:]


def ab_layer_prompt(x, norm_g, w_in, w_out, q_norm, k_norm, cmp_pos_w, cmp_phi, pool_w, pool_scale):
    b, t = x.shape[:2]
    pool_in, rows2, win2, ops = ab_features_prompt(x, norm_g, w_in, q_norm, k_norm)
    rows = rows2.reshape(b, t, 4, NSA_KV_HEADS, HEAD_DIM)
    win = win2.reshape(b, t, 2, NSA_KV_HEADS, HEAD_DIM)
    hist0 = jnp.zeros((b, POOL_HIST, POOL_DIM), x.dtype)
    pool_out = pool_mix(pool_in, hist0, 0, pool_w, pool_scale)
    kc, vc, _ = compress_kv(rows[:, :, 0], rows[:, :, 1], cmp_pos_w, cmp_phi, k_norm[0])
    nsa_out = nsa_prompt_pallas(ops, kc, vc)
    x_new = _mix_out(pool_out, nsa_out, w_out, x)
    keep = min(WINDOW, t)
    return x_new, rows, win[:, t - keep:], pool_in[:, t - POOL_HIST:]


def ab_layer_sample(x, norm_g, pools, page_table, win_bufs, layer, pool_hist, w_in, w_out, q_norm, k_norm,
                    cmp_pos_w, cmp_phi, pool_w, pool_scale):
    ts = x.shape[1]
    past_len = page_table.shape[1] * pools.shape[2]
    pos = past_len + jnp.arange(ts, dtype=jnp.int32)
    pool_in, q, gl, rows, win = ab_features(x, norm_g, pos, w_in, q_norm, k_norm)
    pool_out = pool_mix(pool_in, pool_hist, past_len, pool_w, pool_scale)
    nsa_out = nsa_sample_pallas(q, gl, rows, win, pools, page_table, win_bufs, layer,
                                cmp_pos_w, cmp_phi, k_norm[0])
    win_buf = win_bufs[layer]
    lb = win_buf.shape[1]
    keep = min(WINDOW, lb + ts)
    new_win = jnp.concatenate([win_buf, win.astype(win_buf.dtype)], axis=1)[:, lb + ts - keep:]
    x_new = _mix_out(pool_out, nsa_out, w_out, x)
    new_hist = jnp.concatenate([pool_hist.astype(pool_in.dtype), pool_in], axis=1)[:, -POOL_HIST:]
    return x_new, rows, new_win, new_hist


def wkv_scan(s0, r, w, k, v, kk, a):
    def step(s, inp):
        r_t, w_t, k_t, v_t, kk_t, a_t = inp
        sa = jnp.einsum('bhij,bhj->bhi', s, -kk_t)
        s = (s * w_t[:, :, None, :] + sa[..., None] * (kk_t * a_t)[:, :, None, :]
             + v_t[..., None] * k_t[:, :, None, :])
        return s, jnp.einsum('bhij,bhj->bhi', s, r_t)

    xs = tuple(jnp.moveaxis(z.astype(jnp.float32), 1, 0) for z in (r, w, k, v, kk, a))
    s, o = lax.scan(step, s0.astype(jnp.float32), xs)
    return s, jnp.moveaxis(o, 0, 1)


RW_TM = 256
HEAD_LANES = 128


def _rwkv_pre_kernel(*refs, seq_len, has_vres):
    it = iter(refs)
    x_ref, xprev_ref, fp_ref, g_ref, mu_ref = [next(it) for _ in range(5)]
    wr_ref, wk_ref, wv_ref = [next(it) for _ in range(3)]
    w0_ref, w1_ref, w2_ref, a0_ref, a1_ref, a2_ref, g1_ref, g2_ref = [next(it) for _ in range(8)]
    kkw_ref, kaw_ref, hsum_ref, hexp_ref = [next(it) for _ in range(4)]
    if has_vres:
        v0_ref, v1_ref, v2_ref, vf_ref = [next(it) for _ in range(4)]
    r_ref, k_ref, v_ref, lw_ref, kk_ref, a_ref, gg_ref = [next(it) for _ in range(7)]
    dd = lambda a, b: jnp.dot(a, b, preferred_element_type=f32)
    tm = x_ref.shape[0]
    norm = lambda z: z * lax.rsqrt(jnp.mean(z * z, axis=-1, keepdims=True) + RMS_EPS) * g_ref[...]
    h = norm(x_ref[...])
    row = lax.broadcasted_iota(jnp.int32, (tm, 1), 0)
    rolled = pltpu.roll(h, 1, axis=0)
    if seq_len % tm == 0:
        first = (pl.program_id(0) % (seq_len // tm)) == 0
        last_prev = norm(xprev_ref[...])[xprev_ref.shape[0] - 1:, :]
        prev = jnp.where(row == 0, jnp.where(first, fp_ref[...], last_prev), rolled)
    else:
        nseq = tm // seq_len
        sel = (lax.broadcasted_iota(jnp.int32, (tm, nseq), 0)
               == seq_len * lax.broadcasted_iota(jnp.int32, (tm, nseq), 1)).astype(bf16)
        prev = jnp.where(row % seq_len == 0, _dot_exact_lhs(sel, fp_ref[...]), rolled)
    xx = prev - h
    mix = lambda j: (h + xx * mu_ref[j:j + 1, :]).astype(bf16)
    xr, xw, xk, xv, xa, xg = [mix(j) for j in range(6)]
    k = dd(xk, wk_ref[...])
    v = dd(xv, wv_ref[...])
    r_ref[...] = dd(xr, wr_ref[...])
    z = -(w0_ref[...] + dd(jnp.tanh(dd(xw, w1_ref[...])).astype(bf16), w2_ref[...]))
    softplus = jnp.maximum(z, 0.0) + jnp.log(1.0 + jnp.exp(-jnp.abs(z)))
    lw_ref[...] = -jnp.exp(-softplus - 0.5)
    a = jax.nn.sigmoid(a0_ref[...] + dd(dd(xa, a1_ref[...]).astype(bf16), a2_ref[...]))
    a_ref[...] = a
    gg_ref[...] = dd(jax.nn.sigmoid(dd(xg, g1_ref[...])).astype(bf16), g2_ref[...])
    if has_vres:
        v = v + (vf_ref[...] - v) * jax.nn.sigmoid(
            v0_ref[...] + dd(dd(xv, v1_ref[...]).astype(bf16), v2_ref[...]))
    v_ref[...] = v
    kk = k * kkw_ref[...]
    nrm = jnp.maximum(jnp.sqrt(_dot_exact_rhs(kk * kk, hsum_ref[...])), 1e-12)
    kk_ref[...] = kk * _dot_exact_rhs(1.0 / nrm, hexp_ref[...])
    k_ref[...] = k * (1.0 + (a - 1.0) * kaw_ref[...])


def _rwkv_post_kernel(o_ref, r_ref, k_ref, v_ref, gg_ref, x_ref, gnw_ref, gnb_ref, rk_ref, hsum_ref,
                      hexp_ref, wo_ref, out_ref):
    hs = hsum_ref[...]
    he = hexp_ref[...]
    head_sum = lambda z: _dot_exact_rhs(_dot_exact_rhs(z, hs), he)
    o = o_ref[...]
    v = v_ref[...]
    d = o - head_sum(o) * (1.0 / RWKV_N)
    var = head_sum(d * d) * (1.0 / RWKV_N)
    on = d * lax.rsqrt(var + GN_EPS) * gnw_ref[...] + gnb_ref[...]
    on = on + head_sum(r_ref[...] * k_ref[...] * rk_ref[...]) * v
    y = (on * gg_ref[...]).astype(bf16)
    out_ref[...] = x_ref[...] + jnp.dot(y, wo_ref[...], preferred_element_type=f32)


def rwkv_layer_fused(x, norm_g, shift_prev, s0, v_first, vres, mu, wr, wk, wv, wo, w0, w1, w2, a0, a1, a2,
                     g1, g2, k_k, k_a, r_k, gn_w, gn_b):
    n, t, d = x.shape
    m = n * t
    tm = RW_TM
    assert m % tm == 0 and (t % tm == 0 or tm % t == 0)
    x2 = x.reshape(m, d)
    row = lambda z: z.reshape(1, d).astype(f32)
    cb = lambda z: z.astype(bf16)
    hd = lax.broadcasted_iota(jnp.int32, (d, HEAD_LANES), 0) // RWKV_N
    hsum = (hd == lax.broadcasted_iota(jnp.int32, (d, HEAD_LANES), 1)).astype(bf16)
    hexp = hsum.T
    tile = pl.BlockSpec((tm, d), lambda i: (i, 0))
    full = lambda z: pl.BlockSpec(z.shape, lambda i: (0,) * z.ndim)
    if t % tm == 0:
        tps = t // tm
        fp = shift_prev.reshape(n, 1, d).astype(f32)
        fp_spec = pl.BlockSpec((None, 1, d), lambda i: (i // tps, 0, 0))
    else:
        fp = shift_prev.astype(f32)
        fp_spec = pl.BlockSpec((tm // t, d), lambda i: (i, 0))
    xprev_spec = pl.BlockSpec((8, d), lambda i: (jnp.maximum(i * (tm // 8) - 1, 0), 0))
    mu8 = jnp.pad(mu.astype(f32), ((0, 2), (0, 0)))
    consts = [row(norm_g), mu8, cb(wr), cb(wk), cb(wv), row(w0), cb(w1), cb(w2), row(a0), cb(a1), cb(a2),
              cb(g1), cb(g2), row(k_k), row(k_a), hsum, hexp]
    args = [x2, x2, fp] + consts
    specs = [tile, xprev_spec, fp_spec] + [full(c) for c in consts]
    if vres is not None:
        v0, v1, v2 = vres
        extra = [row(v0), cb(v1), cb(v2)]
        args += extra + [v_first.reshape(m, d)]
        specs += [full(c) for c in extra] + [tile]
    cp = pltpu.CompilerParams(dimension_semantics=("parallel",), vmem_limit_bytes=56 * 1024 * 1024)
    r, k, v, lw, kk, a, gg = pl.pallas_call(
        functools.partial(_rwkv_pre_kernel, seq_len=t, has_vres=vres is not None),
        grid=(m // tm,), in_specs=specs, out_specs=[tile] * 7,
        out_shape=[jax.ShapeDtypeStruct((m, d), f32)] * 7, compiler_params=cp, name="rwkv_pre",
    )(*args)
    if vres is None:
        v_first = v.reshape(n, t, d)
    seq = lambda z: z.reshape(n, t, d)
    o, s = wkv_chunked(seq(r), seq(lw), seq(k), seq(v), seq(kk), seq(a), s0.astype(f32))
    o = o.reshape(m, d)
    post_consts = [row(gn_w), row(gn_b), r_k.reshape(1, d).astype(f32), hsum, hexp, cb(wo)]
    x_new = pl.pallas_call(
        _rwkv_post_kernel, grid=(m // tm,),
        in_specs=[tile] * 6 + [full(c) for c in post_consts], out_specs=tile,
        out_shape=jax.ShapeDtypeStruct((m, d), f32), compiler_params=cp, name="rwkv_post",
    )(o, r, k, v, gg, x2, *post_consts)
    h_last = rmsnorm(x[:, -1], norm_g)
    return x_new.reshape(n, t, d), v_first, s, h_last


def rwkv_layer(h, shift_prev, s0, v_first, vres, mu, wr, wk, wv, wo, w0, w1, w2, a0, a1, a2,
               g1, g2, k_k, k_a, r_k, gn_w, gn_b):
    f32 = jnp.float32
    b, t, d = h.shape
    prev = jnp.concatenate([shift_prev[:, None, :].astype(h.dtype), h[:, :-1]], axis=1)
    xx = prev - h
    xr, xw, xk, xv, xa, xg = [h + xx * mu[j] for j in range(6)]
    r = _mm(xr, wr)
    k = _mm(xk, wk)
    v = _mm(xv, wv)
    w_log = -jax.nn.softplus(-(w0 + jnp.tanh(xw @ w1) @ w2).astype(f32)) - 0.5
    decay = jnp.exp(-jnp.exp(w_log))
    if vres is None:
        v_first = v
    else:
        v0, v1, v2 = vres
        v = v + (v_first - v) * jax.nn.sigmoid(v0 + (xv @ v1) @ v2)
    a = jax.nn.sigmoid((a0 + (xa @ a1) @ a2).astype(f32))
    g = jax.nn.sigmoid(xg @ g1) @ g2

    def heads(z):
        return z.reshape(b, t, RWKV_HEADS, RWKV_N).astype(f32)

    kk = heads(k * k_k)
    kk = kk / jnp.maximum(jnp.sqrt(jnp.sum(kk * kk, axis=-1, keepdims=True)), 1e-12)
    k = k.astype(f32) * (1.0 + (a - 1.0) * k_a.astype(f32))
    rh, kh, vh, ah, dh = heads(r), heads(k), heads(v), heads(a), heads(decay)
    tp = -(-t // WKV_CHUNK) * WKV_CHUNK
    padt = lambda z: jnp.pad(z, ((0, 0), (0, tp - t), (0, 0)))
    o, s = wkv_chunked(padt(r), padt(-jnp.exp(w_log)), padt(k), padt(v), padt(kk.reshape(b, t, d)),
                       padt(a), s0.astype(f32))
    o = o[:, :t].reshape(b, t, RWKV_HEADS, RWKV_N)
    mean = jnp.mean(o, axis=-1, keepdims=True)
    var = jnp.mean(jnp.square(o - mean), axis=-1, keepdims=True)
    o = ((o - mean) * lax.rsqrt(var + GN_EPS) * gn_w.reshape(RWKV_HEADS, RWKV_N).astype(f32)
         + gn_b.reshape(RWKV_HEADS, RWKV_N).astype(f32))
    o = o + jnp.sum(rh * kh * r_k.astype(f32), axis=-1, keepdims=True) * vh
    y = _mm((o.reshape(b, t, d) * g.astype(f32)).astype(h.dtype), wo)
    return y, v_first, s, h[:, -1]


def hier_moe(h, wc, bc, wf, bf, wg, wu, wd):
    f32 = jnp.float32
    hp = lax.Precision.HIGHEST
    lc = jnp.dot(h, wc, precision=hp).astype(f32) + bc.astype(f32)
    g_idx = jnp.argmax(lc, axis=-1)
    g_w = jnp.max(jax.nn.softmax(lc, axis=-1), axis=-1)
    g_hot = jax.nn.one_hot(g_idx, MOE_GROUPS, dtype=f32)
    lf = (jnp.dot(h, wf, precision=hp).astype(f32) + bf.astype(f32)).reshape(h.shape[:-1] + (MOE_GROUPS, MOE_EPG))
    lf_sel = jnp.einsum('btge,btg->bte', lf, g_hot)
    top_v, top_i = lax.top_k(lf_sel, MOE_TOPK)
    top_w = jax.nn.softmax(top_v, axis=-1) * g_w[..., None]
    e_id = g_idx[..., None] * MOE_EPG + top_i
    gate = jnp.einsum('btke,btk->bte', jax.nn.one_hot(e_id, N_EXPERTS, dtype=f32), top_w)
    hg = jnp.einsum('btd,edf->btef', h, wg)
    hu = jnp.einsum('btd,edf->btef', h, wu)
    act = (jax.nn.silu(hg) * hu * gate[..., None].astype(h.dtype)).astype(h.dtype)
    return jnp.einsum('btef,efd->btd', act, wd)


def kernel(x_prompt, x_sample, cache_nsa_kv, cache_win_kv, state_pool, state_wkv, state_shift,
           page_table, norm_mix, norm_ffn, ab_w_in, ab_w_out, ab_q_norm, ab_k_norm, cmp_pos_w,
           cmp_phi, pool_w, pool_scale, rw_mu, rw_wr, rw_wk, rw_wv, rw_wo, rw_w0, rw_w1, rw_w2,
           rw_a0, rw_a1, rw_a2, rw_v0, rw_v1, rw_v2, rw_g1, rw_g2, rw_kk, rw_ka, rw_rk, rw_gn_w,
           rw_gn_b, moe_wc, moe_bc, moe_wf, moe_bf, moe_wg, moe_wu, moe_wd):
    xp, xs = x_prompt, x_sample
    vf_p, vf_s = None, None
    nsa_p, nsa_s, win_p, win_s, pool_p, pool_s = [], [], [], [], [], []
    wkv_p, wkv_s, sh_p, sh_s = [], [], [], []
    for l in range(DEPTH):
        if l % 2 == 0:
            i = l // 2
            wts = (ab_w_in[i], ab_w_out[i], ab_q_norm[i], ab_k_norm[i], cmp_pos_w[i], cmp_phi[i],
                   pool_w[i], pool_scale[i])
            xp, r_p, w_p, h_p = ab_layer_prompt(xp, norm_mix[l], *wts)
            xs, r_s, w_s, h_s = ab_layer_sample(xs, norm_mix[l], cache_nsa_kv, page_table,
                                                cache_win_kv, i, state_pool[i], *wts)
            nsa_p.append(r_p)
            nsa_s.append(r_s)
            win_p.append(w_p)
            win_s.append(w_s)
            pool_p.append(h_p)
            pool_s.append(h_s)
        else:
            j = l // 2
            vres = None if j == 0 else (rw_v0[j - 1], rw_v1[j - 1], rw_v2[j - 1])
            wts = (rw_mu[j], rw_wr[j], rw_wk[j], rw_wv[j], rw_wo[j], rw_w0[j], rw_w1[j], rw_w2[j],
                   rw_a0[j], rw_a1[j], rw_a2[j], rw_g1[j], rw_g2[j], rw_kk[j], rw_ka[j], rw_rk[j],
                   rw_gn_w[j], rw_gn_b[j])
            bp = xp.shape[0]
            zero_shift = jnp.zeros((bp, D_MODEL), xp.dtype)
            zero_state = jnp.zeros((bp, RWKV_HEADS, RWKV_N, RWKV_N), jnp.float32)
            xp, vf_p, s_p, shp = rwkv_layer_fused(xp, norm_mix[l], zero_shift, zero_state, vf_p, vres, *wts)
            xs, vf_s, s_s, shs = rwkv_layer_fused(xs, norm_mix[l], state_shift[j], state_wkv[j], vf_s, vres, *wts)
            wkv_p.append(s_p)
            wkv_s.append(s_s)
            sh_p.append(shp)
            sh_s.append(shs)
        prep =moe_prep(norm_ffn[l], moe_wc[l], moe_bc[l], moe_wf[l], moe_bf[l], moe_wg[l], moe_wu[l], moe_wd[l])
        xp = moe_residual(xp, prep)
        xs = moe_residual(xs, prep)
    return (xp, xs, jnp.stack(nsa_p), jnp.stack(nsa_s), jnp.stack(win_p), jnp.stack(win_s),
            jnp.stack(pool_p), jnp.stack(pool_s), jnp.stack(wkv_p), jnp.stack(wkv_s),
            jnp.stack(sh_p), jnp.stack(sh_s))
```

```python
import functools

import jax
import jax.numpy as jnp
from jax import lax
from jax.experimental import pallas as pl
from jax.experimental.pallas import tpu as pltpu


def _mm_kernel(x_ref, w_ref, o_ref):
    o_ref[...] = jnp.dot(x_ref[...].astype(jnp.bfloat16), w_ref[...],
                         preferred_element_type=jnp.float32)


def _mm(x, w):
    lead = x.shape[:-1]
    k = x.shape[-1]
    n = w.shape[1]
    x2 = x.reshape(-1, k)
    m = x2.shape[0]
    npad = -(-n // 128) * 128
    wb = w.astype(jnp.bfloat16)
    if npad != n:
        wb = jnp.pad(wb, ((0, 0), (0, npad - n)))
    tn = npad
    for cand in (512, 640, 384, 256, 128):
        if npad % cand == 0:
            tn = cand
            break
    tm = 512 if m % 512 == 0 else m
    out = pl.pallas_call(
        _mm_kernel,
        grid=(m // tm, npad // tn),
        in_specs=[pl.BlockSpec((tm, k), lambda i, j: (i, 0)),
                  pl.BlockSpec((k, tn), lambda i, j: (0, j))],
        out_specs=pl.BlockSpec((tm, tn), lambda i, j: (i, j)),
        out_shape=jax.ShapeDtypeStruct((m, npad), jnp.float32),
        name="mm",
    )(x2, wb)
    return out[:, :n].reshape(lead + (n,))


f32 = jnp.float32
bf16 = jnp.bfloat16
WKV_CHUNK = 64
WKV_PAIRS = 8
WKV_SEQS = 2
WKV_PASSES = 1
WKV_GRAM_PASSES = 1


def _split(x):
    hi = x.astype(bf16)
    lo = (x - hi.astype(f32)).astype(bf16)
    return hi, lo


def _mmul(a, b, passes, nt=False):
    dn = (((1,), (1,)), ((), ())) if nt else (((1,), (0,)), ((), ()))
    d = lambda x, y: lax.dot_general(x, y, dn, preferred_element_type=f32)
    if passes == 1:
        return d(a.astype(bf16), b.astype(bf16))
    ah, al = _split(a)
    bh, bl = _split(b)
    return d(ah, bh) + (d(ah, bl) + d(al, bh))


def _wkv_kernel(r_ref, lw_ref, k_ref, v_ref, kk_ref, a_ref, s0_ref, o_ref, sT_ref, st_scr, *, passes):
    C = WKV_CHUNK
    nb = r_ref.shape[0]
    c = pl.program_id(1)
    nc = pl.num_programs(1)
    row = lax.broadcasted_iota(jnp.int32, (2 * C, 2 * C), 0)
    col = lax.broadcasted_iota(jnp.int32, (2 * C, 2 * C), 1)
    bd = (row < C) == (col < C)
    strict = bd & ((row % C) > (col % C))
    incl = bd & ((row % C) >= (col % C))
    eye = (row == col).astype(f32)
    lane_s = col < C
    m1 = lax.broadcasted_iota(jnp.int32, (C, 2 * C), 1) < C
    tri = (lax.broadcasted_iota(jnp.int32, (C, C), 0)
           >= lax.broadcasted_iota(jnp.int32, (C, C), 1)).astype(bf16)

    @pl.when(c == 0)
    def _():
        z = jnp.zeros((C, C), f32)
        for p in range(nb * WKV_PAIRS):
            s1 = s0_ref[p // WKV_PAIRS, 2 * (p % WKV_PAIRS)]
            s2 = s0_ref[p // WKV_PAIRS, 2 * (p % WKV_PAIRS) + 1]
            st_scr[p] =jnp.concatenate([jnp.concatenate([s1, z], axis=1),
                                         jnp.concatenate([z, s2], axis=1)], axis=0)

    def stack2(x):
        return jnp.concatenate([jnp.where(m1, x, 0.0), jnp.where(m1, 0.0, x)], axis=0)

    dd = lambda x, y: jnp.dot(x, y, preferred_element_type=f32)
    pairs = range(nb * WKV_PAIRS)
    sq = [p // WKV_PAIRS for p in pairs]
    sls = [slice((p % WKV_PAIRS) * 2 * C, (p % WKV_PAIRS + 1) * 2 * C) for p in pairs]
    tv = r_ref.shape[1]

    def ld(ref, p):
        x = ref[sq[p], :, sls[p]]
        return x if tv == C else jnp.concatenate([x, jnp.zeros((C - tv, 2 * C), f32)], axis=0)

    def prep(p):
        sl = sls[p]
        lw = ld(lw_ref, p)
        kk = ld(kk_ref, p)
        h1 = lw.astype(bf16)
        r1 = lw - h1.astype(f32)
        h2 = r1.astype(bf16)
        h3 = (r1 - h2.astype(f32)).astype(bf16)
        cw = dd(tri, h1) + (dd(tri, h2) + dd(tri, h3))
        cwC = cw[C - 1:C, :]
        b = kk * ld(a_ref, p)
        k = ld(k_ref, p)
        At = -kk * jnp.exp(cw - lw)
        Rt = ld(r_ref, p) * jnp.exp(cw)
        einv = jnp.exp(-cw)
        efut = jnp.exp(cwC - cw)
        X = jnp.concatenate([stack2(At), stack2(Rt)], axis=0)
        Y = jnp.concatenate([b * einv, k * einv], axis=0)
        AR = jnp.concatenate([At, Rt], axis=0)
        BK = jnp.concatenate([b * efut, k * efut], axis=0)
        return X, Y, AR, BK, jnp.exp(cwC)

    pre = [prep(p) for p in pairs]
    G = [_mmul(pre[p][0], pre[p][1], WKV_GRAM_PASSES, nt=True) for p in pairs]
    ARS = [_mmul(pre[p][2], st_scr[p], passes, nt=True) for p in pairs]
    L, Mak, Mrb, Mrk = [], [], [], []
    for p in pairs:
        GA = G[p][0:2 * C]
        GR = G[p][2 * C:4 * C]
        GAr = pltpu.roll(GA, C, axis=1)
        GRr = pltpu.roll(GR, C, axis=1)
        L.append(jnp.where(strict, jnp.where(lane_s, GA, GAr), 0.0))
        Mak.append(jnp.where(strict, jnp.where(lane_s, GAr, GA), 0.0))
        Mrb.append(jnp.where(incl, jnp.where(lane_s, GR, GRr), 0.0))
        Mrk.append(jnp.where(incl, jnp.where(lane_s, GRr, GR), 0.0))
    V = [ld(v_ref, p) for p in pairs]
    Vs = [stack2(V[p]) for p in pairs]
    Xs = [stack2(ARS[p][0:C]) + _mmul(Mak[p], Vs[p], passes) for p in pairs]
    OV = [_mmul(Mrk[p], Vs[p], passes) for p in pairs]
    P = [eye + L[p] for p in pairs]
    Q = L
    for _ in range(5):
        Q = [_mmul(Q[p], Q[p], passes) for p in pairs]
        P = [P[p] + _mmul(Q[p], P[p], passes) for p in pairs]
    Us = [_mmul(P[p], Xs[p], passes) for p in pairs]
    Os = [_mmul(Mrb[p], Us[p], passes) + OV[p] for p in pairs]
    for p in pairs:
        o_ref[sq[p], :, sls[p]] = (ARS[p][C:2 * C] + Os[p][0:C] + Os[p][C:2 * C])[0:tv]
    for p in pairs:
        U = Us[p][0:C] + Us[p][C:2 * C]
        UV = jnp.concatenate([U, V[p]], axis=0)
        dS = _mmul(UV.T, pre[p][3], passes)
        st_scr[p] = st_scr[p] * pre[p][4] + jnp.where(bd, dS, 0.0)

    @pl.when(c == nc - 1)
    def _():
        for p in pairs:
            Snew = st_scr[p]
            sT_ref[sq[p], 2 * (p % WKV_PAIRS)] = Snew[0:C, 0:C]
            sT_ref[sq[p], 2 * (p % WKV_PAIRS) + 1] = Snew[C:2 * C, C:2 * C]


def wkv_chunked(r, lw, k, v, kk, a, s0):
    B, T, D = r.shape
    H = D // 64
    C = WKV_CHUNK
    nb = WKV_SEQS
    tb = min(T, C)
    assert (T % C == 0 or T < C) and tb % 8 == 0 and D == WKV_PAIRS * 2 * C and B % nb == 0
    blk = pl.BlockSpec((nb, tb, D), lambda b, c: (b, c, 0))
    sblk = pl.BlockSpec((nb, H, 64, 64), lambda b, c: (b, 0, 0, 0))
    return pl.pallas_call(
        functools.partial(_wkv_kernel, passes=WKV_PASSES),
        grid=(B // nb, T // tb),
        in_specs=[blk] * 6 + [sblk],
        out_specs=[blk, sblk],
        out_shape=[jax.ShapeDtypeStruct((B, T, D), f32), jax.ShapeDtypeStruct((B, H, 64, 64), f32)],
        scratch_shapes=[pltpu.VMEM((nb * WKV_PAIRS, 2 * C, 2 * C), f32)],
        compiler_params=pltpu.CompilerParams(dimension_semantics=("parallel", "arbitrary")),
        name="wkv7_chunked",
    )(r, lw, k, v, kk, a, s0)


D_MODEL = 1024
BATCH = 4
SEQ = 4096
DEPTH = 4
DEC_BATCH = 128
DEC_SEQ = 8
PAST_LEN = 2048
PAGE_SIZE = 128

N_NSA_LAYERS = (DEPTH + 1) // 2
N_RWKV_LAYERS = DEPTH // 2
N_VRES = N_RWKV_LAYERS - 1

POOL_DIM = D_MODEL // 2
POOL_WINDOWS = (2, 4, 8, 16)
POOL_GROUPS = len(POOL_WINDOWS)
POOL_GDIM = POOL_DIM // POOL_GROUPS
POOL_HIST = max(POOL_WINDOWS) - 1

HEAD_DIM = 64
NSA_HEADS = (D_MODEL // 2) // HEAD_DIM
NSA_KV_HEADS = 2
NSA_GQ = NSA_HEADS // NSA_KV_HEADS
NSA_DIM = NSA_HEADS * HEAD_DIM
CMP_STRIDE = 16
CMP_LEN = 2 * CMP_STRIDE
SLC_LEN = 64
N_SEL = 16
WINDOW = 512
Q_BLOCK = 128
ROPE_DIM = HEAD_DIM // 4
ROPE_THETA = 500000.0
MIX_DIM = POOL_DIM + NSA_DIM
KV_COLS = 6 * NSA_KV_HEADS * HEAD_DIM
IN_COLS = POOL_DIM + NSA_DIM + KV_COLS + 3 * NSA_HEADS

RWKV_N = 64
RWKV_HEADS = D_MODEL // RWKV_N
LORA_W = 64
LORA_A = 64
LORA_V = 32
LORA_G = 128
GN_EPS = 64e-5

MOE_GROUPS = 4
MOE_EPG = 4
N_EXPERTS = MOE_GROUPS * MOE_EPG
MOE_TOPK = 2
D_FF_E = 256

RMS_EPS = 1e-6
NEG_INF = -1e30
RES_SCALE = (2 * DEPTH) ** -0.5

SEL_TK = 512


def _nsa_prompt_kernel(q_ref, ql_ref, g_ref, kc_ref, kcl_ref, vct_ref, ks_ref, vst_ref, kw_ref, vwt_ref,
                       o_ref, score_scr, sel_scr, *, n_cmp):
    QB = Q_BLOCK
    GQ = NSA_GQ * QB
    i = pl.program_id(2)
    s0 = i * QB
    qT = q_ref[...]
    posq = s0 + lax.broadcasted_iota(jnp.int32, (1, GQ), 1) % QB
    dd = lambda x, y: jnp.dot(x, y, preferred_element_type=f32)

    ncp = kc_ref.shape[0]
    cidx = lax.broadcasted_iota(jnp.int32, (ncp, 1), 0)
    mask_c = cidx * CMP_STRIDE + (CMP_LEN - 1) <= posq
    sc = dd(kc_ref[...], qT) + (dd(kc_ref[...], ql_ref[...]) + dd(kcl_ref[...], qT))
    sc = jnp.where(mask_c, sc, NEG_INF)
    pe = jnp.exp(sc - jnp.max(sc, axis=0, keepdims=True))
    pc = jnp.where(mask_c, pe / jnp.sum(pe, axis=0, keepdims=True), 0.0)
    o_c = dd(vct_ref[...], pc.astype(bf16))

    imp = (pc[:, 0:QB] + pc[:, QB:2 * QB]) + (pc[:, 2 * QB:3 * QB] + pc[:, 3 * QB:4 * QB])
    n_slc = score_scr.shape[0]
    per = SLC_LEN // CMP_STRIDE
    nn = lax.broadcasted_iota(jnp.int32, (n_slc, ncp), 0) * per
    cc = lax.broadcasted_iota(jnp.int32, (n_slc, ncp), 1)
    mt = (0.5 * ((cc >= nn) & (cc < nn + per)).astype(f32)
          + 0.5 * ((cc + 1 >= nn) & (cc + 1 < nn + per)).astype(f32)).astype(bf16)
    i1 = imp.astype(bf16)
    r1 = imp - i1.astype(f32)
    i2 = r1.astype(bf16)
    i3 = (r1 - i2.astype(f32)).astype(bf16)
    imp_blk = dd(mt, i1) + (dd(mt, i2) + dd(mt, i3))
    nidx = lax.broadcasted_iota(jnp.int32, (n_slc, 1), 0)
    cur = (s0 + lax.broadcasted_iota(jnp.int32, (1, QB), 1)) // SLC_LEN
    forced = (nidx == 0) | (nidx == cur) | (nidx == cur - 1)
    score = jnp.where(nidx > cur, -1.0, jnp.where(forced, 1e6, imp_blk))
    score_scr[...] = score
    rank = jnp.zeros((n_slc, QB), jnp.int32)
    for m in range(n_slc):
        sm = score_scr[m:m + 1, :]
        tie = (nidx > m).astype(jnp.int32)
        rank = rank + jnp.where(sm > score, 1, jnp.where(sm == score, tie, 0))
    bias = jnp.where(rank < min(N_SEL, n_slc), 0.0, NEG_INF)
    sel_scr[...] = jnp.concatenate([bias] * NSA_GQ, axis=1)

    def online(carry, s, vt_blk):
        m, l, acc = carry
        m_new = jnp.maximum(m, jnp.max(s, axis=0, keepdims=True))
        alpha = jnp.exp(m - m_new)
        p = jnp.exp(s - m_new)
        l = alpha * l + jnp.sum(p, axis=0, keepdims=True)
        acc = alpha * acc + dd(vt_blk, p.astype(bf16))
        return m_new, l, acc

    init = (jnp.full((1, GQ), NEG_INF, f32), jnp.zeros((1, GQ), f32), jnp.zeros((HEAD_DIM, GQ), f32))

    bpt = SEL_TK // SLC_LEN

    def sel_scores(kt):
        k0 = pl.multiple_of(kt * SEL_TK, SEL_TK)
        rows = sel_scr[pl.ds(pl.multiple_of(kt * bpt, bpt), bpt), :]
        blk = jnp.concatenate([jnp.broadcast_to(rows[j:j + 1, :], (SLC_LEN, GQ)) for j in range(bpt)], axis=0)
        return k0, dd(ks_ref[pl.ds(k0, SEL_TK), :], qT) + blk

    def sel_body(kt, carry):
        k0, s = sel_scores(kt)
        return online(carry, s, vst_ref[:, pl.ds(k0, SEL_TK)])

    def sel_body2(kp, carry):
        k0a, sa = sel_scores(2 * kp)
        k0b, sb = sel_scores(2 * kp + 1)
        carry = online(carry, sa, vst_ref[:, pl.ds(k0a, SEL_TK)])
        return online(carry, sb, vst_ref[:, pl.ds(k0b, SEL_TK)])

    n_full = s0 // SEL_TK
    carry = lax.fori_loop(0, n_full // 2, sel_body2, init)
    carry = lax.fori_loop(2 * (n_full // 2), n_full, sel_body, carry)
    k0, s = sel_scores(n_full)
    kpos = k0 + lax.broadcasted_iota(jnp.int32, (SEL_TK, 1), 0)
    _, l_s, acc_s = online(carry, jnp.where(kpos <= posq, s, NEG_INF), vst_ref[:, pl.ds(k0, SEL_TK)])

    nwt = WINDOW // QB
    carry = init
    for j in range(nwt, -1, -1):
        k0 = s0 - WINDOW + j * QB
        k0c = pl.multiple_of(jnp.maximum(k0, 0), QB)
        s = dd(kw_ref[pl.ds(k0c, QB), :], qT)
        kpos = k0c + lax.broadcasted_iota(jnp.int32, (QB, 1), 0)
        if j == nwt:
            s = jnp.where(kpos <= posq, s, NEG_INF)
        else:
            if j == 0:
                s = jnp.where(posq - kpos < WINDOW, s, NEG_INF)
            s = s + jnp.where(k0 >= 0, 0.0, NEG_INF)
        carry = online(carry, s, vwt_ref[:, pl.ds(k0c, QB)])
    _, l_w, acc_w = carry

    g = jax.nn.sigmoid(g_ref[...])
    o = g[0:1] * o_c + g[1:2] * (acc_s / l_s) + g[2:3] * (acc_w / l_w)
    o_ref[...] = jnp.concatenate([o[:, j * QB:(j + 1) * QB].T for j in range(NSA_GQ)], axis=1)


def nsa_prompt_pallas(ops, kc, vc):
    qT, qTl, gT, ks, vst, kw, vwt = ops
    B, KVH, T, D = ks.shape
    G, QB = NSA_GQ, Q_BLOCK
    assert T % SEL_TK == 0 and T % QB == 0
    nqb = T // QB
    n_cmp = kc.shape[1]
    ncp = -(-n_cmp // 128) * 128
    n_slc = T // SLC_LEN
    kcp, kcl = _split(jnp.pad(kc, ((0, 0), (0, ncp - n_cmp), (0, 0), (0, 0))).transpose(0, 2, 1, 3))
    vct = jnp.pad(vc, ((0, 0), (0, ncp - n_cmp), (0, 0), (0, 0))).transpose(0, 2, 3, 1).astype(bf16)
    bh = lambda *shape: pl.BlockSpec((None, None) + shape, lambda b, h, i: (b, h) + (0,) * len(shape))
    bhi = lambda *shape: pl.BlockSpec((None, None, None) + shape, lambda b, h, i: (b, h, i) + (0,) * len(shape))
    return pl.pallas_call(
        functools.partial(_nsa_prompt_kernel, n_cmp=n_cmp),
        grid=(B, KVH, nqb),
        in_specs=[bhi(D, G * QB), bhi(D, G * QB), bhi(3, G * QB), bh(ncp, D), bh(ncp, D), bh(D, ncp),
                  bh(T, D), bh(D, T), bh(T, D), bh(D, T)],
        out_specs=pl.BlockSpec((None, QB, G * D), lambda b, h, i: (b, i, h)),
        out_shape=jax.ShapeDtypeStruct((B, T, KVH * G * D), f32),
        scratch_shapes=[pltpu.VMEM((n_slc, QB), f32), pltpu.VMEM((n_slc, G * QB), f32)],
        compiler_params=pltpu.CompilerParams(dimension_semantics=("parallel", "parallel", "arbitrary"),
                                             vmem_limit_bytes=48 * 1024 * 1024),
        name="nsa_prompt",
    )(qT, qTl, gT, kcp, kcl, vct, ks, vst, kw, vwt)


def _split3(x):
    h1 = x.astype(bf16)
    r1 = x - h1.astype(f32)
    h2 = r1.astype(bf16)
    return h1, h2, (r1 - h2.astype(f32)).astype(bf16)


def _dot_exact_rhs(x, m):
    d = lambda a: jnp.dot(a, m, preferred_element_type=f32)
    h1, h2, h3 = _split3(x)
    return d(h1) + (d(h2) + d(h3))


def _dot_exact_lhs(m, x):
    d = lambda a: jnp.dot(m, a, preferred_element_type=f32)
    h1, h2, h3 = _split3(x)
    return d(h1) + (d(h2) + d(h3))


def _nsa_sample_kernel(pt_ref, *refs, n_pages, ts, past_len):
    pages = refs[:n_pages]
    (new_ref, wbuf_ref, wnew_ref, qh_ref, ql_ref, gate_ref, wa_ref, wb_ref, phik_ref, phiv_ref,
     gain_ref, cos_ref, sin_ref, o_ref, ssel_scr, a_scr, b_scr, score_scr) = refs[n_pages:]
    P = PAGE_SIZE
    KV = NSA_KV_HEADS * HEAD_DIM
    NCOL = NSA_KV_HEADS * NSA_GQ * ts
    NQ = NSA_KV_HEADS * ts
    cpp = P // CMP_STRIDE
    n_chunk = (past_len + SLC_LEN) // CMP_STRIDE
    n_cmp = n_chunk - 1
    ncp = a_scr.shape[0]
    n_slc = (past_len + SLC_LEN) // SLC_LEN
    nsp = score_scr.shape[0]
    dd = lambda x, y: jnp.dot(x, y, preferred_element_type=f32)
    qh = qh_ref[...]
    ql = ql_ref[...]
    col = lax.broadcasted_iota(jnp.int32, (1, NCOL), 1)
    t_col = col % ts
    zpad = jnp.zeros((P - ts, 4 * KV), f32)
    new_tile = jnp.concatenate([new_ref[...], zpad], axis=0)

    wa = wa_ref[...]
    wb = wb_ref[...]
    a_scr[...] = jnp.zeros(a_scr.shape, f32)
    b_scr[...] = jnp.zeros(b_scr.shape, f32)
    def slab(j, kind):
        if j == n_pages:
            return new_tile[:, kind * KV:(kind + 1) * KV]
        return jnp.concatenate([pages[j][:, kind, h, :] for h in range(NSA_KV_HEADS)], axis=1)

    for j in range(n_pages + 1):
        xc = jnp.concatenate([slab(j, 0), slab(j, 1)], axis=1)
        a_scr[j * cpp:(j + 1) * cpp, :] = (xc * wa).reshape(cpp, CMP_STRIDE, 2 * KV).sum(axis=1)
        b_scr[j * cpp:(j + 1) * cpp, :] = (xc * wb).reshape(cpp, CMP_STRIDE, 2 * KV).sum(axis=1)
        ssel_scr[j * P:(j + 1) * P, :] = dd(slab(j, 2).astype(bf16), qh)

    mean = a_scr[...] + pltpu.roll(b_scr[...], ncp - 1, axis=0)
    kc = _mmul(mean[:, 0:KV], phik_ref[...], 3)
    vc = _mmul(mean[:, KV:2 * KV], phiv_ref[...], 3)
    r_i = lax.broadcasted_iota(jnp.int32, (KV, KV), 0)
    c_i = lax.broadcasted_iota(jnp.int32, (KV, KV), 1)
    same_head = (r_i // HEAD_DIM) == (c_i // HEAD_DIM)
    mavg = jnp.where(same_head, 1.0 / HEAD_DIM, 0.0).astype(bf16)
    kc = kc * lax.rsqrt(_dot_exact_rhs(kc * kc, mavg) + RMS_EPS) * gain_ref[...]
    half = ROPE_DIM // 2
    rd, cd = r_i % HEAD_DIM, c_i % HEAD_DIM
    rot = jnp.where(same_head & (cd < half) & (rd == cd + half), -1.0,
                    jnp.where(same_head & (cd >= half) & (cd < ROPE_DIM) & (rd == cd - half), 1.0, 0.0)).astype(bf16)
    kc = kc * cos_ref[...] + _dot_exact_rhs(kc, rot) * sin_ref[...]

    kch, kcl = _split(kc)
    sc = dd(kch, qh) + (dd(kch, ql) + dd(kcl, qh))
    cidx = lax.broadcasted_iota(jnp.int32, (ncp, 1), 0)
    mask_c = (cidx * CMP_STRIDE + (CMP_LEN - 1) <= past_len + t_col) & (cidx < n_cmp)
    sc = jnp.where(mask_c, sc, NEG_INF)
    pe = jnp.exp(sc - jnp.max(sc, axis=0, keepdims=True))
    pc = jnp.where(mask_c, pe / jnp.sum(pe, axis=0, keepdims=True), 0.0)
    o_c = dd(pc.T.astype(bf16), vc.astype(bf16))

    gr = lax.broadcasted_iota(jnp.int32, (NCOL, NQ), 0)
    gc = lax.broadcasted_iota(jnp.int32, (NCOL, NQ), 1)
    gsum = ((gr // (NSA_GQ * ts) == gc // ts) & (gr % ts == gc % ts)).astype(bf16)
    imp = _dot_exact_rhs(pc, gsum)
    per = SLC_LEN // CMP_STRIDE
    nn = lax.broadcasted_iota(jnp.int32, (nsp, ncp), 0) * per
    cc = lax.broadcasted_iota(jnp.int32, (nsp, ncp), 1)
    mt = (0.5 * ((cc >= nn) & (cc < nn + per)).astype(f32)
          + 0.5 * ((cc + 1 >= nn) & (cc + 1 < nn + per)).astype(f32)).astype(bf16)
    imp_blk = _dot_exact_lhs(mt, imp)
    nidx = lax.broadcasted_iota(jnp.int32, (nsp, 1), 0)
    cur = (past_len + lax.broadcasted_iota(jnp.int32, (1, NQ), 1) % ts) // SLC_LEN
    forced = (nidx == 0) | (nidx == cur) | (nidx == cur - 1)
    score = jnp.where(nidx >= n_slc, -2.0, jnp.where(nidx > cur, -1.0, jnp.where(forced, 1e6, imp_blk)))
    score_scr[...] = score
    rank = jnp.zeros((nsp, NQ), jnp.int32)
    for m in range(n_slc):
        sm = score_scr[m:m + 1, :]
        beats = (sm > score) | ((sm == score) & (nidx > m))
        rank = rank + beats.astype(jnp.int32)
    sel = (rank < min(N_SEL, n_slc)).astype(bf16)
    gr2 = lax.broadcasted_iota(jnp.int32, (NQ, NCOL), 0)
    gc2 = lax.broadcasted_iota(jnp.int32, (NQ, NCOL), 1)
    gexp = ((gc2 // (NSA_GQ * ts) == gr2 // ts) & (gc2 % ts == gr2 % ts)).astype(bf16)
    sel_c = dd(sel, gexp)

    def two_pass(n_tiles, score_tile, mask_tile, v_tile):
        m = jnp.full((1, NCOL), NEG_INF, f32)
        for j in range(n_tiles):
            m = jnp.maximum(m, jnp.max(jnp.where(mask_tile(j), score_tile(j), NEG_INF), axis=0, keepdims=True))
        num = jnp.zeros((NCOL, KV), f32)
        den = jnp.zeros((NCOL, KV), f32)
        ones = jnp.ones((P, KV), bf16)
        for j in range(n_tiles):
            p = jnp.where(mask_tile(j), jnp.exp(score_tile(j) - m), 0.0).T.astype(bf16)
            num = num + dd(p, v_tile(j))
            den = den + dd(p, ones)
        return num, den

    bpp = P // SLC_LEN
    row = lax.broadcasted_iota(jnp.int32, (P, 1), 0)

    def sel_mask(j):
        blk = jnp.concatenate([jnp.broadcast_to(sel_c[j * bpp + i:j * bpp + i + 1, :], (SLC_LEN, NCOL))
                               for i in range(bpp)], axis=0)
        return (blk > 0.5) & (j * P + row <= past_len + t_col)

    num_s, den_s = two_pass(
        n_pages + 1, lambda j: ssel_scr[j * P:(j + 1) * P, :], sel_mask,
        lambda j: slab(j, 3).astype(bf16))

    lb = wbuf_ref.shape[0]
    nwt = lb // P
    wnew = jnp.concatenate([wnew_ref[...], jnp.zeros((P - ts, 2 * KV), f32)], axis=0)

    def w_tile(j):
        return wbuf_ref[j * P:(j + 1) * P, :] if j < nwt else wnew

    def win_mask(j):
        pos_w = (past_len - lb + j * P + row) if j < nwt else (past_len + row)
        dq = past_len + t_col - pos_w
        return (dq >= 0) & (dq < WINDOW) & (pos_w >= 0) & ((row < ts) | (j < nwt))

    num_w, den_w = two_pass(
        nwt + 1, lambda j: dd(w_tile(j)[:, 0:KV].astype(bf16), qh), win_mask,
        lambda j: w_tile(j)[:, KV:2 * KV].astype(bf16))

    g = jax.nn.sigmoid(gate_ref[...])
    o_ref[...] = g[0] * o_c + g[1] * (num_s / den_s) + g[2] * (num_w / den_w)


def nsa_sample_pallas(q, gl, rows_new, win_new, pools, page_table, win_bufs, layer, kc_w, phi, k_gain):
    B, ts = q.shape[:2]
    KVH, G, D, P = NSA_KV_HEADS, NSA_GQ, HEAD_DIM, PAGE_SIZE
    KV = KVH * D
    n_pages = page_table.shape[1]
    past_len = n_pages * P
    n_pool = pools.shape[1]
    assert pools.shape[2] == P and ts <= SLC_LEN and P % SLC_LEN == 0
    lb = win_bufs.shape[2]
    assert lb % P == 0
    NCOL = KVH * G * ts
    n_chunk = (past_len + SLC_LEN) // CMP_STRIDE
    ncp = -(-n_chunk // 8) * 8
    n_slc = (past_len + SLC_LEN) // SLC_LEN
    nsp = -(-n_slc // 8) * 8
    qs = (q * D ** -0.5).reshape(B, ts, KVH, G, D).transpose(0, 2, 4, 3, 1).reshape(B, KVH, D, G * ts)
    z = jnp.zeros_like(qs[:, 0])
    qbd = jnp.concatenate([jnp.concatenate([qs[:, 0], z], axis=2), jnp.concatenate([z, qs[:, 1]], axis=2)], axis=1)
    qh, ql = _split(qbd)
    gate = gl.reshape(B, ts, KVH, G, 3).transpose(0, 4, 2, 3, 1).reshape(B, 3, NCOL, 1)
    gate = jnp.broadcast_to(gate, (B, 3, NCOL, KV)).astype(f32)
    w_lane = jnp.repeat(kc_w.reshape(2 * KVH, CMP_LEN), D, axis=0)
    reps = P // CMP_STRIDE
    wa = jnp.tile(w_lane[:, :CMP_STRIDE].T, (reps, 1)).astype(f32)
    wb = jnp.tile(w_lane[:, CMP_STRIDE:].T, (reps, 1)).astype(f32)
    zz = jnp.zeros((D, D), f32)
    bdiag = lambda m: jnp.concatenate([jnp.concatenate([m, zz], axis=1), jnp.concatenate([zz, m], axis=1)], axis=0)
    phik, phiv = bdiag(phi[0].astype(f32)), bdiag(phi[1].astype(f32))
    gain = jnp.tile(k_gain.astype(f32), KVH).reshape(1, KV)
    half = ROPE_DIM // 2
    inv = ROPE_THETA ** (-jnp.arange(half, dtype=f32) / half)
    cmp_end = (jnp.arange(ncp, dtype=jnp.int32) * CMP_STRIDE + (CMP_LEN - 1)).astype(f32)
    ang = cmp_end[:, None] * inv
    cos_h = jnp.concatenate([jnp.cos(ang), jnp.cos(ang), jnp.ones((ncp, D - ROPE_DIM), f32)], axis=1)
    sin_h = jnp.concatenate([jnp.sin(ang), jnp.sin(ang), jnp.zeros((ncp, D - ROPE_DIM), f32)], axis=1)
    cos_t, sin_t = jnp.tile(cos_h, (1, KVH)), jnp.tile(sin_h, (1, KVH))
    pool2 = pools
    new2 = rows_new.reshape(B, ts, 4 * KV).astype(f32)
    wbuf2 = win_bufs.reshape(win_bufs.shape[0] * B, lb, 2 * KV)
    wnew2 = win_new.reshape(B, ts, 2 * KV).astype(f32)
    page_spec = lambda j: pl.BlockSpec((None, None, P, 4, KVH, D),
                                       lambda b, pt, j=j: (layer, pt[b, j], 0, 0, 0, 0))
    per_b = lambda *s: pl.BlockSpec((None,) + s, lambda b, pt: (b,) + (0,) * len(s))
    wbuf_spec = pl.BlockSpec((None, lb, 2 * KV), lambda b, pt: (layer * B + b, 0, 0))
    const = lambda *s: pl.BlockSpec(s, lambda b, pt: (0,) * len(s))
    grid_spec = pltpu.PrefetchScalarGridSpec(
        num_scalar_prefetch=1, grid=(B,),
        in_specs=[page_spec(j) for j in range(n_pages)] + [
            per_b(ts, 4 * KV), wbuf_spec, per_b(ts, 2 * KV), per_b(KV, NCOL), per_b(KV, NCOL),
            per_b(3, NCOL, KV), const(P, 2 * KV), const(P, 2 * KV), const(KV, KV), const(KV, KV),
            const(1, KV), const(ncp, KV), const(ncp, KV)],
        out_specs=per_b(NCOL, KV),
        scratch_shapes=[pltpu.VMEM(((n_pages + 1) * P, NCOL), f32), pltpu.VMEM((ncp, 2 * KV), f32),
                        pltpu.VMEM((ncp, 2 * KV), f32), pltpu.VMEM((nsp, KVH * ts), f32)])
    out = pl.pallas_call(
        functools.partial(_nsa_sample_kernel, n_pages=n_pages, ts=ts, past_len=past_len),
        grid_spec=grid_spec,
        out_shape=jax.ShapeDtypeStruct((B, NCOL, KV), f32),
        compiler_params=pltpu.CompilerParams(dimension_semantics=("arbitrary",),
                                             vmem_limit_bytes=48 * 1024 * 1024),
        name="nsa_sample",
    )(page_table, *([pool2] * n_pages), new2, wbuf2, wnew2, qh, ql, gate, wa, wb, phik, phiv, gain, cos_t, sin_t)
    o4 = out.reshape(B, KVH, G, ts, KVH, D)
    o = jnp.stack([o4[:, 0, :, :, 0], o4[:, 1, :, :, 1]], axis=1)
    return o.transpose(0, 3, 1, 2, 4).reshape(B, ts, KVH * G * D)


MOE_TM = 512
ROUTER_LANES = 128


def _moe_kernel(x_ref, g_ref, wrh_ref, wrl_ref, br_ref, wg_ref, wu_ref, wd_ref, o_ref,
                h_scr, gate_scr, acc_scr):
    grp = pl.program_id(1)
    dd = lambda a, b: jnp.dot(a, b, preferred_element_type=f32)
    tm = x_ref.shape[0]
    lane = lax.broadcasted_iota(jnp.int32, (tm, ROUTER_LANES), 1).astype(f32)
    far = float(ROUTER_LANES)

    @pl.when(grp == 0)
    def _():
        x = x_ref[...]
        h = x * lax.rsqrt(jnp.mean(x * x, axis=-1, keepdims=True) + RMS_EPS) * g_ref[...]
        hh, hl = _split(h)
        h_scr[...] = hh
        logits = dd(hh, wrh_ref[...]) + (dd(hh, wrl_ref[...]) + dd(hl, wrh_ref[...])) + br_ref[...]
        is_c = lane < MOE_GROUPS
        lc = jnp.where(is_c, logits, NEG_INF)
        mc = jnp.max(lc, axis=1, keepdims=True)
        g_idx = jnp.min(jnp.where(lc == mc, lane, far), axis=1, keepdims=True)
        g_w = 1.0 / jnp.sum(jnp.where(is_c, jnp.exp(lc - mc), 0.0), axis=1, keepdims=True)
        lo = MOE_GROUPS + MOE_EPG * g_idx
        lf = jnp.where((lane >= lo) & (lane < lo + MOE_EPG), logits, NEG_INF)
        v1 = jnp.max(lf, axis=1, keepdims=True)
        i1 = jnp.min(jnp.where(lf == v1, lane, far), axis=1, keepdims=True)
        lf2 = jnp.where(lane == i1, NEG_INF, lf)
        v2 = jnp.max(lf2, axis=1, keepdims=True)
        i2 = jnp.min(jnp.where(lf2 == v2, lane, far), axis=1, keepdims=True)
        e21 = jnp.exp(v2 - v1)
        w1 = g_w / (1.0 + e21)
        gate_scr[...] = jnp.where(lane == i1, w1, jnp.where(lane == i2, e21 * w1, 0.0))
        acc_scr[...] = x

    h = h_scr[...]
    hg = dd(h, wg_ref[...])
    hu = dd(h, wu_ref[...])
    gate = gate_scr[...]
    first = (MOE_GROUPS + MOE_EPG * grp).astype(f32)
    cols = []
    for e in range(MOE_EPG):
        ge = jnp.sum(jnp.where(lane == first + e, gate, 0.0), axis=1, keepdims=True)
        sl = slice(e * D_FF_E, (e + 1) * D_FF_E)
        hge = hg[:, sl]
        cols.append((hge * jax.nn.sigmoid(hge) * hu[:, sl] * ge).astype(bf16))
    acc_scr[...] += dd(jnp.concatenate(cols, axis=1), wd_ref[...])

    @pl.when(grp == MOE_GROUPS - 1)
    def _():
        o_ref[...] = acc_scr[...]


def moe_prep(g, wc, bc, wf, bf, wg, wu, wd):
    d = wc.shape[0]
    pad = ROUTER_LANES - MOE_GROUPS - N_EXPERTS
    wr = jnp.pad(jnp.concatenate([wc, wf], axis=1).astype(f32), ((0, 0), (0, pad)))
    wrh, wrl = _split(wr)
    br = jnp.pad(jnp.concatenate([bc, bf]).astype(f32), (0, pad)).reshape(1, ROUTER_LANES)
    regroup = lambda w: (w.reshape(MOE_GROUPS, MOE_EPG, d, D_FF_E).transpose(0, 2, 1, 3)
                         .reshape(MOE_GROUPS, d, MOE_EPG * D_FF_E).astype(bf16))
    wdg = wd.reshape(MOE_GROUPS, MOE_EPG * D_FF_E, d).astype(bf16)
    return g.reshape(1, d).astype(f32), wrh, wrl, br, regroup(wg), regroup(wu), wdg


def moe_residual(x, prep):
    g, wrh, wrl, br, wgg, wug, wdg = prep
    shp = x.shape
    d = shp[-1]
    x2 = x.reshape(-1, d)
    m = x2.shape[0]
    tm = MOE_TM
    assert m % tm == 0
    gf = MOE_EPG * D_FF_E
    full = lambda r, c: pl.BlockSpec((r, c), lambda i, j: (0, 0))
    out = pl.pallas_call(
        _moe_kernel,
        grid=(m // tm, MOE_GROUPS),
        in_specs=[pl.BlockSpec((tm, d), lambda i, j: (i, 0)), full(1, d),
                  full(d, ROUTER_LANES), full(d, ROUTER_LANES), full(1, ROUTER_LANES),
                  pl.BlockSpec((None, d, gf), lambda i, j: (j, 0, 0)),
                  pl.BlockSpec((None, d, gf), lambda i, j: (j, 0, 0)),
                  pl.BlockSpec((None, gf, d), lambda i, j: (j, 0, 0))],
        out_specs=pl.BlockSpec((tm, d), lambda i, j: (i, 0)),
        out_shape=jax.ShapeDtypeStruct((m, d), f32),
        scratch_shapes=[pltpu.VMEM((tm, d), bf16), pltpu.VMEM((tm, ROUTER_LANES), f32),
                        pltpu.VMEM((tm, d), f32)],
        compiler_params=pltpu.CompilerParams(dimension_semantics=("parallel", "arbitrary"),
                                             vmem_limit_bytes=48 * 1024 * 1024),
        name="moe",
    )(x2, g, wrh, wrl, br, wgg, wug, wdg)
    return out.reshape(shp)


def rmsnorm(x, g):
    xf = x.astype(jnp.float32)
    y = xf * lax.rsqrt(jnp.mean(xf * xf, axis=-1, keepdims=True) + RMS_EPS)
    return (y * g.astype(jnp.float32)).astype(x.dtype)


def rope_partial(x, pos):
    half = ROPE_DIM // 2
    inv = ROPE_THETA ** (-jnp.arange(half, dtype=jnp.float32) / half)
    ang = pos.astype(jnp.float32)[:, None] * inv
    cos = jnp.cos(ang)[:, None, :]
    sin = jnp.sin(ang)[:, None, :]
    xf = x.astype(jnp.float32)
    x1 = xf[..., :half]
    x2 = xf[..., half:ROPE_DIM]
    out = jnp.concatenate([x1 * cos - x2 * sin, x2 * cos + x1 * sin, xf[..., ROPE_DIM:]], axis=-1)
    return out.astype(x.dtype)


def masked_softmax(s, mask):
    s = jnp.where(mask, s.astype(jnp.float32), NEG_INF)
    p = jax.nn.softmax(s, axis=-1)
    return jnp.where(mask, p, 0.0)


def pool_mix(u, hist, p0, w_grp, scale):
    b, t, _ = u.shape
    ext = jnp.concatenate([hist.astype(u.dtype), u], axis=1).astype(jnp.float32)
    cs = jnp.pad(jnp.cumsum(ext, axis=1), ((0, 0), (1, 0), (0, 0)))
    cnt_pos = p0 + jnp.arange(t, dtype=jnp.int32) + 1
    means = []
    for gi, w in enumerate(POOL_WINDOWS):
        c = cs[..., gi * POOL_GDIM:(gi + 1) * POOL_GDIM]
        win_sum = c[:, POOL_HIST + 1:POOL_HIST + 1 + t] - c[:, POOL_HIST + 1 - w:POOL_HIST + 1 - w + t]
        cnt = jnp.minimum(cnt_pos, w).astype(jnp.float32)[None, :, None]
        means.append(win_sum / cnt)
    mean = jnp.stack(means, axis=2)
    d = mean - u.reshape(b, t, POOL_GROUPS, POOL_GDIM).astype(jnp.float32)
    y = jnp.einsum('btgc,gcd->btgd', d, w_grp.astype(jnp.float32)).reshape(b, t, POOL_DIM)
    return (y * scale.astype(jnp.float32)).astype(u.dtype)


PROJ_TM = 512


def _norm_mm_kernel(x_ref, g_ref, w_ref, o_ref):
    x = x_ref[...]
    h = x * lax.rsqrt(jnp.mean(x * x, axis=-1, keepdims=True) + RMS_EPS) * g_ref[...]
    o_ref[...] = jnp.dot(h.astype(bf16), w_ref[...], preferred_element_type=f32)


def _norm_mm(x, g, w):
    lead, d = x.shape[:-1], x.shape[-1]
    n = w.shape[1]
    x2 = x.reshape(-1, d)
    m = x2.shape[0]
    npad = -(-n // 128) * 128
    wb = jnp.pad(w.astype(bf16), ((0, 0), (0, npad - n)))
    tm = PROJ_TM
    assert m % tm == 0
    out = pl.pallas_call(
        _norm_mm_kernel, grid=(m // tm,),
        in_specs=[pl.BlockSpec((tm, d), lambda i: (i, 0)), pl.BlockSpec((1, d), lambda i: (0, 0)),
                  pl.BlockSpec((d, npad), lambda i: (0, 0))],
        out_specs=pl.BlockSpec((tm, npad), lambda i: (i, 0)),
        out_shape=jax.ShapeDtypeStruct((m, npad), f32),
        compiler_params=pltpu.CompilerParams(dimension_semantics=("parallel",),
                                             vmem_limit_bytes=48 * 1024 * 1024),
        name="norm_mm",
    )(x2, g.reshape(1, d).astype(f32), wb)
    return out[:, :n].reshape(lead + (n,))


def _mix_out_kernel(a_ref, b_ref, wa_ref, wb_ref, x_ref, o_ref):
    dd = lambda p, q: jnp.dot(p.astype(bf16), q, preferred_element_type=f32)
    o_ref[...] = x_ref[...] + (dd(a_ref[...], wa_ref[...]) + dd(b_ref[...], wb_ref[...]))


def _mix_out(a, b, w, x):
    d = x.shape[-1]
    ka, kb = a.shape[-1], b.shape[-1]
    x2 = x.reshape(-1, d)
    m = x2.shape[0]
    tm = PROJ_TM
    assert m % tm == 0
    wbf = w.astype(bf16)
    tile = lambda c: pl.BlockSpec((tm, c), lambda i: (i, 0))
    full = lambda r, c: pl.BlockSpec((r, c), lambda i: (0, 0))
    out = pl.pallas_call(
        _mix_out_kernel, grid=(m // tm,),
        in_specs=[tile(ka), tile(kb), full(ka, d), full(kb, d), tile(d)], out_specs=tile(d),
        out_shape=jax.ShapeDtypeStruct((m, d), f32),
        compiler_params=pltpu.CompilerParams(dimension_semantics=("parallel",)),
        name="mix_out",
    )(a.reshape(m, ka), b.reshape(m, kb), wbf[:ka], wbf[ka:], x2)
    return out.reshape(x.shape)


def _ab_feat_kernel(x_ref, g_ref, w_ref, qg_ref, kg_ref, cos_ref, sin_ref, hs_ref, he_ref, rot_ref,
                    pool_ref, rows_ref, win_ref, qh_ref, ql_ref, gt_ref, ks_ref, kw_ref, vst_ref, vwt_ref):
    D, KV = HEAD_DIM, NSA_KV_HEADS * HEAD_DIM
    x = x_ref[...]
    h = x * lax.rsqrt(jnp.mean(x * x, axis=-1, keepdims=True) + RMS_EPS) * g_ref[...]
    u = jnp.dot(h.astype(bf16), w_ref[...], preferred_element_type=f32)
    off_kv = POOL_DIM + NSA_DIM
    pool_ref[...] = u[:, :POOL_DIM]
    cos, sin = cos_ref[...], sin_ref[...]

    def norm_rope(z, gain):
        n = z.shape[1] // D
        hs, he, rot = hs_ref[0:n * D, :], he_ref[:, 0:n * D], rot_ref[0:n * D, 0:n * D]
        ms = _dot_exact_rhs(_dot_exact_rhs(z * z, hs), he) * (1.0 / D)
        zn = z * lax.rsqrt(ms + RMS_EPS) * gain
        wide = lambda t: jnp.concatenate([t] * (n // 2), axis=1)
        return zn * wide(cos) + _dot_exact_rhs(zn, rot) * wide(sin)

    q = norm_rope(u[:, POOL_DIM:off_kv], qg_ref[...]) * (D ** -0.5)
    for kvh in range(NSA_KV_HEADS):
        qt = jnp.concatenate([q[:, (kvh * NSA_GQ + g) * D:(kvh * NSA_GQ + g + 1) * D].T
                              for g in range(NSA_GQ)], axis=1)
        hi, lo = _split(qt)
        qh_ref[kvh] = hi
        ql_ref[kvh] = lo
    kv = u[:, off_kv:off_kv + KV_COLS]
    slab = lambda i: kv[:, i * KV:(i + 1) * KV]
    kr = norm_rope(jnp.concatenate([slab(2), slab(4)], axis=1), kg_ref[...])
    k_slc, k_win = kr[:, 0:KV], kr[:, KV:2 * KV]
    rows_ref[...] = jnp.concatenate([slab(0), slab(1), k_slc, slab(3)], axis=1)
    win_ref[...] = jnp.concatenate([k_win, slab(5)], axis=1)
    for kvh in range(NSA_KV_HEADS):
        hsl = slice(kvh * D, (kvh + 1) * D)
        ks_ref[kvh] = k_slc[:, hsl].astype(bf16)
        kw_ref[kvh] = k_win[:, hsl].astype(bf16)
        vst_ref[kvh] = slab(3)[:, hsl].T.astype(bf16)
        vwt_ref[kvh] = slab(5)[:, hsl].T.astype(bf16)
    gt_ref[...] = u[:, off_kv + KV_COLS:].T


def ab_features_prompt(x, norm_g, w_in, q_norm, k_norm):
    B, T, d = x.shape
    QB, D, KVH, G = Q_BLOCK, HEAD_DIM, NSA_KV_HEADS, NSA_GQ
    KV = KVH * D
    assert T % QB == 0
    nqb = T // QB
    npad = -(-IN_COLS // 128) * 128
    assert npad - (POOL_DIM + NSA_DIM + KV_COLS) == 128
    wb = jnp.pad(w_in.astype(bf16), ((0, 0), (0, npad - IN_COLS)))
    half = ROPE_DIM // 2
    inv = ROPE_THETA ** (-jnp.arange(half, dtype=f32) / half)
    ang = jnp.arange(T, dtype=jnp.int32).astype(f32)[:, None] * inv
    cos_h = jnp.concatenate([jnp.cos(ang), jnp.cos(ang), jnp.ones((T, D - ROPE_DIM), f32)], axis=1)
    sin_h = jnp.concatenate([jnp.sin(ang), jnp.sin(ang), jnp.zeros((T, D - ROPE_DIM), f32)], axis=1)
    cos_t, sin_t = jnp.tile(cos_h, (1, 2)), jnp.tile(sin_h, (1, 2))
    r_i = lax.broadcasted_iota(jnp.int32, (NSA_DIM, NSA_DIM), 0)
    c_i = lax.broadcasted_iota(jnp.int32, (NSA_DIM, NSA_DIM), 1)
    same = (r_i // D) == (c_i // D)
    rd, cd = r_i % D, c_i % D
    rot = jnp.where(same & (cd < half) & (rd == cd + half), -1.0,
                    jnp.where(same & (cd >= half) & (cd < ROPE_DIM) & (rd == cd - half), 1.0, 0.0)).astype(bf16)
    hs = (lax.broadcasted_iota(jnp.int32, (NSA_DIM, 128), 0) // D
          == lax.broadcasted_iota(jnp.int32, (NSA_DIM, 128), 1)).astype(bf16)
    qg = jnp.tile(q_norm.astype(f32), NSA_HEADS).reshape(1, NSA_DIM)
    kg = jnp.concatenate([jnp.tile(k_norm[1].astype(f32), KVH), jnp.tile(k_norm[2].astype(f32), KVH)]).reshape(1, 2 * KV)
    full = lambda a: pl.BlockSpec(a.shape, lambda b, i: (0,) * a.ndim)
    tok = lambda c: pl.BlockSpec((None, QB, c), lambda b, i: (b, i, 0))
    f = jax.ShapeDtypeStruct
    outs = pl.pallas_call(
        _ab_feat_kernel, grid=(B, nqb),
        in_specs=[tok(d), pl.BlockSpec((1, d), lambda b, i: (0, 0)), full(wb), full(qg), full(kg),
                  pl.BlockSpec((QB, 2 * D), lambda b, i: (i, 0)), pl.BlockSpec((QB, 2 * D), lambda b, i: (i, 0)),
                  full(hs), pl.BlockSpec((128, NSA_DIM), lambda b, i: (0, 0)), full(rot)],
        out_specs=[tok(POOL_DIM), tok(4 * KV), tok(2 * KV),
                   pl.BlockSpec((None, KVH, None, D, G * QB), lambda b, i: (b, 0, i, 0, 0)),
                   pl.BlockSpec((None, KVH, None, D, G * QB), lambda b, i: (b, 0, i, 0, 0)),
                   pl.BlockSpec((None, None, 128, QB), lambda b, i: (b, i, 0, 0)),
                   pl.BlockSpec((None, KVH, QB, D), lambda b, i: (b, 0, i, 0)),
                   pl.BlockSpec((None, KVH, QB, D), lambda b, i: (b, 0, i, 0)),
                   pl.BlockSpec((None, KVH, D, QB), lambda b, i: (b, 0, 0, i)),
                   pl.BlockSpec((None, KVH, D, QB), lambda b, i: (b, 0, 0, i))],
        out_shape=[f((B, T, POOL_DIM), f32), f((B, T, 4 * KV), f32), f((B, T, 2 * KV), f32),
                   f((B, KVH, nqb, D, G * QB), bf16), f((B, KVH, nqb, D, G * QB), bf16),
                   f((B, nqb, 128, QB), f32), f((B, KVH, T, D), bf16), f((B, KVH, T, D), bf16),
                   f((B, KVH, D, T), bf16), f((B, KVH, D, T), bf16)],
        compiler_params=pltpu.CompilerParams(dimension_semantics=("parallel", "parallel"),
                                             vmem_limit_bytes=48 * 1024 * 1024),
        name="ab_feat",
    )(x, norm_g.reshape(1, d).astype(f32), wb, qg, kg, cos_t, sin_t, hs, hs.T, rot)
    pool_in, rows, win, qh, ql, gt, ks, kw, vst, vwt = outs
    gT = (gt[:, :, :NSA_HEADS * 3].reshape(B, nqb, KVH, G, 3, QB).transpose(0, 2, 1, 4, 3, 5)
          .reshape(B, KVH, nqb, 3, G * QB))
    return pool_in, rows, win, (qh, ql, gT, ks, vst, kw, vwt)


def ab_features(x, norm_g, pos, w_in, q_norm, k_norm):
    b, t = x.shape[:2]
    u = _norm_mm(x, norm_g, w_in)
    off_kv = POOL_DIM + NSA_DIM
    pool_in = u[..., :POOL_DIM]
    q = u[..., POOL_DIM:off_kv].reshape(b, t, NSA_HEADS, HEAD_DIM)
    kv = u[..., off_kv:off_kv + KV_COLS].reshape(b, t, 6, NSA_KV_HEADS, HEAD_DIM)
    gl = u[..., off_kv + KV_COLS:].reshape(b, t, NSA_HEADS, 3)
    q = rope_partial(rmsnorm(q, q_norm), pos)
    k_slc = rope_partial(rmsnorm(kv[:, :, 2], k_norm[1]), pos)
    k_win = rope_partial(rmsnorm(kv[:, :, 4], k_norm[2]), pos)
    rows = jnp.stack([kv[:, :, 0], kv[:, :, 1], k_slc, kv[:, :, 3]], axis=2)
    win = jnp.stack([k_win, kv[:, :, 5]], axis=2)
    return pool_in, q, gl, rows, win


def compress_kv(k_rows, v_rows, pos_w, phi, k_gain):
    b, length = k_rows.shape[:2]
    n_chunk = length // CMP_STRIDE

    def weighted_block_mean(rows, w):
        ch = rows.reshape(b, n_chunk, CMP_STRIDE, NSA_KV_HEADS, HEAD_DIM)
        return (jnp.einsum('bnlhd,hl->bnhd', ch[:, :-1], w[:, :CMP_STRIDE])
                + jnp.einsum('bnlhd,hl->bnhd', ch[:, 1:], w[:, CMP_STRIDE:]))

    cmp_end = jnp.arange(n_chunk - 1, dtype=jnp.int32) * CMP_STRIDE + (CMP_LEN - 1)
    kc = jnp.einsum('bnhd,de->bnhe', weighted_block_mean(k_rows, pos_w[0]), phi[0])
    kc = rope_partial(rmsnorm(kc, k_gain), cmp_end)
    vc = jnp.einsum('bnhd,de->bnhe', weighted_block_mean(v_rows, pos_w[1]), phi[1])
    return kc, vc, cmp_end


def nsa_attend(q, pos_q, gl, kc, vc, cmp_end, ks, vs, kw, vw, pos_w):
    f32 = jnp.float32
    b, tq = q.shape[:2]
    qg = q.reshape(b, tq, NSA_KV_HEADS, NSA_GQ, HEAD_DIM)
    scale = HEAD_DIM ** -0.5
    s_c = jnp.einsum('bqhgd,bchd->bhgqc', qg, kc) * scale
    p_c = masked_softmax(s_c, cmp_end[None, :] <= pos_q[:, None])
    o_c = jnp.einsum('bhgqc,bchd->bqhgd', p_c, vc.astype(f32))
    imp = p_c.sum(axis=2)
    imp_chunk = 0.5 * (jnp.pad(imp, ((0, 0), (0, 0), (0, 0), (0, 1)))
                       + jnp.pad(imp, ((0, 0), (0, 0), (0, 0), (1, 0))))
    n_slc = ks.shape[1] // SLC_LEN
    imp_blk = imp_chunk.reshape(b, NSA_KV_HEADS, tq, n_slc, SLC_LEN // CMP_STRIDE).sum(-1)
    blk = jnp.arange(n_slc, dtype=jnp.int32)[None, :]
    cur = (pos_q // SLC_LEN)[:, None]
    forced = (blk == 0) | (blk == cur) | (blk == cur - 1)
    score = jnp.where(blk > cur, -1.0, jnp.where(forced, 1e6, imp_blk))
    n_sel = min(N_SEL, n_slc)
    _, idx = lax.top_k(score, n_sel)
    gather = jax.vmap(jax.vmap(lambda rows, i: rows[i]))
    ksb = ks.reshape(b, n_slc, SLC_LEN, NSA_KV_HEADS, HEAD_DIM).transpose(0, 3, 1, 2, 4)
    vsb = vs.reshape(b, n_slc, SLC_LEN, NSA_KV_HEADS, HEAD_DIM).transpose(0, 3, 1, 2, 4)
    kg = gather(ksb, idx)
    vg = gather(vsb, idx)
    kpos = idx[..., None] * SLC_LEN + jnp.arange(SLC_LEN, dtype=jnp.int32)
    n_keys = n_sel * SLC_LEN
    m_s = (kpos <= pos_q[None, None, :, None, None]).reshape(b, NSA_KV_HEADS, 1, tq, n_keys)
    s_s = jnp.einsum('bqhgd,bhqnld->bhgqnl', qg, kg).reshape(b, NSA_KV_HEADS, NSA_GQ, tq, n_keys) * scale
    p_s = masked_softmax(s_s, m_s)
    o_s = jnp.einsum('bhgqk,bhqkd->bqhgd', p_s,
                     vg.reshape(b, NSA_KV_HEADS, tq, n_keys, HEAD_DIM).astype(f32))
    s_w = jnp.einsum('bqhgd,bkhd->bhgqk', qg, kw) * scale
    dq = pos_q[:, None] - pos_w[None, :]
    m_w = (dq >= 0) & (dq < WINDOW) & (pos_w[None, :] >= 0)
    p_w = masked_softmax(s_w, m_w)
    o_w = jnp.einsum('bhgqk,bkhd->bqhgd', p_w, vw.astype(f32))
    g = jax.nn.sigmoid(gl.astype(f32)).reshape(b, tq, NSA_KV_HEADS, NSA_GQ, 3)
    o = g[..., 0:1] * o_c + g[..., 1:2] * o_s + g[..., 2:3] * o_w
    return o.reshape(b, tq, NSA_DIM)


def nsa_prompt(q, gl, rows, win, cmp_pos_w, cmp_phi, k_gain):
    b, t = q.shape[:2]
    kc, vc, cmp_end = compress_kv(rows[:, :, 0], rows[:, :, 1], cmp_pos_w, cmp_phi, k_gain)
    ks, vs = rows[:, :, 2], rows[:, :, 3]
    win_pad = jnp.pad(win, ((0, 0), (WINDOW, 0), (0, 0), (0, 0), (0, 0)))

    def block(i):
        s0 = i * Q_BLOCK
        qb = lax.dynamic_slice_in_dim(q, s0, Q_BLOCK, axis=1)
        gb = lax.dynamic_slice_in_dim(gl, s0, Q_BLOCK, axis=1)
        wb = lax.dynamic_slice_in_dim(win_pad, s0, WINDOW + Q_BLOCK, axis=1)
        pos_q = s0 + jnp.arange(Q_BLOCK, dtype=jnp.int32)
        pos_w = s0 - WINDOW + jnp.arange(WINDOW + Q_BLOCK, dtype=jnp.int32)
        return nsa_attend(qb, pos_q, gb, kc, vc, cmp_end, ks, vs, wb[:, :, 0], wb[:, :, 1], pos_w)

    out = lax.map(block, jnp.arange(t // Q_BLOCK, dtype=jnp.int32))
    return jnp.moveaxis(out, 0, 1).reshape(b, t, NSA_DIM)


def nsa_sample(q, gl, rows_new, win_new, pool_kv, page_table, win_buf, cmp_pos_w, cmp_phi, k_gain):
    bd, ts = q.shape[:2]
    past_len = page_table.shape[1] * pool_kv.shape[1]
    past = pool_kv[page_table].reshape(bd, past_len, 4, NSA_KV_HEADS, HEAD_DIM)
    rows = jnp.concatenate([past, rows_new.astype(past.dtype)], axis=1)
    pad = (-(past_len + ts)) % SLC_LEN
    rows = jnp.pad(rows, ((0, 0), (0, pad), (0, 0), (0, 0), (0, 0)))
    kc, vc, cmp_end = compress_kv(rows[:, :, 0], rows[:, :, 1], cmp_pos_w, cmp_phi, k_gain)
    lb = win_buf.shape[1]
    win = jnp.concatenate([win_buf, win_new.astype(win_buf.dtype)], axis=1)
    pos_q = past_len + jnp.arange(ts, dtype=jnp.int32)
    pos_w = past_len - lb + jnp.arange(lb + ts, dtype=jnp.int32)
    o = nsa_attend(q, pos_q, gl, kc, vc, cmp_end, rows[:, :, 2], rows[:, :, 3],
                   win[:, :, 0], win[:, :, 1], pos_w)
    keep = min(WINDOW, lb + ts)
    return o, win[:, lb + ts - keep:]


def ab_layer_prompt(x, norm_g, w_in, w_out, q_norm, k_norm, cmp_pos_w, cmp_phi, pool_w, pool_scale):
    b, t = x.shape[:2]
    pool_in, rows2, win2, ops = ab_features_prompt(x, norm_g, w_in, q_norm, k_norm)
    rows = rows2.reshape(b, t, 4, NSA_KV_HEADS, HEAD_DIM)
    win = win2.reshape(b, t, 2, NSA_KV_HEADS, HEAD_DIM)
    hist0 = jnp.zeros((b, POOL_HIST, POOL_DIM), x.dtype)
    pool_out = pool_mix(pool_in, hist0, 0, pool_w, pool_scale)
    kc, vc, _ = compress_kv(rows[:, :, 0], rows[:, :, 1], cmp_pos_w, cmp_phi, k_norm[0])
    nsa_out = nsa_prompt_pallas(ops, kc, vc)
    x_new = _mix_out(pool_out, nsa_out, w_out, x)
    keep = min(WINDOW, t)
    return x_new, rows, win[:, t - keep:], pool_in[:, t - POOL_HIST:]


def ab_layer_sample(x, norm_g, pools, page_table, win_bufs, layer, pool_hist, w_in, w_out, q_norm, k_norm,
                    cmp_pos_w, cmp_phi, pool_w, pool_scale):
    ts = x.shape[1]
    past_len = page_table.shape[1] * pools.shape[2]
    pos = past_len + jnp.arange(ts, dtype=jnp.int32)
    pool_in, q, gl, rows, win = ab_features(x, norm_g, pos, w_in, q_norm, k_norm)
    pool_out = pool_mix(pool_in, pool_hist, past_len, pool_w, pool_scale)
    nsa_out = nsa_sample_pallas(q, gl, rows, win, pools, page_table, win_bufs, layer,
                                cmp_pos_w, cmp_phi, k_norm[0])
    win_buf = win_bufs[layer]
    lb = win_buf.shape[1]
    keep = min(WINDOW, lb + ts)
    new_win = jnp.concatenate([win_buf, win.astype(win_buf.dtype)], axis=1)[:, lb + ts - keep:]
    x_new = _mix_out(pool_out, nsa_out, w_out, x)
    new_hist = jnp.concatenate([pool_hist.astype(pool_in.dtype), pool_in], axis=1)[:, -POOL_HIST:]
    return x_new, rows, new_win, new_hist


def wkv_scan(s0, r, w, k, v, kk, a):
    def step(s, inp):
        r_t, w_t, k_t, v_t, kk_t, a_t = inp
        sa = jnp.einsum('bhij,bhj->bhi', s, -kk_t)
        s = (s * w_t[:, :, None, :] + sa[..., None] * (kk_t * a_t)[:, :, None, :]
             + v_t[..., None] * k_t[:, :, None, :])
        return s, jnp.einsum('bhij,bhj->bhi', s, r_t)

    xs = tuple(jnp.moveaxis(z.astype(jnp.float32), 1, 0) for z in (r, w, k, v, kk, a))
    s, o = lax.scan(step, s0.astype(jnp.float32), xs)
    return s, jnp.moveaxis(o, 0, 1)


RW_TM = 256
HEAD_LANES = 128


def _rwkv_pre_kernel(*refs, seq_len, has_vres):
    it = iter(refs)
    x_ref, xprev_ref, fp_ref, g_ref, mu_ref = [next(it) for _ in range(5)]
    wr_ref, wk_ref, wv_ref = [next(it) for _ in range(3)]
    w0_ref, w1_ref, w2_ref, a0_ref, a1_ref, a2_ref, g1_ref, g2_ref = [next(it) for _ in range(8)]
    kkw_ref, kaw_ref, hsum_ref, hexp_ref = [next(it) for _ in range(4)]
    if has_vres:
        v0_ref, v1_ref, v2_ref, vf_ref = [next(it) for _ in range(4)]
    r_ref, k_ref, v_ref, lw_ref, kk_ref, a_ref, gg_ref = [next(it) for _ in range(7)]
    dd = lambda a, b: jnp.dot(a, b, preferred_element_type=f32)
    tm = x_ref.shape[0]
    norm = lambda z: z * lax.rsqrt(jnp.mean(z * z, axis=-1, keepdims=True) + RMS_EPS) * g_ref[...]
    h = norm(x_ref[...])
    row = lax.broadcasted_iota(jnp.int32, (tm, 1), 0)
    rolled = pltpu.roll(h, 1, axis=0)
    if seq_len % tm == 0:
        first = (pl.program_id(0) % (seq_len // tm)) == 0
        last_prev = norm(xprev_ref[...])[xprev_ref.shape[0] - 1:, :]
        prev = jnp.where(row == 0, jnp.where(first, fp_ref[...], last_prev), rolled)
    else:
        nseq = tm // seq_len
        sel = (lax.broadcasted_iota(jnp.int32, (tm, nseq), 0)
               == seq_len * lax.broadcasted_iota(jnp.int32, (tm, nseq), 1)).astype(bf16)
        prev = jnp.where(row % seq_len == 0, _dot_exact_lhs(sel, fp_ref[...]), rolled)
    xx = prev - h
    mix = lambda j: (h + xx * mu_ref[j:j + 1, :]).astype(bf16)
    xr, xw, xk, xv, xa, xg = [mix(j) for j in range(6)]
    k = dd(xk, wk_ref[...])
    v = dd(xv, wv_ref[...])
    r_ref[...] = dd(xr, wr_ref[...])
    z = -(w0_ref[...] + dd(jnp.tanh(dd(xw, w1_ref[...])).astype(bf16), w2_ref[...]))
    softplus = jnp.maximum(z, 0.0) + jnp.log(1.0 + jnp.exp(-jnp.abs(z)))
    lw_ref[...] = -jnp.exp(-softplus - 0.5)
    a = jax.nn.sigmoid(a0_ref[...] + dd(dd(xa, a1_ref[...]).astype(bf16), a2_ref[...]))
    a_ref[...] = a
    gg_ref[...] = dd(jax.nn.sigmoid(dd(xg, g1_ref[...])).astype(bf16), g2_ref[...])
    if has_vres:
        v = v + (vf_ref[...] - v) * jax.nn.sigmoid(
            v0_ref[...] + dd(dd(xv, v1_ref[...]).astype(bf16), v2_ref[...]))
    v_ref[...] = v
    kk = k * kkw_ref[...]
    nrm = jnp.maximum(jnp.sqrt(_dot_exact_rhs(kk * kk, hsum_ref[...])), 1e-12)
    kk_ref[...] = kk * _dot_exact_rhs(1.0 / nrm, hexp_ref[...])
    k_ref[...] = k * (1.0 + (a - 1.0) * kaw_ref[...])


def _rwkv_post_kernel(o_ref, r_ref, k_ref, v_ref, gg_ref, x_ref, gnw_ref, gnb_ref, rk_ref, hsum_ref,
                      hexp_ref, wo_ref, out_ref):
    hs = hsum_ref[...]
    he = hexp_ref[...]
    head_sum = lambda z: _dot_exact_rhs(_dot_exact_rhs(z, hs), he)
    o = o_ref[...]
    v = v_ref[...]
    d = o - head_sum(o) * (1.0 / RWKV_N)
    var = head_sum(d * d) * (1.0 / RWKV_N)
    on = d * lax.rsqrt(var + GN_EPS) * gnw_ref[...] + gnb_ref[...]
    on = on + head_sum(r_ref[...] * k_ref[...] * rk_ref[...]) * v
    y = (on * gg_ref[...]).astype(bf16)
    out_ref[...] = x_ref[...] + jnp.dot(y, wo_ref[...], preferred_element_type=f32)


def rwkv_layer_fused(x, norm_g, shift_prev, s0, v_first, vres, mu, wr, wk, wv, wo, w0, w1, w2, a0, a1, a2,
                     g1, g2, k_k, k_a, r_k, gn_w, gn_b):
    n, t, d = x.shape
    m = n * t
    tm = RW_TM
    assert m % tm == 0 and (t % tm == 0 or tm % t == 0)
    x2 = x.reshape(m, d)
    row = lambda z: z.reshape(1, d).astype(f32)
    cb = lambda z: z.astype(bf16)
    hd = lax.broadcasted_iota(jnp.int32, (d, HEAD_LANES), 0) // RWKV_N
    hsum = (hd == lax.broadcasted_iota(jnp.int32, (d, HEAD_LANES), 1)).astype(bf16)
    hexp = hsum.T
    tile = pl.BlockSpec((tm, d), lambda i: (i, 0))
    full = lambda z: pl.BlockSpec(z.shape, lambda i: (0,) * z.ndim)
    if t % tm == 0:
        tps = t // tm
        fp = shift_prev.reshape(n, 1, d).astype(f32)
        fp_spec = pl.BlockSpec((None, 1, d), lambda i: (i // tps, 0, 0))
    else:
        fp = shift_prev.astype(f32)
        fp_spec = pl.BlockSpec((tm // t, d), lambda i: (i, 0))
    xprev_spec = pl.BlockSpec((8, d), lambda i: (jnp.maximum(i * (tm // 8) - 1, 0), 0))
    mu8 = jnp.pad(mu.astype(f32), ((0, 2), (0, 0)))
    consts = [row(norm_g), mu8, cb(wr), cb(wk), cb(wv), row(w0), cb(w1), cb(w2), row(a0), cb(a1), cb(a2),
              cb(g1), cb(g2), row(k_k), row(k_a), hsum, hexp]
    args = [x2, x2, fp] + consts
    specs = [tile, xprev_spec, fp_spec] + [full(c) for c in consts]
    if vres is not None:
        v0, v1, v2 = vres
        extra = [row(v0), cb(v1), cb(v2)]
        args += extra + [v_first.reshape(m, d)]
        specs += [full(c) for c in extra] + [tile]
    cp = pltpu.CompilerParams(dimension_semantics=("parallel",), vmem_limit_bytes=56 * 1024 * 1024)
    r, k, v, lw, kk, a, gg = pl.pallas_call(
        functools.partial(_rwkv_pre_kernel, seq_len=t, has_vres=vres is not None),
        grid=(m // tm,), in_specs=specs, out_specs=[tile] * 7,
        out_shape=[jax.ShapeDtypeStruct((m, d), f32)] * 7, compiler_params=cp, name="rwkv_pre",
    )(*args)
    if vres is None:
        v_first = v.reshape(n, t, d)
    seq = lambda z: z.reshape(n, t, d)
    o, s = wkv_chunked(seq(r), seq(lw), seq(k), seq(v), seq(kk), seq(a), s0.astype(f32))
    o = o.reshape(m, d)
    post_consts = [row(gn_w), row(gn_b), r_k.reshape(1, d).astype(f32), hsum, hexp, cb(wo)]
    x_new = pl.pallas_call(
        _rwkv_post_kernel, grid=(m // tm,),
        in_specs=[tile] * 6 + [full(c) for c in post_consts], out_specs=tile,
        out_shape=jax.ShapeDtypeStruct((m, d), f32), compiler_params=cp, name="rwkv_post",
    )(o, r, k, v, gg, x2, *post_consts)
    h_last = rmsnorm(x[:, -1], norm_g)
    return x_new.reshape(n, t, d), v_first, s, h_last


def rwkv_layer(h, shift_prev, s0, v_first, vres, mu, wr, wk, wv, wo, w0, w1, w2, a0, a1, a2,
               g1, g2, k_k, k_a, r_k, gn_w, gn_b):
    f32 = jnp.float32
    b, t, d = h.shape
    prev = jnp.concatenate([shift_prev[:, None, :].astype(h.dtype), h[:, :-1]], axis=1)
    xx = prev - h
    xr, xw, xk, xv, xa, xg = [h + xx * mu[j] for j in range(6)]
    r = _mm(xr, wr)
    k = _mm(xk, wk)
    v = _mm(xv, wv)
    w_log = -jax.nn.softplus(-(w0 + jnp.tanh(xw @ w1) @ w2).astype(f32)) - 0.5
    decay = jnp.exp(-jnp.exp(w_log))
    if vres is None:
        v_first = v
    else:
        v0, v1, v2 = vres
        v = v + (v_first - v) * jax.nn.sigmoid(v0 + (xv @ v1) @ v2)
    a = jax.nn.sigmoid((a0 + (xa @ a1) @ a2).astype(f32))
    g = jax.nn.sigmoid(xg @ g1) @ g2

    def heads(z):
        return z.reshape(b, t, RWKV_HEADS, RWKV_N).astype(f32)

    kk = heads(k * k_k)
    kk = kk / jnp.maximum(jnp.sqrt(jnp.sum(kk * kk, axis=-1, keepdims=True)), 1e-12)
    k = k.astype(f32) * (1.0 + (a - 1.0) * k_a.astype(f32))
    rh, kh, vh, ah, dh = heads(r), heads(k), heads(v), heads(a), heads(decay)
    tp = -(-t // WKV_CHUNK) * WKV_CHUNK
    padt = lambda z: jnp.pad(z, ((0, 0), (0, tp - t), (0, 0)))
    o, s = wkv_chunked(padt(r), padt(-jnp.exp(w_log)), padt(k), padt(v), padt(kk.reshape(b, t, d)),
                       padt(a), s0.astype(f32))
    o = o[:, :t].reshape(b, t, RWKV_HEADS, RWKV_N)
    mean = jnp.mean(o, axis=-1, keepdims=True)
    var = jnp.mean(jnp.square(o - mean), axis=-1, keepdims=True)
    o = ((o - mean) * lax.rsqrt(var + GN_EPS) * gn_w.reshape(RWKV_HEADS, RWKV_N).astype(f32)
         + gn_b.reshape(RWKV_HEADS, RWKV_N).astype(f32))
    o = o + jnp.sum(rh * kh * r_k.astype(f32), axis=-1, keepdims=True) * vh
    y = _mm((o.reshape(b, t, d) * g.astype(f32)).astype(h.dtype), wo)
    return y, v_first, s, h[:, -1]


def hier_moe(h, wc, bc, wf, bf, wg, wu, wd):
    f32 = jnp.float32
    hp = lax.Precision.HIGHEST
    lc = jnp.dot(h, wc, precision=hp).astype(f32) + bc.astype(f32)
    g_idx = jnp.argmax(lc, axis=-1)
    g_w = jnp.max(jax.nn.softmax(lc, axis=-1), axis=-1)
    g_hot = jax.nn.one_hot(g_idx, MOE_GROUPS, dtype=f32)
    lf = (jnp.dot(h, wf, precision=hp).astype(f32) + bf.astype(f32)).reshape(h.shape[:-1] + (MOE_GROUPS, MOE_EPG))
    lf_sel = jnp.einsum('btge,btg->bte', lf, g_hot)
    top_v, top_i = lax.top_k(lf_sel, MOE_TOPK)
    top_w = jax.nn.softmax(top_v, axis=-1) * g_w[..., None]
    e_id = g_idx[..., None] * MOE_EPG + top_i
    gate = jnp.einsum('btke,btk->bte', jax.nn.one_hot(e_id, N_EXPERTS, dtype=f32), top_w)
    hg = jnp.einsum('btd,edf->btef', h, wg)
    hu = jnp.einsum('btd,edf->btef', h, wu)
    act = (jax.nn.silu(hg) * hu * gate[..., None].astype(h.dtype)).astype(h.dtype)
    return jnp.einsum('btef,efd->btd', act, wd)


def kernel(x_prompt, x_sample, cache_nsa_kv, cache_win_kv, state_pool, state_wkv, state_shift,
           page_table, norm_mix, norm_ffn, ab_w_in, ab_w_out, ab_q_norm, ab_k_norm, cmp_pos_w,
           cmp_phi, pool_w, pool_scale, rw_mu, rw_wr, rw_wk, rw_wv, rw_wo, rw_w0, rw_w1, rw_w2,
           rw_a0, rw_a1, rw_a2, rw_v0, rw_v1, rw_v2, rw_g1, rw_g2, rw_kk, rw_ka, rw_rk, rw_gn_w,
           rw_gn_b, moe_wc, moe_bc, moe_wf, moe_bf, moe_wg, moe_wu, moe_wd):
    xp, xs = x_prompt, x_sample
    vf_p, vf_s = None, None
    nsa_p, nsa_s, win_p, win_s, pool_p, pool_s = [], [], [], [], [], []
    wkv_p, wkv_s, sh_p, sh_s = [], [], [], []
    for l in range(DEPTH):
        if l % 2 == 0:
            i = l // 2
            wts = (ab_w_in[i], ab_w_out[i], ab_q_norm[i], ab_k_norm[i], cmp_pos_w[i], cmp_phi[i],
                   pool_w[i], pool_scale[i])
            xp, r_p, w_p, h_p = ab_layer_prompt(xp, norm_mix[l], *wts)
            xs, r_s, w_s, h_s = ab_layer_sample(xs, norm_mix[l], cache_nsa_kv, page_table,
                                                cache_win_kv, i, state_pool[i], *wts)
            nsa_p.append(r_p)
            nsa_s.append(r_s)
            win_p.append(w_p)
            win_s.append(w_s)
            pool_p.append(h_p)
            pool_s.append(h_s)
        else:
            j = l // 2
            vres = None if j == 0 else (rw_v0[j - 1], rw_v1[j - 1], rw_v2[j - 1])
            wts = (rw_mu[j], rw_wr[j], rw_wk[j], rw_wv[j], rw_wo[j], rw_w0[j], rw_w1[j], rw_w2[j],
                   rw_a0[j], rw_a1[j], rw_a2[j], rw_g1[j], rw_g2[j], rw_kk[j], rw_ka[j], rw_rk[j],
                   rw_gn_w[j], rw_gn_b[j])
            bp = xp.shape[0]
            zero_shift = jnp.zeros((bp, D_MODEL), xp.dtype)
            zero_state = jnp.zeros((bp, RWKV_HEADS, RWKV_N, RWKV_N), jnp.float32)
            xp, vf_p, s_p, shp = rwkv_layer_fused(xp, norm_mix[l], zero_shift, zero_state, vf_p, vres, *wts)
            xs, vf_s, s_s, shs = rwkv_layer_fused(xs, norm_mix[l], state_shift[j], state_wkv[j], vf_s, vres, *wts)
            wkv_p.append(s_p)
            wkv_s.append(s_s)
            sh_p.append(shp)
            sh_s.append(shs)
        prep =moe_prep(norm_ffn[l], moe_wc[l], moe_bc[l], moe_wf[l], moe_bf[l], moe_wg[l], moe_wu[l], moe_wd[l])
        xp = moe_residual(xp, prep)
        xs = moe_residual(xs, prep)
    return (xp, xs, jnp.stack(nsa_p), jnp.stack(nsa_s), jnp.stack(win_p), jnp.stack(win_s),
            jnp.stack(pool_p), jnp.stack(pool_s), jnp.stack(wkv_p), jnp.stack(wkv_s),
            jnp.stack(sh_p), jnp.stack(sh_s))
```

```python
import functools

import jax
import jax.numpy as jnp
from jax import lax
from jax.experimental import pallas as pl
from jax.experimental.pallas import tpu as pltpu


def _mm_kernel(x_ref, w_ref, o_ref):
    o_ref[...] = jnp.dot(x_ref[...].astype(jnp.bfloat16), w_ref[...],
                         preferred_element_type=jnp.float32)


def _mm(x, w):
    lead = x.shape[:-1]
    k = x.shape[-1]
    n = w.shape[1]
    x2 = x.reshape(-1, k)
    m = x2.shape[0]
    npad = -(-n // 128) * 128
    wb = w.astype(jnp.bfloat16)
    if npad != n:
        wb = jnp.pad(wb, ((0, 0), (0, npad - n)))
    tn = npad
    for cand in (512, 640, 384, 256, 128):
        if npad % cand == 0:
            tn = cand
            break
    tm = 512 if m % 512 == 0 else m
    out = pl.pallas_call(
        _mm_kernel,
        grid=(m // tm, npad // tn),
        in_specs=[pl.BlockSpec((tm, k), lambda i, j: (i, 0)),
                  pl.BlockSpec((k, tn), lambda i, j: (0, j))],
        out_specs=pl.BlockSpec((tm, tn), lambda i, j: (i, j)),
        out_shape=jax.ShapeDtypeStruct((m, npad), jnp.float32),
        name="mm",
    )(x2, wb)
    return out[:, :n].reshape(lead + (n,))


f32 = jnp.float32
bf16 = jnp.bfloat16
WKV_CHUNK = 64
WKV_PAIRS = 8
WKV_SEQS = 2
WKV_PASSES = 1
WKV_GRAM_PASSES = 1


def _split(x):
    hi = x.astype(bf16)
    lo = (x - hi.astype(f32)).astype(bf16)
    return hi, lo


def _mmul(a, b, passes, nt=False):
    dn = (((1,), (1,)), ((), ())) if nt else (((1,), (0,)), ((), ()))
    d = lambda x, y: lax.dot_general(x, y, dn, preferred_element_type=f32)
    if passes == 1:
        return d(a.astype(bf16), b.astype(bf16))
    ah, al = _split(a)
    bh, bl = _split(b)
    return d(ah, bh) + (d(ah, bl) + d(al, bh))


def _wkv_kernel(r_ref, lw_ref, k_ref, v_ref, kk_ref, a_ref, s0_ref, o_ref, sT_ref, st_scr, *, passes):
    C = WKV_CHUNK
    nb = r_ref.shape[0]
    c = pl.program_id(1)
    nc = pl.num_programs(1)
    row = lax.broadcasted_iota(jnp.int32, (2 * C, 2 * C), 0)
    col = lax.broadcasted_iota(jnp.int32, (2 * C, 2 * C), 1)
    bd = (row < C) == (col < C)
    strict = bd & ((row % C) > (col % C))
    incl = bd & ((row % C) >= (col % C))
    eye = (row == col).astype(f32)
    lane_s = col < C
    m1 = lax.broadcasted_iota(jnp.int32, (C, 2 * C), 1) < C
    tri = (lax.broadcasted_iota(jnp.int32, (C, C), 0)
           >= lax.broadcasted_iota(jnp.int32, (C, C), 1)).astype(bf16)

    @pl.when(c == 0)
    def _():
        z = jnp.zeros((C, C), f32)
        for p in range(nb * WKV_PAIRS):
            s1 = s0_ref[p // WKV_PAIRS, 2 * (p % WKV_PAIRS)]
            s2 = s0_ref[p // WKV_PAIRS, 2 * (p % WKV_PAIRS) + 1]
            st_scr[p] =jnp.concatenate([jnp.concatenate([s1, z], axis=1),
                                         jnp.concatenate([z, s2], axis=1)], axis=0)

    def stack2(x):
        return jnp.concatenate([jnp.where(m1, x, 0.0), jnp.where(m1, 0.0, x)], axis=0)

    dd = lambda x, y: jnp.dot(x, y, preferred_element_type=f32)
    pairs = range(nb * WKV_PAIRS)
    sq = [p // WKV_PAIRS for p in pairs]
    sls = [slice((p % WKV_PAIRS) * 2 * C, (p % WKV_PAIRS + 1) * 2 * C) for p in pairs]
    tv = r_ref.shape[1]

    def ld(ref, p):
        x = ref[sq[p], :, sls[p]]
        return x if tv == C else jnp.concatenate([x, jnp.zeros((C - tv, 2 * C), f32)], axis=0)

    def prep(p):
        sl = sls[p]
        lw = ld(lw_ref, p)
        kk = ld(kk_ref, p)
        h1 = lw.astype(bf16)
        r1 = lw - h1.astype(f32)
        h2 = r1.astype(bf16)
        h3 = (r1 - h2.astype(f32)).astype(bf16)
        cw = dd(tri, h1) + (dd(tri, h2) + dd(tri, h3))
        cwC = cw[C - 1:C, :]
        b = kk * ld(a_ref, p)
        k = ld(k_ref, p)
        At = -kk * jnp.exp(cw - lw)
        Rt = ld(r_ref, p) * jnp.exp(cw)
        einv = jnp.exp(-cw)
        efut = jnp.exp(cwC - cw)
        X = jnp.concatenate([stack2(At), stack2(Rt)], axis=0)
        Y = jnp.concatenate([b * einv, k * einv], axis=0)
        AR = jnp.concatenate([At, Rt], axis=0)
        BK = jnp.concatenate([b * efut, k * efut], axis=0)
        return X, Y, AR, BK, jnp.exp(cwC)

    pre = [prep(p) for p in pairs]
    G = [_mmul(pre[p][0], pre[p][1], WKV_GRAM_PASSES, nt=True) for p in pairs]
    ARS = [_mmul(pre[p][2], st_scr[p], passes, nt=True) for p in pairs]
    L, Mak, Mrb, Mrk = [], [], [], []
    for p in pairs:
        GA = G[p][0:2 * C]
        GR = G[p][2 * C:4 * C]
        GAr = pltpu.roll(GA, C, axis=1)
        GRr = pltpu.roll(GR, C, axis=1)
        L.append(jnp.where(strict, jnp.where(lane_s, GA, GAr), 0.0))
        Mak.append(jnp.where(strict, jnp.where(lane_s, GAr, GA), 0.0))
        Mrb.append(jnp.where(incl, jnp.where(lane_s, GR, GRr), 0.0))
        Mrk.append(jnp.where(incl, jnp.where(lane_s, GRr, GR), 0.0))
    V = [ld(v_ref, p) for p in pairs]
    Vs = [stack2(V[p]) for p in pairs]
    Xs = [stack2(ARS[p][0:C]) + _mmul(Mak[p], Vs[p], passes) for p in pairs]
    OV = [_mmul(Mrk[p], Vs[p], passes) for p in pairs]
    P = [eye + L[p] for p in pairs]
    Q = L
    for _ in range(5):
        Q = [_mmul(Q[p], Q[p], passes) for p in pairs]
        P = [P[p] + _mmul(Q[p], P[p], passes) for p in pairs]
    Us = [_mmul(P[p], Xs[p], passes) for p in pairs]
    Os = [_mmul(Mrb[p], Us[p], passes) + OV[p] for p in pairs]
    for p in pairs:
        o_ref[sq[p], :, sls[p]] = (ARS[p][C:2 * C] + Os[p][0:C] + Os[p][C:2 * C])[0:tv]
    for p in pairs:
        U = Us[p][0:C] + Us[p][C:2 * C]
        UV = jnp.concatenate([U, V[p]], axis=0)
        dS = _mmul(UV.T, pre[p][3], passes)
        st_scr[p] = st_scr[p] * pre[p][4] + jnp.where(bd, dS, 0.0)

    @pl.when(c == nc - 1)
    def _():
        for p in pairs:
            Snew = st_scr[p]
            sT_ref[sq[p], 2 * (p % WKV_PAIRS)] = Snew[0:C, 0:C]
            sT_ref[sq[p], 2 * (p % WKV_PAIRS) + 1] = Snew[C:2 * C, C:2 * C]


def wkv_chunked(r, lw, k, v, kk, a, s0):
    B, T, D = r.shape
    H = D // 64
    C = WKV_CHUNK
    nb = WKV_SEQS
    tb = min(T, C)
    assert (T % C == 0 or T < C) and tb % 8 == 0 and D == WKV_PAIRS * 2 * C and B % nb == 0
    blk = pl.BlockSpec((nb, tb, D), lambda b, c: (b, c, 0))
    sblk = pl.BlockSpec((nb, H, 64, 64), lambda b, c: (b, 0, 0, 0))
    return pl.pallas_call(
        functools.partial(_wkv_kernel, passes=WKV_PASSES),
        grid=(B // nb, T // tb),
        in_specs=[blk] * 6 + [sblk],
        out_specs=[blk, sblk],
        out_shape=[jax.ShapeDtypeStruct((B, T, D), f32), jax.ShapeDtypeStruct((B, H, 64, 64), f32)],
        scratch_shapes=[pltpu.VMEM((nb * WKV_PAIRS, 2 * C, 2 * C), f32)],
        compiler_params=pltpu.CompilerParams(dimension_semantics=("parallel", "arbitrary")),
        name="wkv7_chunked",
    )(r, lw, k, v, kk, a, s0)


D_MODEL = 1024
BATCH = 4
SEQ = 4096
DEPTH = 4
DEC_BATCH = 128
DEC_SEQ = 8
PAST_LEN = 2048
PAGE_SIZE = 128

N_NSA_LAYERS = (DEPTH + 1) // 2
N_RWKV_LAYERS = DEPTH // 2
N_VRES = N_RWKV_LAYERS - 1

POOL_DIM = D_MODEL // 2
POOL_WINDOWS = (2, 4, 8, 16)
POOL_GROUPS = len(POOL_WINDOWS)
POOL_GDIM = POOL_DIM // POOL_GROUPS
POOL_HIST = max(POOL_WINDOWS) - 1

HEAD_DIM = 64
NSA_HEADS = (D_MODEL // 2) // HEAD_DIM
NSA_KV_HEADS = 2
NSA_GQ = NSA_HEADS // NSA_KV_HEADS
NSA_DIM = NSA_HEADS * HEAD_DIM
CMP_STRIDE = 16
CMP_LEN = 2 * CMP_STRIDE
SLC_LEN = 64
N_SEL = 16
WINDOW = 512
Q_BLOCK = 128
ROPE_DIM = HEAD_DIM // 4
ROPE_THETA = 500000.0
MIX_DIM = POOL_DIM + NSA_DIM
KV_COLS = 6 * NSA_KV_HEADS * HEAD_DIM
IN_COLS = POOL_DIM + NSA_DIM + KV_COLS + 3 * NSA_HEADS

RWKV_N = 64
RWKV_HEADS = D_MODEL // RWKV_N
LORA_W = 64
LORA_A = 64
LORA_V = 32
LORA_G = 128
GN_EPS = 64e-5

MOE_GROUPS = 4
MOE_EPG = 4
N_EXPERTS = MOE_GROUPS * MOE_EPG
MOE_TOPK = 2
D_FF_E = 256

RMS_EPS = 1e-6
NEG_INF = -1e30
RES_SCALE = (2 * DEPTH) ** -0.5

SEL_TK = 512


def _nsa_prompt_kernel(q_ref, ql_ref, g_ref, kc_ref, kcl_ref, vct_ref, ks_ref, vst_ref, kw_ref, vwt_ref,
                       o_ref, score_scr, sel_scr, *, n_cmp):
    QB = Q_BLOCK
    GQ = NSA_GQ * QB
    i = pl.program_id(2)
    s0 = i * QB
    qT = q_ref[...]
    posq = s0 + lax.broadcasted_iota(jnp.int32, (1, GQ), 1) % QB
    dd = lambda x, y: jnp.dot(x, y, preferred_element_type=f32)

    ncp = kc_ref.shape[0]
    cidx = lax.broadcasted_iota(jnp.int32, (ncp, 1), 0)
    mask_c = cidx * CMP_STRIDE + (CMP_LEN - 1) <= posq
    sc = dd(kc_ref[...], qT) + (dd(kc_ref[...], ql_ref[...]) + dd(kcl_ref[...], qT))
    sc = jnp.where(mask_c, sc, NEG_INF)
    pe = jnp.exp(sc - jnp.max(sc, axis=0, keepdims=True))
    pc = jnp.where(mask_c, pe / jnp.sum(pe, axis=0, keepdims=True), 0.0)
    o_c = dd(vct_ref[...], pc.astype(bf16))

    imp = (pc[:, 0:QB] + pc[:, QB:2 * QB]) + (pc[:, 2 * QB:3 * QB] + pc[:, 3 * QB:4 * QB])
    n_slc = score_scr.shape[0]
    per = SLC_LEN // CMP_STRIDE
    nn = lax.broadcasted_iota(jnp.int32, (n_slc, ncp), 0) * per
    cc = lax.broadcasted_iota(jnp.int32, (n_slc, ncp), 1)
    mt = (0.5 * ((cc >= nn) & (cc < nn + per)).astype(f32)
          + 0.5 * ((cc + 1 >= nn) & (cc + 1 < nn + per)).astype(f32)).astype(bf16)
    i1 = imp.astype(bf16)
    r1 = imp - i1.astype(f32)
    i2 = r1.astype(bf16)
    i3 = (r1 - i2.astype(f32)).astype(bf16)
    imp_blk = dd(mt, i1) + (dd(mt, i2) + dd(mt, i3))
    nidx = lax.broadcasted_iota(jnp.int32, (n_slc, 1), 0)
    cur = (s0 + lax.broadcasted_iota(jnp.int32, (1, QB), 1)) // SLC_LEN
    forced = (nidx == 0) | (nidx == cur) | (nidx == cur - 1)
    score = jnp.where(nidx > cur, -1.0, jnp.where(forced, 1e6, imp_blk))
    score_scr[...] = score
    rank = jnp.zeros((n_slc, QB), jnp.int32)
    for m in range(n_slc):
        sm = score_scr[m:m + 1, :]
        tie = (nidx > m).astype(jnp.int32)
        rank = rank + jnp.where(sm > score, 1, jnp.where(sm == score, tie, 0))
    bias = jnp.where(rank < min(N_SEL, n_slc), 0.0, NEG_INF)
    sel_scr[...] = jnp.concatenate([bias] * NSA_GQ, axis=1)

    def online(carry, s, vt_blk):
        m, l, acc = carry
        m_new = jnp.maximum(m, jnp.max(s, axis=0, keepdims=True))
        alpha = jnp.exp(m - m_new)
        p = jnp.exp(s - m_new)
        l = alpha * l + jnp.sum(p, axis=0, keepdims=True)
        acc = alpha * acc + dd(vt_blk, p.astype(bf16))
        return m_new, l, acc

    init = (jnp.full((1, GQ), NEG_INF, f32), jnp.zeros((1, GQ), f32), jnp.zeros((HEAD_DIM, GQ), f32))

    bpt = SEL_TK // SLC_LEN

    def sel_scores(kt):
        k0 = pl.multiple_of(kt * SEL_TK, SEL_TK)
        rows = sel_scr[pl.ds(pl.multiple_of(kt * bpt, bpt), bpt), :]
        blk = jnp.concatenate([jnp.broadcast_to(rows[j:j + 1, :], (SLC_LEN, GQ)) for j in range(bpt)], axis=0)
        return k0, dd(ks_ref[pl.ds(k0, SEL_TK), :], qT) + blk

    def sel_body(kt, carry):
        k0, s = sel_scores(kt)
        return online(carry, s, vst_ref[:, pl.ds(k0, SEL_TK)])

    def sel_body2(kp, carry):
        k0a, sa = sel_scores(2 * kp)
        k0b, sb = sel_scores(2 * kp + 1)
        carry = online(carry, sa, vst_ref[:, pl.ds(k0a, SEL_TK)])
        return online(carry, sb, vst_ref[:, pl.ds(k0b, SEL_TK)])

    n_full = s0 // SEL_TK
    carry = lax.fori_loop(0, n_full // 2, sel_body2, init)
    carry = lax.fori_loop(2 * (n_full // 2), n_full, sel_body, carry)
    k0, s = sel_scores(n_full)
    kpos = k0 + lax.broadcasted_iota(jnp.int32, (SEL_TK, 1), 0)
    _, l_s, acc_s = online(carry, jnp.where(kpos <= posq, s, NEG_INF), vst_ref[:, pl.ds(k0, SEL_TK)])

    nwt = WINDOW // QB
    tiles = []
    for j in range(nwt + 1):
        k0 = s0 - WINDOW + j * QB
        k0c = pl.multiple_of(jnp.maximum(k0, 0), QB)
        s = dd(kw_ref[pl.ds(k0c, QB), :], qT)
        kpos = k0c + lax.broadcasted_iota(jnp.int32, (QB, 1), 0)
        if j == nwt:
            s = jnp.where(kpos <= posq, s, NEG_INF)
        else:
            if j == 0:
                s = jnp.where(posq - kpos < WINDOW, s, NEG_INF)
            s = s + jnp.where(k0 >= 0, 0.0, NEG_INF)
        tiles.append((k0c, s))
    m_w = functools.reduce(jnp.maximum, [jnp.max(s, axis=0, keepdims=True) for _, s in tiles])
    l_w = jnp.zeros((1, GQ), f32)
    acc_w = jnp.zeros((HEAD_DIM, GQ), f32)
    for k0c, s in tiles:
        p = jnp.exp(s - m_w)
        l_w = l_w + jnp.sum(p, axis=0, keepdims=True)
        acc_w = acc_w + dd(vwt_ref[:, pl.ds(k0c, QB)], p.astype(bf16))

    g = jax.nn.sigmoid(g_ref[...])
    o = g[0:1] * o_c + g[1:2] * (acc_s / l_s) + g[2:3] * (acc_w / l_w)
    o_ref[...] = jnp.concatenate([o[:, j * QB:(j + 1) * QB].T for j in range(NSA_GQ)], axis=1)


def nsa_prompt_pallas(ops, kc, vc):
    qT, qTl, gT, ks, vst, kw, vwt = ops
    B, KVH, T, D = ks.shape
    G, QB = NSA_GQ, Q_BLOCK
    assert T % SEL_TK == 0 and T % QB == 0
    nqb = T // QB
    n_cmp = kc.shape[1]
    ncp = -(-n_cmp // 128) * 128
    n_slc = T // SLC_LEN
    kcp, kcl = _split(jnp.pad(kc, ((0, 0), (0, ncp - n_cmp), (0, 0), (0, 0))).transpose(0, 2, 1, 3))
    vct = jnp.pad(vc, ((0, 0), (0, ncp - n_cmp), (0, 0), (0, 0))).transpose(0, 2, 3, 1).astype(bf16)
    bh = lambda *shape: pl.BlockSpec((None, None) + shape, lambda b, h, i: (b, h) + (0,) * len(shape))
    bhi = lambda *shape: pl.BlockSpec((None, None, None) + shape, lambda b, h, i: (b, h, i) + (0,) * len(shape))
    return pl.pallas_call(
        functools.partial(_nsa_prompt_kernel, n_cmp=n_cmp),
        grid=(B, KVH, nqb),
        in_specs=[bhi(D, G * QB), bhi(D, G * QB), bhi(3, G * QB), bh(ncp, D), bh(ncp, D), bh(D, ncp),
                  bh(T, D), bh(D, T), bh(T, D), bh(D, T)],
        out_specs=pl.BlockSpec((None, QB, G * D), lambda b, h, i: (b, i, h)),
        out_shape=jax.ShapeDtypeStruct((B, T, KVH * G * D), f32),
        scratch_shapes=[pltpu.VMEM((n_slc, QB), f32), pltpu.VMEM((n_slc, G * QB), f32)],
        compiler_params=pltpu.CompilerParams(dimension_semantics=("parallel", "parallel", "arbitrary"),
                                             vmem_limit_bytes=48 * 1024 * 1024),
        name="nsa_prompt",
    )(qT, qTl, gT, kcp, kcl, vct, ks, vst, kw, vwt)


def _split3(x):
    h1 = x.astype(bf16)
    r1 = x - h1.astype(f32)
    h2 = r1.astype(bf16)
    return h1, h2, (r1 - h2.astype(f32)).astype(bf16)


def _dot_exact_rhs(x, m):
    d = lambda a: jnp.dot(a, m, preferred_element_type=f32)
    h1, h2, h3 = _split3(x)
    return d(h1) + (d(h2) + d(h3))


def _dot_exact_lhs(m, x):
    d = lambda a: jnp.dot(m, a, preferred_element_type=f32)
    h1, h2, h3 = _split3(x)
    return d(h1) + (d(h2) + d(h3))


def _nsa_sample_kernel(pt_ref, *refs, n_pages, ts, past_len):
    pages = refs[:n_pages]
    (new_ref, wbuf_ref, wnew_ref, qh_ref, ql_ref, gate_ref, wa_ref, wb_ref, phik_ref, phiv_ref,
     gain_ref, cos_ref, sin_ref, o_ref, ssel_scr, a_scr, b_scr, score_scr) = refs[n_pages:]
    P = PAGE_SIZE
    KV = NSA_KV_HEADS * HEAD_DIM
    NCOL = NSA_KV_HEADS * NSA_GQ * ts
    NQ = NSA_KV_HEADS * ts
    cpp = P // CMP_STRIDE
    n_chunk = (past_len + SLC_LEN) // CMP_STRIDE
    n_cmp = n_chunk - 1
    ncp = a_scr.shape[0]
    n_slc = (past_len + SLC_LEN) // SLC_LEN
    nsp = score_scr.shape[0]
    dd = lambda x, y: jnp.dot(x, y, preferred_element_type=f32)
    qh = qh_ref[...]
    ql = ql_ref[...]
    col = lax.broadcasted_iota(jnp.int32, (1, NCOL), 1)
    t_col = col % ts
    zpad = jnp.zeros((P - ts, 4 * KV), f32)
    new_tile = jnp.concatenate([new_ref[...], zpad], axis=0)

    wa = wa_ref[...]
    wb = wb_ref[...]
    a_scr[...] = jnp.zeros(a_scr.shape, f32)
    b_scr[...] = jnp.zeros(b_scr.shape, f32)
    def slab(j, kind):
        if j == n_pages:
            return new_tile[:, kind * KV:(kind + 1) * KV]
        return pages[j][:, kind * KV:(kind + 1) * KV]

    for j in range(n_pages + 1):
        xc = jnp.concatenate([slab(j, 0), slab(j, 1)], axis=1)
        a_scr[j * cpp:(j + 1) * cpp, :] = (xc * wa).reshape(cpp, CMP_STRIDE, 2 * KV).sum(axis=1)
        b_scr[j * cpp:(j + 1) * cpp, :] = (xc * wb).reshape(cpp, CMP_STRIDE, 2 * KV).sum(axis=1)
        ssel_scr[j * P:(j + 1) * P, :] = dd(slab(j, 2).astype(bf16), qh)

    mean = a_scr[...] + pltpu.roll(b_scr[...], ncp - 1, axis=0)
    kc = _mmul(mean[:, 0:KV], phik_ref[...], 3)
    vc = _mmul(mean[:, KV:2 * KV], phiv_ref[...], 3)
    r_i = lax.broadcasted_iota(jnp.int32, (KV, KV), 0)
    c_i = lax.broadcasted_iota(jnp.int32, (KV, KV), 1)
    same_head = (r_i // HEAD_DIM) == (c_i // HEAD_DIM)
    mavg = jnp.where(same_head, 1.0 / HEAD_DIM, 0.0).astype(bf16)
    kc = kc * lax.rsqrt(_dot_exact_rhs(kc * kc, mavg) + RMS_EPS) * gain_ref[...]
    half = ROPE_DIM // 2
    rd, cd = r_i % HEAD_DIM, c_i % HEAD_DIM
    rot = jnp.where(same_head & (cd < half) & (rd == cd + half), -1.0,
                    jnp.where(same_head & (cd >= half) & (cd < ROPE_DIM) & (rd == cd - half), 1.0, 0.0)).astype(bf16)
    kc = kc * cos_ref[...] + _dot_exact_rhs(kc, rot) * sin_ref[...]

    kch, kcl = _split(kc)
    sc = dd(kch, qh) + (dd(kch, ql) + dd(kcl, qh))
    cidx = lax.broadcasted_iota(jnp.int32, (ncp, 1), 0)
    mask_c = (cidx * CMP_STRIDE + (CMP_LEN - 1) <= past_len + t_col) & (cidx < n_cmp)
    sc = jnp.where(mask_c, sc, NEG_INF)
    pe = jnp.exp(sc - jnp.max(sc, axis=0, keepdims=True))
    pc = jnp.where(mask_c, pe / jnp.sum(pe, axis=0, keepdims=True), 0.0)
    o_c = dd(pc.T.astype(bf16), vc.astype(bf16))

    gr = lax.broadcasted_iota(jnp.int32, (NCOL, NQ), 0)
    gc = lax.broadcasted_iota(jnp.int32, (NCOL, NQ), 1)
    gsum = ((gr // (NSA_GQ * ts) == gc // ts) & (gr % ts == gc % ts)).astype(bf16)
    imp = _dot_exact_rhs(pc, gsum)
    per = SLC_LEN // CMP_STRIDE
    nn = lax.broadcasted_iota(jnp.int32, (nsp, ncp), 0) * per
    cc = lax.broadcasted_iota(jnp.int32, (nsp, ncp), 1)
    mt = (0.5 * ((cc >= nn) & (cc < nn + per)).astype(f32)
          + 0.5 * ((cc + 1 >= nn) & (cc + 1 < nn + per)).astype(f32)).astype(bf16)
    imp_blk = _dot_exact_lhs(mt, imp)
    nidx = lax.broadcasted_iota(jnp.int32, (nsp, 1), 0)
    cur = (past_len + lax.broadcasted_iota(jnp.int32, (1, NQ), 1) % ts) // SLC_LEN
    forced = (nidx == 0) | (nidx == cur) | (nidx == cur - 1)
    score = jnp.where(nidx >= n_slc, -2.0, jnp.where(nidx > cur, -1.0, jnp.where(forced, 1e6, imp_blk)))
    score_scr[...] = score
    rank = jnp.zeros((nsp, NQ), jnp.int32)
    for m in range(n_slc):
        sm = score_scr[m:m + 1, :]
        beats = (sm > score) | ((sm == score) & (nidx > m))
        rank = rank + beats.astype(jnp.int32)
    sel = (rank < min(N_SEL, n_slc)).astype(bf16)
    gr2 = lax.broadcasted_iota(jnp.int32, (NQ, NCOL), 0)
    gc2 = lax.broadcasted_iota(jnp.int32, (NQ, NCOL), 1)
    gexp = ((gc2 // (NSA_GQ * ts) == gr2 // ts) & (gc2 % ts == gr2 % ts)).astype(bf16)
    sel_c = dd(sel, gexp)

    def two_pass(n_tiles, score_tile, mask_tile, v_tile):
        m = jnp.full((1, NCOL), NEG_INF, f32)
        for j in range(n_tiles):
            m = jnp.maximum(m, jnp.max(jnp.where(mask_tile(j), score_tile(j), NEG_INF), axis=0, keepdims=True))
        num = jnp.zeros((NCOL, KV), f32)
        den = jnp.zeros((NCOL, KV), f32)
        ones = jnp.ones((P, KV), bf16)
        for j in range(n_tiles):
            p = jnp.where(mask_tile(j), jnp.exp(score_tile(j) - m), 0.0).T.astype(bf16)
            num = num + dd(p, v_tile(j))
            den = den + dd(p, ones)
        return num, den

    bpp = P // SLC_LEN
    row = lax.broadcasted_iota(jnp.int32, (P, 1), 0)

    def sel_mask(j):
        blk = jnp.concatenate([jnp.broadcast_to(sel_c[j * bpp + i:j * bpp + i + 1, :], (SLC_LEN, NCOL))
                               for i in range(bpp)], axis=0)
        return (blk > 0.5) & (j * P + row <= past_len + t_col)

    num_s, den_s = two_pass(
        n_pages + 1, lambda j: ssel_scr[j * P:(j + 1) * P, :], sel_mask,
        lambda j: slab(j, 3).astype(bf16))

    lb = wbuf_ref.shape[0]
    nwt = lb // P
    wnew = jnp.concatenate([wnew_ref[...], jnp.zeros((P - ts, 2 * KV), f32)], axis=0)

    def w_tile(j):
        return wbuf_ref[j * P:(j + 1) * P, :] if j < nwt else wnew

    def win_mask(j):
        pos_w = (past_len - lb + j * P + row) if j < nwt else (past_len + row)
        dq = past_len + t_col - pos_w
        return (dq >= 0) & (dq < WINDOW) & (pos_w >= 0) & ((row < ts) | (j < nwt))

    num_w, den_w = two_pass(
        nwt + 1, lambda j: dd(w_tile(j)[:, 0:KV].astype(bf16), qh), win_mask,
        lambda j: w_tile(j)[:, KV:2 * KV].astype(bf16))

    g = jax.nn.sigmoid(gate_ref[...])
    o_ref[...] = g[0] * o_c + g[1] * (num_s / den_s) + g[2] * (num_w / den_w)


def nsa_sample_pallas(q, gl, rows_new, win_new, pools, page_table, win_bufs, layer, kc_w, phi, k_gain):
    B, ts = q.shape[:2]
    KVH, G, D, P = NSA_KV_HEADS, NSA_GQ, HEAD_DIM, PAGE_SIZE
    KV = KVH * D
    n_pages = page_table.shape[1]
    past_len = n_pages * P
    n_pool = pools.shape[1]
    assert pools.shape[2] == P and ts <= SLC_LEN and P % SLC_LEN == 0
    lb = win_bufs.shape[2]
    assert lb % P == 0
    NCOL = KVH * G * ts
    n_chunk = (past_len + SLC_LEN) // CMP_STRIDE
    ncp = -(-n_chunk // 8) * 8
    n_slc = (past_len + SLC_LEN) // SLC_LEN
    nsp = -(-n_slc // 8) * 8
    qs = (q * D ** -0.5).reshape(B, ts, KVH, G, D).transpose(0, 2, 4, 3, 1).reshape(B, KVH, D, G * ts)
    z = jnp.zeros_like(qs[:, 0])
    qbd = jnp.concatenate([jnp.concatenate([qs[:, 0], z], axis=2), jnp.concatenate([z, qs[:, 1]], axis=2)], axis=1)
    qh, ql = _split(qbd)
    gate = gl.reshape(B, ts, KVH, G, 3).transpose(0, 4, 2, 3, 1).reshape(B, 3, NCOL, 1)
    gate = jnp.broadcast_to(gate, (B, 3, NCOL, KV)).astype(f32)
    w_lane = jnp.repeat(kc_w.reshape(2 * KVH, CMP_LEN), D, axis=0)
    reps = P // CMP_STRIDE
    wa = jnp.tile(w_lane[:, :CMP_STRIDE].T, (reps, 1)).astype(f32)
    wb = jnp.tile(w_lane[:, CMP_STRIDE:].T, (reps, 1)).astype(f32)
    zz = jnp.zeros((D, D), f32)
    bdiag = lambda m: jnp.concatenate([jnp.concatenate([m, zz], axis=1), jnp.concatenate([zz, m], axis=1)], axis=0)
    phik, phiv = bdiag(phi[0].astype(f32)), bdiag(phi[1].astype(f32))
    gain = jnp.tile(k_gain.astype(f32), KVH).reshape(1, KV)
    half = ROPE_DIM // 2
    inv = ROPE_THETA ** (-jnp.arange(half, dtype=f32) / half)
    cmp_end = (jnp.arange(ncp, dtype=jnp.int32) * CMP_STRIDE + (CMP_LEN - 1)).astype(f32)
    ang = cmp_end[:, None] * inv
    cos_h = jnp.concatenate([jnp.cos(ang), jnp.cos(ang), jnp.ones((ncp, D - ROPE_DIM), f32)], axis=1)
    sin_h = jnp.concatenate([jnp.sin(ang), jnp.sin(ang), jnp.zeros((ncp, D - ROPE_DIM), f32)], axis=1)
    cos_t, sin_t = jnp.tile(cos_h, (1, KVH)), jnp.tile(sin_h, (1, KVH))
    pool2 = pools.reshape(pools.shape[0] * n_pool, P, 4 * KV)
    new2 = rows_new.reshape(B, ts, 4 * KV).astype(f32)
    wbuf2 = win_bufs.reshape(win_bufs.shape[0] * B, lb, 2 * KV)
    wnew2 = win_new.reshape(B, ts, 2 * KV).astype(f32)
    page_spec = lambda j: pl.BlockSpec((None, P, 4 * KV),
                                       lambda b, pt, j=j: (layer * n_pool + pt[b, j], 0, 0))
    per_b = lambda *s: pl.BlockSpec((None,) + s, lambda b, pt: (b,) + (0,) * len(s))
    wbuf_spec = pl.BlockSpec((None, lb, 2 * KV), lambda b, pt: (layer * B + b, 0, 0))
    const = lambda *s: pl.BlockSpec(s, lambda b, pt: (0,) * len(s))
    grid_spec = pltpu.PrefetchScalarGridSpec(
        num_scalar_prefetch=1, grid=(B,),
        in_specs=[page_spec(j) for j in range(n_pages)] + [
            per_b(ts, 4 * KV), wbuf_spec, per_b(ts, 2 * KV), per_b(KV, NCOL), per_b(KV, NCOL),
            per_b(3, NCOL, KV), const(P, 2 * KV), const(P, 2 * KV), const(KV, KV), const(KV, KV),
            const(1, KV), const(ncp, KV), const(ncp, KV)],
        out_specs=per_b(NCOL, KV),
        scratch_shapes=[pltpu.VMEM(((n_pages + 1) * P, NCOL), f32), pltpu.VMEM((ncp, 2 * KV), f32),
                        pltpu.VMEM((ncp, 2 * KV), f32), pltpu.VMEM((nsp, KVH * ts), f32)])
    out = pl.pallas_call(
        functools.partial(_nsa_sample_kernel, n_pages=n_pages, ts=ts, past_len=past_len),
        grid_spec=grid_spec,
        out_shape=jax.ShapeDtypeStruct((B, NCOL, KV), f32),
        compiler_params=pltpu.CompilerParams(dimension_semantics=("arbitrary",),
                                             vmem_limit_bytes=48 * 1024 * 1024),
        name="nsa_sample",
    )(page_table, *([pool2] * n_pages), new2, wbuf2, wnew2, qh, ql, gate, wa, wb, phik, phiv, gain, cos_t, sin_t)
    o4 = out.reshape(B, KVH, G, ts, KVH, D)
    o = jnp.stack([o4[:, 0, :, :, 0], o4[:, 1, :, :, 1]], axis=1)
    return o.transpose(0, 3, 1, 2, 4).reshape(B, ts, KVH * G * D)


MOE_TM = 512
ROUTER_LANES = 128


def _moe_kernel(x_ref, g_ref, wrh_ref, wrl_ref, br_ref, wg_ref, wu_ref, wd_ref, o_ref,
                h_scr, gate_scr, acc_scr):
    grp = pl.program_id(1)
    dd = lambda a, b: jnp.dot(a, b, preferred_element_type=f32)
    tm = x_ref.shape[0]
    lane = lax.broadcasted_iota(jnp.int32, (tm, ROUTER_LANES), 1).astype(f32)
    far = float(ROUTER_LANES)

    @pl.when(grp == 0)
    def _():
        x = x_ref[...]
        h = x * lax.rsqrt(jnp.mean(x * x, axis=-1, keepdims=True) + RMS_EPS) * g_ref[...]
        hh, hl = _split(h)
        h_scr[...] = hh
        logits = dd(hh, wrh_ref[...]) + (dd(hh, wrl_ref[...]) + dd(hl, wrh_ref[...])) + br_ref[...]
        is_c = lane < MOE_GROUPS
        lc = jnp.where(is_c, logits, NEG_INF)
        mc = jnp.max(lc, axis=1, keepdims=True)
        g_idx = jnp.min(jnp.where(lc == mc, lane, far), axis=1, keepdims=True)
        g_w = 1.0 / jnp.sum(jnp.where(is_c, jnp.exp(lc - mc), 0.0), axis=1, keepdims=True)
        lo = MOE_GROUPS + MOE_EPG * g_idx
        lf = jnp.where((lane >= lo) & (lane < lo + MOE_EPG), logits, NEG_INF)
        v1 = jnp.max(lf, axis=1, keepdims=True)
        i1 = jnp.min(jnp.where(lf == v1, lane, far), axis=1, keepdims=True)
        lf2 = jnp.where(lane == i1, NEG_INF, lf)
        v2 = jnp.max(lf2, axis=1, keepdims=True)
        i2 = jnp.min(jnp.where(lf2 == v2, lane, far), axis=1, keepdims=True)
        e21 = jnp.exp(v2 - v1)
        w1 = g_w / (1.0 + e21)
        gate_scr[...] = jnp.where(lane == i1, w1, jnp.where(lane == i2, e21 * w1, 0.0))
        acc_scr[...] = x

    h = h_scr[...]
    hg = dd(h, wg_ref[...])
    hu = dd(h, wu_ref[...])
    gate = gate_scr[...]
    first = (MOE_GROUPS + MOE_EPG * grp).astype(f32)
    cols = []
    for e in range(MOE_EPG):
        ge = jnp.sum(jnp.where(lane == first + e, gate, 0.0), axis=1, keepdims=True)
        sl = slice(e * D_FF_E, (e + 1) * D_FF_E)
        hge = hg[:, sl]
        cols.append((hge * jax.nn.sigmoid(hge) * hu[:, sl] * ge).astype(bf16))
    acc_scr[...] += dd(jnp.concatenate(cols, axis=1), wd_ref[...])

    @pl.when(grp == MOE_GROUPS - 1)
    def _():
        o_ref[...] = acc_scr[...]


def moe_prep(g, wc, bc, wf, bf, wg, wu, wd):
    d = wc.shape[0]
    pad = ROUTER_LANES - MOE_GROUPS - N_EXPERTS
    wr = jnp.pad(jnp.concatenate([wc, wf], axis=1).astype(f32), ((0, 0), (0, pad)))
    wrh, wrl = _split(wr)
    br = jnp.pad(jnp.concatenate([bc, bf]).astype(f32), (0, pad)).reshape(1, ROUTER_LANES)
    regroup = lambda w: (w.reshape(MOE_GROUPS, MOE_EPG, d, D_FF_E).transpose(0, 2, 1, 3)
                         .reshape(MOE_GROUPS, d, MOE_EPG * D_FF_E).astype(bf16))
    wdg = wd.reshape(MOE_GROUPS, MOE_EPG * D_FF_E, d).astype(bf16)
    return g.reshape(1, d).astype(f32), wrh, wrl, br, regroup(wg), regroup(wu), wdg


def moe_residual(x, prep):
    g, wrh, wrl, br, wgg, wug, wdg = prep
    shp = x.shape
    d = shp[-1]
    x2 = x.reshape(-1, d)
    m = x2.shape[0]
    tm = MOE_TM
    assert m % tm == 0
    gf = MOE_EPG * D_FF_E
    full = lambda r, c: pl.BlockSpec((r, c), lambda i, j: (0, 0))
    out = pl.pallas_call(
        _moe_kernel,
        grid=(m // tm, MOE_GROUPS),
        in_specs=[pl.BlockSpec((tm, d), lambda i, j: (i, 0)), full(1, d),
                  full(d, ROUTER_LANES), full(d, ROUTER_LANES), full(1, ROUTER_LANES),
                  pl.BlockSpec((None, d, gf), lambda i, j: (j, 0, 0)),
                  pl.BlockSpec((None, d, gf), lambda i, j: (j, 0, 0)),
                  pl.BlockSpec((None, gf, d), lambda i, j: (j, 0, 0))],
        out_specs=pl.BlockSpec((tm, d), lambda i, j: (i, 0)),
        out_shape=jax.ShapeDtypeStruct((m, d), f32),
        scratch_shapes=[pltpu.VMEM((tm, d), bf16), pltpu.VMEM((tm, ROUTER_LANES), f32),
                        pltpu.VMEM((tm, d), f32)],
        compiler_params=pltpu.CompilerParams(dimension_semantics=("parallel", "arbitrary"),
                                             vmem_limit_bytes=48 * 1024 * 1024),
        name="moe",
    )(x2, g, wrh, wrl, br, wgg, wug, wdg)
    return out.reshape(shp)


def rmsnorm(x, g):
    xf = x.astype(jnp.float32)
    y = xf * lax.rsqrt(jnp.mean(xf * xf, axis=-1, keepdims=True) + RMS_EPS)
    return (y * g.astype(jnp.float32)).astype(x.dtype)


def rope_partial(x, pos):
    half = ROPE_DIM // 2
    inv = ROPE_THETA ** (-jnp.arange(half, dtype=jnp.float32) / half)
    ang = pos.astype(jnp.float32)[:, None] * inv
    cos = jnp.cos(ang)[:, None, :]
    sin = jnp.sin(ang)[:, None, :]
    xf = x.astype(jnp.float32)
    x1 = xf[..., :half]
    x2 = xf[..., half:ROPE_DIM]
    out = jnp.concatenate([x1 * cos - x2 * sin, x2 * cos + x1 * sin, xf[..., ROPE_DIM:]], axis=-1)
    return out.astype(x.dtype)


def masked_softmax(s, mask):
    s = jnp.where(mask, s.astype(jnp.float32), NEG_INF)
    p = jax.nn.softmax(s, axis=-1)
    return jnp.where(mask, p, 0.0)


def pool_mix(u, hist, p0, w_grp, scale):
    b, t, _ = u.shape
    ext = jnp.concatenate([hist.astype(u.dtype), u], axis=1).astype(jnp.float32)
    cs = jnp.pad(jnp.cumsum(ext, axis=1), ((0, 0), (1, 0), (0, 0)))
    cnt_pos = p0 + jnp.arange(t, dtype=jnp.int32) + 1
    means = []
    for gi, w in enumerate(POOL_WINDOWS):
        c = cs[..., gi * POOL_GDIM:(gi + 1) * POOL_GDIM]
        win_sum = c[:, POOL_HIST + 1:POOL_HIST + 1 + t] - c[:, POOL_HIST + 1 - w:POOL_HIST + 1 - w + t]
        cnt = jnp.minimum(cnt_pos, w).astype(jnp.float32)[None, :, None]
        means.append(win_sum / cnt)
    mean = jnp.stack(means, axis=2)
    d = mean - u.reshape(b, t, POOL_GROUPS, POOL_GDIM).astype(jnp.float32)
    y = jnp.einsum('btgc,gcd->btgd', d, w_grp.astype(jnp.float32)).reshape(b, t, POOL_DIM)
    return (y * scale.astype(jnp.float32)).astype(u.dtype)


PROJ_TM = 512


def _norm_mm_kernel(x_ref, g_ref, w_ref, o_ref):
    x = x_ref[...]
    h = x * lax.rsqrt(jnp.mean(x * x, axis=-1, keepdims=True) + RMS_EPS) * g_ref[...]
    o_ref[...] = jnp.dot(h.astype(bf16), w_ref[...], preferred_element_type=f32)


def _norm_mm(x, g, w):
    lead, d = x.shape[:-1], x.shape[-1]
    n = w.shape[1]
    x2 = x.reshape(-1, d)
    m = x2.shape[0]
    npad = -(-n // 128) * 128
    wb = jnp.pad(w.astype(bf16), ((0, 0), (0, npad - n)))
    tm = PROJ_TM
    assert m % tm == 0
    out = pl.pallas_call(
        _norm_mm_kernel, grid=(m // tm,),
        in_specs=[pl.BlockSpec((tm, d), lambda i: (i, 0)), pl.BlockSpec((1, d), lambda i: (0, 0)),
                  pl.BlockSpec((d, npad), lambda i: (0, 0))],
        out_specs=pl.BlockSpec((tm, npad), lambda i: (i, 0)),
        out_shape=jax.ShapeDtypeStruct((m, npad), f32),
        compiler_params=pltpu.CompilerParams(dimension_semantics=("parallel",),
                                             vmem_limit_bytes=48 * 1024 * 1024),
        name="norm_mm",
    )(x2, g.reshape(1, d).astype(f32), wb)
    return out[:, :n].reshape(lead + (n,))


def _mix_out_kernel(a_ref, b_ref, wa_ref, wb_ref, x_ref, o_ref):
    dd = lambda p, q: jnp.dot(p.astype(bf16), q, preferred_element_type=f32)
    o_ref[...] = x_ref[...] + (dd(a_ref[...], wa_ref[...]) + dd(b_ref[...], wb_ref[...]))


def _mix_out(a, b, w, x):
    d = x.shape[-1]
    ka, kb = a.shape[-1], b.shape[-1]
    x2 = x.reshape(-1, d)
    m = x2.shape[0]
    tm = PROJ_TM
    assert m % tm == 0
    wbf = w.astype(bf16)
    tile = lambda c: pl.BlockSpec((tm, c), lambda i: (i, 0))
    full = lambda r, c: pl.BlockSpec((r, c), lambda i: (0, 0))
    out = pl.pallas_call(
        _mix_out_kernel, grid=(m // tm,),
        in_specs=[tile(ka), tile(kb), full(ka, d), full(kb, d), tile(d)], out_specs=tile(d),
        out_shape=jax.ShapeDtypeStruct((m, d), f32),
        compiler_params=pltpu.CompilerParams(dimension_semantics=("parallel",)),
        name="mix_out",
    )(a.reshape(m, ka), b.reshape(m, kb), wbf[:ka], wbf[ka:], x2)
    return out.reshape(x.shape)


def _ab_feat_kernel(x_ref, g_ref, w_ref, qg_ref, kg_ref, cos_ref, sin_ref, hs_ref, he_ref, rot_ref,
                    pool_ref, rows_ref, win_ref, qh_ref, ql_ref, gt_ref, ks_ref, kw_ref, vst_ref, vwt_ref):
    D, KV = HEAD_DIM, NSA_KV_HEADS * HEAD_DIM
    x = x_ref[...]
    h = x * lax.rsqrt(jnp.mean(x * x, axis=-1, keepdims=True) + RMS_EPS) * g_ref[...]
    u = jnp.dot(h.astype(bf16), w_ref[...], preferred_element_type=f32)
    off_kv = POOL_DIM + NSA_DIM
    pool_ref[...] = u[:, :POOL_DIM]
    cos, sin = cos_ref[...], sin_ref[...]

    def norm_rope(z, gain):
        n = z.shape[1] // D
        hs, he, rot = hs_ref[0:n * D, :], he_ref[:, 0:n * D], rot_ref[0:n * D, 0:n * D]
        ms = _dot_exact_rhs(_dot_exact_rhs(z * z, hs), he) * (1.0 / D)
        zn = z * lax.rsqrt(ms + RMS_EPS) * gain
        wide = lambda t: jnp.concatenate([t] * (n // 2), axis=1)
        return zn * wide(cos) + _dot_exact_rhs(zn, rot) * wide(sin)

    q = norm_rope(u[:, POOL_DIM:off_kv], qg_ref[...]) * (D ** -0.5)
    for kvh in range(NSA_KV_HEADS):
        qt = jnp.concatenate([q[:, (kvh * NSA_GQ + g) * D:(kvh * NSA_GQ + g + 1) * D].T
                              for g in range(NSA_GQ)], axis=1)
        hi, lo = _split(qt)
        qh_ref[kvh] = hi
        ql_ref[kvh] = lo
    kv = u[:, off_kv:off_kv + KV_COLS]
    slab = lambda i: kv[:, i * KV:(i + 1) * KV]
    kr = norm_rope(jnp.concatenate([slab(2), slab(4)], axis=1), kg_ref[...])
    k_slc, k_win = kr[:, 0:KV], kr[:, KV:2 * KV]
    rows_ref[...] = jnp.concatenate([slab(0), slab(1), k_slc, slab(3)], axis=1)
    win_ref[...] = jnp.concatenate([k_win, slab(5)], axis=1)
    for kvh in range(NSA_KV_HEADS):
        hsl = slice(kvh * D, (kvh + 1) * D)
        ks_ref[kvh] = k_slc[:, hsl].astype(bf16)
        kw_ref[kvh] = k_win[:, hsl].astype(bf16)
        vst_ref[kvh] = slab(3)[:, hsl].T.astype(bf16)
        vwt_ref[kvh] = slab(5)[:, hsl].T.astype(bf16)
    gt_ref[...] = u[:, off_kv + KV_COLS:].T


def ab_features_prompt(x, norm_g, w_in, q_norm, k_norm):
    B, T, d = x.shape
    QB, D, KVH, G = Q_BLOCK, HEAD_DIM, NSA_KV_HEADS, NSA_GQ
    KV = KVH * D
    assert T % QB == 0
    nqb = T // QB
    npad = -(-IN_COLS // 128) * 128
    assert npad - (POOL_DIM + NSA_DIM + KV_COLS) == 128
    wb = jnp.pad(w_in.astype(bf16), ((0, 0), (0, npad - IN_COLS)))
    half = ROPE_DIM // 2
    inv = ROPE_THETA ** (-jnp.arange(half, dtype=f32) / half)
    ang = jnp.arange(T, dtype=jnp.int32).astype(f32)[:, None] * inv
    cos_h = jnp.concatenate([jnp.cos(ang), jnp.cos(ang), jnp.ones((T, D - ROPE_DIM), f32)], axis=1)
    sin_h = jnp.concatenate([jnp.sin(ang), jnp.sin(ang), jnp.zeros((T, D - ROPE_DIM), f32)], axis=1)
    cos_t, sin_t = jnp.tile(cos_h, (1, 2)), jnp.tile(sin_h, (1, 2))
    r_i = lax.broadcasted_iota(jnp.int32, (NSA_DIM, NSA_DIM), 0)
    c_i = lax.broadcasted_iota(jnp.int32, (NSA_DIM, NSA_DIM), 1)
    same = (r_i // D) == (c_i // D)
    rd, cd = r_i % D, c_i % D
    rot = jnp.where(same & (cd < half) & (rd == cd + half), -1.0,
                    jnp.where(same & (cd >= half) & (cd < ROPE_DIM) & (rd == cd - half), 1.0, 0.0)).astype(bf16)
    hs = (lax.broadcasted_iota(jnp.int32, (NSA_DIM, 128), 0) // D
          == lax.broadcasted_iota(jnp.int32, (NSA_DIM, 128), 1)).astype(bf16)
    qg = jnp.tile(q_norm.astype(f32), NSA_HEADS).reshape(1, NSA_DIM)
    kg = jnp.concatenate([jnp.tile(k_norm[1].astype(f32), KVH), jnp.tile(k_norm[2].astype(f32), KVH)]).reshape(1, 2 * KV)
    full = lambda a: pl.BlockSpec(a.shape, lambda b, i: (0,) * a.ndim)
    tok = lambda c: pl.BlockSpec((None, QB, c), lambda b, i: (b, i, 0))
    f = jax.ShapeDtypeStruct
    outs = pl.pallas_call(
        _ab_feat_kernel, grid=(B, nqb),
        in_specs=[tok(d), pl.BlockSpec((1, d), lambda b, i: (0, 0)), full(wb), full(qg), full(kg),
                  pl.BlockSpec((QB, 2 * D), lambda b, i: (i, 0)), pl.BlockSpec((QB, 2 * D), lambda b, i: (i, 0)),
                  full(hs), pl.BlockSpec((128, NSA_DIM), lambda b, i: (0, 0)), full(rot)],
        out_specs=[tok(POOL_DIM), tok(4 * KV), tok(2 * KV),
                   pl.BlockSpec((None, KVH, None, D, G * QB), lambda b, i: (b, 0, i, 0, 0)),
                   pl.BlockSpec((None, KVH, None, D, G * QB), lambda b, i: (b, 0, i, 0, 0)),
                   pl.BlockSpec((None, None, 128, QB), lambda b, i: (b, i, 0, 0)),
                   pl.BlockSpec((None, KVH, QB, D), lambda b, i: (b, 0, i, 0)),
                   pl.BlockSpec((None, KVH, QB, D), lambda b, i: (b, 0, i, 0)),
                   pl.BlockSpec((None, KVH, D, QB), lambda b, i: (b, 0, 0, i)),
                   pl.BlockSpec((None, KVH, D, QB), lambda b, i: (b, 0, 0, i))],
        out_shape=[f((B, T, POOL_DIM), f32), f((B, T, 4 * KV), f32), f((B, T, 2 * KV), f32),
                   f((B, KVH, nqb, D, G * QB), bf16), f((B, KVH, nqb, D, G * QB), bf16),
                   f((B, nqb, 128, QB), f32), f((B, KVH, T, D), bf16), f((B, KVH, T, D), bf16),
                   f((B, KVH, D, T), bf16), f((B, KVH, D, T), bf16)],
        compiler_params=pltpu.CompilerParams(dimension_semantics=("parallel", "parallel"),
                                             vmem_limit_bytes=48 * 1024 * 1024),
        name="ab_feat",
    )(x, norm_g.reshape(1, d).astype(f32), wb, qg, kg, cos_t, sin_t, hs, hs.T, rot)
    pool_in, rows, win, qh, ql, gt, ks, kw, vst, vwt = outs
    gT = (gt[:, :, :NSA_HEADS * 3].reshape(B, nqb, KVH, G, 3, QB).transpose(0, 2, 1, 4, 3, 5)
          .reshape(B, KVH, nqb, 3, G * QB))
    return pool_in, rows, win, (qh, ql, gT, ks, vst, kw, vwt)


def _pool_mix_out_kernel(u_ref, halo_ref, nsa_ref, x_ref, band_ref, wg_ref, sc_ref, wa_ref, wb_ref, o_ref,
                         *, seq_len):
    tm = u_ref.shape[0]
    hrows = halo_ref.shape[0]
    gd = POOL_GDIM
    tile = pl.program_id(0) % (seq_len // tm)
    u = u_ref[...]
    halo = jnp.where(tile == 0, 0.0, halo_ref[...])
    ext = jnp.concatenate([halo, u], axis=0)
    pos1 = tile * tm + lax.broadcasted_iota(jnp.int32, (tm, 1), 0) + 1
    outs = []
    for gi, w in enumerate(POOL_WINDOWS):
        sl = slice(gi * gd, (gi + 1) * gd)
        win_sum = _dot_exact_lhs(band_ref[gi], ext[:, sl])
        d = win_sum / jnp.minimum(pos1, w).astype(f32) - u[:, sl]
        outs.append(jnp.dot(d.astype(bf16), wg_ref[gi], preferred_element_type=f32))
    pool_out = (jnp.concatenate(outs, axis=1) * sc_ref[...]).astype(bf16)
    dd = lambda p, q: jnp.dot(p, q, preferred_element_type=f32)
    o_ref[...] = x_ref[...] + (dd(pool_out, wa_ref[...]) + dd(nsa_ref[...].astype(bf16), wb_ref[...]))


def pool_mix_out(pool_in, nsa_out, x, pool_w, pool_scale, w_out):
    b, t, d = x.shape
    m = b * t
    tm = PROJ_TM
    hrows = POOL_HIST + 1
    assert t % tm == 0 and tm % hrows == 0 and POOL_GDIM % 128 == 0
    r_i = lax.broadcasted_iota(jnp.int32, (tm, hrows + tm), 0) + hrows
    j_i = lax.broadcasted_iota(jnp.int32, (tm, hrows + tm), 1)
    band = jnp.stack([((j_i <= r_i) & (j_i > r_i - w)).astype(bf16) for w in POOL_WINDOWS])
    wbf = w_out.astype(bf16)
    u2, n2, x2 = pool_in.reshape(m, POOL_DIM), nsa_out.reshape(m, NSA_DIM), x.reshape(m, d)
    tile = lambda c: pl.BlockSpec((tm, c), lambda i: (i, 0))
    full = lambda a: pl.BlockSpec(a.shape, lambda i: (0,) * a.ndim)
    halo = pl.BlockSpec((hrows, POOL_DIM), lambda i: (jnp.maximum(i * (tm // hrows) - 1, 0), 0))
    consts = [band, pool_w.astype(bf16), pool_scale.reshape(1, POOL_DIM).astype(f32), wbf[:POOL_DIM], wbf[POOL_DIM:]]
    out = pl.pallas_call(
        functools.partial(_pool_mix_out_kernel, seq_len=t), grid=(m // tm,),
        in_specs=[tile(POOL_DIM), halo, tile(NSA_DIM), tile(d)] + [full(c) for c in consts],
        out_specs=tile(d), out_shape=jax.ShapeDtypeStruct((m, d), f32),
        compiler_params=pltpu.CompilerParams(dimension_semantics=("parallel",),
                                             vmem_limit_bytes=48 * 1024 * 1024),
        name="pool_mix_out",
    )(u2, u2, n2, x2, *consts)
    return out.reshape(b, t, d)


def ab_features(x, norm_g, pos, w_in, q_norm, k_norm):
    b, t = x.shape[:2]
    u = _norm_mm(x, norm_g, w_in)
    off_kv = POOL_DIM + NSA_DIM
    pool_in = u[..., :POOL_DIM]
    q = u[..., POOL_DIM:off_kv].reshape(b, t, NSA_HEADS, HEAD_DIM)
    kv = u[..., off_kv:off_kv + KV_COLS].reshape(b, t, 6, NSA_KV_HEADS, HEAD_DIM)
    gl = u[..., off_kv + KV_COLS:].reshape(b, t, NSA_HEADS, 3)
    q = rope_partial(rmsnorm(q, q_norm), pos)
    k_slc = rope_partial(rmsnorm(kv[:, :, 2], k_norm[1]), pos)
    k_win = rope_partial(rmsnorm(kv[:, :, 4], k_norm[2]), pos)
    rows = jnp.stack([kv[:, :, 0], kv[:, :, 1], k_slc, kv[:, :, 3]], axis=2)
    win = jnp.stack([k_win, kv[:, :, 5]], axis=2)
    return pool_in, q, gl, rows, win


def compress_kv(k_rows, v_rows, pos_w, phi, k_gain):
    b, length = k_rows.shape[:2]
    n_chunk = length // CMP_STRIDE

    def weighted_block_mean(rows, w):
        ch = rows.reshape(b, n_chunk, CMP_STRIDE, NSA_KV_HEADS, HEAD_DIM)
        return (jnp.einsum('bnlhd,hl->bnhd', ch[:, :-1], w[:, :CMP_STRIDE])
                + jnp.einsum('bnlhd,hl->bnhd', ch[:, 1:], w[:, CMP_STRIDE:]))

    cmp_end = jnp.arange(n_chunk - 1, dtype=jnp.int32) * CMP_STRIDE + (CMP_LEN - 1)
    kc = jnp.einsum('bnhd,de->bnhe', weighted_block_mean(k_rows, pos_w[0]), phi[0])
    kc = rope_partial(rmsnorm(kc, k_gain), cmp_end)
    vc = jnp.einsum('bnhd,de->bnhe', weighted_block_mean(v_rows, pos_w[1]), phi[1])
    return kc, vc, cmp_end


def nsa_attend(q, pos_q, gl, kc, vc, cmp_end, ks, vs, kw, vw, pos_w):
    f32 = jnp.float32
    b, tq = q.shape[:2]
    qg = q.reshape(b, tq, NSA_KV_HEADS, NSA_GQ, HEAD_DIM)
    scale = HEAD_DIM ** -0.5
    s_c = jnp.einsum('bqhgd,bchd->bhgqc', qg, kc) * scale
    p_c = masked_softmax(s_c, cmp_end[None, :] <= pos_q[:, None])
    o_c = jnp.einsum('bhgqc,bchd->bqhgd', p_c, vc.astype(f32))
    imp = p_c.sum(axis=2)
    imp_chunk = 0.5 * (jnp.pad(imp, ((0, 0), (0, 0), (0, 0), (0, 1)))
                       + jnp.pad(imp, ((0, 0), (0, 0), (0, 0), (1, 0))))
    n_slc = ks.shape[1] // SLC_LEN
    imp_blk = imp_chunk.reshape(b, NSA_KV_HEADS, tq, n_slc, SLC_LEN // CMP_STRIDE).sum(-1)
    blk = jnp.arange(n_slc, dtype=jnp.int32)[None, :]
    cur = (pos_q // SLC_LEN)[:, None]
    forced = (blk == 0) | (blk == cur) | (blk == cur - 1)
    score = jnp.where(blk > cur, -1.0, jnp.where(forced, 1e6, imp_blk))
    n_sel = min(N_SEL, n_slc)
    _, idx = lax.top_k(score, n_sel)
    gather = jax.vmap(jax.vmap(lambda rows, i: rows[i]))
    ksb = ks.reshape(b, n_slc, SLC_LEN, NSA_KV_HEADS, HEAD_DIM).transpose(0, 3, 1, 2, 4)
    vsb = vs.reshape(b, n_slc, SLC_LEN, NSA_KV_HEADS, HEAD_DIM).transpose(0, 3, 1, 2, 4)
    kg = gather(ksb, idx)
    vg = gather(vsb, idx)
    kpos = idx[..., None] * SLC_LEN + jnp.arange(SLC_LEN, dtype=jnp.int32)
    n_keys = n_sel * SLC_LEN
    m_s = (kpos <= pos_q[None, None, :, None, None]).reshape(b, NSA_KV_HEADS, 1, tq, n_keys)
    s_s = jnp.einsum('bqhgd,bhqnld->bhgqnl', qg, kg).reshape(b, NSA_KV_HEADS, NSA_GQ, tq, n_keys) * scale
    p_s = masked_softmax(s_s, m_s)
    o_s = jnp.einsum('bhgqk,bhqkd->bqhgd', p_s,
                     vg.reshape(b, NSA_KV_HEADS, tq, n_keys, HEAD_DIM).astype(f32))
    s_w = jnp.einsum('bqhgd,bkhd->bhgqk', qg, kw) * scale
    dq = pos_q[:, None] - pos_w[None, :]
    m_w = (dq >= 0) & (dq < WINDOW) & (pos_w[None, :] >= 0)
    p_w = masked_softmax(s_w, m_w)
    o_w = jnp.einsum('bhgqk,bkhd->bqhgd', p_w, vw.astype(f32))
    g = jax.nn.sigmoid(gl.astype(f32)).reshape(b, tq, NSA_KV_HEADS, NSA_GQ, 3)
    o = g[..., 0:1] * o_c + g[..., 1:2] * o_s + g[..., 2:3] * o_w
    return o.reshape(b, tq, NSA_DIM)


def nsa_prompt(q, gl, rows, win, cmp_pos_w, cmp_phi, k_gain):
    b, t = q.shape[:2]
    kc, vc, cmp_end = compress_kv(rows[:, :, 0], rows[:, :, 1], cmp_pos_w, cmp_phi, k_gain)
    ks, vs = rows[:, :, 2], rows[:, :, 3]
    win_pad = jnp.pad(win, ((0, 0), (WINDOW, 0), (0, 0), (0, 0), (0, 0)))

    def block(i):
        s0 = i * Q_BLOCK
        qb = lax.dynamic_slice_in_dim(q, s0, Q_BLOCK, axis=1)
        gb = lax.dynamic_slice_in_dim(gl, s0, Q_BLOCK, axis=1)
        wb = lax.dynamic_slice_in_dim(win_pad, s0, WINDOW + Q_BLOCK, axis=1)
        pos_q = s0 + jnp.arange(Q_BLOCK, dtype=jnp.int32)
        pos_w = s0 - WINDOW + jnp.arange(WINDOW + Q_BLOCK, dtype=jnp.int32)
        return nsa_attend(qb, pos_q, gb, kc, vc, cmp_end, ks, vs, wb[:, :, 0], wb[:, :, 1], pos_w)

    out = lax.map(block, jnp.arange(t // Q_BLOCK, dtype=jnp.int32))
    return jnp.moveaxis(out, 0, 1).reshape(b, t, NSA_DIM)


def nsa_sample(q, gl, rows_new, win_new, pool_kv, page_table, win_buf, cmp_pos_w, cmp_phi, k_gain):
    bd, ts = q.shape[:2]
    past_len = page_table.shape[1] * pool_kv.shape[1]
    past = pool_kv[page_table].reshape(bd, past_len, 4, NSA_KV_HEADS, HEAD_DIM)
    rows = jnp.concatenate([past, rows_new.astype(past.dtype)], axis=1)
    pad = (-(past_len + ts)) % SLC_LEN
    rows = jnp.pad(rows, ((0, 0), (0, pad), (0, 0), (0, 0), (0, 0)))
    kc, vc, cmp_end = compress_kv(rows[:, :, 0], rows[:, :, 1], cmp_pos_w, cmp_phi, k_gain)
    lb = win_buf.shape[1]
    win = jnp.concatenate([win_buf, win_new.astype(win_buf.dtype)], axis=1)
    pos_q = past_len + jnp.arange(ts, dtype=jnp.int32)
    pos_w = past_len - lb + jnp.arange(lb + ts, dtype=jnp.int32)
    o = nsa_attend(q, pos_q, gl, kc, vc, cmp_end, rows[:, :, 2], rows[:, :, 3],
                   win[:, :, 0], win[:, :, 1], pos_w)
    keep = min(WINDOW, lb + ts)
    return o, win[:, lb + ts - keep:]


def ab_layer_prompt(x, norm_g, w_in, w_out, q_norm, k_norm, cmp_pos_w, cmp_phi, pool_w, pool_scale):
    b, t = x.shape[:2]
    pool_in, rows2, win2, ops = ab_features_prompt(x, norm_g, w_in, q_norm, k_norm)
    rows = rows2.reshape(b, t, 4, NSA_KV_HEADS, HEAD_DIM)
    win = win2.reshape(b, t, 2, NSA_KV_HEADS, HEAD_DIM)
    kc, vc, _ = compress_kv(rows[:, :, 0], rows[:, :, 1], cmp_pos_w, cmp_phi, k_norm[0])
    nsa_out = nsa_prompt_pallas(ops, kc, vc)
    x_new = pool_mix_out(pool_in, nsa_out, x, pool_w, pool_scale, w_out)
    keep = min(WINDOW, t)
    return x_new, rows, win[:, t - keep:], pool_in[:, t - POOL_HIST:]


def ab_layer_sample(x, norm_g, pools, page_table, win_bufs, layer, pool_hist, w_in, w_out, q_norm, k_norm,
                    cmp_pos_w, cmp_phi, pool_w, pool_scale):
    ts = x.shape[1]
    past_len = page_table.shape[1] * pools.shape[2]
    pos = past_len + jnp.arange(ts, dtype=jnp.int32)
    pool_in, q, gl, rows, win = ab_features(x, norm_g, pos, w_in, q_norm, k_norm)
    pool_out = pool_mix(pool_in, pool_hist, past_len, pool_w, pool_scale)
    nsa_out = nsa_sample_pallas(q, gl, rows, win, pools, page_table, win_bufs, layer,
                                cmp_pos_w, cmp_phi, k_norm[0])
    win_buf = win_bufs[layer]
    lb = win_buf.shape[1]
    keep = min(WINDOW, lb + ts)
    new_win = jnp.concatenate([win_buf, win.astype(win_buf.dtype)], axis=1)[:, lb + ts - keep:]
    x_new = _mix_out(pool_out, nsa_out, w_out, x)
    new_hist = jnp.concatenate([pool_hist.astype(pool_in.dtype), pool_in], axis=1)[:, -POOL_HIST:]
    return x_new, rows, new_win, new_hist


def wkv_scan(s0, r, w, k, v, kk, a):
    def step(s, inp):
        r_t, w_t, k_t, v_t, kk_t, a_t = inp
        sa = jnp.einsum('bhij,bhj->bhi', s, -kk_t)
        s = (s * w_t[:, :, None, :] + sa[..., None] * (kk_t * a_t)[:, :, None, :]
             + v_t[..., None] * k_t[:, :, None, :])
        return s, jnp.einsum('bhij,bhj->bhi', s, r_t)

    xs = tuple(jnp.moveaxis(z.astype(jnp.float32), 1, 0) for z in (r, w, k, v, kk, a))
    s, o = lax.scan(step, s0.astype(jnp.float32), xs)
    return s, jnp.moveaxis(o, 0, 1)


RW_TM = 256
HEAD_LANES = 128


def _rwkv_pre_kernel(*refs, seq_len, has_vres):
    it = iter(refs)
    x_ref, xprev_ref, fp_ref, g_ref, mu_ref = [next(it) for _ in range(5)]
    wr_ref, wk_ref, wv_ref = [next(it) for _ in range(3)]
    w0_ref, w1_ref, w2_ref, a0_ref, a1_ref, a2_ref, g1_ref, g2_ref = [next(it) for _ in range(8)]
    kkw_ref, kaw_ref, hsum_ref, hexp_ref = [next(it) for _ in range(4)]
    if has_vres:
        v0_ref, v1_ref, v2_ref, vf_ref = [next(it) for _ in range(4)]
    r_ref, k_ref, v_ref, lw_ref, kk_ref, a_ref, gg_ref = [next(it) for _ in range(7)]
    dd = lambda a, b: jnp.dot(a, b, preferred_element_type=f32)
    tm = x_ref.shape[0]
    norm = lambda z: z * lax.rsqrt(jnp.mean(z * z, axis=-1, keepdims=True) + RMS_EPS) * g_ref[...]
    h = norm(x_ref[...])
    row = lax.broadcasted_iota(jnp.int32, (tm, 1), 0)
    rolled = pltpu.roll(h, 1, axis=0)
    if seq_len % tm == 0:
        first = (pl.program_id(0) % (seq_len // tm)) == 0
        last_prev = norm(xprev_ref[...])[xprev_ref.shape[0] - 1:, :]
        prev = jnp.where(row == 0, jnp.where(first, fp_ref[...], last_prev), rolled)
    else:
        nseq = tm // seq_len
        sel = (lax.broadcasted_iota(jnp.int32, (tm, nseq), 0)
               == seq_len * lax.broadcasted_iota(jnp.int32, (tm, nseq), 1)).astype(bf16)
        prev = jnp.where(row % seq_len == 0, _dot_exact_lhs(sel, fp_ref[...]), rolled)
    xx = prev - h
    mix = lambda j: (h + xx * mu_ref[j:j + 1, :]).astype(bf16)
    xr, xw, xk, xv, xa, xg = [mix(j) for j in range(6)]
    k = dd(xk, wk_ref[...])
    v = dd(xv, wv_ref[...])
    r_ref[...] = dd(xr, wr_ref[...])
    z = -(w0_ref[...] + dd(jnp.tanh(dd(xw, w1_ref[...])).astype(bf16), w2_ref[...]))
    softplus = jnp.maximum(z, 0.0) + jnp.log(1.0 + jnp.exp(-jnp.abs(z)))
    lw_ref[...] = -jnp.exp(-softplus - 0.5)
    a = jax.nn.sigmoid(a0_ref[...] + dd(dd(xa, a1_ref[...]).astype(bf16), a2_ref[...]))
    a_ref[...] = a
    gg_ref[...] = dd(jax.nn.sigmoid(dd(xg, g1_ref[...])).astype(bf16), g2_ref[...])
    if has_vres:
        v = v + (vf_ref[...] - v) * jax.nn.sigmoid(
            v0_ref[...] + dd(dd(xv, v1_ref[...]).astype(bf16), v2_ref[...]))
    v_ref[...] = v
    kk = k * kkw_ref[...]
    nrm = jnp.maximum(jnp.sqrt(_dot_exact_rhs(kk * kk, hsum_ref[...])), 1e-12)
    kk_ref[...] = kk * _dot_exact_rhs(1.0 / nrm, hexp_ref[...])
    k_ref[...] = k * (1.0 + (a - 1.0) * kaw_ref[...])


def _rwkv_post_kernel(o_ref, r_ref, k_ref, v_ref, gg_ref, x_ref, gnw_ref, gnb_ref, rk_ref, hsum_ref,
                      hexp_ref, wo_ref, out_ref):
    hs = hsum_ref[...]
    he = hexp_ref[...]
    head_sum = lambda z: _dot_exact_rhs(_dot_exact_rhs(z, hs), he)
    o = o_ref[...]
    v = v_ref[...]
    d = o - head_sum(o) * (1.0 / RWKV_N)
    var = head_sum(d * d) * (1.0 / RWKV_N)
    on = d * lax.rsqrt(var + GN_EPS) * gnw_ref[...] + gnb_ref[...]
    on = on + head_sum(r_ref[...] * k_ref[...] * rk_ref[...]) * v
    y = (on * gg_ref[...]).astype(bf16)
    out_ref[...] = x_ref[...] + jnp.dot(y, wo_ref[...], preferred_element_type=f32)


def rwkv_layer_fused(x, norm_g, shift_prev, s0, v_first, vres, mu, wr, wk, wv, wo, w0, w1, w2, a0, a1, a2,
                     g1, g2, k_k, k_a, r_k, gn_w, gn_b):
    n, t, d = x.shape
    m = n * t
    tm = RW_TM
    assert m % tm == 0 and (t % tm == 0 or tm % t == 0)
    x2 = x.reshape(m, d)
    row = lambda z: z.reshape(1, d).astype(f32)
    cb = lambda z: z.astype(bf16)
    hd = lax.broadcasted_iota(jnp.int32, (d, HEAD_LANES), 0) // RWKV_N
    hsum = (hd == lax.broadcasted_iota(jnp.int32, (d, HEAD_LANES), 1)).astype(bf16)
    hexp = hsum.T
    tile = pl.BlockSpec((tm, d), lambda i: (i, 0))
    full = lambda z: pl.BlockSpec(z.shape, lambda i: (0,) * z.ndim)
    if t % tm == 0:
        tps = t // tm
        fp = shift_prev.reshape(n, 1, d).astype(f32)
        fp_spec = pl.BlockSpec((None, 1, d), lambda i: (i // tps, 0, 0))
    else:
        fp = shift_prev.astype(f32)
        fp_spec = pl.BlockSpec((tm // t, d), lambda i: (i, 0))
    xprev_spec = pl.BlockSpec((8, d), lambda i: (jnp.maximum(i * (tm // 8) - 1, 0), 0))
    mu8 = jnp.pad(mu.astype(f32), ((0, 2), (0, 0)))
    consts = [row(norm_g), mu8, cb(wr), cb(wk), cb(wv), row(w0), cb(w1), cb(w2), row(a0), cb(a1), cb(a2),
              cb(g1), cb(g2), row(k_k), row(k_a), hsum, hexp]
    args = [x2, x2, fp] + consts
    specs = [tile, xprev_spec, fp_spec] + [full(c) for c in consts]
    if vres is not None:
        v0, v1, v2 = vres
        extra = [row(v0), cb(v1), cb(v2)]
        args += extra + [v_first.reshape(m, d)]
        specs += [full(c) for c in extra] + [tile]
    cp = pltpu.CompilerParams(dimension_semantics=("parallel",), vmem_limit_bytes=56 * 1024 * 1024)
    r, k, v, lw, kk, a, gg = pl.pallas_call(
        functools.partial(_rwkv_pre_kernel, seq_len=t, has_vres=vres is not None),
        grid=(m // tm,), in_specs=specs, out_specs=[tile] * 7,
        out_shape=[jax.ShapeDtypeStruct((m, d), f32)] * 7, compiler_params=cp, name="rwkv_pre",
    )(*args)
    if vres is None:
        v_first = v.reshape(n, t, d)
    seq = lambda z: z.reshape(n, t, d)
    o, s = wkv_chunked(seq(r), seq(lw), seq(k), seq(v), seq(kk), seq(a), s0.astype(f32))
    o = o.reshape(m, d)
    post_consts = [row(gn_w), row(gn_b), r_k.reshape(1, d).astype(f32), hsum, hexp, cb(wo)]
    x_new = pl.pallas_call(
        _rwkv_post_kernel, grid=(m // tm,),
        in_specs=[tile] * 6 + [full(c) for c in post_consts], out_specs=tile,
        out_shape=jax.ShapeDtypeStruct((m, d), f32), compiler_params=cp, name="rwkv_post",
    )(o, r, k, v, gg, x2, *post_consts)
    h_last = rmsnorm(x[:, -1], norm_g)
    return x_new.reshape(n, t, d), v_first, s, h_last


def rwkv_layer(h, shift_prev, s0, v_first, vres, mu, wr, wk, wv, wo, w0, w1, w2, a0, a1, a2,
               g1, g2, k_k, k_a, r_k, gn_w, gn_b):
    f32 = jnp.float32
    b, t, d = h.shape
    prev = jnp.concatenate([shift_prev[:, None, :].astype(h.dtype), h[:, :-1]], axis=1)
    xx = prev - h
    xr, xw, xk, xv, xa, xg = [h + xx * mu[j] for j in range(6)]
    r = _mm(xr, wr)
    k = _mm(xk, wk)
    v = _mm(xv, wv)
    w_log = -jax.nn.softplus(-(w0 + jnp.tanh(xw @ w1) @ w2).astype(f32)) - 0.5
    decay = jnp.exp(-jnp.exp(w_log))
    if vres is None:
        v_first = v
    else:
        v0, v1, v2 = vres
        v = v + (v_first - v) * jax.nn.sigmoid(v0 + (xv @ v1) @ v2)
    a = jax.nn.sigmoid((a0 + (xa @ a1) @ a2).astype(f32))
    g = jax.nn.sigmoid(xg @ g1) @ g2

    def heads(z):
        return z.reshape(b, t, RWKV_HEADS, RWKV_N).astype(f32)

    kk = heads(k * k_k)
    kk = kk / jnp.maximum(jnp.sqrt(jnp.sum(kk * kk, axis=-1, keepdims=True)), 1e-12)
    k = k.astype(f32) * (1.0 + (a - 1.0) * k_a.astype(f32))
    rh, kh, vh, ah, dh = heads(r), heads(k), heads(v), heads(a), heads(decay)
    tp = -(-t // WKV_CHUNK) * WKV_CHUNK
    padt = lambda z: jnp.pad(z, ((0, 0), (0, tp - t), (0, 0)))
    o, s = wkv_chunked(padt(r), padt(-jnp.exp(w_log)), padt(k), padt(v), padt(kk.reshape(b, t, d)),
                       padt(a), s0.astype(f32))
    o = o[:, :t].reshape(b, t, RWKV_HEADS, RWKV_N)
    mean = jnp.mean(o, axis=-1, keepdims=True)
    var = jnp.mean(jnp.square(o - mean), axis=-1, keepdims=True)
    o = ((o - mean) * lax.rsqrt(var + GN_EPS) * gn_w.reshape(RWKV_HEADS, RWKV_N).astype(f32)
         + gn_b.reshape(RWKV_HEADS, RWKV_N).astype(f32))
    o = o + jnp.sum(rh * kh * r_k.astype(f32), axis=-1, keepdims=True) * vh
    y = _mm((o.reshape(b, t, d) * g.astype(f32)).astype(h.dtype), wo)
    return y, v_first, s, h[:, -1]


def hier_moe(h, wc, bc, wf, bf, wg, wu, wd):
    f32 = jnp.float32
    hp = lax.Precision.HIGHEST
    lc = jnp.dot(h, wc, precision=hp).astype(f32) + bc.astype(f32)
    g_idx = jnp.argmax(lc, axis=-1)
    g_w = jnp.max(jax.nn.softmax(lc, axis=-1), axis=-1)
    g_hot = jax.nn.one_hot(g_idx, MOE_GROUPS, dtype=f32)
    lf = (jnp.dot(h, wf, precision=hp).astype(f32) + bf.astype(f32)).reshape(h.shape[:-1] + (MOE_GROUPS, MOE_EPG))
    lf_sel = jnp.einsum('btge,btg->bte', lf, g_hot)
    top_v, top_i = lax.top_k(lf_sel, MOE_TOPK)
    top_w = jax.nn.softmax(top_v, axis=-1) * g_w[..., None]
    e_id = g_idx[..., None] * MOE_EPG + top_i
    gate = jnp.einsum('btke,btk->bte', jax.nn.one_hot(e_id, N_EXPERTS, dtype=f32), top_w)
    hg = jnp.einsum('btd,edf->btef', h, wg)
    hu = jnp.einsum('btd,edf->btef', h, wu)
    act = (jax.nn.silu(hg) * hu * gate[..., None].astype(h.dtype)).astype(h.dtype)
    return jnp.einsum('btef,efd->btd', act, wd)


def kernel(x_prompt, x_sample, cache_nsa_kv, cache_win_kv, state_pool, state_wkv, state_shift,
           page_table, norm_mix, norm_ffn, ab_w_in, ab_w_out, ab_q_norm, ab_k_norm, cmp_pos_w,
           cmp_phi, pool_w, pool_scale, rw_mu, rw_wr, rw_wk, rw_wv, rw_wo, rw_w0, rw_w1, rw_w2,
           rw_a0, rw_a1, rw_a2, rw_v0, rw_v1, rw_v2, rw_g1, rw_g2, rw_kk, rw_ka, rw_rk, rw_gn_w,
           rw_gn_b, moe_wc, moe_bc, moe_wf, moe_bf, moe_wg, moe_wu, moe_wd):
    xp, xs = x_prompt, x_sample
    vf_p, vf_s = None, None
    nsa_p, nsa_s, win_p, win_s, pool_p, pool_s = [], [], [], [], [], []
    wkv_p, wkv_s, sh_p, sh_s = [], [], [], []
    for l in range(DEPTH):
        if l % 2 == 0:
            i = l // 2
            wts = (ab_w_in[i], ab_w_out[i], ab_q_norm[i], ab_k_norm[i], cmp_pos_w[i], cmp_phi[i],
                   pool_w[i], pool_scale[i])
            xp, r_p, w_p, h_p = ab_layer_prompt(xp, norm_mix[l], *wts)
            xs, r_s, w_s, h_s = ab_layer_sample(xs, norm_mix[l], cache_nsa_kv, page_table,
                                                cache_win_kv, i, state_pool[i], *wts)
            nsa_p.append(r_p)
            nsa_s.append(r_s)
            win_p.append(w_p)
            win_s.append(w_s)
            pool_p.append(h_p)
            pool_s.append(h_s)
        else:
            j = l // 2
            vres = None if j == 0 else (rw_v0[j - 1], rw_v1[j - 1], rw_v2[j - 1])
            wts = (rw_mu[j], rw_wr[j], rw_wk[j], rw_wv[j], rw_wo[j], rw_w0[j], rw_w1[j], rw_w2[j],
                   rw_a0[j], rw_a1[j], rw_a2[j], rw_g1[j], rw_g2[j], rw_kk[j], rw_ka[j], rw_rk[j],
                   rw_gn_w[j], rw_gn_b[j])
            bp = xp.shape[0]
            zero_shift = jnp.zeros((bp, D_MODEL), xp.dtype)
            zero_state = jnp.zeros((bp, RWKV_HEADS, RWKV_N, RWKV_N), jnp.float32)
            xp, vf_p, s_p, shp = rwkv_layer_fused(xp, norm_mix[l], zero_shift, zero_state, vf_p, vres, *wts)
            xs, vf_s, s_s, shs = rwkv_layer_fused(xs, norm_mix[l], state_shift[j], state_wkv[j], vf_s, vres, *wts)
            wkv_p.append(s_p)
            wkv_s.append(s_s)
            sh_p.append(shp)
            sh_s.append(shs)
        prep =moe_prep(norm_ffn[l], moe_wc[l], moe_bc[l], moe_wf[l], moe_bf[l], moe_wg[l], moe_wu[l], moe_wd[l])
        xp = moe_residual(xp, prep)
        xs = moe_residual(xs, prep)
    return (xp, xs, jnp.stack(nsa_p), jnp.stack(nsa_s), jnp.stack(win_p), jnp.stack(win_s),
            jnp.stack(pool_p), jnp.stack(pool_s), jnp.stack(wkv_p), jnp.stack(wkv_s),
            jnp.stack(sh_p), jnp.stack(sh_s))
```

```python
import functools

import jax
import jax.numpy as jnp
from jax import lax
from jax.experimental import pallas as pl
from jax.experimental.pallas import tpu as pltpu


f32 = jnp.float32
bf16 = jnp.bfloat16
WKV_CHUNK = 64
WKV_PAIRS = 8
WKV_SEQS = 2
WKV_PASSES = 1
WKV_GRAM_PASSES = 1


def _split(x):
    hi = x.astype(bf16)
    lo = (x - hi.astype(f32)).astype(bf16)
    return hi, lo


def _mmul(a, b, passes, nt=False):
    dn = (((1,), (1,)), ((), ())) if nt else (((1,), (0,)), ((), ()))
    d = lambda x, y: lax.dot_general(x, y, dn, preferred_element_type=f32)
    if passes == 1:
        return d(a.astype(bf16), b.astype(bf16))
    ah, al = _split(a)
    bh, bl = _split(b)
    return d(ah, bh) + (d(ah, bl) + d(al, bh))


def _wkv_kernel(r_ref, lw_ref, k_ref, v_ref, kk_ref, a_ref, s0_ref, o_ref, sT_ref, st_scr, *, passes):
    C = WKV_CHUNK
    nb = r_ref.shape[0]
    c = pl.program_id(1)
    nc = pl.num_programs(1)
    row = lax.broadcasted_iota(jnp.int32, (2 * C, 2 * C), 0)
    col = lax.broadcasted_iota(jnp.int32, (2 * C, 2 * C), 1)
    bd = (row < C) == (col < C)
    strict = bd & ((row % C) > (col % C))
    incl = bd & ((row % C) >= (col % C))
    eye = (row == col).astype(f32)
    lane_s = col < C
    m1 = lax.broadcasted_iota(jnp.int32, (C, 2 * C), 1) < C
    tri = (lax.broadcasted_iota(jnp.int32, (C, C), 0)
           >= lax.broadcasted_iota(jnp.int32, (C, C), 1)).astype(bf16)

    @pl.when(c == 0)
    def _():
        z = jnp.zeros((C, C), f32)
        for p in range(nb * WKV_PAIRS):
            s1 = s0_ref[p // WKV_PAIRS, 2 * (p % WKV_PAIRS)]
            s2 = s0_ref[p // WKV_PAIRS, 2 * (p % WKV_PAIRS) + 1]
            st_scr[p] =jnp.concatenate([jnp.concatenate([s1, z], axis=1),
                                         jnp.concatenate([z, s2], axis=1)], axis=0)

    def stack2(x):
        return jnp.concatenate([jnp.where(m1, x, 0.0), jnp.where(m1, 0.0, x)], axis=0)

    dd = lambda x, y: jnp.dot(x, y, preferred_element_type=f32)
    pairs = range(nb * WKV_PAIRS)
    sq = [p // WKV_PAIRS for p in pairs]
    sls = [slice((p % WKV_PAIRS) * 2 * C, (p % WKV_PAIRS + 1) * 2 * C) for p in pairs]
    tv = r_ref.shape[1]

    def ld(ref, p):
        x = ref[sq[p], :, sls[p]]
        return x if tv == C else jnp.concatenate([x, jnp.zeros((C - tv, 2 * C), f32)], axis=0)

    def prep(p):
        sl = sls[p]
        lw = ld(lw_ref, p)
        kk = ld(kk_ref, p)
        h1 = lw.astype(bf16)
        r1 = lw - h1.astype(f32)
        h2 = r1.astype(bf16)
        h3 = (r1 - h2.astype(f32)).astype(bf16)
        cw = dd(tri, h1) + (dd(tri, h2) + dd(tri, h3))
        cwC = cw[C - 1:C, :]
        b = kk * ld(a_ref, p)
        k = ld(k_ref, p)
        At = -kk * jnp.exp(cw - lw)
        Rt = ld(r_ref, p) * jnp.exp(cw)
        einv = jnp.exp(-cw)
        efut = jnp.exp(cwC - cw)
        X = jnp.concatenate([stack2(At), stack2(Rt)], axis=0)
        Y = jnp.concatenate([b * einv, k * einv], axis=0)
        AR = jnp.concatenate([At, Rt], axis=0)
        BK = jnp.concatenate([b * efut, k * efut], axis=0)
        return X, Y, AR, BK, jnp.exp(cwC)

    pre = [prep(p) for p in pairs]
    G = [_mmul(pre[p][0], pre[p][1], WKV_GRAM_PASSES, nt=True) for p in pairs]
    ARS = [_mmul(pre[p][2], st_scr[p], passes, nt=True) for p in pairs]
    L, Mak, Mrb, Mrk = [], [], [], []
    for p in pairs:
        GA = G[p][0:2 * C]
        GR = G[p][2 * C:4 * C]
        GAr = pltpu.roll(GA, C, axis=1)
        GRr = pltpu.roll(GR, C, axis=1)
        L.append(jnp.where(strict, jnp.where(lane_s, GA, GAr), 0.0))
        Mak.append(jnp.where(strict, jnp.where(lane_s, GAr, GA), 0.0))
        Mrb.append(jnp.where(incl, jnp.where(lane_s, GR, GRr), 0.0))
        Mrk.append(jnp.where(incl, jnp.where(lane_s, GRr, GR), 0.0))
    V = [ld(v_ref, p) for p in pairs]
    Vs = [stack2(V[p]) for p in pairs]
    Xs = [stack2(ARS[p][0:C]) + _mmul(Mak[p], Vs[p], passes) for p in pairs]
    OV = [_mmul(Mrk[p], Vs[p], passes) for p in pairs]
    P = [eye + L[p] for p in pairs]
    Q = L
    for _ in range(5):
        Q = [_mmul(Q[p], Q[p], passes) for p in pairs]
        P = [P[p] + _mmul(Q[p], P[p], passes) for p in pairs]
    Us = [_mmul(P[p], Xs[p], passes) for p in pairs]
    Os = [_mmul(Mrb[p], Us[p], passes) + OV[p] for p in pairs]
    for p in pairs:
        o_ref[sq[p], :, sls[p]] = (ARS[p][C:2 * C] + Os[p][0:C] + Os[p][C:2 * C])[0:tv]
    for p in pairs:
        U = Us[p][0:C] + Us[p][C:2 * C]
        UV = jnp.concatenate([U, V[p]], axis=0)
        dS = _mmul(UV.T, pre[p][3], passes)
        st_scr[p] = st_scr[p] * pre[p][4] + jnp.where(bd, dS, 0.0)

    @pl.when(c == nc - 1)
    def _():
        for p in pairs:
            Snew = st_scr[p]
            sT_ref[sq[p], 2 * (p % WKV_PAIRS)] = Snew[0:C, 0:C]
            sT_ref[sq[p], 2 * (p % WKV_PAIRS) + 1] = Snew[C:2 * C, C:2 * C]


def wkv_chunked(r, lw, k, v, kk, a, s0):
    B, T, D = r.shape
    H = D // 64
    C = WKV_CHUNK
    nb = WKV_SEQS
    tb = min(T, C)
    assert (T % C == 0 or T < C) and tb % 8 == 0 and D == WKV_PAIRS * 2 * C and B % nb == 0
    blk = pl.BlockSpec((nb, tb, D), lambda b, c: (b, c, 0))
    sblk = pl.BlockSpec((nb, H, 64, 64), lambda b, c: (b, 0, 0, 0))
    return pl.pallas_call(
        functools.partial(_wkv_kernel, passes=WKV_PASSES),
        grid=(B // nb, T // tb),
        in_specs=[blk] * 6 + [sblk],
        out_specs=[blk, sblk],
        out_shape=[jax.ShapeDtypeStruct((B, T, D), f32), jax.ShapeDtypeStruct((B, H, 64, 64), f32)],
        scratch_shapes=[pltpu.VMEM((nb * WKV_PAIRS, 2 * C, 2 * C), f32)],
        compiler_params=pltpu.CompilerParams(dimension_semantics=("parallel", "arbitrary")),
        name="wkv7_chunked",
    )(r, lw, k, v, kk, a, s0)


D_MODEL = 1024
BATCH = 4
SEQ = 4096
DEPTH = 4
DEC_BATCH = 128
DEC_SEQ = 8
PAST_LEN = 2048
PAGE_SIZE = 128

N_NSA_LAYERS = (DEPTH + 1) // 2
N_RWKV_LAYERS = DEPTH // 2
N_VRES = N_RWKV_LAYERS - 1

POOL_DIM = D_MODEL // 2
POOL_WINDOWS = (2, 4, 8, 16)
POOL_GROUPS = len(POOL_WINDOWS)
POOL_GDIM = POOL_DIM // POOL_GROUPS
POOL_HIST = max(POOL_WINDOWS) - 1

HEAD_DIM = 64
NSA_HEADS = (D_MODEL // 2) // HEAD_DIM
NSA_KV_HEADS = 2
NSA_GQ = NSA_HEADS // NSA_KV_HEADS
NSA_DIM = NSA_HEADS * HEAD_DIM
CMP_STRIDE = 16
CMP_LEN = 2 * CMP_STRIDE
SLC_LEN = 64
N_SEL = 16
WINDOW = 512
Q_BLOCK = 128
ROPE_DIM = HEAD_DIM // 4
ROPE_THETA = 500000.0
MIX_DIM = POOL_DIM + NSA_DIM
KV_COLS = 6 * NSA_KV_HEADS * HEAD_DIM
IN_COLS = POOL_DIM + NSA_DIM + KV_COLS + 3 * NSA_HEADS

RWKV_N = 64
RWKV_HEADS = D_MODEL // RWKV_N
LORA_W = 64
LORA_A = 64
LORA_V = 32
LORA_G = 128
GN_EPS = 64e-5

MOE_GROUPS = 4
MOE_EPG = 4
N_EXPERTS = MOE_GROUPS * MOE_EPG
MOE_TOPK = 2
D_FF_E = 256

RMS_EPS = 1e-6
NEG_INF = -1e30
RES_SCALE = (2 * DEPTH) ** -0.5

SEL_TK = 512


def _nsa_prompt_kernel(q_ref, ql_ref, g_ref, kc_ref, kcl_ref, vct_ref, ks_ref, vst_ref, kw_ref, vwt_ref,
                       o_ref, score_scr, sel_scr, *, n_cmp):
    QB = Q_BLOCK
    GQ = NSA_GQ * QB
    i = pl.program_id(2)
    s0 = i * QB
    qT = q_ref[...]
    posq = s0 + lax.broadcasted_iota(jnp.int32, (1, GQ), 1) % QB
    dd = lambda x, y: jnp.dot(x, y, preferred_element_type=f32)

    ncp = kc_ref.shape[0]
    cidx = lax.broadcasted_iota(jnp.int32, (ncp, 1), 0)
    mask_c = cidx * CMP_STRIDE + (CMP_LEN - 1) <= posq
    sc = dd(kc_ref[...], qT) + (dd(kc_ref[...], ql_ref[...]) + dd(kcl_ref[...], qT))
    sc = jnp.where(mask_c, sc, NEG_INF)
    pe = jnp.exp(sc - jnp.max(sc, axis=0, keepdims=True))
    pc = jnp.where(mask_c, pe / jnp.sum(pe, axis=0, keepdims=True), 0.0)
    o_c = dd(vct_ref[...], pc.astype(bf16))

    imp = (pc[:, 0:QB] + pc[:, QB:2 * QB]) + (pc[:, 2 * QB:3 * QB] + pc[:, 3 * QB:4 * QB])
    n_slc = score_scr.shape[0]
    per = SLC_LEN // CMP_STRIDE
    nn = lax.broadcasted_iota(jnp.int32, (n_slc, ncp), 0) * per
    cc = lax.broadcasted_iota(jnp.int32, (n_slc, ncp), 1)
    mt = (0.5 * ((cc >= nn) & (cc < nn + per)).astype(f32)
          + 0.5 * ((cc + 1 >= nn) & (cc + 1 < nn + per)).astype(f32)).astype(bf16)
    i1 = imp.astype(bf16)
    r1 = imp - i1.astype(f32)
    i2 = r1.astype(bf16)
    i3 = (r1 - i2.astype(f32)).astype(bf16)
    imp_blk = dd(mt, i1) + (dd(mt, i2) + dd(mt, i3))
    nidx = lax.broadcasted_iota(jnp.int32, (n_slc, 1), 0)
    cur = (s0 + lax.broadcasted_iota(jnp.int32, (1, QB), 1)) // SLC_LEN
    forced = (nidx == 0) | (nidx == cur) | (nidx == cur - 1)
    score = jnp.where(nidx > cur, -1.0, jnp.where(forced, 1e6, imp_blk))
    score_scr[...] = score
    rank = jnp.zeros((n_slc, QB), jnp.int32)
    for m in range(n_slc):
        sm = score_scr[m:m + 1, :]
        tie = (nidx > m).astype(jnp.int32)
        rank = rank + jnp.where(sm > score, 1, jnp.where(sm == score, tie, 0))
    bias = jnp.where(rank < min(N_SEL, n_slc), 0.0, NEG_INF)
    sel_scr[...] = jnp.concatenate([bias] * NSA_GQ, axis=1)

    def online(carry, s, vt_blk):
        m, l, acc = carry
        m_new = jnp.maximum(m, jnp.max(s, axis=0, keepdims=True))
        alpha = jnp.exp(m - m_new)
        p = jnp.exp(s - m_new)
        l = alpha * l + jnp.sum(p, axis=0, keepdims=True)
        acc = alpha * acc + dd(vt_blk, p.astype(bf16))
        return m_new, l, acc

    init = (jnp.full((1, GQ), NEG_INF, f32), jnp.zeros((1, GQ), f32), jnp.zeros((HEAD_DIM, GQ), f32))

    bpt = SEL_TK // SLC_LEN

    def sel_scores(kt):
        k0 = pl.multiple_of(kt * SEL_TK, SEL_TK)
        rows = sel_scr[pl.ds(pl.multiple_of(kt * bpt, bpt), bpt), :]
        blk = jnp.concatenate([jnp.broadcast_to(rows[j:j + 1, :], (SLC_LEN, GQ)) for j in range(bpt)], axis=0)
        return k0, dd(ks_ref[pl.ds(k0, SEL_TK), :], qT) + blk

    def sel_body(kt, carry):
        k0, s = sel_scores(kt)
        return online(carry, s, vst_ref[:, pl.ds(k0, SEL_TK)])

    def sel_body2(kp, carry):
        k0a, sa = sel_scores(2 * kp)
        k0b, sb = sel_scores(2 * kp + 1)
        carry = online(carry, sa, vst_ref[:, pl.ds(k0a, SEL_TK)])
        return online(carry, sb, vst_ref[:, pl.ds(k0b, SEL_TK)])

    n_full = s0 // SEL_TK
    carry = lax.fori_loop(0, n_full // 2, sel_body2, init)
    carry = lax.fori_loop(2 * (n_full // 2), n_full, sel_body, carry)
    k0, s = sel_scores(n_full)
    kpos = k0 + lax.broadcasted_iota(jnp.int32, (SEL_TK, 1), 0)
    _, l_s, acc_s = online(carry, jnp.where(kpos <= posq, s, NEG_INF), vst_ref[:, pl.ds(k0, SEL_TK)])

    nwt = WINDOW // QB
    tiles = []
    for j in range(nwt + 1):
        k0 = s0 - WINDOW + j * QB
        k0c = pl.multiple_of(jnp.maximum(k0, 0), QB)
        s = dd(kw_ref[pl.ds(k0c, QB), :], qT)
        kpos = k0c + lax.broadcasted_iota(jnp.int32, (QB, 1), 0)
        if j == nwt:
            s = jnp.where(kpos <= posq, s, NEG_INF)
        else:
            if j == 0:
                s = jnp.where(posq - kpos < WINDOW, s, NEG_INF)
            s = s + jnp.where(k0 >= 0, 0.0, NEG_INF)
        tiles.append((k0c, s))
    m_w = functools.reduce(jnp.maximum, [jnp.max(s, axis=0, keepdims=True) for _, s in tiles])
    l_w = jnp.zeros((1, GQ), f32)
    acc_w = jnp.zeros((HEAD_DIM, GQ), f32)
    for k0c, s in tiles:
        p = jnp.exp(s - m_w)
        l_w = l_w + jnp.sum(p, axis=0, keepdims=True)
        acc_w = acc_w + dd(vwt_ref[:, pl.ds(k0c, QB)], p.astype(bf16))

    g = jax.nn.sigmoid(g_ref[...])
    o = g[0:1] * o_c + g[1:2] * (acc_s / l_s) + g[2:3] * (acc_w / l_w)
    o_ref[...] = jnp.concatenate([o[:, j * QB:(j + 1) * QB].T for j in range(NSA_GQ)], axis=1)


def nsa_prompt_pallas(ops, kc, vc):
    qT, qTl, gT, ks, vst, kw, vwt = ops
    B, KVH, T, D = ks.shape
    G, QB = NSA_GQ, Q_BLOCK
    assert T % SEL_TK == 0 and T % QB == 0
    nqb = T // QB
    n_cmp = kc.shape[1]
    ncp = -(-n_cmp // 128) * 128
    n_slc = T // SLC_LEN
    kcp, kcl = _split(jnp.pad(kc, ((0, 0), (0, ncp - n_cmp), (0, 0), (0, 0))).transpose(0, 2, 1, 3))
    vct = jnp.pad(vc, ((0, 0), (0, ncp - n_cmp), (0, 0), (0, 0))).transpose(0, 2, 3, 1).astype(bf16)
    bh = lambda *shape: pl.BlockSpec((None, None) + shape, lambda b, h, i: (b, h) + (0,) * len(shape))
    bhi = lambda *shape: pl.BlockSpec((None, None, None) + shape, lambda b, h, i: (b, h, i) + (0,) * len(shape))
    return pl.pallas_call(
        functools.partial(_nsa_prompt_kernel, n_cmp=n_cmp),
        grid=(B, KVH, nqb),
        in_specs=[bhi(D, G * QB), bhi(D, G * QB), bhi(3, G * QB), bh(ncp, D), bh(ncp, D), bh(D, ncp),
                  bh(T, D), bh(D, T), bh(T, D), bh(D, T)],
        out_specs=pl.BlockSpec((None, QB, G * D), lambda b, h, i: (b, i, h)),
        out_shape=jax.ShapeDtypeStruct((B, T, KVH * G * D), f32),
        scratch_shapes=[pltpu.VMEM((n_slc, QB), f32), pltpu.VMEM((n_slc, G * QB), f32)],
        compiler_params=pltpu.CompilerParams(dimension_semantics=("parallel", "parallel", "arbitrary"),
                                             vmem_limit_bytes=48 * 1024 * 1024),
        name="nsa_prompt",
    )(qT, qTl, gT, kcp, kcl, vct, ks, vst, kw, vwt)


def _split3(x):
    h1 = x.astype(bf16)
    r1 = x - h1.astype(f32)
    h2 = r1.astype(bf16)
    return h1, h2, (r1 - h2.astype(f32)).astype(bf16)


def _dot_exact_rhs(x, m):
    d = lambda a: jnp.dot(a, m, preferred_element_type=f32)
    h1, h2, h3 = _split3(x)
    return d(h1) + (d(h2) + d(h3))


def _dot_exact_lhs(m, x):
    d = lambda a: jnp.dot(m, a, preferred_element_type=f32)
    h1, h2, h3 = _split3(x)
    return d(h1) + (d(h2) + d(h3))


def _nsa_sample_kernel(pt_ref, *refs, n_pages, ts, past_len):
    pages = refs[:n_pages]
    (new_ref, wbuf_ref, wnew_ref, qh_ref, ql_ref, gate_ref, wa_ref, wb_ref, phik_ref, phiv_ref,
     gain_ref, cos_ref, sin_ref, o_ref, ssel_scr, a_scr, b_scr, score_scr) = refs[n_pages:]
    P = PAGE_SIZE
    KV = NSA_KV_HEADS * HEAD_DIM
    NCOL = NSA_KV_HEADS * NSA_GQ * ts
    NQ = NSA_KV_HEADS * ts
    cpp = P // CMP_STRIDE
    n_chunk = (past_len + SLC_LEN) // CMP_STRIDE
    n_cmp = n_chunk - 1
    ncp = a_scr.shape[0]
    n_slc = (past_len + SLC_LEN) // SLC_LEN
    nsp = score_scr.shape[0]
    dd = lambda x, y: jnp.dot(x, y, preferred_element_type=f32)
    qh = qh_ref[...]
    ql = ql_ref[...]
    col = lax.broadcasted_iota(jnp.int32, (1, NCOL), 1)
    t_col = col % ts
    zpad = jnp.zeros((P - ts, 4 * KV), f32)
    new_tile = jnp.concatenate([new_ref[...], zpad], axis=0)

    wa = wa_ref[...]
    wb = wb_ref[...]
    a_scr[...] = jnp.zeros(a_scr.shape, f32)
    b_scr[...] = jnp.zeros(b_scr.shape, f32)
    def slab(j, kind):
        if j == n_pages:
            return new_tile[:, kind * KV:(kind + 1) * KV]
        return pages[j][:, kind * KV:(kind + 1) * KV]

    for j in range(n_pages + 1):
        xc = jnp.concatenate([slab(j, 0), slab(j, 1)], axis=1)
        a_scr[j * cpp:(j + 1) * cpp, :] = (xc * wa).reshape(cpp, CMP_STRIDE, 2 * KV).sum(axis=1)
        b_scr[j * cpp:(j + 1) * cpp, :] = (xc * wb).reshape(cpp, CMP_STRIDE, 2 * KV).sum(axis=1)
        ssel_scr[j * P:(j + 1) * P, :] = dd(slab(j, 2).astype(bf16), qh)

    mean = a_scr[...] + pltpu.roll(b_scr[...], ncp - 1, axis=0)
    kc = _mmul(mean[:, 0:KV], phik_ref[...], 3)
    vc = _mmul(mean[:, KV:2 * KV], phiv_ref[...], 3)
    r_i = lax.broadcasted_iota(jnp.int32, (KV, KV), 0)
    c_i = lax.broadcasted_iota(jnp.int32, (KV, KV), 1)
    same_head = (r_i // HEAD_DIM) == (c_i // HEAD_DIM)
    mavg = jnp.where(same_head, 1.0 / HEAD_DIM, 0.0).astype(bf16)
    kc = kc * lax.rsqrt(_dot_exact_rhs(kc * kc, mavg) + RMS_EPS) * gain_ref[...]
    half = ROPE_DIM // 2
    rd, cd = r_i % HEAD_DIM, c_i % HEAD_DIM
    rot = jnp.where(same_head & (cd < half) & (rd == cd + half), -1.0,
                    jnp.where(same_head & (cd >= half) & (cd < ROPE_DIM) & (rd == cd - half), 1.0, 0.0)).astype(bf16)
    kc = kc * cos_ref[...] + _dot_exact_rhs(kc, rot) * sin_ref[...]

    kch, kcl = _split(kc)
    sc = dd(kch, qh) + (dd(kch, ql) + dd(kcl, qh))
    cidx = lax.broadcasted_iota(jnp.int32, (ncp, 1), 0)
    mask_c = (cidx * CMP_STRIDE + (CMP_LEN - 1) <= past_len + t_col) & (cidx < n_cmp)
    sc = jnp.where(mask_c, sc, NEG_INF)
    pe = jnp.exp(sc - jnp.max(sc, axis=0, keepdims=True))
    pc = jnp.where(mask_c, pe / jnp.sum(pe, axis=0, keepdims=True), 0.0)
    o_c = dd(pc.T.astype(bf16), vc.astype(bf16))

    gr = lax.broadcasted_iota(jnp.int32, (NCOL, NQ), 0)
    gc = lax.broadcasted_iota(jnp.int32, (NCOL, NQ), 1)
    gsum = ((gr // (NSA_GQ * ts) == gc // ts) & (gr % ts == gc % ts)).astype(bf16)
    imp = _dot_exact_rhs(pc, gsum)
    per = SLC_LEN // CMP_STRIDE
    nn = lax.broadcasted_iota(jnp.int32, (nsp, ncp), 0) * per
    cc = lax.broadcasted_iota(jnp.int32, (nsp, ncp), 1)
    mt = (0.5 * ((cc >= nn) & (cc < nn + per)).astype(f32)
          + 0.5 * ((cc + 1 >= nn) & (cc + 1 < nn + per)).astype(f32)).astype(bf16)
    imp_blk = _dot_exact_lhs(mt, imp)
    nidx = lax.broadcasted_iota(jnp.int32, (nsp, 1), 0)
    cur = (past_len + lax.broadcasted_iota(jnp.int32, (1, NQ), 1) % ts) // SLC_LEN
    forced = (nidx == 0) | (nidx == cur) | (nidx == cur - 1)
    score = jnp.where(nidx >= n_slc, -2.0, jnp.where(nidx > cur, -1.0, jnp.where(forced, 1e6, imp_blk)))
    score_scr[...] = score
    rank = jnp.zeros((nsp, NQ), jnp.int32)
    for m in range(n_slc):
        sm = score_scr[m:m + 1, :]
        beats = (sm > score) | ((sm == score) & (nidx > m))
        rank = rank + beats.astype(jnp.int32)
    sel = (rank < min(N_SEL, n_slc)).astype(bf16)
    gr2 = lax.broadcasted_iota(jnp.int32, (NQ, NCOL), 0)
    gc2 = lax.broadcasted_iota(jnp.int32, (NQ, NCOL), 1)
    gexp = ((gc2 // (NSA_GQ * ts) == gr2 // ts) & (gc2 % ts == gr2 % ts)).astype(bf16)
    sel_c = dd(sel, gexp)

    def two_pass(n_tiles, score_tile, mask_tile, v_tile):
        sm = [jnp.where(mask_tile(j), score_tile(j), NEG_INF) for j in range(n_tiles)]
        m = functools.reduce(jnp.maximum, [jnp.max(s, axis=0, keepdims=True) for s in sm])
        num = jnp.zeros((NCOL, KV), f32)
        den = jnp.zeros((NCOL, KV), f32)
        ones = jnp.ones((P, KV), bf16)
        for j in range(n_tiles):
            p = jnp.exp(sm[j] - m).T.astype(bf16)
            num = num + dd(p, v_tile(j))
            den = den + dd(p, ones)
        return num, den

    bpp = P // SLC_LEN
    row = lax.broadcasted_iota(jnp.int32, (P, 1), 0)

    def sel_mask(j):
        blk = jnp.concatenate([jnp.broadcast_to(sel_c[j * bpp + i:j * bpp + i + 1, :], (SLC_LEN, NCOL))
                               for i in range(bpp)], axis=0)
        return (blk > 0.5) & (j * P + row <= past_len + t_col)

    num_s, den_s = two_pass(
        n_pages + 1, lambda j: ssel_scr[j * P:(j + 1) * P, :], sel_mask,
        lambda j: slab(j, 3).astype(bf16))

    lb = wbuf_ref.shape[0]
    nwt = lb // P
    wnew = jnp.concatenate([wnew_ref[...], jnp.zeros((P - ts, 2 * KV), f32)], axis=0)

    def w_tile(j):
        return wbuf_ref[j * P:(j + 1) * P, :] if j < nwt else wnew

    def win_mask(j):
        pos_w = (past_len - lb + j * P + row) if j < nwt else (past_len + row)
        dq = past_len + t_col - pos_w
        return (dq >= 0) & (dq < WINDOW) & (pos_w >= 0) & ((row < ts) | (j < nwt))

    num_w, den_w = two_pass(
        nwt + 1, lambda j: dd(w_tile(j)[:, 0:KV].astype(bf16), qh), win_mask,
        lambda j: w_tile(j)[:, KV:2 * KV].astype(bf16))

    g = jax.nn.sigmoid(gate_ref[...])
    o_ref[...] = g[0] * o_c + g[1] * (num_s / den_s) + g[2] * (num_w / den_w)


def nsa_sample_pallas(q, gl, rows_new, win_new, pools, page_table, win_bufs, layer, kc_w, phi, k_gain):
    B, ts = q.shape[:2]
    KVH, G, D, P = NSA_KV_HEADS, NSA_GQ, HEAD_DIM, PAGE_SIZE
    KV = KVH * D
    n_pages = page_table.shape[1]
    past_len = n_pages * P
    n_pool = pools.shape[1]
    assert pools.shape[2] == P and ts <= SLC_LEN and P % SLC_LEN == 0
    lb = win_bufs.shape[2]
    assert lb % P == 0
    NCOL = KVH * G * ts
    n_chunk = (past_len + SLC_LEN) // CMP_STRIDE
    ncp = -(-n_chunk // 8) * 8
    n_slc = (past_len + SLC_LEN) // SLC_LEN
    nsp = -(-n_slc // 8) * 8
    qs = (q * D ** -0.5).reshape(B, ts, KVH, G, D).transpose(0, 2, 4, 3, 1).reshape(B, KVH, D, G * ts)
    z = jnp.zeros_like(qs[:, 0])
    qbd = jnp.concatenate([jnp.concatenate([qs[:, 0], z], axis=2), jnp.concatenate([z, qs[:, 1]], axis=2)], axis=1)
    qh, ql = _split(qbd)
    gate = gl.reshape(B, ts, KVH, G, 3).transpose(0, 4, 2, 3, 1).reshape(B, 3, NCOL, 1)
    gate = jnp.broadcast_to(gate, (B, 3, NCOL, KV)).astype(f32)
    w_lane = jnp.repeat(kc_w.reshape(2 * KVH, CMP_LEN), D, axis=0)
    reps = P // CMP_STRIDE
    wa = jnp.tile(w_lane[:, :CMP_STRIDE].T, (reps, 1)).astype(f32)
    wb = jnp.tile(w_lane[:, CMP_STRIDE:].T, (reps, 1)).astype(f32)
    zz = jnp.zeros((D, D), f32)
    bdiag = lambda m: jnp.concatenate([jnp.concatenate([m, zz], axis=1), jnp.concatenate([zz, m], axis=1)], axis=0)
    phik, phiv = bdiag(phi[0].astype(f32)), bdiag(phi[1].astype(f32))
    gain = jnp.tile(k_gain.astype(f32), KVH).reshape(1, KV)
    half = ROPE_DIM // 2
    inv = ROPE_THETA ** (-jnp.arange(half, dtype=f32) / half)
    cmp_end = (jnp.arange(ncp, dtype=jnp.int32) * CMP_STRIDE + (CMP_LEN - 1)).astype(f32)
    ang = cmp_end[:, None] * inv
    cos_h = jnp.concatenate([jnp.cos(ang), jnp.cos(ang), jnp.ones((ncp, D - ROPE_DIM), f32)], axis=1)
    sin_h = jnp.concatenate([jnp.sin(ang), jnp.sin(ang), jnp.zeros((ncp, D - ROPE_DIM), f32)], axis=1)
    cos_t, sin_t = jnp.tile(cos_h, (1, KVH)), jnp.tile(sin_h, (1, KVH))
    pool2 = pools.reshape(pools.shape[0] * n_pool, P, 4 * KV)
    new2 = rows_new.reshape(B, ts, 4 * KV).astype(f32)
    wbuf2 = win_bufs.reshape(win_bufs.shape[0] * B, lb, 2 * KV)
    wnew2 = win_new.reshape(B, ts, 2 * KV).astype(f32)
    page_spec = lambda j: pl.BlockSpec((None, P, 4 * KV),
                                       lambda b, pt, j=j: (layer * n_pool + pt[b, j], 0, 0))
    per_b = lambda *s: pl.BlockSpec((None,) + s, lambda b, pt: (b,) + (0,) * len(s))
    wbuf_spec = pl.BlockSpec((None, lb, 2 * KV), lambda b, pt: (layer * B + b, 0, 0))
    const = lambda *s: pl.BlockSpec(s, lambda b, pt: (0,) * len(s))
    grid_spec = pltpu.PrefetchScalarGridSpec(
        num_scalar_prefetch=1, grid=(B,),
        in_specs=[page_spec(j) for j in range(n_pages)] + [
            per_b(ts, 4 * KV), wbuf_spec, per_b(ts, 2 * KV), per_b(KV, NCOL), per_b(KV, NCOL),
            per_b(3, NCOL, KV), const(P, 2 * KV), const(P, 2 * KV), const(KV, KV), const(KV, KV),
            const(1, KV), const(ncp, KV), const(ncp, KV)],
        out_specs=per_b(NCOL, KV),
        scratch_shapes=[pltpu.VMEM(((n_pages + 1) * P, NCOL), f32), pltpu.VMEM((ncp, 2 * KV), f32),
                        pltpu.VMEM((ncp, 2 * KV), f32), pltpu.VMEM((nsp, KVH * ts), f32)])
    out = pl.pallas_call(
        functools.partial(_nsa_sample_kernel, n_pages=n_pages, ts=ts, past_len=past_len),
        grid_spec=grid_spec,
        out_shape=jax.ShapeDtypeStruct((B, NCOL, KV), f32),
        compiler_params=pltpu.CompilerParams(dimension_semantics=("arbitrary",),
                                             vmem_limit_bytes=48 * 1024 * 1024),
        name="nsa_sample",
    )(page_table, *([pool2] * n_pages), new2, wbuf2, wnew2, qh, ql, gate, wa, wb, phik, phiv, gain, cos_t, sin_t)
    o4 = out.reshape(B, KVH, G, ts, KVH, D)
    o = jnp.stack([o4[:, 0, :, :, 0], o4[:, 1, :, :, 1]], axis=1)
    return o.transpose(0, 3, 1, 2, 4).reshape(B, ts, KVH * G * D)


MOE_TM = 512
ROUTER_LANES = 128


def _moe_kernel(x_ref, g_ref, wrh_ref, wrl_ref, br_ref, wg_ref, wu_ref, wd_ref, o_ref,
                h_scr, gate_scr, acc_scr):
    grp = pl.program_id(1)
    dd = lambda a, b: jnp.dot(a, b, preferred_element_type=f32)
    tm = x_ref.shape[0]
    lane = lax.broadcasted_iota(jnp.int32, (tm, ROUTER_LANES), 1).astype(f32)
    far = float(ROUTER_LANES)

    @pl.when(grp == 0)
    def _():
        x = x_ref[...]
        h = x * lax.rsqrt(jnp.mean(x * x, axis=-1, keepdims=True) + RMS_EPS) * g_ref[...]
        hh, hl = _split(h)
        h_scr[...] = hh
        logits = dd(hh, wrh_ref[...]) + (dd(hh, wrl_ref[...]) + dd(hl, wrh_ref[...])) + br_ref[...]
        is_c = lane < MOE_GROUPS
        lc = jnp.where(is_c, logits, NEG_INF)
        mc = jnp.max(lc, axis=1, keepdims=True)
        g_idx = jnp.min(jnp.where(lc == mc, lane, far), axis=1, keepdims=True)
        g_w = 1.0 / jnp.sum(jnp.where(is_c, jnp.exp(lc - mc), 0.0), axis=1, keepdims=True)
        lo = MOE_GROUPS + MOE_EPG * g_idx
        lf = jnp.where((lane >= lo) & (lane < lo + MOE_EPG), logits, NEG_INF)
        v1 = jnp.max(lf, axis=1, keepdims=True)
        i1 = jnp.min(jnp.where(lf == v1, lane, far), axis=1, keepdims=True)
        lf2 = jnp.where(lane == i1, NEG_INF, lf)
        v2 = jnp.max(lf2, axis=1, keepdims=True)
        i2 = jnp.min(jnp.where(lf2 == v2, lane, far), axis=1, keepdims=True)
        e21 = jnp.exp(v2 - v1)
        w1 = g_w / (1.0 + e21)
        gate_scr[...] = jnp.where(lane == i1, w1, jnp.where(lane == i2, e21 * w1, 0.0))
        acc_scr[...] = x

    h = h_scr[...]
    hg = dd(h, wg_ref[...])
    hu = dd(h, wu_ref[...])
    gate = gate_scr[...]
    first = (MOE_GROUPS + MOE_EPG * grp).astype(f32)
    cols = []
    for e in range(MOE_EPG):
        ge = jnp.sum(jnp.where(lane == first + e, gate, 0.0), axis=1, keepdims=True)
        sl = slice(e * D_FF_E, (e + 1) * D_FF_E)
        hge = hg[:, sl]
        cols.append((hge * jax.nn.sigmoid(hge) * hu[:, sl] * ge).astype(bf16))
    acc_scr[...] += dd(jnp.concatenate(cols, axis=1), wd_ref[...])

    @pl.when(grp == MOE_GROUPS - 1)
    def _():
        o_ref[...] = acc_scr[...]


def moe_prep(g, wc, bc, wf, bf, wg, wu, wd):
    d = wc.shape[0]
    pad = ROUTER_LANES - MOE_GROUPS - N_EXPERTS
    wr = jnp.pad(jnp.concatenate([wc, wf], axis=1).astype(f32), ((0, 0), (0, pad)))
    wrh, wrl = _split(wr)
    br = jnp.pad(jnp.concatenate([bc, bf]).astype(f32), (0, pad)).reshape(1, ROUTER_LANES)
    regroup = lambda w: (w.reshape(MOE_GROUPS, MOE_EPG, d, D_FF_E).transpose(0, 2, 1, 3)
                         .reshape(MOE_GROUPS, d, MOE_EPG * D_FF_E).astype(bf16))
    wdg = wd.reshape(MOE_GROUPS, MOE_EPG * D_FF_E, d).astype(bf16)
    return g.reshape(1, d).astype(f32), wrh, wrl, br, regroup(wg), regroup(wu), wdg


def moe_residual(x, prep):
    g, wrh, wrl, br, wgg, wug, wdg = prep
    shp = x.shape
    d = shp[-1]
    x2 = x.reshape(-1, d)
    m = x2.shape[0]
    tm = MOE_TM
    assert m % tm == 0
    gf = MOE_EPG * D_FF_E
    full = lambda r, c: pl.BlockSpec((r, c), lambda i, j: (0, 0))
    out = pl.pallas_call(
        _moe_kernel,
        grid=(m // tm, MOE_GROUPS),
        in_specs=[pl.BlockSpec((tm, d), lambda i, j: (i, 0)), full(1, d),
                  full(d, ROUTER_LANES), full(d, ROUTER_LANES), full(1, ROUTER_LANES),
                  pl.BlockSpec((None, d, gf), lambda i, j: (j, 0, 0)),
                  pl.BlockSpec((None, d, gf), lambda i, j: (j, 0, 0)),
                  pl.BlockSpec((None, gf, d), lambda i, j: (j, 0, 0))],
        out_specs=pl.BlockSpec((tm, d), lambda i, j: (i, 0)),
        out_shape=jax.ShapeDtypeStruct((m, d), f32),
        scratch_shapes=[pltpu.VMEM((tm, d), bf16), pltpu.VMEM((tm, ROUTER_LANES), f32),
                        pltpu.VMEM((tm, d), f32)],
        compiler_params=pltpu.CompilerParams(dimension_semantics=("parallel", "arbitrary"),
                                             vmem_limit_bytes=48 * 1024 * 1024),
        name="moe",
    )(x2, g, wrh, wrl, br, wgg, wug, wdg)
    return out.reshape(shp)


def rmsnorm(x, g):
    xf = x.astype(jnp.float32)
    y = xf * lax.rsqrt(jnp.mean(xf * xf, axis=-1, keepdims=True) + RMS_EPS)
    return (y * g.astype(jnp.float32)).astype(x.dtype)


def rope_partial(x, pos):
    half = ROPE_DIM // 2
    inv = ROPE_THETA ** (-jnp.arange(half, dtype=jnp.float32) / half)
    ang = pos.astype(jnp.float32)[:, None] * inv
    cos = jnp.cos(ang)[:, None, :]
    sin = jnp.sin(ang)[:, None, :]
    xf = x.astype(jnp.float32)
    x1 = xf[..., :half]
    x2 = xf[..., half:ROPE_DIM]
    out = jnp.concatenate([x1 * cos - x2 * sin, x2 * cos + x1 * sin, xf[..., ROPE_DIM:]], axis=-1)
    return out.astype(x.dtype)


def pool_mix(u, hist, p0, w_grp, scale):
    b, t, _ = u.shape
    ext = jnp.concatenate([hist.astype(u.dtype), u], axis=1).astype(jnp.float32)
    cs = jnp.pad(jnp.cumsum(ext, axis=1), ((0, 0), (1, 0), (0, 0)))
    cnt_pos = p0 + jnp.arange(t, dtype=jnp.int32) + 1
    means = []
    for gi, w in enumerate(POOL_WINDOWS):
        c = cs[..., gi * POOL_GDIM:(gi + 1) * POOL_GDIM]
        win_sum = c[:, POOL_HIST + 1:POOL_HIST + 1 + t] - c[:, POOL_HIST + 1 - w:POOL_HIST + 1 - w + t]
        cnt = jnp.minimum(cnt_pos, w).astype(jnp.float32)[None, :, None]
        means.append(win_sum / cnt)
    mean = jnp.stack(means, axis=2)
    d = mean - u.reshape(b, t, POOL_GROUPS, POOL_GDIM).astype(jnp.float32)
    y = jnp.einsum('btgc,gcd->btgd', d, w_grp.astype(jnp.float32)).reshape(b, t, POOL_DIM)
    return (y * scale.astype(jnp.float32)).astype(u.dtype)


PROJ_TM = 512


def _norm_mm_kernel(x_ref, g_ref, w_ref, o_ref):
    x = x_ref[...]
    h = x * lax.rsqrt(jnp.mean(x * x, axis=-1, keepdims=True) + RMS_EPS) * g_ref[...]
    o_ref[...] = jnp.dot(h.astype(bf16), w_ref[...], preferred_element_type=f32)


def _norm_mm(x, g, w):
    lead, d = x.shape[:-1], x.shape[-1]
    n = w.shape[1]
    x2 = x.reshape(-1, d)
    m = x2.shape[0]
    npad = -(-n // 128) * 128
    wb = jnp.pad(w.astype(bf16), ((0, 0), (0, npad - n)))
    tm = PROJ_TM
    assert m % tm == 0
    out = pl.pallas_call(
        _norm_mm_kernel, grid=(m // tm,),
        in_specs=[pl.BlockSpec((tm, d), lambda i: (i, 0)), pl.BlockSpec((1, d), lambda i: (0, 0)),
                  pl.BlockSpec((d, npad), lambda i: (0, 0))],
        out_specs=pl.BlockSpec((tm, npad), lambda i: (i, 0)),
        out_shape=jax.ShapeDtypeStruct((m, npad), f32),
        compiler_params=pltpu.CompilerParams(dimension_semantics=("parallel",),
                                             vmem_limit_bytes=48 * 1024 * 1024),
        name="norm_mm",
    )(x2, g.reshape(1, d).astype(f32), wb)
    return out[:, :n].reshape(lead + (n,))


def _mix_out_kernel(a_ref, b_ref, wa_ref, wb_ref, x_ref, o_ref):
    dd = lambda p, q: jnp.dot(p.astype(bf16), q, preferred_element_type=f32)
    o_ref[...] = x_ref[...] + (dd(a_ref[...], wa_ref[...]) + dd(b_ref[...], wb_ref[...]))


def _mix_out(a, b, w, x):
    d = x.shape[-1]
    ka, kb = a.shape[-1], b.shape[-1]
    x2 = x.reshape(-1, d)
    m = x2.shape[0]
    tm = PROJ_TM
    assert m % tm == 0
    wbf = w.astype(bf16)
    tile = lambda c: pl.BlockSpec((tm, c), lambda i: (i, 0))
    full = lambda r, c: pl.BlockSpec((r, c), lambda i: (0, 0))
    out = pl.pallas_call(
        _mix_out_kernel, grid=(m // tm,),
        in_specs=[tile(ka), tile(kb), full(ka, d), full(kb, d), tile(d)], out_specs=tile(d),
        out_shape=jax.ShapeDtypeStruct((m, d), f32),
        compiler_params=pltpu.CompilerParams(dimension_semantics=("parallel",)),
        name="mix_out",
    )(a.reshape(m, ka), b.reshape(m, kb), wbf[:ka], wbf[ka:], x2)
    return out.reshape(x.shape)


def _ab_feat_kernel(x_ref, g_ref, w_ref, qg_ref, kg_ref, cos_ref, sin_ref, hs_ref, he_ref, rot_ref,
                    pool_ref, rows_ref, win_ref, qh_ref, ql_ref, gt_ref, ks_ref, kw_ref, vst_ref, vwt_ref):
    D, KV = HEAD_DIM, NSA_KV_HEADS * HEAD_DIM
    x = x_ref[...]
    h = x * lax.rsqrt(jnp.mean(x * x, axis=-1, keepdims=True) + RMS_EPS) * g_ref[...]
    u = jnp.dot(h.astype(bf16), w_ref[...], preferred_element_type=f32)
    off_kv = POOL_DIM + NSA_DIM
    pool_ref[...] = u[:, :POOL_DIM]
    cos, sin = cos_ref[...], sin_ref[...]

    def norm_rope(z, gain):
        n = z.shape[1] // D
        hs, he, rot = hs_ref[0:n * D, :], he_ref[:, 0:n * D], rot_ref[0:n * D, 0:n * D]
        ms = _dot_exact_rhs(_dot_exact_rhs(z * z, hs), he) * (1.0 / D)
        zn = z * lax.rsqrt(ms + RMS_EPS) * gain
        wide = lambda t: jnp.concatenate([t] * (n // 2), axis=1)
        return zn * wide(cos) + _dot_exact_rhs(zn, rot) * wide(sin)

    q = norm_rope(u[:, POOL_DIM:off_kv], qg_ref[...]) * (D ** -0.5)
    for kvh in range(NSA_KV_HEADS):
        qt = jnp.concatenate([q[:, (kvh * NSA_GQ + g) * D:(kvh * NSA_GQ + g + 1) * D].T
                              for g in range(NSA_GQ)], axis=1)
        hi, lo = _split(qt)
        qh_ref[kvh] = hi
        ql_ref[kvh] = lo
    kv = u[:, off_kv:off_kv + KV_COLS]
    slab = lambda i: kv[:, i * KV:(i + 1) * KV]
    kr = norm_rope(jnp.concatenate([slab(2), slab(4)], axis=1), kg_ref[...])
    k_slc, k_win = kr[:, 0:KV], kr[:, KV:2 * KV]
    rows_ref[...] = jnp.concatenate([slab(0), slab(1), k_slc, slab(3)], axis=1)
    win_ref[...] = jnp.concatenate([k_win, slab(5)], axis=1)
    for kvh in range(NSA_KV_HEADS):
        hsl = slice(kvh * D, (kvh + 1) * D)
        ks_ref[kvh] = k_slc[:, hsl].astype(bf16)
        kw_ref[kvh] = k_win[:, hsl].astype(bf16)
        vst_ref[kvh] = slab(3)[:, hsl].T.astype(bf16)
        vwt_ref[kvh] = slab(5)[:, hsl].T.astype(bf16)
    gt_ref[...] = u[:, off_kv + KV_COLS:].T


def ab_features_prompt(x, norm_g, w_in, q_norm, k_norm):
    B, T, d = x.shape
    QB, D, KVH, G = Q_BLOCK, HEAD_DIM, NSA_KV_HEADS, NSA_GQ
    KV = KVH * D
    assert T % QB == 0
    nqb = T // QB
    npad = -(-IN_COLS // 128) * 128
    assert npad - (POOL_DIM + NSA_DIM + KV_COLS) == 128
    wb = jnp.pad(w_in.astype(bf16), ((0, 0), (0, npad - IN_COLS)))
    half = ROPE_DIM // 2
    inv = ROPE_THETA ** (-jnp.arange(half, dtype=f32) / half)
    ang = jnp.arange(T, dtype=jnp.int32).astype(f32)[:, None] * inv
    cos_h = jnp.concatenate([jnp.cos(ang), jnp.cos(ang), jnp.ones((T, D - ROPE_DIM), f32)], axis=1)
    sin_h = jnp.concatenate([jnp.sin(ang), jnp.sin(ang), jnp.zeros((T, D - ROPE_DIM), f32)], axis=1)
    cos_t, sin_t = jnp.tile(cos_h, (1, 2)), jnp.tile(sin_h, (1, 2))
    r_i = lax.broadcasted_iota(jnp.int32, (NSA_DIM, NSA_DIM), 0)
    c_i = lax.broadcasted_iota(jnp.int32, (NSA_DIM, NSA_DIM), 1)
    same = (r_i // D) == (c_i // D)
    rd, cd = r_i % D, c_i % D
    rot = jnp.where(same & (cd < half) & (rd == cd + half), -1.0,
                    jnp.where(same & (cd >= half) & (cd < ROPE_DIM) & (rd == cd - half), 1.0, 0.0)).astype(bf16)
    hs = (lax.broadcasted_iota(jnp.int32, (NSA_DIM, 128), 0) // D
          == lax.broadcasted_iota(jnp.int32, (NSA_DIM, 128), 1)).astype(bf16)
    qg = jnp.tile(q_norm.astype(f32), NSA_HEADS).reshape(1, NSA_DIM)
    kg = jnp.concatenate([jnp.tile(k_norm[1].astype(f32), KVH), jnp.tile(k_norm[2].astype(f32), KVH)]).reshape(1, 2 * KV)
    full = lambda a: pl.BlockSpec(a.shape, lambda b, i: (0,) * a.ndim)
    tok = lambda c: pl.BlockSpec((None, QB, c), lambda b, i: (b, i, 0))
    f = jax.ShapeDtypeStruct
    outs = pl.pallas_call(
        _ab_feat_kernel, grid=(B, nqb),
        in_specs=[tok(d), pl.BlockSpec((1, d), lambda b, i: (0, 0)), full(wb), full(qg), full(kg),
                  pl.BlockSpec((QB, 2 * D), lambda b, i: (i, 0)), pl.BlockSpec((QB, 2 * D), lambda b, i: (i, 0)),
                  full(hs), pl.BlockSpec((128, NSA_DIM), lambda b, i: (0, 0)), full(rot)],
        out_specs=[tok(POOL_DIM), tok(4 * KV), tok(2 * KV),
                   pl.BlockSpec((None, KVH, None, D, G * QB), lambda b, i: (b, 0, i, 0, 0)),
                   pl.BlockSpec((None, KVH, None, D, G * QB), lambda b, i: (b, 0, i, 0, 0)),
                   pl.BlockSpec((None, None, 128, QB), lambda b, i: (b, i, 0, 0)),
                   pl.BlockSpec((None, KVH, QB, D), lambda b, i: (b, 0, i, 0)),
                   pl.BlockSpec((None, KVH, QB, D), lambda b, i: (b, 0, i, 0)),
                   pl.BlockSpec((None, KVH, D, QB), lambda b, i: (b, 0, 0, i)),
                   pl.BlockSpec((None, KVH, D, QB), lambda b, i: (b, 0, 0, i))],
        out_shape=[f((B, T, POOL_DIM), f32), f((B, T, 4 * KV), f32), f((B, T, 2 * KV), f32),
                   f((B, KVH, nqb, D, G * QB), bf16), f((B, KVH, nqb, D, G * QB), bf16),
                   f((B, nqb, 128, QB), f32), f((B, KVH, T, D), bf16), f((B, KVH, T, D), bf16),
                   f((B, KVH, D, T), bf16), f((B, KVH, D, T), bf16)],
        compiler_params=pltpu.CompilerParams(dimension_semantics=("parallel", "parallel"),
                                             vmem_limit_bytes=48 * 1024 * 1024),
        name="ab_feat",
    )(x, norm_g.reshape(1, d).astype(f32), wb, qg, kg, cos_t, sin_t, hs, hs.T, rot)
    pool_in, rows, win, qh, ql, gt, ks, kw, vst, vwt = outs
    gT = (gt[:, :, :NSA_HEADS * 3].reshape(B, nqb, KVH, G, 3, QB).transpose(0, 2, 1, 4, 3, 5)
          .reshape(B, KVH, nqb, 3, G * QB))
    return pool_in, rows, win, (qh, ql, gT, ks, vst, kw, vwt)


def _pool_mix_out_kernel(u_ref, halo_ref, nsa_ref, x_ref, band_ref, wg_ref, sc_ref, wa_ref, wb_ref, o_ref,
                         *, seq_len):
    tm = u_ref.shape[0]
    hrows = halo_ref.shape[0]
    gd = POOL_GDIM
    tile = pl.program_id(0) % (seq_len // tm)
    u = u_ref[...]
    halo = jnp.where(tile == 0, 0.0, halo_ref[...])
    ext = jnp.concatenate([halo, u], axis=0)
    pos1 = tile * tm + lax.broadcasted_iota(jnp.int32, (tm, 1), 0) + 1
    outs = []
    for gi, w in enumerate(POOL_WINDOWS):
        sl = slice(gi * gd, (gi + 1) * gd)
        win_sum = _dot_exact_lhs(band_ref[gi], ext[:, sl])
        d = win_sum / jnp.minimum(pos1, w).astype(f32) - u[:, sl]
        outs.append(jnp.dot(d.astype(bf16), wg_ref[gi], preferred_element_type=f32))
    pool_out = (jnp.concatenate(outs, axis=1) * sc_ref[...]).astype(bf16)
    dd = lambda p, q: jnp.dot(p, q, preferred_element_type=f32)
    o_ref[...] = x_ref[...] + (dd(pool_out, wa_ref[...]) + dd(nsa_ref[...].astype(bf16), wb_ref[...]))


def pool_mix_out(pool_in, nsa_out, x, pool_w, pool_scale, w_out):
    b, t, d = x.shape
    m = b * t
    tm = PROJ_TM
    hrows = POOL_HIST + 1
    assert t % tm == 0 and tm % hrows == 0 and POOL_GDIM % 128 == 0
    r_i = lax.broadcasted_iota(jnp.int32, (tm, hrows + tm), 0) + hrows
    j_i = lax.broadcasted_iota(jnp.int32, (tm, hrows + tm), 1)
    band = jnp.stack([((j_i <= r_i) & (j_i > r_i - w)).astype(bf16) for w in POOL_WINDOWS])
    wbf = w_out.astype(bf16)
    u2, n2, x2 = pool_in.reshape(m, POOL_DIM), nsa_out.reshape(m, NSA_DIM), x.reshape(m, d)
    tile = lambda c: pl.BlockSpec((tm, c), lambda i: (i, 0))
    full = lambda a: pl.BlockSpec(a.shape, lambda i: (0,) * a.ndim)
    halo = pl.BlockSpec((hrows, POOL_DIM), lambda i: (jnp.maximum(i * (tm // hrows) - 1, 0), 0))
    consts = [band, pool_w.astype(bf16), pool_scale.reshape(1, POOL_DIM).astype(f32), wbf[:POOL_DIM], wbf[POOL_DIM:]]
    out = pl.pallas_call(
        functools.partial(_pool_mix_out_kernel, seq_len=t), grid=(m // tm,),
        in_specs=[tile(POOL_DIM), halo, tile(NSA_DIM), tile(d)] + [full(c) for c in consts],
        out_specs=tile(d), out_shape=jax.ShapeDtypeStruct((m, d), f32),
        compiler_params=pltpu.CompilerParams(dimension_semantics=("parallel",),
                                             vmem_limit_bytes=48 * 1024 * 1024),
        name="pool_mix_out",
    )(u2, u2, n2, x2, *consts)
    return out.reshape(b, t, d)


def ab_features(x, norm_g, pos, w_in, q_norm, k_norm):
    b, t = x.shape[:2]
    u = _norm_mm(x, norm_g, w_in)
    off_kv = POOL_DIM + NSA_DIM
    pool_in = u[..., :POOL_DIM]
    q = u[..., POOL_DIM:off_kv].reshape(b, t, NSA_HEADS, HEAD_DIM)
    kv = u[..., off_kv:off_kv + KV_COLS].reshape(b, t, 6, NSA_KV_HEADS, HEAD_DIM)
    gl = u[..., off_kv + KV_COLS:].reshape(b, t, NSA_HEADS, 3)
    q = rope_partial(rmsnorm(q, q_norm), pos)
    k_slc = rope_partial(rmsnorm(kv[:, :, 2], k_norm[1]), pos)
    k_win = rope_partial(rmsnorm(kv[:, :, 4], k_norm[2]), pos)
    rows = jnp.stack([kv[:, :, 0], kv[:, :, 1], k_slc, kv[:, :, 3]], axis=2)
    win = jnp.stack([k_win, kv[:, :, 5]], axis=2)
    return pool_in, q, gl, rows, win


def compress_kv(k_rows, v_rows, pos_w, phi, k_gain):
    b, length = k_rows.shape[:2]
    n_chunk = length // CMP_STRIDE

    def weighted_block_mean(rows, w):
        ch = rows.reshape(b, n_chunk, CMP_STRIDE, NSA_KV_HEADS, HEAD_DIM)
        return (jnp.einsum('bnlhd,hl->bnhd', ch[:, :-1], w[:, :CMP_STRIDE])
                + jnp.einsum('bnlhd,hl->bnhd', ch[:, 1:], w[:, CMP_STRIDE:]))

    cmp_end = jnp.arange(n_chunk - 1, dtype=jnp.int32) * CMP_STRIDE + (CMP_LEN - 1)
    kc = jnp.einsum('bnhd,de->bnhe', weighted_block_mean(k_rows, pos_w[0]), phi[0])
    kc = rope_partial(rmsnorm(kc, k_gain), cmp_end)
    vc = jnp.einsum('bnhd,de->bnhe', weighted_block_mean(v_rows, pos_w[1]), phi[1])
    return kc, vc, cmp_end


def ab_layer_prompt(x, norm_g, w_in, w_out, q_norm, k_norm, cmp_pos_w, cmp_phi, pool_w, pool_scale):
    b, t = x.shape[:2]
    pool_in, rows2, win2, ops = ab_features_prompt(x, norm_g, w_in, q_norm, k_norm)
    rows = rows2.reshape(b, t, 4, NSA_KV_HEADS, HEAD_DIM)
    win = win2.reshape(b, t, 2, NSA_KV_HEADS, HEAD_DIM)
    kc, vc, _ = compress_kv(rows[:, :, 0], rows[:, :, 1], cmp_pos_w, cmp_phi, k_norm[0])
    nsa_out = nsa_prompt_pallas(ops, kc, vc)
    x_new = pool_mix_out(pool_in, nsa_out, x, pool_w, pool_scale, w_out)
    keep = min(WINDOW, t)
    return x_new, rows, win[:, t - keep:], pool_in[:, t - POOL_HIST:]


def ab_layer_sample(x, norm_g, pools, page_table, win_bufs, layer, pool_hist, w_in, w_out, q_norm, k_norm,
                    cmp_pos_w, cmp_phi, pool_w, pool_scale):
    ts = x.shape[1]
    past_len = page_table.shape[1] * pools.shape[2]
    pos = past_len + jnp.arange(ts, dtype=jnp.int32)
    pool_in, q, gl, rows, win = ab_features(x, norm_g, pos, w_in, q_norm, k_norm)
    pool_out = pool_mix(pool_in, pool_hist, past_len, pool_w, pool_scale)
    nsa_out = nsa_sample_pallas(q, gl, rows, win, pools, page_table, win_bufs, layer,
                                cmp_pos_w, cmp_phi, k_norm[0])
    win_buf = win_bufs[layer]
    lb = win_buf.shape[1]
    keep = min(WINDOW, lb + ts)
    new_win = jnp.concatenate([win_buf, win.astype(win_buf.dtype)], axis=1)[:, lb + ts - keep:]
    x_new = _mix_out(pool_out, nsa_out, w_out, x)
    new_hist = jnp.concatenate([pool_hist.astype(pool_in.dtype), pool_in], axis=1)[:, -POOL_HIST:]
    return x_new, rows, new_win, new_hist


RW_TM = 256
HEAD_LANES = 128


def _rwkv_pre_kernel(*refs, seq_len, has_vres):
    it = iter(refs)
    x_ref, xprev_ref, fp_ref, g_ref, mu_ref = [next(it) for _ in range(5)]
    wr_ref, wk_ref, wv_ref = [next(it) for _ in range(3)]
    w0_ref, w1_ref, w2_ref, a0_ref, a1_ref, a2_ref, g1_ref, g2_ref = [next(it) for _ in range(8)]
    kkw_ref, kaw_ref, hsum_ref, hexp_ref = [next(it) for _ in range(4)]
    if has_vres:
        v0_ref, v1_ref, v2_ref, vf_ref = [next(it) for _ in range(4)]
    r_ref, k_ref, v_ref, lw_ref, kk_ref, a_ref, gg_ref = [next(it) for _ in range(7)]
    dd = lambda a, b: jnp.dot(a, b, preferred_element_type=f32)
    tm = x_ref.shape[0]
    norm = lambda z: z * lax.rsqrt(jnp.mean(z * z, axis=-1, keepdims=True) + RMS_EPS) * g_ref[...]
    h = norm(x_ref[...])
    row = lax.broadcasted_iota(jnp.int32, (tm, 1), 0)
    rolled = pltpu.roll(h, 1, axis=0)
    if seq_len % tm == 0:
        first = (pl.program_id(0) % (seq_len // tm)) == 0
        last_prev = norm(xprev_ref[...])[xprev_ref.shape[0] - 1:, :]
        prev = jnp.where(row == 0, jnp.where(first, fp_ref[...], last_prev), rolled)
    else:
        nseq = tm // seq_len
        sel = (lax.broadcasted_iota(jnp.int32, (tm, nseq), 0)
               == seq_len * lax.broadcasted_iota(jnp.int32, (tm, nseq), 1)).astype(bf16)
        prev = jnp.where(row % seq_len == 0, _dot_exact_lhs(sel, fp_ref[...]), rolled)
    xx = prev - h
    mix = lambda j: (h + xx * mu_ref[j:j + 1, :]).astype(bf16)
    xr, xw, xk, xv, xa, xg = [mix(j) for j in range(6)]
    k = dd(xk, wk_ref[...])
    v = dd(xv, wv_ref[...])
    r_ref[...] = dd(xr, wr_ref[...])
    z = -(w0_ref[...] + dd(jnp.tanh(dd(xw, w1_ref[...])).astype(bf16), w2_ref[...]))
    softplus = jnp.maximum(z, 0.0) + jnp.log(1.0 + jnp.exp(-jnp.abs(z)))
    lw_ref[...] = -jnp.exp(-softplus - 0.5)
    a = jax.nn.sigmoid(a0_ref[...] + dd(dd(xa, a1_ref[...]).astype(bf16), a2_ref[...]))
    a_ref[...] = a
    gg_ref[...] = dd(jax.nn.sigmoid(dd(xg, g1_ref[...])).astype(bf16), g2_ref[...])
    if has_vres:
        v = v + (vf_ref[...] - v) * jax.nn.sigmoid(
            v0_ref[...] + dd(dd(xv, v1_ref[...]).astype(bf16), v2_ref[...]))
    v_ref[...] = v
    kk = k * kkw_ref[...]
    nrm = jnp.maximum(jnp.sqrt(_dot_exact_rhs(kk * kk, hsum_ref[...])), 1e-12)
    kk_ref[...] = kk * _dot_exact_rhs(1.0 / nrm, hexp_ref[...])
    k_ref[...] = k * (1.0 + (a - 1.0) * kaw_ref[...])


def _rwkv_post_kernel(o_ref, r_ref, k_ref, v_ref, gg_ref, x_ref, gnw_ref, gnb_ref, rk_ref, hsum_ref,
                      hexp_ref, wo_ref, out_ref):
    hs = hsum_ref[...]
    he = hexp_ref[...]
    head_sum = lambda z: _dot_exact_rhs(_dot_exact_rhs(z, hs), he)
    o = o_ref[...]
    v = v_ref[...]
    d = o - head_sum(o) * (1.0 / RWKV_N)
    var = head_sum(d * d) * (1.0 / RWKV_N)
    on = d * lax.rsqrt(var + GN_EPS) * gnw_ref[...] + gnb_ref[...]
    on = on + head_sum(r_ref[...] * k_ref[...] * rk_ref[...]) * v
    y = (on * gg_ref[...]).astype(bf16)
    out_ref[...] = x_ref[...] + jnp.dot(y, wo_ref[...], preferred_element_type=f32)


def rwkv_layer_fused(x, norm_g, shift_prev, s0, v_first, vres, mu, wr, wk, wv, wo, w0, w1, w2, a0, a1, a2,
                     g1, g2, k_k, k_a, r_k, gn_w, gn_b):
    n, t, d = x.shape
    m = n * t
    tm = RW_TM
    assert m % tm == 0 and (t % tm == 0 or tm % t == 0)
    x2 = x.reshape(m, d)
    row = lambda z: z.reshape(1, d).astype(f32)
    cb = lambda z: z.astype(bf16)
    hd = lax.broadcasted_iota(jnp.int32, (d, HEAD_LANES), 0) // RWKV_N
    hsum = (hd == lax.broadcasted_iota(jnp.int32, (d, HEAD_LANES), 1)).astype(bf16)
    hexp = hsum.T
    tile = pl.BlockSpec((tm, d), lambda i: (i, 0))
    full = lambda z: pl.BlockSpec(z.shape, lambda i: (0,) * z.ndim)
    if t % tm == 0:
        tps = t // tm
        fp = shift_prev.reshape(n, 1, d).astype(f32)
        fp_spec = pl.BlockSpec((None, 1, d), lambda i: (i // tps, 0, 0))
    else:
        fp = shift_prev.astype(f32)
        fp_spec = pl.BlockSpec((tm // t, d), lambda i: (i, 0))
    xprev_spec = pl.BlockSpec((8, d), lambda i: (jnp.maximum(i * (tm // 8) - 1, 0), 0))
    mu8 = jnp.pad(mu.astype(f32), ((0, 2), (0, 0)))
    consts = [row(norm_g), mu8, cb(wr), cb(wk), cb(wv), row(w0), cb(w1), cb(w2), row(a0), cb(a1), cb(a2),
              cb(g1), cb(g2), row(k_k), row(k_a), hsum, hexp]
    args = [x2, x2, fp] + consts
    specs = [tile, xprev_spec, fp_spec] + [full(c) for c in consts]
    if vres is not None:
        v0, v1, v2 = vres
        extra = [row(v0), cb(v1), cb(v2)]
        args += extra + [v_first.reshape(m, d)]
        specs += [full(c) for c in extra] + [tile]
    cp = pltpu.CompilerParams(dimension_semantics=("parallel",), vmem_limit_bytes=56 * 1024 * 1024)
    r, k, v, lw, kk, a, gg = pl.pallas_call(
        functools.partial(_rwkv_pre_kernel, seq_len=t, has_vres=vres is not None),
        grid=(m // tm,), in_specs=specs, out_specs=[tile] * 7,
        out_shape=[jax.ShapeDtypeStruct((m, d), f32)] * 7, compiler_params=cp, name="rwkv_pre",
    )(*args)
    if vres is None:
        v_first = v.reshape(n, t, d)
    seq = lambda z: z.reshape(n, t, d)
    o, s = wkv_chunked(seq(r), seq(lw), seq(k), seq(v), seq(kk), seq(a), s0.astype(f32))
    o = o.reshape(m, d)
    post_consts = [row(gn_w), row(gn_b), r_k.reshape(1, d).astype(f32), hsum, hexp, cb(wo)]
    x_new = pl.pallas_call(
        _rwkv_post_kernel, grid=(m // tm,),
        in_specs=[tile] * 6 + [full(c) for c in post_consts], out_specs=tile,
        out_shape=jax.ShapeDtypeStruct((m, d), f32), compiler_params=cp, name="rwkv_post",
    )(o, r, k, v, gg, x2, *post_consts)
    h_last = rmsnorm(x[:, -1], norm_g)
    return x_new.reshape(n, t, d), v_first, s, h_last


def kernel(x_prompt, x_sample, cache_nsa_kv, cache_win_kv, state_pool, state_wkv, state_shift,
           page_table, norm_mix, norm_ffn, ab_w_in, ab_w_out, ab_q_norm, ab_k_norm, cmp_pos_w,
           cmp_phi, pool_w, pool_scale, rw_mu, rw_wr, rw_wk, rw_wv, rw_wo, rw_w0, rw_w1, rw_w2,
           rw_a0, rw_a1, rw_a2, rw_v0, rw_v1, rw_v2, rw_g1, rw_g2, rw_kk, rw_ka, rw_rk, rw_gn_w,
           rw_gn_b, moe_wc, moe_bc, moe_wf, moe_bf, moe_wg, moe_wu, moe_wd):
    xp, xs = x_prompt, x_sample
    vf_p, vf_s = None, None
    nsa_p, nsa_s, win_p, win_s, pool_p, pool_s = [], [], [], [], [], []
    wkv_p, wkv_s, sh_p, sh_s = [], [], [], []
    for l in range(DEPTH):
        if l % 2 == 0:
            i = l // 2
            wts = (ab_w_in[i], ab_w_out[i], ab_q_norm[i], ab_k_norm[i], cmp_pos_w[i], cmp_phi[i],
                   pool_w[i], pool_scale[i])
            xp, r_p, w_p, h_p = ab_layer_prompt(xp, norm_mix[l], *wts)
            xs, r_s, w_s, h_s = ab_layer_sample(xs, norm_mix[l], cache_nsa_kv, page_table,
                                                cache_win_kv, i, state_pool[i], *wts)
            nsa_p.append(r_p)
            nsa_s.append(r_s)
            win_p.append(w_p)
            win_s.append(w_s)
            pool_p.append(h_p)
            pool_s.append(h_s)
        else:
            j = l // 2
            vres = None if j == 0 else (rw_v0[j - 1], rw_v1[j - 1], rw_v2[j - 1])
            wts = (rw_mu[j], rw_wr[j], rw_wk[j], rw_wv[j], rw_wo[j], rw_w0[j], rw_w1[j], rw_w2[j],
                   rw_a0[j], rw_a1[j], rw_a2[j], rw_g1[j], rw_g2[j], rw_kk[j], rw_ka[j], rw_rk[j],
                   rw_gn_w[j], rw_gn_b[j])
            bp = xp.shape[0]
            zero_shift = jnp.zeros((bp, D_MODEL), xp.dtype)
            zero_state = jnp.zeros((bp, RWKV_HEADS, RWKV_N, RWKV_N), jnp.float32)
            xp, vf_p, s_p, shp = rwkv_layer_fused(xp, norm_mix[l], zero_shift, zero_state, vf_p, vres, *wts)
            xs, vf_s, s_s, shs = rwkv_layer_fused(xs, norm_mix[l], state_shift[j], state_wkv[j], vf_s, vres, *wts)
            wkv_p.append(s_p)
            wkv_s.append(s_s)
            sh_p.append(shp)
            sh_s.append(shs)
        prep =moe_prep(norm_ffn[l], moe_wc[l], moe_bc[l], moe_wf[l], moe_bf[l], moe_wg[l], moe_wu[l], moe_wd[l])
        xp = moe_residual(xp, prep)
        xs = moe_residual(xs, prep)
    return (xp, xs, jnp.stack(nsa_p), jnp.stack(nsa_s), jnp.stack(win_p), jnp.stack(win_s),
            jnp.stack(pool_p), jnp.stack(pool_s), jnp.stack(wkv_p), jnp.stack(wkv_s),
            jnp.stack(sh_p), jnp.stack(sh_s))
```

```python
import functools

import jax
import jax.numpy as jnp
from jax import lax
from jax.experimental import pallas as pl
from jax.experimental.pallas import tpu as pltpu


f32 = jnp.float32
bf16 = jnp.bfloat16
WKV_CHUNK = 64
WKV_PAIRS = 8
WKV_SEQS = 2
WKV_PASSES = 1
WKV_GRAM_PASSES = 1


def _split(x):
    hi = x.astype(bf16)
    lo = (x - hi.astype(f32)).astype(bf16)
    return hi, lo


def _mmul(a, b, passes, nt=False):
    dn = (((1,), (1,)), ((), ())) if nt else (((1,), (0,)), ((), ()))
    d = lambda x, y: lax.dot_general(x, y, dn, preferred_element_type=f32)
    if passes == 1:
        return d(a.astype(bf16), b.astype(bf16))
    ah, al = _split(a)
    bh, bl = _split(b)
    return d(ah, bh) + (d(ah, bl) + d(al, bh))


def _wkv_kernel(r_ref, lw_ref, k_ref, v_ref, kk_ref, a_ref, s0_ref, o_ref, sT_ref, st_scr, *, passes):
    C = WKV_CHUNK
    nb = r_ref.shape[0]
    c = pl.program_id(1)
    nc = pl.num_programs(1)
    row = lax.broadcasted_iota(jnp.int32, (2 * C, 2 * C), 0)
    col = lax.broadcasted_iota(jnp.int32, (2 * C, 2 * C), 1)
    bd = (row < C) == (col < C)
    strict = bd & ((row % C) > (col % C))
    incl = bd & ((row % C) >= (col % C))
    eye = (row == col).astype(f32)
    lane_s = col < C
    m1 = lax.broadcasted_iota(jnp.int32, (C, 2 * C), 1) < C
    tri = (lax.broadcasted_iota(jnp.int32, (C, C), 0)
           >= lax.broadcasted_iota(jnp.int32, (C, C), 1)).astype(bf16)

    @pl.when(c == 0)
    def _():
        z = jnp.zeros((C, C), f32)
        for p in range(nb * WKV_PAIRS):
            s1 = s0_ref[p // WKV_PAIRS, 2 * (p % WKV_PAIRS)]
            s2 = s0_ref[p // WKV_PAIRS, 2 * (p % WKV_PAIRS) + 1]
            st_scr[p] =jnp.concatenate([jnp.concatenate([s1, z], axis=1),
                                         jnp.concatenate([z, s2], axis=1)], axis=0)

    def stack2(x):
        return jnp.concatenate([jnp.where(m1, x, 0.0), jnp.where(m1, 0.0, x)], axis=0)

    dd = lambda x, y: jnp.dot(x, y, preferred_element_type=f32)
    pairs = range(nb * WKV_PAIRS)
    sq = [p // WKV_PAIRS for p in pairs]
    sls = [slice((p % WKV_PAIRS) * 2 * C, (p % WKV_PAIRS + 1) * 2 * C) for p in pairs]
    tv = r_ref.shape[1]

    def ld(ref, p):
        x = ref[sq[p], :, sls[p]]
        return x if tv == C else jnp.concatenate([x, jnp.zeros((C - tv, 2 * C), f32)], axis=0)

    def prep(p):
        sl = sls[p]
        lw = ld(lw_ref, p)
        kk = ld(kk_ref, p)
        h1 = lw.astype(bf16)
        r1 = lw - h1.astype(f32)
        h2 = r1.astype(bf16)
        h3 = (r1 - h2.astype(f32)).astype(bf16)
        cw = dd(tri, h1) + (dd(tri, h2) + dd(tri, h3))
        cwC = cw[C - 1:C, :]
        b = kk * ld(a_ref, p)
        k = ld(k_ref, p)
        At = -kk * jnp.exp(cw - lw)
        Rt = ld(r_ref, p) * jnp.exp(cw)
        einv = jnp.exp(-cw)
        efut = jnp.exp(cwC - cw)
        X = jnp.concatenate([stack2(At), stack2(Rt)], axis=0)
        Y = jnp.concatenate([b * einv, k * einv], axis=0)
        AR = jnp.concatenate([At, Rt], axis=0)
        BK = jnp.concatenate([b * efut, k * efut], axis=0)
        return X, Y, AR, BK, jnp.exp(cwC)

    pre = [prep(p) for p in pairs]
    G = [_mmul(pre[p][0], pre[p][1], WKV_GRAM_PASSES, nt=True) for p in pairs]
    ARS = [_mmul(pre[p][2], st_scr[p], passes, nt=True) for p in pairs]
    L, Mak, Mrb, Mrk = [], [], [], []
    for p in pairs:
        GA = G[p][0:2 * C]
        GR = G[p][2 * C:4 * C]
        GAr = pltpu.roll(GA, C, axis=1)
        GRr = pltpu.roll(GR, C, axis=1)
        L.append(jnp.where(strict, jnp.where(lane_s, GA, GAr), 0.0))
        Mak.append(jnp.where(strict, jnp.where(lane_s, GAr, GA), 0.0))
        Mrb.append(jnp.where(incl, jnp.where(lane_s, GR, GRr), 0.0))
        Mrk.append(jnp.where(incl, jnp.where(lane_s, GRr, GR), 0.0))
    V = [ld(v_ref, p) for p in pairs]
    Vs = [stack2(V[p]) for p in pairs]
    Xs = [stack2(ARS[p][0:C]) + _mmul(Mak[p], Vs[p], passes) for p in pairs]
    OV = [_mmul(Mrk[p], Vs[p], passes) for p in pairs]
    P = [eye + L[p] for p in pairs]
    Q = L
    for _ in range(5):
        Q = [_mmul(Q[p], Q[p], passes) for p in pairs]
        P = [P[p] + _mmul(Q[p], P[p], passes) for p in pairs]
    Us = [_mmul(P[p], Xs[p], passes) for p in pairs]
    Os = [_mmul(Mrb[p], Us[p], passes) + OV[p] for p in pairs]
    for p in pairs:
        o_ref[sq[p], :, sls[p]] = (ARS[p][C:2 * C] + Os[p][0:C] + Os[p][C:2 * C])[0:tv]
    for p in pairs:
        U = Us[p][0:C] + Us[p][C:2 * C]
        UV = jnp.concatenate([U, V[p]], axis=0)
        dS = _mmul(UV.T, pre[p][3], passes)
        st_scr[p] = st_scr[p] * pre[p][4] + jnp.where(bd, dS, 0.0)

    @pl.when(c == nc - 1)
    def _():
        for p in pairs:
            Snew = st_scr[p]
            sT_ref[sq[p], 2 * (p % WKV_PAIRS)] = Snew[0:C, 0:C]
            sT_ref[sq[p], 2 * (p % WKV_PAIRS) + 1] = Snew[C:2 * C, C:2 * C]


def wkv_chunked(r, lw, k, v, kk, a, s0):
    B, T, D = r.shape
    H = D // 64
    C = WKV_CHUNK
    nb = WKV_SEQS
    tb = min(T, C)
    assert (T % C == 0 or T < C) and tb % 8 == 0 and D == WKV_PAIRS * 2 * C and B % nb == 0
    blk = pl.BlockSpec((nb, tb, D), lambda b, c: (b, c, 0))
    sblk = pl.BlockSpec((nb, H, 64, 64), lambda b, c: (b, 0, 0, 0))
    return pl.pallas_call(
        functools.partial(_wkv_kernel, passes=WKV_PASSES),
        grid=(B // nb, T // tb),
        in_specs=[blk] * 6 + [sblk],
        out_specs=[blk, sblk],
        out_shape=[jax.ShapeDtypeStruct((B, T, D), f32), jax.ShapeDtypeStruct((B, H, 64, 64), f32)],
        scratch_shapes=[pltpu.VMEM((nb * WKV_PAIRS, 2 * C, 2 * C), f32)],
        compiler_params=pltpu.CompilerParams(dimension_semantics=("parallel", "arbitrary")),
        name="wkv7_chunked",
    )(r, lw, k, v, kk, a, s0)


D_MODEL = 1024
BATCH = 4
SEQ = 4096
DEPTH = 4
DEC_BATCH = 128
DEC_SEQ = 8
PAST_LEN = 2048
PAGE_SIZE = 128

N_NSA_LAYERS = (DEPTH + 1) // 2
N_RWKV_LAYERS = DEPTH // 2
N_VRES = N_RWKV_LAYERS - 1

POOL_DIM = D_MODEL // 2
POOL_WINDOWS = (2, 4, 8, 16)
POOL_GROUPS = len(POOL_WINDOWS)
POOL_GDIM = POOL_DIM // POOL_GROUPS
POOL_HIST = max(POOL_WINDOWS) - 1

HEAD_DIM = 64
NSA_HEADS = (D_MODEL // 2) // HEAD_DIM
NSA_KV_HEADS = 2
NSA_GQ = NSA_HEADS // NSA_KV_HEADS
NSA_DIM = NSA_HEADS * HEAD_DIM
CMP_STRIDE = 16
CMP_LEN = 2 * CMP_STRIDE
SLC_LEN = 64
N_SEL = 16
WINDOW = 512
Q_BLOCK = 128
ROPE_DIM = HEAD_DIM // 4
ROPE_THETA = 500000.0
MIX_DIM = POOL_DIM + NSA_DIM
KV_COLS = 6 * NSA_KV_HEADS * HEAD_DIM
IN_COLS = POOL_DIM + NSA_DIM + KV_COLS + 3 * NSA_HEADS

RWKV_N = 64
RWKV_HEADS = D_MODEL // RWKV_N
LORA_W = 64
LORA_A = 64
LORA_V = 32
LORA_G = 128
GN_EPS = 64e-5

MOE_GROUPS = 4
MOE_EPG = 4
N_EXPERTS = MOE_GROUPS * MOE_EPG
MOE_TOPK = 2
D_FF_E = 256

RMS_EPS = 1e-6
NEG_INF = -1e30
RES_SCALE = (2 * DEPTH) ** -0.5

SEL_TK = 512


def _nsa_prompt_kernel(q_ref, ql_ref, g_ref, kc_ref, kcl_ref, vct_ref, ks_ref, vst_ref, kw_ref, vwt_ref,
                       o_ref, score_scr, sel_scr, *, n_cmp):
    QB = Q_BLOCK
    GQ = NSA_GQ * QB
    i = pl.program_id(2)
    s0 = i * QB
    qT = q_ref[...]
    posq = s0 + lax.broadcasted_iota(jnp.int32, (1, GQ), 1) % QB
    dd = lambda x, y: jnp.dot(x, y, preferred_element_type=f32)

    ncp = kc_ref.shape[0]
    cidx = lax.broadcasted_iota(jnp.int32, (ncp, 1), 0)
    mask_c = cidx * CMP_STRIDE + (CMP_LEN - 1) <= posq
    sc = dd(kc_ref[...], qT) + (dd(kc_ref[...], ql_ref[...]) + dd(kcl_ref[...], qT))
    sc = jnp.where(mask_c, sc, NEG_INF)
    pe = jnp.exp(sc - jnp.max(sc, axis=0, keepdims=True))
    pc = jnp.where(mask_c, pe / jnp.sum(pe, axis=0, keepdims=True), 0.0)
    o_c = dd(vct_ref[...], pc.astype(bf16))

    imp = (pc[:, 0:QB] + pc[:, QB:2 * QB]) + (pc[:, 2 * QB:3 * QB] + pc[:, 3 * QB:4 * QB])
    n_slc = score_scr.shape[0]
    per = SLC_LEN // CMP_STRIDE
    nn = lax.broadcasted_iota(jnp.int32, (n_slc, ncp), 0) * per
    cc = lax.broadcasted_iota(jnp.int32, (n_slc, ncp), 1)
    mt = (0.5 * ((cc >= nn) & (cc < nn + per)).astype(f32)
          + 0.5 * ((cc + 1 >= nn) & (cc + 1 < nn + per)).astype(f32)).astype(bf16)
    i1 = imp.astype(bf16)
    r1 = imp - i1.astype(f32)
    i2 = r1.astype(bf16)
    i3 = (r1 - i2.astype(f32)).astype(bf16)
    imp_blk = dd(mt, i1) + (dd(mt, i2) + dd(mt, i3))
    nidx = lax.broadcasted_iota(jnp.int32, (n_slc, 1), 0)
    cur = (s0 + lax.broadcasted_iota(jnp.int32, (1, QB), 1)) // SLC_LEN
    forced = (nidx == 0) | (nidx == cur) | (nidx == cur - 1)
    score = jnp.where(nidx > cur, -1.0, jnp.where(forced, 1e6, imp_blk))
    score_scr[...] = score
    rank = jnp.zeros((n_slc, QB), jnp.int32)
    for m in range(n_slc):
        sm = score_scr[m:m + 1, :]
        tie = (nidx > m).astype(jnp.int32)
        rank = rank + jnp.where(sm > score, 1, jnp.where(sm == score, tie, 0))
    bias = jnp.where(rank < min(N_SEL, n_slc), 0.0, NEG_INF)
    sel_scr[...] = jnp.concatenate([bias] * NSA_GQ, axis=1)

    def online(carry, s, vt_blk):
        m, l, acc = carry
        m_new = jnp.maximum(m, jnp.max(s, axis=0, keepdims=True))
        alpha = jnp.exp(m - m_new)
        p = jnp.exp(s - m_new)
        l = alpha * l + jnp.sum(p, axis=0, keepdims=True)
        acc = alpha * acc + dd(vt_blk, p.astype(bf16))
        return m_new, l, acc

    init = (jnp.full((1, GQ), NEG_INF, f32), jnp.zeros((1, GQ), f32), jnp.zeros((HEAD_DIM, GQ), f32))

    bpt = SEL_TK // SLC_LEN

    def sel_scores(kt):
        k0 = pl.multiple_of(kt * SEL_TK, SEL_TK)
        rows = sel_scr[pl.ds(pl.multiple_of(kt * bpt, bpt), bpt), :]
        blk = jnp.concatenate([jnp.broadcast_to(rows[j:j + 1, :], (SLC_LEN, GQ)) for j in range(bpt)], axis=0)
        return k0, dd(ks_ref[pl.ds(k0, SEL_TK), :], qT) + blk

    def sel_body(kt, carry):
        k0, s = sel_scores(kt)
        return online(carry, s, vst_ref[:, pl.ds(k0, SEL_TK)])

    def sel_body2(kp, carry):
        k0a, sa = sel_scores(2 * kp)
        k0b, sb = sel_scores(2 * kp + 1)
        carry = online(carry, sa, vst_ref[:, pl.ds(k0a, SEL_TK)])
        return online(carry, sb, vst_ref[:, pl.ds(k0b, SEL_TK)])

    n_full = s0 // SEL_TK
    carry = lax.fori_loop(0, n_full // 2, sel_body2, init)
    carry = lax.fori_loop(2 * (n_full // 2), n_full, sel_body, carry)
    k0, s = sel_scores(n_full)
    kpos = k0 + lax.broadcasted_iota(jnp.int32, (SEL_TK, 1), 0)
    _, l_s, acc_s = online(carry, jnp.where(kpos <= posq, s, NEG_INF), vst_ref[:, pl.ds(k0, SEL_TK)])

    nwt = WINDOW // QB
    tiles = []
    for j in range(nwt + 1):
        k0 = s0 - WINDOW + j * QB
        k0c = pl.multiple_of(jnp.maximum(k0, 0), QB)
        s = dd(kw_ref[pl.ds(k0c, QB), :], qT)
        kpos = k0c + lax.broadcasted_iota(jnp.int32, (QB, 1), 0)
        if j == nwt:
            s = jnp.where(kpos <= posq, s, NEG_INF)
        else:
            if j == 0:
                s = jnp.where(posq - kpos < WINDOW, s, NEG_INF)
            s = s + jnp.where(k0 >= 0, 0.0, NEG_INF)
        tiles.append((k0c, s))
    m_w = functools.reduce(jnp.maximum, [jnp.max(s, axis=0, keepdims=True) for _, s in tiles])
    l_w = jnp.zeros((1, GQ), f32)
    acc_w = jnp.zeros((HEAD_DIM, GQ), f32)
    for k0c, s in tiles:
        p = jnp.exp(s - m_w)
        l_w = l_w + jnp.sum(p, axis=0, keepdims=True)
        acc_w = acc_w + dd(vwt_ref[:, pl.ds(k0c, QB)], p.astype(bf16))

    g = jax.nn.sigmoid(g_ref[...])
    o = g[0:1] * o_c + g[1:2] * (acc_s / l_s) + g[2:3] * (acc_w / l_w)
    o_ref[...] = jnp.concatenate([o[:, j * QB:(j + 1) * QB].T for j in range(NSA_GQ)], axis=1)


def nsa_prompt_pallas(ops, kc, vc):
    qT, qTl, gT, ks, vst, kw, vwt = ops
    B, KVH, T, D = ks.shape
    G, QB = NSA_GQ, Q_BLOCK
    assert T % SEL_TK == 0 and T % QB == 0
    nqb = T // QB
    n_cmp = kc.shape[1]
    ncp = -(-n_cmp // 128) * 128
    n_slc = T // SLC_LEN
    kcp, kcl = _split(jnp.pad(kc, ((0, 0), (0, ncp - n_cmp), (0, 0), (0, 0))).transpose(0, 2, 1, 3))
    vct = jnp.pad(vc, ((0, 0), (0, ncp - n_cmp), (0, 0), (0, 0))).transpose(0, 2, 3, 1).astype(bf16)
    bh = lambda *shape: pl.BlockSpec((None, None) + shape, lambda b, h, i: (b, h) + (0,) * len(shape))
    bhi = lambda *shape: pl.BlockSpec((None, None, None) + shape, lambda b, h, i: (b, h, i) + (0,) * len(shape))
    return pl.pallas_call(
        functools.partial(_nsa_prompt_kernel, n_cmp=n_cmp),
        grid=(B, KVH, nqb),
        in_specs=[bhi(D, G * QB), bhi(D, G * QB), bhi(3, G * QB), bh(ncp, D), bh(ncp, D), bh(D, ncp),
                  bh(T, D), bh(D, T), bh(T, D), bh(D, T)],
        out_specs=pl.BlockSpec((None, QB, G * D), lambda b, h, i: (b, i, h)),
        out_shape=jax.ShapeDtypeStruct((B, T, KVH * G * D), f32),
        scratch_shapes=[pltpu.VMEM((n_slc, QB), f32), pltpu.VMEM((n_slc, G * QB), f32)],
        compiler_params=pltpu.CompilerParams(dimension_semantics=("parallel", "parallel", "arbitrary"),
                                             vmem_limit_bytes=48 * 1024 * 1024),
        name="nsa_prompt",
    )(qT, qTl, gT, kcp, kcl, vct, ks, vst, kw, vwt)


def _split3(x):
    h1 = x.astype(bf16)
    r1 = x - h1.astype(f32)
    h2 = r1.astype(bf16)
    return h1, h2, (r1 - h2.astype(f32)).astype(bf16)


def _dot_exact_rhs(x, m):
    d = lambda a: jnp.dot(a, m, preferred_element_type=f32)
    h1, h2, h3 = _split3(x)
    return d(h1) + (d(h2) + d(h3))


def _dot_exact_lhs(m, x):
    d = lambda a: jnp.dot(m, a, preferred_element_type=f32)
    h1, h2, h3 = _split3(x)
    return d(h1) + (d(h2) + d(h3))


def _nsa_sample_kernel(pt_ref, *refs, n_pages, ts, past_len):
    pages = refs[:n_pages]
    (new_ref, wbuf_ref, wnew_ref, qh_ref, ql_ref, gate_ref, wa_ref, wb_ref, phik_ref, phiv_ref,
     gain_ref, cos_ref, sin_ref, o_ref, ssel_scr, a_scr, b_scr, score_scr) = refs[n_pages:]
    P = PAGE_SIZE
    KV = NSA_KV_HEADS * HEAD_DIM
    NCOL = NSA_KV_HEADS * NSA_GQ * ts
    NQ = NSA_KV_HEADS * ts
    cpp = P // CMP_STRIDE
    n_chunk = (past_len + SLC_LEN) // CMP_STRIDE
    n_cmp = n_chunk - 1
    ncp = a_scr.shape[0]
    n_slc = (past_len + SLC_LEN) // SLC_LEN
    nsp = score_scr.shape[0]
    dd = lambda x, y: jnp.dot(x, y, preferred_element_type=f32)
    qh = qh_ref[...]
    ql = ql_ref[...]
    col = lax.broadcasted_iota(jnp.int32, (1, NCOL), 1)
    t_col = col % ts
    zpad = jnp.zeros((P - ts, 4 * KV), f32)
    new_tile = jnp.concatenate([new_ref[...], zpad], axis=0)

    wa = wa_ref[...]
    wb = wb_ref[...]
    a_scr[...] = jnp.zeros(a_scr.shape, f32)
    b_scr[...] = jnp.zeros(b_scr.shape, f32)
    def slab(j, kind):
        if j == n_pages:
            return new_tile[:, kind * KV:(kind + 1) * KV]
        per_row = 4 * NSA_KV_HEADS
        return jnp.concatenate([pages[j][pl.ds(kind * NSA_KV_HEADS + h, P, stride=per_row), :]
                                for h in range(NSA_KV_HEADS)], axis=1)

    for j in range(n_pages + 1):
        xc = jnp.concatenate([slab(j, 0), slab(j, 1)], axis=1)
        a_scr[j * cpp:(j + 1) * cpp, :] = (xc * wa).reshape(cpp, CMP_STRIDE, 2 * KV).sum(axis=1)
        b_scr[j * cpp:(j + 1) * cpp, :] = (xc * wb).reshape(cpp, CMP_STRIDE, 2 * KV).sum(axis=1)
        ssel_scr[j * P:(j + 1) * P, :] = dd(slab(j, 2).astype(bf16), qh)

    mean = a_scr[...] + pltpu.roll(b_scr[...], ncp - 1, axis=0)
    kc = _mmul(mean[:, 0:KV], phik_ref[...], 3)
    vc = _mmul(mean[:, KV:2 * KV], phiv_ref[...], 3)
    r_i = lax.broadcasted_iota(jnp.int32, (KV, KV), 0)
    c_i = lax.broadcasted_iota(jnp.int32, (KV, KV), 1)
    same_head = (r_i // HEAD_DIM) == (c_i // HEAD_DIM)
    mavg = jnp.where(same_head, 1.0 / HEAD_DIM, 0.0).astype(bf16)
    kc = kc * lax.rsqrt(_dot_exact_rhs(kc * kc, mavg) + RMS_EPS) * gain_ref[...]
    half = ROPE_DIM // 2
    rd, cd = r_i % HEAD_DIM, c_i % HEAD_DIM
    rot = jnp.where(same_head & (cd < half) & (rd == cd + half), -1.0,
                    jnp.where(same_head & (cd >= half) & (cd < ROPE_DIM) & (rd == cd - half), 1.0, 0.0)).astype(bf16)
    kc = kc * cos_ref[...] + _dot_exact_rhs(kc, rot) * sin_ref[...]

    kch, kcl = _split(kc)
    sc = dd(kch, qh) + (dd(kch, ql) + dd(kcl, qh))
    cidx = lax.broadcasted_iota(jnp.int32, (ncp, 1), 0)
    mask_c = (cidx * CMP_STRIDE + (CMP_LEN - 1) <= past_len + t_col) & (cidx < n_cmp)
    sc = jnp.where(mask_c, sc, NEG_INF)
    pe = jnp.exp(sc - jnp.max(sc, axis=0, keepdims=True))
    pc = jnp.where(mask_c, pe / jnp.sum(pe, axis=0, keepdims=True), 0.0)
    o_c = dd(pc.T.astype(bf16), vc.astype(bf16))

    gr = lax.broadcasted_iota(jnp.int32, (NCOL, NQ), 0)
    gc = lax.broadcasted_iota(jnp.int32, (NCOL, NQ), 1)
    gsum = ((gr // (NSA_GQ * ts) == gc // ts) & (gr % ts == gc % ts)).astype(bf16)
    imp = _dot_exact_rhs(pc, gsum)
    per = SLC_LEN // CMP_STRIDE
    nn = lax.broadcasted_iota(jnp.int32, (nsp, ncp), 0) * per
    cc = lax.broadcasted_iota(jnp.int32, (nsp, ncp), 1)
    mt = (0.5 * ((cc >= nn) & (cc < nn + per)).astype(f32)
          + 0.5 * ((cc + 1 >= nn) & (cc + 1 < nn + per)).astype(f32)).astype(bf16)
    imp_blk = _dot_exact_lhs(mt, imp)
    nidx = lax.broadcasted_iota(jnp.int32, (nsp, 1), 0)
    cur = (past_len + lax.broadcasted_iota(jnp.int32, (1, NQ), 1) % ts) // SLC_LEN
    forced = (nidx == 0) | (nidx == cur) | (nidx == cur - 1)
    score = jnp.where(nidx >= n_slc, -2.0, jnp.where(nidx > cur, -1.0, jnp.where(forced, 1e6, imp_blk)))
    score_scr[...] = score
    rank = jnp.zeros((nsp, NQ), jnp.int32)
    for m in range(n_slc):
        sm = score_scr[m:m + 1, :]
        beats = (sm > score) | ((sm == score) & (nidx > m))
        rank = rank + beats.astype(jnp.int32)
    sel = (rank < min(N_SEL, n_slc)).astype(bf16)
    gr2 = lax.broadcasted_iota(jnp.int32, (NQ, NCOL), 0)
    gc2 = lax.broadcasted_iota(jnp.int32, (NQ, NCOL), 1)
    gexp = ((gc2 // (NSA_GQ * ts) == gr2 // ts) & (gc2 % ts == gr2 % ts)).astype(bf16)
    sel_c = dd(sel, gexp)

    def two_pass(n_tiles, score_tile, mask_tile, v_tile):
        sm = [jnp.where(mask_tile(j), score_tile(j), NEG_INF) for j in range(n_tiles)]
        m = functools.reduce(jnp.maximum, [jnp.max(s, axis=0, keepdims=True) for s in sm])
        num = jnp.zeros((NCOL, KV), f32)
        den = jnp.zeros((NCOL, KV), f32)
        ones = jnp.ones((P, KV), bf16)
        for j in range(n_tiles):
            p = jnp.exp(sm[j] - m).T.astype(bf16)
            num = num + dd(p, v_tile(j))
            den = den + dd(p, ones)
        return num, den

    bpp = P // SLC_LEN
    row = lax.broadcasted_iota(jnp.int32, (P, 1), 0)

    def sel_mask(j):
        blk = jnp.concatenate([jnp.broadcast_to(sel_c[j * bpp + i:j * bpp + i + 1, :], (SLC_LEN, NCOL))
                               for i in range(bpp)], axis=0)
        return (blk > 0.5) & (j * P + row <= past_len + t_col)

    num_s, den_s = two_pass(
        n_pages + 1, lambda j: ssel_scr[j * P:(j + 1) * P, :], sel_mask,
        lambda j: slab(j, 3).astype(bf16))

    wrow = 2 * NSA_KV_HEADS
    lb = wbuf_ref.shape[0] // wrow
    nwt = lb // P
    wnew = jnp.concatenate([wnew_ref[...], jnp.zeros((P - ts, 2 * KV), f32)], axis=0)

    def w_tile(j):
        if j == nwt:
            return wnew
        return jnp.concatenate([wbuf_ref[pl.ds(j * P * wrow + s, P, stride=wrow), :] for s in range(wrow)], axis=1)

    def win_mask(j):
        pos_w = (past_len - lb + j * P + row) if j < nwt else (past_len + row)
        dq = past_len + t_col - pos_w
        return (dq >= 0) & (dq < WINDOW) & (pos_w >= 0) & ((row < ts) | (j < nwt))

    num_w, den_w = two_pass(
        nwt + 1, lambda j: dd(w_tile(j)[:, 0:KV].astype(bf16), qh), win_mask,
        lambda j: w_tile(j)[:, KV:2 * KV].astype(bf16))

    g = jax.nn.sigmoid(gate_ref[...])
    o_ref[...] = g[0] * o_c + g[1] * (num_s / den_s) + g[2] * (num_w / den_w)


def nsa_sample_pallas(q, gl, rows_new, win_new, pools, page_table, win_bufs, layer, kc_w, phi, k_gain):
    B, ts = q.shape[:2]
    KVH, G, D, P = NSA_KV_HEADS, NSA_GQ, HEAD_DIM, PAGE_SIZE
    KV = KVH * D
    n_pages = page_table.shape[1]
    past_len = n_pages * P
    n_pool = pools.shape[1]
    assert pools.shape[2] == P and ts <= SLC_LEN and P % SLC_LEN == 0
    lb = win_bufs.shape[2]
    assert lb % P == 0
    NCOL = KVH * G * ts
    n_chunk = (past_len + SLC_LEN) // CMP_STRIDE
    ncp = -(-n_chunk // 8) * 8
    n_slc = (past_len + SLC_LEN) // SLC_LEN
    nsp = -(-n_slc // 8) * 8
    qs = (q * D ** -0.5).reshape(B, ts, KVH, G, D).transpose(0, 2, 4, 3, 1).reshape(B, KVH, D, G * ts)
    z = jnp.zeros_like(qs[:, 0])
    qbd = jnp.concatenate([jnp.concatenate([qs[:, 0], z], axis=2), jnp.concatenate([z, qs[:, 1]], axis=2)], axis=1)
    qh, ql = _split(qbd)
    gate = gl.reshape(B, ts, KVH, G, 3).transpose(0, 4, 2, 3, 1).reshape(B, 3, NCOL, 1)
    gate = jnp.broadcast_to(gate, (B, 3, NCOL, KV)).astype(f32)
    w_lane = jnp.repeat(kc_w.reshape(2 * KVH, CMP_LEN), D, axis=0)
    reps = P // CMP_STRIDE
    wa = jnp.tile(w_lane[:, :CMP_STRIDE].T, (reps, 1)).astype(f32)
    wb = jnp.tile(w_lane[:, CMP_STRIDE:].T, (reps, 1)).astype(f32)
    zz = jnp.zeros((D, D), f32)
    bdiag = lambda m: jnp.concatenate([jnp.concatenate([m, zz], axis=1), jnp.concatenate([zz, m], axis=1)], axis=0)
    phik, phiv = bdiag(phi[0].astype(f32)), bdiag(phi[1].astype(f32))
    gain = jnp.tile(k_gain.astype(f32), KVH).reshape(1, KV)
    half = ROPE_DIM // 2
    inv = ROPE_THETA ** (-jnp.arange(half, dtype=f32) / half)
    cmp_end = (jnp.arange(ncp, dtype=jnp.int32) * CMP_STRIDE + (CMP_LEN - 1)).astype(f32)
    ang = cmp_end[:, None] * inv
    cos_h = jnp.concatenate([jnp.cos(ang), jnp.cos(ang), jnp.ones((ncp, D - ROPE_DIM), f32)], axis=1)
    sin_h = jnp.concatenate([jnp.sin(ang), jnp.sin(ang), jnp.zeros((ncp, D - ROPE_DIM), f32)], axis=1)
    cos_t, sin_t = jnp.tile(cos_h, (1, KVH)), jnp.tile(sin_h, (1, KVH))
    pool2 = pools.reshape(pools.shape[0] * n_pool, P * 4 * KVH, D)
    new2 = rows_new.reshape(B, ts, 4 * KV).astype(f32)
    wbuf2 = win_bufs.reshape(win_bufs.shape[0] * B, lb * 2 * KVH, D)
    wnew2 = win_new.reshape(B, ts, 2 * KV).astype(f32)
    page_spec = lambda j: pl.BlockSpec((None, P * 4 * KVH, D),
                                       lambda b, pt, j=j: (layer * n_pool + pt[b, j], 0, 0))
    per_b = lambda *s: pl.BlockSpec((None,) + s, lambda b, pt: (b,) + (0,) * len(s))
    wbuf_spec = pl.BlockSpec((None, lb * 2 * KVH, D), lambda b, pt: (layer * B + b, 0, 0))
    const = lambda *s: pl.BlockSpec(s, lambda b, pt: (0,) * len(s))
    grid_spec = pltpu.PrefetchScalarGridSpec(
        num_scalar_prefetch=1, grid=(B,),
        in_specs=[page_spec(j) for j in range(n_pages)] + [
            per_b(ts, 4 * KV), wbuf_spec, per_b(ts, 2 * KV), per_b(KV, NCOL), per_b(KV, NCOL),
            per_b(3, NCOL, KV), const(P, 2 * KV), const(P, 2 * KV), const(KV, KV), const(KV, KV),
            const(1, KV), const(ncp, KV), const(ncp, KV)],
        out_specs=per_b(NCOL, KV),
        scratch_shapes=[pltpu.VMEM(((n_pages + 1) * P, NCOL), f32), pltpu.VMEM((ncp, 2 * KV), f32),
                        pltpu.VMEM((ncp, 2 * KV), f32), pltpu.VMEM((nsp, KVH * ts), f32)])
    out = pl.pallas_call(
        functools.partial(_nsa_sample_kernel, n_pages=n_pages, ts=ts, past_len=past_len),
        grid_spec=grid_spec,
        out_shape=jax.ShapeDtypeStruct((B, NCOL, KV), f32),
        compiler_params=pltpu.CompilerParams(dimension_semantics=("arbitrary",),
                                             vmem_limit_bytes=48 * 1024 * 1024),
        name="nsa_sample",
    )(page_table, *([pool2] * n_pages), new2, wbuf2, wnew2, qh, ql, gate, wa, wb, phik, phiv, gain, cos_t, sin_t)
    o4 = out.reshape(B, KVH, G, ts, KVH, D)
    o = jnp.stack([o4[:, 0, :, :, 0], o4[:, 1, :, :, 1]], axis=1)
    return o.transpose(0, 3, 1, 2, 4).reshape(B, ts, KVH * G * D)


MOE_TM = 512
ROUTER_LANES = 128


def _moe_kernel(x_ref, g_ref, wrh_ref, wrl_ref, br_ref, wg_ref, wu_ref, wd_ref, o_ref,
                h_scr, gate_scr, acc_scr):
    grp = pl.program_id(1)
    dd = lambda a, b: jnp.dot(a, b, preferred_element_type=f32)
    tm = x_ref.shape[0]
    lane = lax.broadcasted_iota(jnp.int32, (tm, ROUTER_LANES), 1).astype(f32)
    far = float(ROUTER_LANES)

    @pl.when(grp == 0)
    def _():
        x = x_ref[...]
        h = x * lax.rsqrt(jnp.mean(x * x, axis=-1, keepdims=True) + RMS_EPS) * g_ref[...]
        hh, hl = _split(h)
        h_scr[...] = hh
        logits = dd(hh, wrh_ref[...]) + (dd(hh, wrl_ref[...]) + dd(hl, wrh_ref[...])) + br_ref[...]
        is_c = lane < MOE_GROUPS
        lc = jnp.where(is_c, logits, NEG_INF)
        mc = jnp.max(lc, axis=1, keepdims=True)
        g_idx = jnp.min(jnp.where(lc == mc, lane, far), axis=1, keepdims=True)
        g_w = 1.0 / jnp.sum(jnp.where(is_c, jnp.exp(lc - mc), 0.0), axis=1, keepdims=True)
        lo = MOE_GROUPS + MOE_EPG * g_idx
        lf = jnp.where((lane >= lo) & (lane < lo + MOE_EPG), logits, NEG_INF)
        v1 = jnp.max(lf, axis=1, keepdims=True)
        i1 = jnp.min(jnp.where(lf == v1, lane, far), axis=1, keepdims=True)
        lf2 = jnp.where(lane == i1, NEG_INF, lf)
        v2 = jnp.max(lf2, axis=1, keepdims=True)
        i2 = jnp.min(jnp.where(lf2 == v2, lane, far), axis=1, keepdims=True)
        e21 = jnp.exp(v2 - v1)
        w1 = g_w / (1.0 + e21)
        gate_scr[...] = jnp.where(lane == i1, w1, jnp.where(lane == i2, e21 * w1, 0.0))
        acc_scr[...] = x

    h = h_scr[...]
    hg = dd(h, wg_ref[...])
    hu = dd(h, wu_ref[...])
    gate = gate_scr[...]
    first = (MOE_GROUPS + MOE_EPG * grp).astype(f32)
    cols = []
    for e in range(MOE_EPG):
        ge = jnp.sum(jnp.where(lane == first + e, gate, 0.0), axis=1, keepdims=True)
        sl = slice(e * D_FF_E, (e + 1) * D_FF_E)
        hge = hg[:, sl]
        cols.append((hge * jax.nn.sigmoid(hge) * hu[:, sl] * ge).astype(bf16))
    acc_scr[...] += dd(jnp.concatenate(cols, axis=1), wd_ref[...])

    @pl.when(grp == MOE_GROUPS - 1)
    def _():
        o_ref[...] = acc_scr[...]


def moe_prep(g, wc, bc, wf, bf, wg, wu, wd):
    d = wc.shape[0]
    pad = ROUTER_LANES - MOE_GROUPS - N_EXPERTS
    wr = jnp.pad(jnp.concatenate([wc, wf], axis=1).astype(f32), ((0, 0), (0, pad)))
    wrh, wrl = _split(wr)
    br = jnp.pad(jnp.concatenate([bc, bf]).astype(f32), (0, pad)).reshape(1, ROUTER_LANES)
    regroup = lambda w: (w.reshape(MOE_GROUPS, MOE_EPG, d, D_FF_E).transpose(0, 2, 1, 3)
                         .reshape(MOE_GROUPS, d, MOE_EPG * D_FF_E).astype(bf16))
    wdg = wd.reshape(MOE_GROUPS, MOE_EPG * D_FF_E, d).astype(bf16)
    return g.reshape(1, d).astype(f32), wrh, wrl, br, regroup(wg), regroup(wu), wdg


def moe_residual(x, prep):
    g, wrh, wrl, br, wgg, wug, wdg = prep
    shp = x.shape
    d = shp[-1]
    x2 = x.reshape(-1, d)
    m = x2.shape[0]
    tm = MOE_TM
    assert m % tm == 0
    gf = MOE_EPG * D_FF_E
    full = lambda r, c: pl.BlockSpec((r, c), lambda i, j: (0, 0))
    out = pl.pallas_call(
        _moe_kernel,
        grid=(m // tm, MOE_GROUPS),
        in_specs=[pl.BlockSpec((tm, d), lambda i, j: (i, 0)), full(1, d),
                  full(d, ROUTER_LANES), full(d, ROUTER_LANES), full(1, ROUTER_LANES),
                  pl.BlockSpec((None, d, gf), lambda i, j: (j, 0, 0)),
                  pl.BlockSpec((None, d, gf), lambda i, j: (j, 0, 0)),
                  pl.BlockSpec((None, gf, d), lambda i, j: (j, 0, 0))],
        out_specs=pl.BlockSpec((tm, d), lambda i, j: (i, 0)),
        out_shape=jax.ShapeDtypeStruct((m, d), f32),
        scratch_shapes=[pltpu.VMEM((tm, d), bf16), pltpu.VMEM((tm, ROUTER_LANES), f32),
                        pltpu.VMEM((tm, d), f32)],
        compiler_params=pltpu.CompilerParams(dimension_semantics=("parallel", "arbitrary"),
                                             vmem_limit_bytes=48 * 1024 * 1024),
        name="moe",
    )(x2, g, wrh, wrl, br, wgg, wug, wdg)
    return out.reshape(shp)


def rmsnorm(x, g):
    xf = x.astype(jnp.float32)
    y = xf * lax.rsqrt(jnp.mean(xf * xf, axis=-1, keepdims=True) + RMS_EPS)
    return (y * g.astype(jnp.float32)).astype(x.dtype)


def rope_partial(x, pos):
    half = ROPE_DIM // 2
    inv = ROPE_THETA ** (-jnp.arange(half, dtype=jnp.float32) / half)
    ang = pos.astype(jnp.float32)[:, None] * inv
    cos = jnp.cos(ang)[:, None, :]
    sin = jnp.sin(ang)[:, None, :]
    xf = x.astype(jnp.float32)
    x1 = xf[..., :half]
    x2 = xf[..., half:ROPE_DIM]
    out = jnp.concatenate([x1 * cos - x2 * sin, x2 * cos + x1 * sin, xf[..., ROPE_DIM:]], axis=-1)
    return out.astype(x.dtype)


def pool_mix(u, hist, p0, w_grp, scale):
    b, t, _ = u.shape
    ext = jnp.concatenate([hist.astype(u.dtype), u], axis=1).astype(jnp.float32)
    cs = jnp.pad(jnp.cumsum(ext, axis=1), ((0, 0), (1, 0), (0, 0)))
    cnt_pos = p0 + jnp.arange(t, dtype=jnp.int32) + 1
    means = []
    for gi, w in enumerate(POOL_WINDOWS):
        c = cs[..., gi * POOL_GDIM:(gi + 1) * POOL_GDIM]
        win_sum = c[:, POOL_HIST + 1:POOL_HIST + 1 + t] - c[:, POOL_HIST + 1 - w:POOL_HIST + 1 - w + t]
        cnt = jnp.minimum(cnt_pos, w).astype(jnp.float32)[None, :, None]
        means.append(win_sum / cnt)
    mean = jnp.stack(means, axis=2)
    d = mean - u.reshape(b, t, POOL_GROUPS, POOL_GDIM).astype(jnp.float32)
    y = jnp.einsum('btgc,gcd->btgd', d, w_grp.astype(jnp.float32)).reshape(b, t, POOL_DIM)
    return (y * scale.astype(jnp.float32)).astype(u.dtype)


PROJ_TM = 512


def _norm_mm_kernel(x_ref, g_ref, w_ref, o_ref):
    x = x_ref[...]
    h = x * lax.rsqrt(jnp.mean(x * x, axis=-1, keepdims=True) + RMS_EPS) * g_ref[...]
    o_ref[...] = jnp.dot(h.astype(bf16), w_ref[...], preferred_element_type=f32)


def _norm_mm(x, g, w):
    lead, d = x.shape[:-1], x.shape[-1]
    n = w.shape[1]
    x2 = x.reshape(-1, d)
    m = x2.shape[0]
    npad = -(-n // 128) * 128
    wb = jnp.pad(w.astype(bf16), ((0, 0), (0, npad - n)))
    tm = PROJ_TM
    assert m % tm == 0
    out = pl.pallas_call(
        _norm_mm_kernel, grid=(m // tm,),
        in_specs=[pl.BlockSpec((tm, d), lambda i: (i, 0)), pl.BlockSpec((1, d), lambda i: (0, 0)),
                  pl.BlockSpec((d, npad), lambda i: (0, 0))],
        out_specs=pl.BlockSpec((tm, npad), lambda i: (i, 0)),
        out_shape=jax.ShapeDtypeStruct((m, npad), f32),
        compiler_params=pltpu.CompilerParams(dimension_semantics=("parallel",),
                                             vmem_limit_bytes=48 * 1024 * 1024),
        name="norm_mm",
    )(x2, g.reshape(1, d).astype(f32), wb)
    return out[:, :n].reshape(lead + (n,))


def _mix_out_kernel(a_ref, b_ref, wa_ref, wb_ref, x_ref, o_ref):
    dd = lambda p, q: jnp.dot(p.astype(bf16), q, preferred_element_type=f32)
    o_ref[...] = x_ref[...] + (dd(a_ref[...], wa_ref[...]) + dd(b_ref[...], wb_ref[...]))


def _mix_out(a, b, w, x):
    d = x.shape[-1]
    ka, kb = a.shape[-1], b.shape[-1]
    x2 = x.reshape(-1, d)
    m = x2.shape[0]
    tm = PROJ_TM
    assert m % tm == 0
    wbf = w.astype(bf16)
    tile = lambda c: pl.BlockSpec((tm, c), lambda i: (i, 0))
    full = lambda r, c: pl.BlockSpec((r, c), lambda i: (0, 0))
    out = pl.pallas_call(
        _mix_out_kernel, grid=(m // tm,),
        in_specs=[tile(ka), tile(kb), full(ka, d), full(kb, d), tile(d)], out_specs=tile(d),
        out_shape=jax.ShapeDtypeStruct((m, d), f32),
        compiler_params=pltpu.CompilerParams(dimension_semantics=("parallel",)),
        name="mix_out",
    )(a.reshape(m, ka), b.reshape(m, kb), wbf[:ka], wbf[ka:], x2)
    return out.reshape(x.shape)


def _ab_feat_kernel(x_ref, g_ref, w_ref, qg_ref, kg_ref, cos_ref, sin_ref, hs_ref, he_ref, rot_ref,
                    pool_ref, rows_ref, win_ref, qh_ref, ql_ref, gt_ref, ks_ref, kw_ref, vst_ref, vwt_ref):
    D, KV = HEAD_DIM, NSA_KV_HEADS * HEAD_DIM
    x = x_ref[...]
    h = x * lax.rsqrt(jnp.mean(x * x, axis=-1, keepdims=True) + RMS_EPS) * g_ref[...]
    u = jnp.dot(h.astype(bf16), w_ref[...], preferred_element_type=f32)
    off_kv = POOL_DIM + NSA_DIM
    pool_ref[...] = u[:, :POOL_DIM]
    cos, sin = cos_ref[...], sin_ref[...]

    def norm_rope(z, gain):
        n = z.shape[1] // D
        hs, he, rot = hs_ref[0:n * D, :], he_ref[:, 0:n * D], rot_ref[0:n * D, 0:n * D]
        ms = _dot_exact_rhs(_dot_exact_rhs(z * z, hs), he) * (1.0 / D)
        zn = z * lax.rsqrt(ms + RMS_EPS) * gain
        wide = lambda t: jnp.concatenate([t] * (n // 2), axis=1)
        return zn * wide(cos) + _dot_exact_rhs(zn, rot) * wide(sin)

    q = norm_rope(u[:, POOL_DIM:off_kv], qg_ref[...]) * (D ** -0.5)
    for kvh in range(NSA_KV_HEADS):
        qt = jnp.concatenate([q[:, (kvh * NSA_GQ + g) * D:(kvh * NSA_GQ + g + 1) * D].T
                              for g in range(NSA_GQ)], axis=1)
        hi, lo = _split(qt)
        qh_ref[kvh] = hi
        ql_ref[kvh] = lo
    kv = u[:, off_kv:off_kv + KV_COLS]
    slab = lambda i: kv[:, i * KV:(i + 1) * KV]
    kr = norm_rope(jnp.concatenate([slab(2), slab(4)], axis=1), kg_ref[...])
    k_slc, k_win = kr[:, 0:KV], kr[:, KV:2 * KV]
    rows_ref[...] = jnp.concatenate([slab(0), slab(1), k_slc, slab(3)], axis=1)
    win_ref[...] = jnp.concatenate([k_win, slab(5)], axis=1)
    for kvh in range(NSA_KV_HEADS):
        hsl = slice(kvh * D, (kvh + 1) * D)
        ks_ref[kvh] = k_slc[:, hsl].astype(bf16)
        kw_ref[kvh] = k_win[:, hsl].astype(bf16)
        vst_ref[kvh] = slab(3)[:, hsl].T.astype(bf16)
        vwt_ref[kvh] = slab(5)[:, hsl].T.astype(bf16)
    gt_ref[...] = u[:, off_kv + KV_COLS:].T


def ab_features_prompt(x, norm_g, w_in, q_norm, k_norm):
    B, T, d = x.shape
    QB, D, KVH, G = Q_BLOCK, HEAD_DIM, NSA_KV_HEADS, NSA_GQ
    KV = KVH * D
    assert T % QB == 0
    nqb = T // QB
    npad = -(-IN_COLS // 128) * 128
    assert npad - (POOL_DIM + NSA_DIM + KV_COLS) == 128
    wb = jnp.pad(w_in.astype(bf16), ((0, 0), (0, npad - IN_COLS)))
    half = ROPE_DIM // 2
    inv = ROPE_THETA ** (-jnp.arange(half, dtype=f32) / half)
    ang = jnp.arange(T, dtype=jnp.int32).astype(f32)[:, None] * inv
    cos_h = jnp.concatenate([jnp.cos(ang), jnp.cos(ang), jnp.ones((T, D - ROPE_DIM), f32)], axis=1)
    sin_h = jnp.concatenate([jnp.sin(ang), jnp.sin(ang), jnp.zeros((T, D - ROPE_DIM), f32)], axis=1)
    cos_t, sin_t = jnp.tile(cos_h, (1, 2)), jnp.tile(sin_h, (1, 2))
    r_i = lax.broadcasted_iota(jnp.int32, (NSA_DIM, NSA_DIM), 0)
    c_i = lax.broadcasted_iota(jnp.int32, (NSA_DIM, NSA_DIM), 1)
    same = (r_i // D) == (c_i // D)
    rd, cd = r_i % D, c_i % D
    rot = jnp.where(same & (cd < half) & (rd == cd + half), -1.0,
                    jnp.where(same & (cd >= half) & (cd < ROPE_DIM) & (rd == cd - half), 1.0, 0.0)).astype(bf16)
    hs = (lax.broadcasted_iota(jnp.int32, (NSA_DIM, 128), 0) // D
          == lax.broadcasted_iota(jnp.int32, (NSA_DIM, 128), 1)).astype(bf16)
    qg = jnp.tile(q_norm.astype(f32), NSA_HEADS).reshape(1, NSA_DIM)
    kg = jnp.concatenate([jnp.tile(k_norm[1].astype(f32), KVH), jnp.tile(k_norm[2].astype(f32), KVH)]).reshape(1, 2 * KV)
    full = lambda a: pl.BlockSpec(a.shape, lambda b, i: (0,) * a.ndim)
    tok = lambda c: pl.BlockSpec((None, QB, c), lambda b, i: (b, i, 0))
    f = jax.ShapeDtypeStruct
    outs = pl.pallas_call(
        _ab_feat_kernel, grid=(B, nqb),
        in_specs=[tok(d), pl.BlockSpec((1, d), lambda b, i: (0, 0)), full(wb), full(qg), full(kg),
                  pl.BlockSpec((QB, 2 * D), lambda b, i: (i, 0)), pl.BlockSpec((QB, 2 * D), lambda b, i: (i, 0)),
                  full(hs), pl.BlockSpec((128, NSA_DIM), lambda b, i: (0, 0)), full(rot)],
        out_specs=[tok(POOL_DIM), tok(4 * KV), tok(2 * KV),
                   pl.BlockSpec((None, KVH, None, D, G * QB), lambda b, i: (b, 0, i, 0, 0)),
                   pl.BlockSpec((None, KVH, None, D, G * QB), lambda b, i: (b, 0, i, 0, 0)),
                   pl.BlockSpec((None, None, 128, QB), lambda b, i: (b, i, 0, 0)),
                   pl.BlockSpec((None, KVH, QB, D), lambda b, i: (b, 0, i, 0)),
                   pl.BlockSpec((None, KVH, QB, D), lambda b, i: (b, 0, i, 0)),
                   pl.BlockSpec((None, KVH, D, QB), lambda b, i: (b, 0, 0, i)),
                   pl.BlockSpec((None, KVH, D, QB), lambda b, i: (b, 0, 0, i))],
        out_shape=[f((B, T, POOL_DIM), f32), f((B, T, 4 * KV), f32), f((B, T, 2 * KV), f32),
                   f((B, KVH, nqb, D, G * QB), bf16), f((B, KVH, nqb, D, G * QB), bf16),
                   f((B, nqb, 128, QB), f32), f((B, KVH, T, D), bf16), f((B, KVH, T, D), bf16),
                   f((B, KVH, D, T), bf16), f((B, KVH, D, T), bf16)],
        compiler_params=pltpu.CompilerParams(dimension_semantics=("parallel", "parallel"),
                                             vmem_limit_bytes=48 * 1024 * 1024),
        name="ab_feat",
    )(x, norm_g.reshape(1, d).astype(f32), wb, qg, kg, cos_t, sin_t, hs, hs.T, rot)
    pool_in, rows, win, qh, ql, gt, ks, kw, vst, vwt = outs
    gT = (gt[:, :, :NSA_HEADS * 3].reshape(B, nqb, KVH, G, 3, QB).transpose(0, 2, 1, 4, 3, 5)
          .reshape(B, KVH, nqb, 3, G * QB))
    return pool_in, rows, win, (qh, ql, gT, ks, vst, kw, vwt)


def _pool_mix_out_kernel(u_ref, halo_ref, nsa_ref, x_ref, band_ref, wg_ref, sc_ref, wa_ref, wb_ref, o_ref,
                         *, seq_len):
    tm = u_ref.shape[0]
    hrows = halo_ref.shape[0]
    gd = POOL_GDIM
    tile = pl.program_id(0) % (seq_len // tm)
    u = u_ref[...]
    halo = jnp.where(tile == 0, 0.0, halo_ref[...])
    ext = jnp.concatenate([halo, u], axis=0)
    pos1 = tile * tm + lax.broadcasted_iota(jnp.int32, (tm, 1), 0) + 1
    outs = []
    for gi, w in enumerate(POOL_WINDOWS):
        sl = slice(gi * gd, (gi + 1) * gd)
        win_sum = _dot_exact_lhs(band_ref[gi], ext[:, sl])
        d = win_sum / jnp.minimum(pos1, w).astype(f32) - u[:, sl]
        outs.append(jnp.dot(d.astype(bf16), wg_ref[gi], preferred_element_type=f32))
    pool_out = (jnp.concatenate(outs, axis=1) * sc_ref[...]).astype(bf16)
    dd = lambda p, q: jnp.dot(p, q, preferred_element_type=f32)
    o_ref[...] = x_ref[...] + (dd(pool_out, wa_ref[...]) + dd(nsa_ref[...].astype(bf16), wb_ref[...]))


def pool_mix_out(pool_in, nsa_out, x, pool_w, pool_scale, w_out):
    b, t, d = x.shape
    m = b * t
    tm = PROJ_TM
    hrows = POOL_HIST + 1
    assert t % tm == 0 and tm % hrows == 0 and POOL_GDIM % 128 == 0
    r_i = lax.broadcasted_iota(jnp.int32, (tm, hrows + tm), 0) + hrows
    j_i = lax.broadcasted_iota(jnp.int32, (tm, hrows + tm), 1)
    band = jnp.stack([((j_i <= r_i) & (j_i > r_i - w)).astype(bf16) for w in POOL_WINDOWS])
    wbf = w_out.astype(bf16)
    u2, n2, x2 = pool_in.reshape(m, POOL_DIM), nsa_out.reshape(m, NSA_DIM), x.reshape(m, d)
    tile = lambda c: pl.BlockSpec((tm, c), lambda i: (i, 0))
    full = lambda a: pl.BlockSpec(a.shape, lambda i: (0,) * a.ndim)
    halo = pl.BlockSpec((hrows, POOL_DIM), lambda i: (jnp.maximum(i * (tm // hrows) - 1, 0), 0))
    consts = [band, pool_w.astype(bf16), pool_scale.reshape(1, POOL_DIM).astype(f32), wbf[:POOL_DIM], wbf[POOL_DIM:]]
    out = pl.pallas_call(
        functools.partial(_pool_mix_out_kernel, seq_len=t), grid=(m // tm,),
        in_specs=[tile(POOL_DIM), halo, tile(NSA_DIM), tile(d)] + [full(c) for c in consts],
        out_specs=tile(d), out_shape=jax.ShapeDtypeStruct((m, d), f32),
        compiler_params=pltpu.CompilerParams(dimension_semantics=("parallel",),
                                             vmem_limit_bytes=48 * 1024 * 1024),
        name="pool_mix_out",
    )(u2, u2, n2, x2, *consts)
    return out.reshape(b, t, d)


def ab_features(x, norm_g, pos, w_in, q_norm, k_norm):
    b, t = x.shape[:2]
    u = _norm_mm(x, norm_g, w_in)
    off_kv = POOL_DIM + NSA_DIM
    pool_in = u[..., :POOL_DIM]
    q = u[..., POOL_DIM:off_kv].reshape(b, t, NSA_HEADS, HEAD_DIM)
    kv = u[..., off_kv:off_kv + KV_COLS].reshape(b, t, 6, NSA_KV_HEADS, HEAD_DIM)
    gl = u[..., off_kv + KV_COLS:].reshape(b, t, NSA_HEADS, 3)
    q = rope_partial(rmsnorm(q, q_norm), pos)
    k_slc = rope_partial(rmsnorm(kv[:, :, 2], k_norm[1]), pos)
    k_win = rope_partial(rmsnorm(kv[:, :, 4], k_norm[2]), pos)
    rows = jnp.stack([kv[:, :, 0], kv[:, :, 1], k_slc, kv[:, :, 3]], axis=2)
    win = jnp.stack([k_win, kv[:, :, 5]], axis=2)
    return pool_in, q, gl, rows, win


def compress_kv(k_rows, v_rows, pos_w, phi, k_gain):
    b, length = k_rows.shape[:2]
    n_chunk = length // CMP_STRIDE

    def weighted_block_mean(rows, w):
        ch = rows.reshape(b, n_chunk, CMP_STRIDE, NSA_KV_HEADS, HEAD_DIM)
        return (jnp.einsum('bnlhd,hl->bnhd', ch[:, :-1], w[:, :CMP_STRIDE])
                + jnp.einsum('bnlhd,hl->bnhd', ch[:, 1:], w[:, CMP_STRIDE:]))

    cmp_end = jnp.arange(n_chunk - 1, dtype=jnp.int32) * CMP_STRIDE + (CMP_LEN - 1)
    kc = jnp.einsum('bnhd,de->bnhe', weighted_block_mean(k_rows, pos_w[0]), phi[0])
    kc = rope_partial(rmsnorm(kc, k_gain), cmp_end)
    vc = jnp.einsum('bnhd,de->bnhe', weighted_block_mean(v_rows, pos_w[1]), phi[1])
    return kc, vc, cmp_end


def ab_layer_prompt(x, norm_g, w_in, w_out, q_norm, k_norm, cmp_pos_w, cmp_phi, pool_w, pool_scale):
    b, t = x.shape[:2]
    pool_in, rows2, win2, ops = ab_features_prompt(x, norm_g, w_in, q_norm, k_norm)
    rows = rows2.reshape(b, t, 4, NSA_KV_HEADS, HEAD_DIM)
    win = win2.reshape(b, t, 2, NSA_KV_HEADS, HEAD_DIM)
    kc, vc, _ = compress_kv(rows[:, :, 0], rows[:, :, 1], cmp_pos_w, cmp_phi, k_norm[0])
    nsa_out = nsa_prompt_pallas(ops, kc, vc)
    x_new = pool_mix_out(pool_in, nsa_out, x, pool_w, pool_scale, w_out)
    keep = min(WINDOW, t)
    return x_new, rows, win[:, t - keep:], pool_in[:, t - POOL_HIST:]


def ab_layer_sample(x, norm_g, pools, page_table, win_bufs, layer, pool_hist, w_in, w_out, q_norm, k_norm,
                    cmp_pos_w, cmp_phi, pool_w, pool_scale):
    ts = x.shape[1]
    past_len = page_table.shape[1] * pools.shape[2]
    pos = past_len + jnp.arange(ts, dtype=jnp.int32)
    pool_in, q, gl, rows, win = ab_features(x, norm_g, pos, w_in, q_norm, k_norm)
    pool_out = pool_mix(pool_in, pool_hist, past_len, pool_w, pool_scale)
    nsa_out = nsa_sample_pallas(q, gl, rows, win, pools, page_table, win_bufs, layer,
                                cmp_pos_w, cmp_phi, k_norm[0])
    win_buf = win_bufs[layer]
    lb = win_buf.shape[1]
    keep = min(WINDOW, lb + ts)
    new_win = jnp.concatenate([win_buf, win.astype(win_buf.dtype)], axis=1)[:, lb + ts - keep:]
    x_new = _mix_out(pool_out, nsa_out, w_out, x)
    new_hist = jnp.concatenate([pool_hist.astype(pool_in.dtype), pool_in], axis=1)[:, -POOL_HIST:]
    return x_new, rows, new_win, new_hist


RW_TM = 256
HEAD_LANES = 128


def _rwkv_pre_kernel(*refs, seq_len, has_vres):
    it = iter(refs)
    x_ref, xprev_ref, fp_ref, g_ref, mu_ref = [next(it) for _ in range(5)]
    wr_ref, wk_ref, wv_ref = [next(it) for _ in range(3)]
    w0_ref, w1_ref, w2_ref, a0_ref, a1_ref, a2_ref, g1_ref, g2_ref = [next(it) for _ in range(8)]
    kkw_ref, kaw_ref, hsum_ref, hexp_ref = [next(it) for _ in range(4)]
    if has_vres:
        v0_ref, v1_ref, v2_ref, vf_ref = [next(it) for _ in range(4)]
    r_ref, k_ref, v_ref, lw_ref, kk_ref, a_ref, gg_ref = [next(it) for _ in range(7)]
    dd = lambda a, b: jnp.dot(a, b, preferred_element_type=f32)
    tm = x_ref.shape[0]
    norm = lambda z: z * lax.rsqrt(jnp.mean(z * z, axis=-1, keepdims=True) + RMS_EPS) * g_ref[...]
    h = norm(x_ref[...])
    row = lax.broadcasted_iota(jnp.int32, (tm, 1), 0)
    rolled = pltpu.roll(h, 1, axis=0)
    if seq_len % tm == 0:
        first = (pl.program_id(0) % (seq_len // tm)) == 0
        last_prev = norm(xprev_ref[...])[xprev_ref.shape[0] - 1:, :]
        prev = jnp.where(row == 0, jnp.where(first, fp_ref[...], last_prev), rolled)
    else:
        nseq = tm // seq_len
        sel = (lax.broadcasted_iota(jnp.int32, (tm, nseq), 0)
               == seq_len * lax.broadcasted_iota(jnp.int32, (tm, nseq), 1)).astype(bf16)
        prev = jnp.where(row % seq_len == 0, _dot_exact_lhs(sel, fp_ref[...]), rolled)
    xx = prev - h
    mix = lambda j: (h + xx * mu_ref[j:j + 1, :]).astype(bf16)
    xr, xw, xk, xv, xa, xg = [mix(j) for j in range(6)]
    k = dd(xk, wk_ref[...])
    v = dd(xv, wv_ref[...])
    r_ref[...] = dd(xr, wr_ref[...])
    z = -(w0_ref[...] + dd(jnp.tanh(dd(xw, w1_ref[...])).astype(bf16), w2_ref[...]))
    softplus = jnp.maximum(z, 0.0) + jnp.log(1.0 + jnp.exp(-jnp.abs(z)))
    lw_ref[...] = -jnp.exp(-softplus - 0.5)
    a = jax.nn.sigmoid(a0_ref[...] + dd(dd(xa, a1_ref[...]).astype(bf16), a2_ref[...]))
    a_ref[...] = a
    gg_ref[...] = dd(jax.nn.sigmoid(dd(xg, g1_ref[...])).astype(bf16), g2_ref[...])
    if has_vres:
        v = v + (vf_ref[...] - v) * jax.nn.sigmoid(
            v0_ref[...] + dd(dd(xv, v1_ref[...]).astype(bf16), v2_ref[...]))
    v_ref[...] = v
    kk = k * kkw_ref[...]
    nrm = jnp.maximum(jnp.sqrt(_dot_exact_rhs(kk * kk, hsum_ref[...])), 1e-12)
    kk_ref[...] = kk * _dot_exact_rhs(1.0 / nrm, hexp_ref[...])
    k_ref[...] = k * (1.0 + (a - 1.0) * kaw_ref[...])


def _rwkv_post_kernel(o_ref, r_ref, k_ref, v_ref, gg_ref, x_ref, gnw_ref, gnb_ref, rk_ref, hsum_ref,
                      hexp_ref, wo_ref, out_ref):
    hs = hsum_ref[...]
    he = hexp_ref[...]
    head_sum = lambda z: _dot_exact_rhs(_dot_exact_rhs(z, hs), he)
    o = o_ref[...]
    v = v_ref[...]
    d = o - head_sum(o) * (1.0 / RWKV_N)
    var = head_sum(d * d) * (1.0 / RWKV_N)
    on = d * lax.rsqrt(var + GN_EPS) * gnw_ref[...] + gnb_ref[...]
    on = on + head_sum(r_ref[...] * k_ref[...] * rk_ref[...]) * v
    y = (on * gg_ref[...]).astype(bf16)
    out_ref[...] = x_ref[...] + jnp.dot(y, wo_ref[...], preferred_element_type=f32)


def rwkv_layer_fused(x, norm_g, shift_prev, s0, v_first, vres, mu, wr, wk, wv, wo, w0, w1, w2, a0, a1, a2,
                     g1, g2, k_k, k_a, r_k, gn_w, gn_b):
    n, t, d = x.shape
    m = n * t
    tm = RW_TM
    assert m % tm == 0 and (t % tm == 0 or tm % t == 0)
    x2 = x.reshape(m, d)
    row = lambda z: z.reshape(1, d).astype(f32)
    cb = lambda z: z.astype(bf16)
    hd = lax.broadcasted_iota(jnp.int32, (d, HEAD_LANES), 0) // RWKV_N
    hsum = (hd == lax.broadcasted_iota(jnp.int32, (d, HEAD_LANES), 1)).astype(bf16)
    hexp = hsum.T
    tile = pl.BlockSpec((tm, d), lambda i: (i, 0))
    full = lambda z: pl.BlockSpec(z.shape, lambda i: (0,) * z.ndim)
    if t % tm == 0:
        tps = t // tm
        fp = shift_prev.reshape(n, 1, d).astype(f32)
        fp_spec = pl.BlockSpec((None, 1, d), lambda i: (i // tps, 0, 0))
    else:
        fp = shift_prev.astype(f32)
        fp_spec = pl.BlockSpec((tm // t, d), lambda i: (i, 0))
    xprev_spec = pl.BlockSpec((8, d), lambda i: (jnp.maximum(i * (tm // 8) - 1, 0), 0))
    mu8 = jnp.pad(mu.astype(f32), ((0, 2), (0, 0)))
    consts = [row(norm_g), mu8, cb(wr), cb(wk), cb(wv), row(w0), cb(w1), cb(w2), row(a0), cb(a1), cb(a2),
              cb(g1), cb(g2), row(k_k), row(k_a), hsum, hexp]
    args = [x2, x2, fp] + consts
    specs = [tile, xprev_spec, fp_spec] + [full(c) for c in consts]
    if vres is not None:
        v0, v1, v2 = vres
        extra = [row(v0), cb(v1), cb(v2)]
        args += extra + [v_first.reshape(m, d)]
        specs += [full(c) for c in extra] + [tile]
    cp = pltpu.CompilerParams(dimension_semantics=("parallel",), vmem_limit_bytes=56 * 1024 * 1024)
    r, k, v, lw, kk, a, gg = pl.pallas_call(
        functools.partial(_rwkv_pre_kernel, seq_len=t, has_vres=vres is not None),
        grid=(m // tm,), in_specs=specs, out_specs=[tile] * 7,
        out_shape=[jax.ShapeDtypeStruct((m, d), f32)] * 7, compiler_params=cp, name="rwkv_pre",
    )(*args)
    if vres is None:
        v_first = v.reshape(n, t, d)
    seq = lambda z: z.reshape(n, t, d)
    o, s = wkv_chunked(seq(r), seq(lw), seq(k), seq(v), seq(kk), seq(a), s0.astype(f32))
    o = o.reshape(m, d)
    post_consts = [row(gn_w), row(gn_b), r_k.reshape(1, d).astype(f32), hsum, hexp, cb(wo)]
    x_new = pl.pallas_call(
        _rwkv_post_kernel, grid=(m // tm,),
        in_specs=[tile] * 6 + [full(c) for c in post_consts], out_specs=tile,
        out_shape=jax.ShapeDtypeStruct((m, d), f32), compiler_params=cp, name="rwkv_post",
    )(o, r, k, v, gg, x2, *post_consts)
    h_last = rmsnorm(x[:, -1], norm_g)
    return x_new.reshape(n, t, d), v_first, s, h_last


def kernel(x_prompt, x_sample, cache_nsa_kv, cache_win_kv, state_pool, state_wkv, state_shift,
           page_table, norm_mix, norm_ffn, ab_w_in, ab_w_out, ab_q_norm, ab_k_norm, cmp_pos_w,
           cmp_phi, pool_w, pool_scale, rw_mu, rw_wr, rw_wk, rw_wv, rw_wo, rw_w0, rw_w1, rw_w2,
           rw_a0, rw_a1, rw_a2, rw_v0, rw_v1, rw_v2, rw_g1, rw_g2, rw_kk, rw_ka, rw_rk, rw_gn_w,
           rw_gn_b, moe_wc, moe_bc, moe_wf, moe_bf, moe_wg, moe_wu, moe_wd):
    xp, xs = x_prompt, x_sample
    vf_p, vf_s = None, None
    nsa_p, nsa_s, win_p, win_s, pool_p, pool_s = [], [], [], [], [], []
    wkv_p, wkv_s, sh_p, sh_s = [], [], [], []
    for l in range(DEPTH):
        if l % 2 == 0:
            i = l // 2
            wts = (ab_w_in[i], ab_w_out[i], ab_q_norm[i], ab_k_norm[i], cmp_pos_w[i], cmp_phi[i],
                   pool_w[i], pool_scale[i])
            xp, r_p, w_p, h_p = ab_layer_prompt(xp, norm_mix[l], *wts)
            xs, r_s, w_s, h_s = ab_layer_sample(xs, norm_mix[l], cache_nsa_kv, page_table,
                                                cache_win_kv, i, state_pool[i], *wts)
            nsa_p.append(r_p)
            nsa_s.append(r_s)
            win_p.append(w_p)
            win_s.append(w_s)
            pool_p.append(h_p)
            pool_s.append(h_s)
        else:
            j = l // 2
            vres = None if j == 0 else (rw_v0[j - 1], rw_v1[j - 1], rw_v2[j - 1])
            wts = (rw_mu[j], rw_wr[j], rw_wk[j], rw_wv[j], rw_wo[j], rw_w0[j], rw_w1[j], rw_w2[j],
                   rw_a0[j], rw_a1[j], rw_a2[j], rw_g1[j], rw_g2[j], rw_kk[j], rw_ka[j], rw_rk[j],
                   rw_gn_w[j], rw_gn_b[j])
            bp = xp.shape[0]
            zero_shift = jnp.zeros((bp, D_MODEL), xp.dtype)
            zero_state = jnp.zeros((bp, RWKV_HEADS, RWKV_N, RWKV_N), jnp.float32)
            xp, vf_p, s_p, shp = rwkv_layer_fused(xp, norm_mix[l], zero_shift, zero_state, vf_p, vres, *wts)
            xs, vf_s, s_s, shs = rwkv_layer_fused(xs, norm_mix[l], state_shift[j], state_wkv[j], vf_s, vres, *wts)
            wkv_p.append(s_p)
            wkv_s.append(s_s)
            sh_p.append(shp)
            sh_s.append(shs)
        prep =moe_prep(norm_ffn[l], moe_wc[l], moe_bc[l], moe_wf[l], moe_bf[l], moe_wg[l], moe_wu[l], moe_wd[l])
        xp = moe_residual(xp, prep)
        xs = moe_residual(xs, prep)
    return (xp, xs, jnp.stack(nsa_p), jnp.stack(nsa_s), jnp.stack(win_p), jnp.stack(win_s),
            jnp.stack(pool_p), jnp.stack(pool_s), jnp.stack(wkv_p), jnp.stack(wkv_s),
            jnp.stack(sh_p), jnp.stack(sh_s))
```

```python
import functools

import jax
import jax.numpy as jnp
from jax import lax
from jax.experimental import pallas as pl
from jax.experimental.pallas import tpu as pltpu


f32 = jnp.float32
bf16 = jnp.bfloat16
WKV_CHUNK = 64
WKV_PAIRS = 8
WKV_SEQS = 2
WKV_PASSES = 1
WKV_GRAM_PASSES = 1


def _split(x):
    hi = x.astype(bf16)
    lo = (x - hi.astype(f32)).astype(bf16)
    return hi, lo


def _mmul(a, b, passes, nt=False):
    dn = (((1,), (1,)), ((), ())) if nt else (((1,), (0,)), ((), ()))
    d = lambda x, y: lax.dot_general(x, y, dn, preferred_element_type=f32)
    if passes == 1:
        return d(a.astype(bf16), b.astype(bf16))
    ah, al = _split(a)
    bh, bl = _split(b)
    return d(ah, bh) + (d(ah, bl) + d(al, bh))


def _wkv_kernel(r_ref, lw_ref, k_ref, v_ref, kk_ref, a_ref, s0_ref, o_ref, sT_ref, st_scr, *, passes):
    C = WKV_CHUNK
    nb = r_ref.shape[0]
    c = pl.program_id(1)
    nc = pl.num_programs(1)
    row = lax.broadcasted_iota(jnp.int32, (2 * C, 2 * C), 0)
    col = lax.broadcasted_iota(jnp.int32, (2 * C, 2 * C), 1)
    bd = (row < C) == (col < C)
    strict = bd & ((row % C) > (col % C))
    incl = bd & ((row % C) >= (col % C))
    eye = (row == col).astype(f32)
    lane_s = col < C
    m1 = lax.broadcasted_iota(jnp.int32, (C, 2 * C), 1) < C
    tri = (lax.broadcasted_iota(jnp.int32, (C, C), 0)
           >= lax.broadcasted_iota(jnp.int32, (C, C), 1)).astype(bf16)

    @pl.when(c == 0)
    def _():
        z = jnp.zeros((C, C), f32)
        for p in range(nb * WKV_PAIRS):
            s1 = s0_ref[p // WKV_PAIRS, 2 * (p % WKV_PAIRS)]
            s2 = s0_ref[p // WKV_PAIRS, 2 * (p % WKV_PAIRS) + 1]
            st_scr[p] =jnp.concatenate([jnp.concatenate([s1, z], axis=1),
                                         jnp.concatenate([z, s2], axis=1)], axis=0)

    def stack2(x):
        return jnp.concatenate([jnp.where(m1, x, 0.0), jnp.where(m1, 0.0, x)], axis=0)

    dd = lambda x, y: jnp.dot(x, y, preferred_element_type=f32)
    pairs = range(nb * WKV_PAIRS)
    sq = [p // WKV_PAIRS for p in pairs]
    sls = [slice((p % WKV_PAIRS) * 2 * C, (p % WKV_PAIRS + 1) * 2 * C) for p in pairs]
    tv = r_ref.shape[1]

    def ld(ref, p):
        x = ref[sq[p], :, sls[p]]
        return x if tv == C else jnp.concatenate([x, jnp.zeros((C - tv, 2 * C), f32)], axis=0)

    def prep(p):
        sl = sls[p]
        lw = ld(lw_ref, p)
        kk = ld(kk_ref, p)
        h1 = lw.astype(bf16)
        r1 = lw - h1.astype(f32)
        h2 = r1.astype(bf16)
        h3 = (r1 - h2.astype(f32)).astype(bf16)
        cw = dd(tri, h1) + (dd(tri, h2) + dd(tri, h3))
        cwC = cw[C - 1:C, :]
        b = kk * ld(a_ref, p)
        k = ld(k_ref, p)
        At = -kk * jnp.exp(cw - lw)
        Rt = ld(r_ref, p) * jnp.exp(cw)
        einv = jnp.exp(-cw)
        efut = jnp.exp(cwC - cw)
        X = jnp.concatenate([stack2(At), stack2(Rt)], axis=0)
        Y = jnp.concatenate([b * einv, k * einv], axis=0)
        AR = jnp.concatenate([At, Rt], axis=0)
        BK = jnp.concatenate([b * efut, k * efut], axis=0)
        return X, Y, AR, BK, jnp.exp(cwC)

    pre = [prep(p) for p in pairs]
    G = [_mmul(pre[p][0], pre[p][1], WKV_GRAM_PASSES, nt=True) for p in pairs]
    ARS = [_mmul(pre[p][2], st_scr[p], passes, nt=True) for p in pairs]
    L, Mak, Mrb, Mrk = [], [], [], []
    for p in pairs:
        GA = G[p][0:2 * C]
        GR = G[p][2 * C:4 * C]
        GAr = pltpu.roll(GA, C, axis=1)
        GRr = pltpu.roll(GR, C, axis=1)
        L.append(jnp.where(strict, jnp.where(lane_s, GA, GAr), 0.0))
        Mak.append(jnp.where(strict, jnp.where(lane_s, GAr, GA), 0.0))
        Mrb.append(jnp.where(incl, jnp.where(lane_s, GR, GRr), 0.0))
        Mrk.append(jnp.where(incl, jnp.where(lane_s, GRr, GR), 0.0))
    V = [ld(v_ref, p) for p in pairs]
    Vs = [stack2(V[p]) for p in pairs]
    Xs = [stack2(ARS[p][0:C]) + _mmul(Mak[p], Vs[p], passes) for p in pairs]
    OV = [_mmul(Mrk[p], Vs[p], passes) for p in pairs]
    P = [eye + L[p] for p in pairs]
    Q = L
    for _ in range(5):
        Q = [_mmul(Q[p], Q[p], passes) for p in pairs]
        P = [P[p] + _mmul(Q[p], P[p], passes) for p in pairs]
    Us = [_mmul(P[p], Xs[p], passes) for p in pairs]
    Os = [_mmul(Mrb[p], Us[p], passes) + OV[p] for p in pairs]
    for p in pairs:
        o_ref[sq[p], :, sls[p]] = (ARS[p][C:2 * C] + Os[p][0:C] + Os[p][C:2 * C])[0:tv]
    for p in pairs:
        U = Us[p][0:C] + Us[p][C:2 * C]
        UV = jnp.concatenate([U, V[p]], axis=0)
        dS = _mmul(UV.T, pre[p][3], passes)
        st_scr[p] = st_scr[p] * pre[p][4] + jnp.where(bd, dS, 0.0)

    @pl.when(c == nc - 1)
    def _():
        for p in pairs:
            Snew = st_scr[p]
            sT_ref[sq[p], 2 * (p % WKV_PAIRS)] = Snew[0:C, 0:C]
            sT_ref[sq[p], 2 * (p % WKV_PAIRS) + 1] = Snew[C:2 * C, C:2 * C]


def wkv_chunked(r, lw, k, v, kk, a, s0):
    B, T, D = r.shape
    H = D // 64
    C = WKV_CHUNK
    nb = WKV_SEQS
    tb = min(T, C)
    assert (T % C == 0 or T < C) and tb % 8 == 0 and D == WKV_PAIRS * 2 * C and B % nb == 0
    blk = pl.BlockSpec((nb, tb, D), lambda b, c: (b, c, 0))
    sblk = pl.BlockSpec((nb, H, 64, 64), lambda b, c: (b, 0, 0, 0))
    return pl.pallas_call(
        functools.partial(_wkv_kernel, passes=WKV_PASSES),
        grid=(B // nb, T // tb),
        in_specs=[blk] * 6 + [sblk],
        out_specs=[blk, sblk],
        out_shape=[jax.ShapeDtypeStruct((B, T, D), f32), jax.ShapeDtypeStruct((B, H, 64, 64), f32)],
        scratch_shapes=[pltpu.VMEM((nb * WKV_PAIRS, 2 * C, 2 * C), f32)],
        compiler_params=pltpu.CompilerParams(dimension_semantics=("parallel", "arbitrary")),
        name="wkv7_chunked",
    )(r, lw, k, v, kk, a, s0)


D_MODEL = 1024
BATCH = 4
SEQ = 4096
DEPTH = 4
DEC_BATCH = 128
DEC_SEQ = 8
PAST_LEN = 2048
PAGE_SIZE = 128

N_NSA_LAYERS = (DEPTH + 1) // 2
N_RWKV_LAYERS = DEPTH // 2
N_VRES = N_RWKV_LAYERS - 1

POOL_DIM = D_MODEL // 2
POOL_WINDOWS = (2, 4, 8, 16)
POOL_GROUPS = len(POOL_WINDOWS)
POOL_GDIM = POOL_DIM // POOL_GROUPS
POOL_HIST = max(POOL_WINDOWS) - 1

HEAD_DIM = 64
NSA_HEADS = (D_MODEL // 2) // HEAD_DIM
NSA_KV_HEADS = 2
NSA_GQ = NSA_HEADS // NSA_KV_HEADS
NSA_DIM = NSA_HEADS * HEAD_DIM
CMP_STRIDE = 16
CMP_LEN = 2 * CMP_STRIDE
SLC_LEN = 64
N_SEL = 16
WINDOW = 512
Q_BLOCK = 128
ROPE_DIM = HEAD_DIM // 4
ROPE_THETA = 500000.0
MIX_DIM = POOL_DIM + NSA_DIM
KV_COLS = 6 * NSA_KV_HEADS * HEAD_DIM
IN_COLS = POOL_DIM + NSA_DIM + KV_COLS + 3 * NSA_HEADS

RWKV_N = 64
RWKV_HEADS = D_MODEL // RWKV_N
LORA_W = 64
LORA_A = 64
LORA_V = 32
LORA_G = 128
GN_EPS = 64e-5

MOE_GROUPS = 4
MOE_EPG = 4
N_EXPERTS = MOE_GROUPS * MOE_EPG
MOE_TOPK = 2
D_FF_E = 256

RMS_EPS = 1e-6
NEG_INF = -1e30
RES_SCALE = (2 * DEPTH) ** -0.5

SEL_TK = 512


def _nsa_prompt_kernel(q_ref, ql_ref, g_ref, kc_ref, kcl_ref, vct_ref, ks_ref, vst_ref, kw_ref, vwt_ref,
                       o_ref, score_scr, sel_scr):
    QB = Q_BLOCK
    GQ = NSA_GQ * QB
    H = range(NSA_KV_HEADS)
    i = pl.program_id(1)
    s0 = i * QB
    qT = [q_ref[h] for h in H]
    posq = s0 + lax.broadcasted_iota(jnp.int32, (1, GQ), 1) % QB
    dd = lambda x, y: jnp.dot(x, y, preferred_element_type=f32)

    ncp = kc_ref.shape[1]
    cidx = lax.broadcasted_iota(jnp.int32, (ncp, 1), 0)
    mask_c = cidx * CMP_STRIDE + (CMP_LEN - 1) <= posq
    sc = [dd(kc_ref[h], qT[h]) + (dd(kc_ref[h], ql_ref[h]) + dd(kcl_ref[h], qT[h])) for h in H]
    sc = [jnp.where(mask_c, s, NEG_INF) for s in sc]
    pe = [jnp.exp(s - jnp.max(s, axis=0, keepdims=True)) for s in sc]
    pc = [jnp.where(mask_c, p / jnp.sum(p, axis=0, keepdims=True), 0.0) for p in pe]
    o_c = [dd(vct_ref[h], pc[h].astype(bf16)) for h in H]

    imp = [(p[:, 0:QB] + p[:, QB:2 * QB]) + (p[:, 2 * QB:3 * QB] + p[:, 3 * QB:4 * QB]) for p in pc]
    n_slc = score_scr.shape[1]
    per = SLC_LEN // CMP_STRIDE
    nn = lax.broadcasted_iota(jnp.int32, (n_slc, ncp), 0) * per
    cc = lax.broadcasted_iota(jnp.int32, (n_slc, ncp), 1)
    mt = (0.5 * ((cc >= nn) & (cc < nn + per)).astype(f32)
          + 0.5 * ((cc + 1 >= nn) & (cc + 1 < nn + per)).astype(f32)).astype(bf16)
    nidx = lax.broadcasted_iota(jnp.int32, (n_slc, 1), 0)
    cur = (s0 + lax.broadcasted_iota(jnp.int32, (1, QB), 1)) // SLC_LEN
    forced = (nidx == 0) | (nidx == cur) | (nidx == cur - 1)
    score = []
    for h in H:
        i1 = imp[h].astype(bf16)
        r1 = imp[h] - i1.astype(f32)
        i2 = r1.astype(bf16)
        i3 = (r1 - i2.astype(f32)).astype(bf16)
        imp_blk = dd(mt, i1) + (dd(mt, i2) + dd(mt, i3))
        score.append(jnp.where(nidx > cur, -1.0, jnp.where(forced, 1e6, imp_blk)))
        score_scr[h] = score[h]
    rank = [jnp.zeros((n_slc, QB), jnp.int32) for _ in H]
    for m in range(n_slc):
        tie = (nidx > m).astype(jnp.int32)
        for h in H:
            sm = score_scr[h, m:m + 1, :]
            rank[h] = rank[h] + jnp.where(sm > score[h], 1, jnp.where(sm == score[h], tie, 0))
    for h in H:
        bias = jnp.where(rank[h] < min(N_SEL, n_slc), 0.0, NEG_INF)
        sel_scr[h] = jnp.concatenate([bias] * NSA_GQ, axis=1)

    def online(carry, s, vt_blk):
        m, l, acc = carry
        m_new = jnp.maximum(m, jnp.max(s, axis=0, keepdims=True))
        alpha = jnp.exp(m - m_new)
        p = jnp.exp(s - m_new)
        l = alpha * l + jnp.sum(p, axis=0, keepdims=True)
        acc = alpha * acc + dd(vt_blk, p.astype(bf16))
        return m_new, l, acc

    init1 = (jnp.full((1, GQ), NEG_INF, f32), jnp.zeros((1, GQ), f32), jnp.zeros((HEAD_DIM, GQ), f32))
    init = tuple(init1 for _ in H)

    bpt = SEL_TK // SLC_LEN

    def sel_scores(h, kt):
        k0 = pl.multiple_of(kt * SEL_TK, SEL_TK)
        rows = sel_scr[h, pl.ds(pl.multiple_of(kt * bpt, bpt), bpt), :]
        blk = jnp.concatenate([jnp.broadcast_to(rows[j:j + 1, :], (SLC_LEN, GQ)) for j in range(bpt)], axis=0)
        return k0, dd(ks_ref[h, pl.ds(k0, SEL_TK), :], qT[h]) + blk

    def sel_tiles(kts, carry):
        scored = [[sel_scores(h, kt) for h in H] for kt in kts]
        carry = list(carry)
        for per_head in scored:
            for h in H:
                k0, s = per_head[h]
                carry[h] = online(carry[h], s, vst_ref[h, :, pl.ds(k0, SEL_TK)])
        return tuple(carry)

    n_full = s0 // SEL_TK
    carry = lax.fori_loop(0, n_full // 2, lambda kp, c: sel_tiles((2 * kp, 2 * kp + 1), c), init)
    carry = lax.fori_loop(2 * (n_full // 2), n_full, lambda kt, c: sel_tiles((kt,), c), carry)
    kpos = pl.multiple_of(n_full * SEL_TK, SEL_TK) + lax.broadcasted_iota(jnp.int32, (SEL_TK, 1), 0)
    diag = [sel_scores(h, n_full) for h in H]
    sel = [online(carry[h], jnp.where(kpos <= posq, diag[h][1], NEG_INF),
                  vst_ref[h, :, pl.ds(diag[h][0], SEL_TK)]) for h in H]

    nwt = WINDOW // QB
    tiles = []
    for j in range(nwt + 1):
        k0 = s0 - WINDOW + j * QB
        k0c = pl.multiple_of(jnp.maximum(k0, 0), QB)
        kpos = k0c + lax.broadcasted_iota(jnp.int32, (QB, 1), 0)
        per_head = []
        for h in H:
            s = dd(kw_ref[h, pl.ds(k0c, QB), :], qT[h])
            if j == nwt:
                s = jnp.where(kpos <= posq, s, NEG_INF)
            else:
                if j == 0:
                    s = jnp.where(posq - kpos < WINDOW, s, NEG_INF)
                s = s + jnp.where(k0 >= 0, 0.0, NEG_INF)
            per_head.append(s)
        tiles.append((k0c, per_head))
    m_w = [functools.reduce(jnp.maximum, [jnp.max(t[1][h], axis=0, keepdims=True) for t in tiles]) for h in H]
    l_w = [jnp.zeros((1, GQ), f32) for _ in H]
    acc_w = [jnp.zeros((HEAD_DIM, GQ), f32) for _ in H]
    for k0c, per_head in tiles:
        for h in H:
            p = jnp.exp(per_head[h] - m_w[h])
            l_w[h] = l_w[h] + jnp.sum(p, axis=0, keepdims=True)
            acc_w[h] = acc_w[h] + dd(vwt_ref[h, :, pl.ds(k0c, QB)], p.astype(bf16))

    pieces = []
    for h in H:
        g = jax.nn.sigmoid(g_ref[h])
        _, l_s, acc_s = sel[h]
        o = g[0:1] * o_c[h] + g[1:2] * (acc_s / l_s) + g[2:3] * (acc_w[h] / l_w[h])
        pieces += [o[:, j * QB:(j + 1) * QB].T for j in range(NSA_GQ)]
    o_ref[...] = jnp.concatenate(pieces, axis=1)


def nsa_prompt_pallas(ops, kc, vc):
    qT, qTl, gT, ks, vst, kw, vwt = ops
    B, KVH, T, D = ks.shape
    G, QB = NSA_GQ, Q_BLOCK
    assert T % SEL_TK == 0 and T % QB == 0
    nqb = T // QB
    n_cmp = kc.shape[1]
    ncp = -(-n_cmp // 128) * 128
    n_slc = T // SLC_LEN
    kcp, kcl = _split(jnp.pad(kc, ((0, 0), (0, ncp - n_cmp), (0, 0), (0, 0))).transpose(0, 2, 1, 3))
    vct = jnp.pad(vc, ((0, 0), (0, ncp - n_cmp), (0, 0), (0, 0))).transpose(0, 2, 3, 1).astype(bf16)
    bh = lambda *shape: pl.BlockSpec((None, KVH) + shape, lambda b, i: (b, 0) + (0,) * len(shape))
    bhi = lambda *shape: pl.BlockSpec((None, KVH, None) + shape, lambda b, i: (b, 0, i) + (0,) * len(shape))
    return pl.pallas_call(
        _nsa_prompt_kernel,
        grid=(B, nqb),
        in_specs=[bhi(D, G * QB), bhi(D, G * QB), bhi(3, G * QB), bh(ncp, D), bh(ncp, D), bh(D, ncp),
                  bh(T, D), bh(D, T), bh(T, D), bh(D, T)],
        out_specs=pl.BlockSpec((None, QB, KVH * G * D), lambda b, i: (b, i, 0)),
        out_shape=jax.ShapeDtypeStruct((B, T, KVH * G * D), f32),
        scratch_shapes=[pltpu.VMEM((KVH, n_slc, QB), f32), pltpu.VMEM((KVH, n_slc, G * QB), f32)],
        compiler_params=pltpu.CompilerParams(dimension_semantics=("parallel", "arbitrary"),
                                             vmem_limit_bytes=48 * 1024 * 1024),
        name="nsa_prompt",
    )(qT, qTl, gT, kcp, kcl, vct, ks, vst, kw, vwt)


def _split3(x):
    h1 = x.astype(bf16)
    r1 = x - h1.astype(f32)
    h2 = r1.astype(bf16)
    return h1, h2, (r1 - h2.astype(f32)).astype(bf16)


def _dot_exact_rhs(x, m):
    d = lambda a: jnp.dot(a, m, preferred_element_type=f32)
    h1, h2, h3 = _split3(x)
    return d(h1) + (d(h2) + d(h3))


def _dot_exact_lhs(m, x):
    d = lambda a: jnp.dot(m, a, preferred_element_type=f32)
    h1, h2, h3 = _split3(x)
    return d(h1) + (d(h2) + d(h3))


def _nsa_sample_kernel(pt_ref, *refs, n_pages, ts, past_len):
    pages = refs[:n_pages]
    (new_ref, wbuf_ref, wnew_ref, qh_ref, ql_ref, gate_ref, wa_ref, wb_ref, phik_ref, phiv_ref,
     gain_ref, cos_ref, sin_ref, o_ref, ssel_scr, a_scr, b_scr, score_scr) = refs[n_pages:]
    P = PAGE_SIZE
    KV = NSA_KV_HEADS * HEAD_DIM
    NCOL = NSA_KV_HEADS * NSA_GQ * ts
    NQ = NSA_KV_HEADS * ts
    cpp = P // CMP_STRIDE
    n_chunk = (past_len + SLC_LEN) // CMP_STRIDE
    n_cmp = n_chunk - 1
    ncp = a_scr.shape[0]
    n_slc = (past_len + SLC_LEN) // SLC_LEN
    nsp = score_scr.shape[0]
    dd = lambda x, y: jnp.dot(x, y, preferred_element_type=f32)
    qh = qh_ref[...]
    ql = ql_ref[...]
    col = lax.broadcasted_iota(jnp.int32, (1, NCOL), 1)
    t_col = col % ts
    zpad = jnp.zeros((P - ts, 4 * KV), f32)
    new_tile = jnp.concatenate([new_ref[...], zpad], axis=0)

    wa = wa_ref[...]
    wb = wb_ref[...]
    a_scr[...] = jnp.zeros(a_scr.shape, f32)
    b_scr[...] = jnp.zeros(b_scr.shape, f32)
    def slab(j, kind):
        if j == n_pages:
            return new_tile[:, kind * KV:(kind + 1) * KV]
        return pages[j][:, kind * KV:(kind + 1) * KV]

    for j in range(n_pages + 1):
        xc = jnp.concatenate([slab(j, 0), slab(j, 1)], axis=1)
        a_scr[j * cpp:(j + 1) * cpp, :] = (xc * wa).reshape(cpp, CMP_STRIDE, 2 * KV).sum(axis=1)
        b_scr[j * cpp:(j + 1) * cpp, :] = (xc * wb).reshape(cpp, CMP_STRIDE, 2 * KV).sum(axis=1)
        ssel_scr[j * P:(j + 1) * P, :] = dd(slab(j, 2).astype(bf16), qh)

    mean = a_scr[...] + pltpu.roll(b_scr[...], ncp - 1, axis=0)
    kc = _mmul(mean[:, 0:KV], phik_ref[...], 3)
    vc = _mmul(mean[:, KV:2 * KV], phiv_ref[...], 3)
    r_i = lax.broadcasted_iota(jnp.int32, (KV, KV), 0)
    c_i = lax.broadcasted_iota(jnp.int32, (KV, KV), 1)
    same_head = (r_i // HEAD_DIM) == (c_i // HEAD_DIM)
    mavg = jnp.where(same_head, 1.0 / HEAD_DIM, 0.0).astype(bf16)
    kc = kc * lax.rsqrt(_dot_exact_rhs(kc * kc, mavg) + RMS_EPS) * gain_ref[...]
    half = ROPE_DIM // 2
    rd, cd = r_i % HEAD_DIM, c_i % HEAD_DIM
    rot = jnp.where(same_head & (cd < half) & (rd == cd + half), -1.0,
                    jnp.where(same_head & (cd >= half) & (cd < ROPE_DIM) & (rd == cd - half), 1.0, 0.0)).astype(bf16)
    kc = kc * cos_ref[...] + _dot_exact_rhs(kc, rot) * sin_ref[...]

    kch, kcl = _split(kc)
    sc = dd(kch, qh) + (dd(kch, ql) + dd(kcl, qh))
    cidx = lax.broadcasted_iota(jnp.int32, (ncp, 1), 0)
    mask_c = (cidx * CMP_STRIDE + (CMP_LEN - 1) <= past_len + t_col) & (cidx < n_cmp)
    sc = jnp.where(mask_c, sc, NEG_INF)
    pe = jnp.exp(sc - jnp.max(sc, axis=0, keepdims=True))
    pc = jnp.where(mask_c, pe / jnp.sum(pe, axis=0, keepdims=True), 0.0)
    o_c = dd(pc.T.astype(bf16), vc.astype(bf16))

    gr = lax.broadcasted_iota(jnp.int32, (NCOL, NQ), 0)
    gc = lax.broadcasted_iota(jnp.int32, (NCOL, NQ), 1)
    gsum = ((gr // (NSA_GQ * ts) == gc // ts) & (gr % ts == gc % ts)).astype(bf16)
    imp = _dot_exact_rhs(pc, gsum)
    per = SLC_LEN // CMP_STRIDE
    nn = lax.broadcasted_iota(jnp.int32, (nsp, ncp), 0) * per
    cc = lax.broadcasted_iota(jnp.int32, (nsp, ncp), 1)
    mt = (0.5 * ((cc >= nn) & (cc < nn + per)).astype(f32)
          + 0.5 * ((cc + 1 >= nn) & (cc + 1 < nn + per)).astype(f32)).astype(bf16)
    imp_blk = _dot_exact_lhs(mt, imp)
    nidx = lax.broadcasted_iota(jnp.int32, (nsp, 1), 0)
    cur = (past_len + lax.broadcasted_iota(jnp.int32, (1, NQ), 1) % ts) // SLC_LEN
    forced = (nidx == 0) | (nidx == cur) | (nidx == cur - 1)
    score = jnp.where(nidx >= n_slc, -2.0, jnp.where(nidx > cur, -1.0, jnp.where(forced, 1e6, imp_blk)))
    score_scr[...] = score
    rank = jnp.zeros((nsp, NQ), jnp.int32)
    for m in range(n_slc):
        sm = score_scr[m:m + 1, :]
        beats = (sm > score) | ((sm == score) & (nidx > m))
        rank = rank + beats.astype(jnp.int32)
    sel = (rank < min(N_SEL, n_slc)).astype(bf16)
    gr2 = lax.broadcasted_iota(jnp.int32, (NQ, NCOL), 0)
    gc2 = lax.broadcasted_iota(jnp.int32, (NQ, NCOL), 1)
    gexp = ((gc2 // (NSA_GQ * ts) == gr2 // ts) & (gc2 % ts == gr2 % ts)).astype(bf16)
    sel_c = dd(sel, gexp)

    def two_pass(n_tiles, score_tile, mask_tile, v_tile):
        sm = [jnp.where(mask_tile(j), score_tile(j), NEG_INF) for j in range(n_tiles)]
        m = functools.reduce(jnp.maximum, [jnp.max(s, axis=0, keepdims=True) for s in sm])
        num = jnp.zeros((NCOL, KV), f32)
        den = jnp.zeros((NCOL, KV), f32)
        ones = jnp.ones((P, KV), bf16)
        for j in range(n_tiles):
            p = jnp.exp(sm[j] - m).T.astype(bf16)
            num = num + dd(p, v_tile(j))
            den = den + dd(p, ones)
        return num, den

    bpp = P // SLC_LEN
    row = lax.broadcasted_iota(jnp.int32, (P, 1), 0)

    def sel_mask(j):
        blk = jnp.concatenate([jnp.broadcast_to(sel_c[j * bpp + i:j * bpp + i + 1, :], (SLC_LEN, NCOL))
                               for i in range(bpp)], axis=0)
        return (blk > 0.5) & (j * P + row <= past_len + t_col)

    num_s, den_s = two_pass(
        n_pages + 1, lambda j: ssel_scr[j * P:(j + 1) * P, :], sel_mask,
        lambda j: slab(j, 3).astype(bf16))

    lb = wbuf_ref.shape[0]
    nwt = lb // P
    wnew = jnp.concatenate([wnew_ref[...], jnp.zeros((P - ts, 2 * KV), f32)], axis=0)

    def w_tile(j):
        return wbuf_ref[j * P:(j + 1) * P, :] if j < nwt else wnew

    def win_mask(j):
        pos_w = (past_len - lb + j * P + row) if j < nwt else (past_len + row)
        dq = past_len + t_col - pos_w
        return (dq >= 0) & (dq < WINDOW) & (pos_w >= 0) & ((row < ts) | (j < nwt))

    num_w, den_w = two_pass(
        nwt + 1, lambda j: dd(w_tile(j)[:, 0:KV].astype(bf16), qh), win_mask,
        lambda j: w_tile(j)[:, KV:2 * KV].astype(bf16))

    g = jax.nn.sigmoid(gate_ref[...])
    o_ref[...] = g[0] * o_c + g[1] * (num_s / den_s) + g[2] * (num_w / den_w)


def nsa_sample_pallas(q, gl, rows_new, win_new, pools, page_table, win_bufs, layer, kc_w, phi, k_gain):
    B, ts = q.shape[:2]
    KVH, G, D, P = NSA_KV_HEADS, NSA_GQ, HEAD_DIM, PAGE_SIZE
    KV = KVH * D
    n_pages = page_table.shape[1]
    past_len = n_pages * P
    n_pool = pools.shape[1]
    assert pools.shape[2] == P and ts <= SLC_LEN and P % SLC_LEN == 0
    lb = win_bufs.shape[2]
    assert lb % P == 0
    NCOL = KVH * G * ts
    n_chunk = (past_len + SLC_LEN) // CMP_STRIDE
    ncp = -(-n_chunk // 8) * 8
    n_slc = (past_len + SLC_LEN) // SLC_LEN
    nsp = -(-n_slc // 8) * 8
    qs = (q * D ** -0.5).reshape(B, ts, KVH, G, D).transpose(0, 2, 4, 3, 1).reshape(B, KVH, D, G * ts)
    z = jnp.zeros_like(qs[:, 0])
    qbd = jnp.concatenate([jnp.concatenate([qs[:, 0], z], axis=2), jnp.concatenate([z, qs[:, 1]], axis=2)], axis=1)
    qh, ql = _split(qbd)
    gate = gl.reshape(B, ts, KVH, G, 3).transpose(0, 4, 2, 3, 1).reshape(B, 3, NCOL, 1)
    gate = jnp.broadcast_to(gate, (B, 3, NCOL, KV)).astype(f32)
    w_lane = jnp.repeat(kc_w.reshape(2 * KVH, CMP_LEN), D, axis=0)
    reps = P // CMP_STRIDE
    wa = jnp.tile(w_lane[:, :CMP_STRIDE].T, (reps, 1)).astype(f32)
    wb = jnp.tile(w_lane[:, CMP_STRIDE:].T, (reps, 1)).astype(f32)
    zz = jnp.zeros((D, D), f32)
    bdiag = lambda m: jnp.concatenate([jnp.concatenate([m, zz], axis=1), jnp.concatenate([zz, m], axis=1)], axis=0)
    phik, phiv = bdiag(phi[0].astype(f32)), bdiag(phi[1].astype(f32))
    gain = jnp.tile(k_gain.astype(f32), KVH).reshape(1, KV)
    half = ROPE_DIM // 2
    inv = ROPE_THETA ** (-jnp.arange(half, dtype=f32) / half)
    cmp_end = (jnp.arange(ncp, dtype=jnp.int32) * CMP_STRIDE + (CMP_LEN - 1)).astype(f32)
    ang = cmp_end[:, None] * inv
    cos_h = jnp.concatenate([jnp.cos(ang), jnp.cos(ang), jnp.ones((ncp, D - ROPE_DIM), f32)], axis=1)
    sin_h = jnp.concatenate([jnp.sin(ang), jnp.sin(ang), jnp.zeros((ncp, D - ROPE_DIM), f32)], axis=1)
    cos_t, sin_t = jnp.tile(cos_h, (1, KVH)), jnp.tile(sin_h, (1, KVH))
    pool2 = pools.reshape(pools.shape[0] * n_pool, P, 4 * KV)
    new2 = rows_new.reshape(B, ts, 4 * KV).astype(f32)
    wbuf2 = win_bufs.reshape(win_bufs.shape[0] * B, lb, 2 * KV)
    wnew2 = win_new.reshape(B, ts, 2 * KV).astype(f32)
    page_spec = lambda j: pl.BlockSpec((None, P, 4 * KV),
                                       lambda b, pt, j=j: (layer * n_pool + pt[b, j], 0, 0))
    per_b = lambda *s: pl.BlockSpec((None,) + s, lambda b, pt: (b,) + (0,) * len(s))
    wbuf_spec = pl.BlockSpec((None, lb, 2 * KV), lambda b, pt: (layer * B + b, 0, 0))
    const = lambda *s: pl.BlockSpec(s, lambda b, pt: (0,) * len(s))
    grid_spec = pltpu.PrefetchScalarGridSpec(
        num_scalar_prefetch=1, grid=(B,),
        in_specs=[page_spec(j) for j in range(n_pages)] + [
            per_b(ts, 4 * KV), wbuf_spec, per_b(ts, 2 * KV), per_b(KV, NCOL), per_b(KV, NCOL),
            per_b(3, NCOL, KV), const(P, 2 * KV), const(P, 2 * KV), const(KV, KV), const(KV, KV),
            const(1, KV), const(ncp, KV), const(ncp, KV)],
        out_specs=per_b(NCOL, KV),
        scratch_shapes=[pltpu.VMEM(((n_pages + 1) * P, NCOL), f32), pltpu.VMEM((ncp, 2 * KV), f32),
                        pltpu.VMEM((ncp, 2 * KV), f32), pltpu.VMEM((nsp, KVH * ts), f32)])
    out = pl.pallas_call(
        functools.partial(_nsa_sample_kernel, n_pages=n_pages, ts=ts, past_len=past_len),
        grid_spec=grid_spec,
        out_shape=jax.ShapeDtypeStruct((B, NCOL, KV), f32),
        compiler_params=pltpu.CompilerParams(dimension_semantics=("arbitrary",),
                                             vmem_limit_bytes=48 * 1024 * 1024),
        name="nsa_sample",
    )(page_table, *([pool2] * n_pages), new2, wbuf2, wnew2, qh, ql, gate, wa, wb, phik, phiv, gain, cos_t, sin_t)
    o4 = out.reshape(B, KVH, G, ts, KVH, D)
    o = jnp.stack([o4[:, 0, :, :, 0], o4[:, 1, :, :, 1]], axis=1)
    return o.transpose(0, 3, 1, 2, 4).reshape(B, ts, KVH * G * D)


MOE_TM = 512
ROUTER_LANES = 128


def _moe_kernel(x_ref, g_ref, wrh_ref, wrl_ref, br_ref, wg_ref, wu_ref, wd_ref, o_ref,
                h_scr, gate_scr, acc_scr):
    grp = pl.program_id(1)
    dd = lambda a, b: jnp.dot(a, b, preferred_element_type=f32)
    tm = x_ref.shape[0]
    lane = lax.broadcasted_iota(jnp.int32, (tm, ROUTER_LANES), 1).astype(f32)
    far = float(ROUTER_LANES)

    @pl.when(grp == 0)
    def _():
        x = x_ref[...]
        h = x * lax.rsqrt(jnp.mean(x * x, axis=-1, keepdims=True) + RMS_EPS) * g_ref[...]
        hh, hl = _split(h)
        h_scr[...] = hh
        logits = dd(hh, wrh_ref[...]) + (dd(hh, wrl_ref[...]) + dd(hl, wrh_ref[...])) + br_ref[...]
        is_c = lane < MOE_GROUPS
        lc = jnp.where(is_c, logits, NEG_INF)
        mc = jnp.max(lc, axis=1, keepdims=True)
        g_idx = jnp.min(jnp.where(lc == mc, lane, far), axis=1, keepdims=True)
        g_w = 1.0 / jnp.sum(jnp.where(is_c, jnp.exp(lc - mc), 0.0), axis=1, keepdims=True)
        lo = MOE_GROUPS + MOE_EPG * g_idx
        lf = jnp.where((lane >= lo) & (lane < lo + MOE_EPG), logits, NEG_INF)
        v1 = jnp.max(lf, axis=1, keepdims=True)
        i1 = jnp.min(jnp.where(lf == v1, lane, far), axis=1, keepdims=True)
        lf2 = jnp.where(lane == i1, NEG_INF, lf)
        v2 = jnp.max(lf2, axis=1, keepdims=True)
        i2 = jnp.min(jnp.where(lf2 == v2, lane, far), axis=1, keepdims=True)
        e21 = jnp.exp(v2 - v1)
        w1 = g_w / (1.0 + e21)
        gate_scr[...] = jnp.where(lane == i1, w1, jnp.where(lane == i2, e21 * w1, 0.0))
        acc_scr[...] = x

    h = h_scr[...]
    hg = dd(h, wg_ref[...])
    hu = dd(h, wu_ref[...])
    gate = gate_scr[...]
    first = (MOE_GROUPS + MOE_EPG * grp).astype(f32)
    cols = []
    for e in range(MOE_EPG):
        ge = jnp.sum(jnp.where(lane == first + e, gate, 0.0), axis=1, keepdims=True)
        sl = slice(e * D_FF_E, (e + 1) * D_FF_E)
        hge = hg[:, sl]
        cols.append((hge * jax.nn.sigmoid(hge) * hu[:, sl] * ge).astype(bf16))
    acc_scr[...] += dd(jnp.concatenate(cols, axis=1), wd_ref[...])

    @pl.when(grp == MOE_GROUPS - 1)
    def _():
        o_ref[...] = acc_scr[...]


def moe_prep(g, wc, bc, wf, bf, wg, wu, wd):
    d = wc.shape[0]
    pad = ROUTER_LANES - MOE_GROUPS - N_EXPERTS
    wr = jnp.pad(jnp.concatenate([wc, wf], axis=1).astype(f32), ((0, 0), (0, pad)))
    wrh, wrl = _split(wr)
    br = jnp.pad(jnp.concatenate([bc, bf]).astype(f32), (0, pad)).reshape(1, ROUTER_LANES)
    regroup = lambda w: (w.reshape(MOE_GROUPS, MOE_EPG, d, D_FF_E).transpose(0, 2, 1, 3)
                         .reshape(MOE_GROUPS, d, MOE_EPG * D_FF_E).astype(bf16))
    wdg = wd.reshape(MOE_GROUPS, MOE_EPG * D_FF_E, d).astype(bf16)
    return g.reshape(1, d).astype(f32), wrh, wrl, br, regroup(wg), regroup(wu), wdg


def moe_residual(x, prep):
    g, wrh, wrl, br, wgg, wug, wdg = prep
    shp = x.shape
    d = shp[-1]
    x2 = x.reshape(-1, d)
    m = x2.shape[0]
    tm = MOE_TM
    assert m % tm == 0
    gf = MOE_EPG * D_FF_E
    full = lambda r, c: pl.BlockSpec((r, c), lambda i, j: (0, 0))
    out = pl.pallas_call(
        _moe_kernel,
        grid=(m // tm, MOE_GROUPS),
        in_specs=[pl.BlockSpec((tm, d), lambda i, j: (i, 0)), full(1, d),
                  full(d, ROUTER_LANES), full(d, ROUTER_LANES), full(1, ROUTER_LANES),
                  pl.BlockSpec((None, d, gf), lambda i, j: (j, 0, 0)),
                  pl.BlockSpec((None, d, gf), lambda i, j: (j, 0, 0)),
                  pl.BlockSpec((None, gf, d), lambda i, j: (j, 0, 0))],
        out_specs=pl.BlockSpec((tm, d), lambda i, j: (i, 0)),
        out_shape=jax.ShapeDtypeStruct((m, d), f32),
        scratch_shapes=[pltpu.VMEM((tm, d), bf16), pltpu.VMEM((tm, ROUTER_LANES), f32),
                        pltpu.VMEM((tm, d), f32)],
        compiler_params=pltpu.CompilerParams(dimension_semantics=("parallel", "arbitrary"),
                                             vmem_limit_bytes=48 * 1024 * 1024),
        name="moe",
    )(x2, g, wrh, wrl, br, wgg, wug, wdg)
    return out.reshape(shp)


def rmsnorm(x, g):
    xf = x.astype(jnp.float32)
    y = xf * lax.rsqrt(jnp.mean(xf * xf, axis=-1, keepdims=True) + RMS_EPS)
    return (y * g.astype(jnp.float32)).astype(x.dtype)


def rope_partial(x, pos):
    half = ROPE_DIM // 2
    inv = ROPE_THETA ** (-jnp.arange(half, dtype=jnp.float32) / half)
    ang = pos.astype(jnp.float32)[:, None] * inv
    cos = jnp.cos(ang)[:, None, :]
    sin = jnp.sin(ang)[:, None, :]
    xf = x.astype(jnp.float32)
    x1 = xf[..., :half]
    x2 = xf[..., half:ROPE_DIM]
    out = jnp.concatenate([x1 * cos - x2 * sin, x2 * cos + x1 * sin, xf[..., ROPE_DIM:]], axis=-1)
    return out.astype(x.dtype)


def pool_mix(u, hist, p0, w_grp, scale):
    b, t, _ = u.shape
    ext = jnp.concatenate([hist.astype(u.dtype), u], axis=1).astype(jnp.float32)
    cs = jnp.pad(jnp.cumsum(ext, axis=1), ((0, 0), (1, 0), (0, 0)))
    cnt_pos = p0 + jnp.arange(t, dtype=jnp.int32) + 1
    means = []
    for gi, w in enumerate(POOL_WINDOWS):
        c = cs[..., gi * POOL_GDIM:(gi + 1) * POOL_GDIM]
        win_sum = c[:, POOL_HIST + 1:POOL_HIST + 1 + t] - c[:, POOL_HIST + 1 - w:POOL_HIST + 1 - w + t]
        cnt = jnp.minimum(cnt_pos, w).astype(jnp.float32)[None, :, None]
        means.append(win_sum / cnt)
    mean = jnp.stack(means, axis=2)
    d = mean - u.reshape(b, t, POOL_GROUPS, POOL_GDIM).astype(jnp.float32)
    y = jnp.einsum('btgc,gcd->btgd', d, w_grp.astype(jnp.float32)).reshape(b, t, POOL_DIM)
    return (y * scale.astype(jnp.float32)).astype(u.dtype)


PROJ_TM = 512


def _norm_mm_kernel(x_ref, g_ref, w_ref, o_ref):
    x = x_ref[...]
    h = x * lax.rsqrt(jnp.mean(x * x, axis=-1, keepdims=True) + RMS_EPS) * g_ref[...]
    o_ref[...] = jnp.dot(h.astype(bf16), w_ref[...], preferred_element_type=f32)


def _norm_mm(x, g, w):
    lead, d = x.shape[:-1], x.shape[-1]
    n = w.shape[1]
    x2 = x.reshape(-1, d)
    m = x2.shape[0]
    npad = -(-n // 128) * 128
    wb = jnp.pad(w.astype(bf16), ((0, 0), (0, npad - n)))
    tm = PROJ_TM
    assert m % tm == 0
    out = pl.pallas_call(
        _norm_mm_kernel, grid=(m // tm,),
        in_specs=[pl.BlockSpec((tm, d), lambda i: (i, 0)), pl.BlockSpec((1, d), lambda i: (0, 0)),
                  pl.BlockSpec((d, npad), lambda i: (0, 0))],
        out_specs=pl.BlockSpec((tm, npad), lambda i: (i, 0)),
        out_shape=jax.ShapeDtypeStruct((m, npad), f32),
        compiler_params=pltpu.CompilerParams(dimension_semantics=("parallel",),
                                             vmem_limit_bytes=48 * 1024 * 1024),
        name="norm_mm",
    )(x2, g.reshape(1, d).astype(f32), wb)
    return out[:, :n].reshape(lead + (n,))


def _mix_out_kernel(a_ref, b_ref, wa_ref, wb_ref, x_ref, o_ref):
    dd = lambda p, q: jnp.dot(p.astype(bf16), q, preferred_element_type=f32)
    o_ref[...] = x_ref[...] + (dd(a_ref[...], wa_ref[...]) + dd(b_ref[...], wb_ref[...]))


def _mix_out(a, b, w, x):
    d = x.shape[-1]
    ka, kb = a.shape[-1], b.shape[-1]
    x2 = x.reshape(-1, d)
    m = x2.shape[0]
    tm = PROJ_TM
    assert m % tm == 0
    wbf = w.astype(bf16)
    tile = lambda c: pl.BlockSpec((tm, c), lambda i: (i, 0))
    full = lambda r, c: pl.BlockSpec((r, c), lambda i: (0, 0))
    out = pl.pallas_call(
        _mix_out_kernel, grid=(m // tm,),
        in_specs=[tile(ka), tile(kb), full(ka, d), full(kb, d), tile(d)], out_specs=tile(d),
        out_shape=jax.ShapeDtypeStruct((m, d), f32),
        compiler_params=pltpu.CompilerParams(dimension_semantics=("parallel",)),
        name="mix_out",
    )(a.reshape(m, ka), b.reshape(m, kb), wbf[:ka], wbf[ka:], x2)
    return out.reshape(x.shape)


def _ab_feat_kernel(x_ref, g_ref, w_ref, qg_ref, kg_ref, cos_ref, sin_ref, hs_ref, he_ref, rot_ref,
                    pool_ref, rows_ref, win_ref, qh_ref, ql_ref, gt_ref, ks_ref, kw_ref, vst_ref, vwt_ref):
    D, KV = HEAD_DIM, NSA_KV_HEADS * HEAD_DIM
    x = x_ref[...]
    h = x * lax.rsqrt(jnp.mean(x * x, axis=-1, keepdims=True) + RMS_EPS) * g_ref[...]
    u = jnp.dot(h.astype(bf16), w_ref[...], preferred_element_type=f32)
    off_kv = POOL_DIM + NSA_DIM
    pool_ref[...] = u[:, :POOL_DIM]
    cos, sin = cos_ref[...], sin_ref[...]

    def norm_rope(z, gain):
        n = z.shape[1] // D
        hs, he, rot = hs_ref[0:n * D, :], he_ref[:, 0:n * D], rot_ref[0:n * D, 0:n * D]
        ms = _dot_exact_rhs(_dot_exact_rhs(z * z, hs), he) * (1.0 / D)
        zn = z * lax.rsqrt(ms + RMS_EPS) * gain
        wide = lambda t: jnp.concatenate([t] * (n // 2), axis=1)
        return zn * wide(cos) + _dot_exact_rhs(zn, rot) * wide(sin)

    q = norm_rope(u[:, POOL_DIM:off_kv], qg_ref[...]) * (D ** -0.5)
    for kvh in range(NSA_KV_HEADS):
        qt = jnp.concatenate([q[:, (kvh * NSA_GQ + g) * D:(kvh * NSA_GQ + g + 1) * D].T
                              for g in range(NSA_GQ)], axis=1)
        hi, lo = _split(qt)
        qh_ref[kvh] = hi
        ql_ref[kvh] = lo
    kv = u[:, off_kv:off_kv + KV_COLS]
    slab = lambda i: kv[:, i * KV:(i + 1) * KV]
    kr = norm_rope(jnp.concatenate([slab(2), slab(4)], axis=1), kg_ref[...])
    k_slc, k_win = kr[:, 0:KV], kr[:, KV:2 * KV]
    rows_ref[...] = jnp.concatenate([slab(0), slab(1), k_slc, slab(3)], axis=1)
    win_ref[...] = jnp.concatenate([k_win, slab(5)], axis=1)
    for kvh in range(NSA_KV_HEADS):
        hsl = slice(kvh * D, (kvh + 1) * D)
        ks_ref[kvh] = k_slc[:, hsl].astype(bf16)
        kw_ref[kvh] = k_win[:, hsl].astype(bf16)
        vst_ref[kvh] = slab(3)[:, hsl].T.astype(bf16)
        vwt_ref[kvh] = slab(5)[:, hsl].T.astype(bf16)
    gt_ref[...] = u[:, off_kv + KV_COLS:].T


def ab_features_prompt(x, norm_g, w_in, q_norm, k_norm):
    B, T, d = x.shape
    QB, D, KVH, G = Q_BLOCK, HEAD_DIM, NSA_KV_HEADS, NSA_GQ
    KV = KVH * D
    assert T % QB == 0
    nqb = T // QB
    npad = -(-IN_COLS // 128) * 128
    assert npad - (POOL_DIM + NSA_DIM + KV_COLS) == 128
    wb = jnp.pad(w_in.astype(bf16), ((0, 0), (0, npad - IN_COLS)))
    half = ROPE_DIM // 2
    inv = ROPE_THETA ** (-jnp.arange(half, dtype=f32) / half)
    ang = jnp.arange(T, dtype=jnp.int32).astype(f32)[:, None] * inv
    cos_h = jnp.concatenate([jnp.cos(ang), jnp.cos(ang), jnp.ones((T, D - ROPE_DIM), f32)], axis=1)
    sin_h = jnp.concatenate([jnp.sin(ang), jnp.sin(ang), jnp.zeros((T, D - ROPE_DIM), f32)], axis=1)
    cos_t, sin_t = jnp.tile(cos_h, (1, 2)), jnp.tile(sin_h, (1, 2))
    r_i = lax.broadcasted_iota(jnp.int32, (NSA_DIM, NSA_DIM), 0)
    c_i = lax.broadcasted_iota(jnp.int32, (NSA_DIM, NSA_DIM), 1)
    same = (r_i // D) == (c_i // D)
    rd, cd = r_i % D, c_i % D
    rot = jnp.where(same & (cd < half) & (rd == cd + half), -1.0,
                    jnp.where(same & (cd >= half) & (cd < ROPE_DIM) & (rd == cd - half), 1.0, 0.0)).astype(bf16)
    hs = (lax.broadcasted_iota(jnp.int32, (NSA_DIM, 128), 0) // D
          == lax.broadcasted_iota(jnp.int32, (NSA_DIM, 128), 1)).astype(bf16)
    qg = jnp.tile(q_norm.astype(f32), NSA_HEADS).reshape(1, NSA_DIM)
    kg = jnp.concatenate([jnp.tile(k_norm[1].astype(f32), KVH), jnp.tile(k_norm[2].astype(f32), KVH)]).reshape(1, 2 * KV)
    full = lambda a: pl.BlockSpec(a.shape, lambda b, i: (0,) * a.ndim)
    tok = lambda c: pl.BlockSpec((None, QB, c), lambda b, i: (b, i, 0))
    f = jax.ShapeDtypeStruct
    outs = pl.pallas_call(
        _ab_feat_kernel, grid=(B, nqb),
        in_specs=[tok(d), pl.BlockSpec((1, d), lambda b, i: (0, 0)), full(wb), full(qg), full(kg),
                  pl.BlockSpec((QB, 2 * D), lambda b, i: (i, 0)), pl.BlockSpec((QB, 2 * D), lambda b, i: (i, 0)),
                  full(hs), pl.BlockSpec((128, NSA_DIM), lambda b, i: (0, 0)), full(rot)],
        out_specs=[tok(POOL_DIM), tok(4 * KV), tok(2 * KV),
                   pl.BlockSpec((None, KVH, None, D, G * QB), lambda b, i: (b, 0, i, 0, 0)),
                   pl.BlockSpec((None, KVH, None, D, G * QB), lambda b, i: (b, 0, i, 0, 0)),
                   pl.BlockSpec((None, None, 128, QB), lambda b, i: (b, i, 0, 0)),
                   pl.BlockSpec((None, KVH, QB, D), lambda b, i: (b, 0, i, 0)),
                   pl.BlockSpec((None, KVH, QB, D), lambda b, i: (b, 0, i, 0)),
                   pl.BlockSpec((None, KVH, D, QB), lambda b, i: (b, 0, 0, i)),
                   pl.BlockSpec((None, KVH, D, QB), lambda b, i: (b, 0, 0, i))],
        out_shape=[f((B, T, POOL_DIM), f32), f((B, T, 4 * KV), f32), f((B, T, 2 * KV), f32),
                   f((B, KVH, nqb, D, G * QB), bf16), f((B, KVH, nqb, D, G * QB), bf16),
                   f((B, nqb, 128, QB), f32), f((B, KVH, T, D), bf16), f((B, KVH, T, D), bf16),
                   f((B, KVH, D, T), bf16), f((B, KVH, D, T), bf16)],
        compiler_params=pltpu.CompilerParams(dimension_semantics=("parallel", "parallel"),
                                             vmem_limit_bytes=48 * 1024 * 1024),
        name="ab_feat",
    )(x, norm_g.reshape(1, d).astype(f32), wb, qg, kg, cos_t, sin_t, hs, hs.T, rot)
    pool_in, rows, win, qh, ql, gt, ks, kw, vst, vwt = outs
    gT = (gt[:, :, :NSA_HEADS * 3].reshape(B, nqb, KVH, G, 3, QB).transpose(0, 2, 1, 4, 3, 5)
          .reshape(B, KVH, nqb, 3, G * QB))
    return pool_in, rows, win, (qh, ql, gT, ks, vst, kw, vwt)


def _pool_mix_out_kernel(u_ref, halo_ref, nsa_ref, x_ref, band_ref, wg_ref, sc_ref, wa_ref, wb_ref, o_ref,
                         *, seq_len):
    tm = u_ref.shape[0]
    hrows = halo_ref.shape[0]
    gd = POOL_GDIM
    tile = pl.program_id(0) % (seq_len // tm)
    u = u_ref[...]
    halo = jnp.where(tile == 0, 0.0, halo_ref[...])
    ext = jnp.concatenate([halo, u], axis=0)
    pos1 = tile * tm + lax.broadcasted_iota(jnp.int32, (tm, 1), 0) + 1
    outs = []
    for gi, w in enumerate(POOL_WINDOWS):
        sl = slice(gi * gd, (gi + 1) * gd)
        win_sum = _dot_exact_lhs(band_ref[gi], ext[:, sl])
        d = win_sum / jnp.minimum(pos1, w).astype(f32) - u[:, sl]
        outs.append(jnp.dot(d.astype(bf16), wg_ref[gi], preferred_element_type=f32))
    pool_out = (jnp.concatenate(outs, axis=1) * sc_ref[...]).astype(bf16)
    dd = lambda p, q: jnp.dot(p, q, preferred_element_type=f32)
    o_ref[...] = x_ref[...] + (dd(pool_out, wa_ref[...]) + dd(nsa_ref[...].astype(bf16), wb_ref[...]))


def pool_mix_out(pool_in, nsa_out, x, pool_w, pool_scale, w_out):
    b, t, d = x.shape
    m = b * t
    tm = PROJ_TM
    hrows = POOL_HIST + 1
    assert t % tm == 0 and tm % hrows == 0 and POOL_GDIM % 128 == 0
    r_i = lax.broadcasted_iota(jnp.int32, (tm, hrows + tm), 0) + hrows
    j_i = lax.broadcasted_iota(jnp.int32, (tm, hrows + tm), 1)
    band = jnp.stack([((j_i <= r_i) & (j_i > r_i - w)).astype(bf16) for w in POOL_WINDOWS])
    wbf = w_out.astype(bf16)
    u2, n2, x2 = pool_in.reshape(m, POOL_DIM), nsa_out.reshape(m, NSA_DIM), x.reshape(m, d)
    tile = lambda c: pl.BlockSpec((tm, c), lambda i: (i, 0))
    full = lambda a: pl.BlockSpec(a.shape, lambda i: (0,) * a.ndim)
    halo = pl.BlockSpec((hrows, POOL_DIM), lambda i: (jnp.maximum(i * (tm // hrows) - 1, 0), 0))
    consts = [band, pool_w.astype(bf16), pool_scale.reshape(1, POOL_DIM).astype(f32), wbf[:POOL_DIM], wbf[POOL_DIM:]]
    out = pl.pallas_call(
        functools.partial(_pool_mix_out_kernel, seq_len=t), grid=(m // tm,),
        in_specs=[tile(POOL_DIM), halo, tile(NSA_DIM), tile(d)] + [full(c) for c in consts],
        out_specs=tile(d), out_shape=jax.ShapeDtypeStruct((m, d), f32),
        compiler_params=pltpu.CompilerParams(dimension_semantics=("parallel",),
                                             vmem_limit_bytes=48 * 1024 * 1024),
        name="pool_mix_out",
    )(u2, u2, n2, x2, *consts)
    return out.reshape(b, t, d)


def ab_features(x, norm_g, pos, w_in, q_norm, k_norm):
    b, t = x.shape[:2]
    u = _norm_mm(x, norm_g, w_in)
    off_kv = POOL_DIM + NSA_DIM
    pool_in = u[..., :POOL_DIM]
    q = u[..., POOL_DIM:off_kv].reshape(b, t, NSA_HEADS, HEAD_DIM)
    kv = u[..., off_kv:off_kv + KV_COLS].reshape(b, t, 6, NSA_KV_HEADS, HEAD_DIM)
    gl = u[..., off_kv + KV_COLS:].reshape(b, t, NSA_HEADS, 3)
    q = rope_partial(rmsnorm(q, q_norm), pos)
    k_slc = rope_partial(rmsnorm(kv[:, :, 2], k_norm[1]), pos)
    k_win = rope_partial(rmsnorm(kv[:, :, 4], k_norm[2]), pos)
    rows = jnp.stack([kv[:, :, 0], kv[:, :, 1], k_slc, kv[:, :, 3]], axis=2)
    win = jnp.stack([k_win, kv[:, :, 5]], axis=2)
    return pool_in, q, gl, rows, win


def compress_kv(k_rows, v_rows, pos_w, phi, k_gain):
    b, length = k_rows.shape[:2]
    n_chunk = length // CMP_STRIDE

    def weighted_block_mean(rows, w):
        ch = rows.reshape(b, n_chunk, CMP_STRIDE, NSA_KV_HEADS, HEAD_DIM)
        return (jnp.einsum('bnlhd,hl->bnhd', ch[:, :-1], w[:, :CMP_STRIDE])
                + jnp.einsum('bnlhd,hl->bnhd', ch[:, 1:], w[:, CMP_STRIDE:]))

    cmp_end = jnp.arange(n_chunk - 1, dtype=jnp.int32) * CMP_STRIDE + (CMP_LEN - 1)
    kc = jnp.einsum('bnhd,de->bnhe', weighted_block_mean(k_rows, pos_w[0]), phi[0])
    kc = rope_partial(rmsnorm(kc, k_gain), cmp_end)
    vc = jnp.einsum('bnhd,de->bnhe', weighted_block_mean(v_rows, pos_w[1]), phi[1])
    return kc, vc, cmp_end


def ab_layer_prompt(x, norm_g, w_in, w_out, q_norm, k_norm, cmp_pos_w, cmp_phi, pool_w, pool_scale):
    b, t = x.shape[:2]
    pool_in, rows2, win2, ops = ab_features_prompt(x, norm_g, w_in, q_norm, k_norm)
    rows = rows2.reshape(b, t, 4, NSA_KV_HEADS, HEAD_DIM)
    win = win2.reshape(b, t, 2, NSA_KV_HEADS, HEAD_DIM)
    kc, vc, _ = compress_kv(rows[:, :, 0], rows[:, :, 1], cmp_pos_w, cmp_phi, k_norm[0])
    nsa_out = nsa_prompt_pallas(ops, kc, vc)
    x_new = pool_mix_out(pool_in, nsa_out, x, pool_w, pool_scale, w_out)
    keep = min(WINDOW, t)
    return x_new, rows, win[:, t - keep:], pool_in[:, t - POOL_HIST:]


def ab_layer_sample(x, norm_g, pools, page_table, win_bufs, layer, pool_hist, w_in, w_out, q_norm, k_norm,
                    cmp_pos_w, cmp_phi, pool_w, pool_scale):
    ts = x.shape[1]
    past_len = page_table.shape[1] * pools.shape[2]
    pos = past_len + jnp.arange(ts, dtype=jnp.int32)
    pool_in, q, gl, rows, win = ab_features(x, norm_g, pos, w_in, q_norm, k_norm)
    pool_out = pool_mix(pool_in, pool_hist, past_len, pool_w, pool_scale)
    nsa_out = nsa_sample_pallas(q, gl, rows, win, pools, page_table, win_bufs, layer,
                                cmp_pos_w, cmp_phi, k_norm[0])
    win_buf = win_bufs[layer]
    lb = win_buf.shape[1]
    keep = min(WINDOW, lb + ts)
    new_win = jnp.concatenate([win_buf, win.astype(win_buf.dtype)], axis=1)[:, lb + ts - keep:]
    x_new = _mix_out(pool_out, nsa_out, w_out, x)
    new_hist = jnp.concatenate([pool_hist.astype(pool_in.dtype), pool_in], axis=1)[:, -POOL_HIST:]
    return x_new, rows, new_win, new_hist


RW_TM = 256
HEAD_LANES = 128


def _rwkv_pre_kernel(*refs, seq_len, has_vres):
    it = iter(refs)
    x_ref, xprev_ref, fp_ref, g_ref, mu_ref = [next(it) for _ in range(5)]
    wr_ref, wk_ref, wv_ref = [next(it) for _ in range(3)]
    w0_ref, w1_ref, w2_ref, a0_ref, a1_ref, a2_ref, g1_ref, g2_ref = [next(it) for _ in range(8)]
    kkw_ref, kaw_ref, hsum_ref, hexp_ref = [next(it) for _ in range(4)]
    if has_vres:
        v0_ref, v1_ref, v2_ref, vf_ref = [next(it) for _ in range(4)]
    r_ref, k_ref, v_ref, lw_ref, kk_ref, a_ref, gg_ref = [next(it) for _ in range(7)]
    dd = lambda a, b: jnp.dot(a, b, preferred_element_type=f32)
    tm = x_ref.shape[0]
    norm = lambda z: z * lax.rsqrt(jnp.mean(z * z, axis=-1, keepdims=True) + RMS_EPS) * g_ref[...]
    h = norm(x_ref[...])
    row = lax.broadcasted_iota(jnp.int32, (tm, 1), 0)
    rolled = pltpu.roll(h, 1, axis=0)
    if seq_len % tm == 0:
        first = (pl.program_id(0) % (seq_len // tm)) == 0
        last_prev = norm(xprev_ref[...])[xprev_ref.shape[0] - 1:, :]
        prev = jnp.where(row == 0, jnp.where(first, fp_ref[...], last_prev), rolled)
    else:
        nseq = tm // seq_len
        sel = (lax.broadcasted_iota(jnp.int32, (tm, nseq), 0)
               == seq_len * lax.broadcasted_iota(jnp.int32, (tm, nseq), 1)).astype(bf16)
        prev = jnp.where(row % seq_len == 0, _dot_exact_lhs(sel, fp_ref[...]), rolled)
    xx = prev - h
    mix = lambda j: (h + xx * mu_ref[j:j + 1, :]).astype(bf16)
    xr, xw, xk, xv, xa, xg = [mix(j) for j in range(6)]
    k = dd(xk, wk_ref[...])
    v = dd(xv, wv_ref[...])
    r_ref[...] = dd(xr, wr_ref[...])
    z = -(w0_ref[...] + dd(jnp.tanh(dd(xw, w1_ref[...])).astype(bf16), w2_ref[...]))
    softplus = jnp.maximum(z, 0.0) + jnp.log(1.0 + jnp.exp(-jnp.abs(z)))
    lw_ref[...] = -jnp.exp(-softplus - 0.5)
    a = jax.nn.sigmoid(a0_ref[...] + dd(dd(xa, a1_ref[...]).astype(bf16), a2_ref[...]))
    a_ref[...] = a
    gg_ref[...] = dd(jax.nn.sigmoid(dd(xg, g1_ref[...])).astype(bf16), g2_ref[...])
    if has_vres:
        v = v + (vf_ref[...] - v) * jax.nn.sigmoid(
            v0_ref[...] + dd(dd(xv, v1_ref[...]).astype(bf16), v2_ref[...]))
    v_ref[...] = v
    kk = k * kkw_ref[...]
    nrm = jnp.maximum(jnp.sqrt(_dot_exact_rhs(kk * kk, hsum_ref[...])), 1e-12)
    kk_ref[...] = kk * _dot_exact_rhs(1.0 / nrm, hexp_ref[...])
    k_ref[...] = k * (1.0 + (a - 1.0) * kaw_ref[...])


def _rwkv_post_kernel(o_ref, r_ref, k_ref, v_ref, gg_ref, x_ref, gnw_ref, gnb_ref, rk_ref, hsum_ref,
                      hexp_ref, wo_ref, out_ref):
    hs = hsum_ref[...]
    he = hexp_ref[...]
    head_sum = lambda z: _dot_exact_rhs(_dot_exact_rhs(z, hs), he)
    o = o_ref[...]
    v = v_ref[...]
    d = o - head_sum(o) * (1.0 / RWKV_N)
    var = head_sum(d * d) * (1.0 / RWKV_N)
    on = d * lax.rsqrt(var + GN_EPS) * gnw_ref[...] + gnb_ref[...]
    on = on + head_sum(r_ref[...] * k_ref[...] * rk_ref[...]) * v
    y = (on * gg_ref[...]).astype(bf16)
    out_ref[...] = x_ref[...] + jnp.dot(y, wo_ref[...], preferred_element_type=f32)


def rwkv_layer_fused(x, norm_g, shift_prev, s0, v_first, vres, mu, wr, wk, wv, wo, w0, w1, w2, a0, a1, a2,
                     g1, g2, k_k, k_a, r_k, gn_w, gn_b):
    n, t, d = x.shape
    m = n * t
    tm = RW_TM
    assert m % tm == 0 and (t % tm == 0 or tm % t == 0)
    x2 = x.reshape(m, d)
    row = lambda z: z.reshape(1, d).astype(f32)
    cb = lambda z: z.astype(bf16)
    hd = lax.broadcasted_iota(jnp.int32, (d, HEAD_LANES), 0) // RWKV_N
    hsum = (hd == lax.broadcasted_iota(jnp.int32, (d, HEAD_LANES), 1)).astype(bf16)
    hexp = hsum.T
    tile = pl.BlockSpec((tm, d), lambda i: (i, 0))
    full = lambda z: pl.BlockSpec(z.shape, lambda i: (0,) * z.ndim)
    if t % tm == 0:
        tps = t // tm
        fp = shift_prev.reshape(n, 1, d).astype(f32)
        fp_spec = pl.BlockSpec((None, 1, d), lambda i: (i // tps, 0, 0))
    else:
        fp = shift_prev.astype(f32)
        fp_spec = pl.BlockSpec((tm // t, d), lambda i: (i, 0))
    xprev_spec = pl.BlockSpec((8, d), lambda i: (jnp.maximum(i * (tm // 8) - 1, 0), 0))
    mu8 = jnp.pad(mu.astype(f32), ((0, 2), (0, 0)))
    consts = [row(norm_g), mu8, cb(wr), cb(wk), cb(wv), row(w0), cb(w1), cb(w2), row(a0), cb(a1), cb(a2),
              cb(g1), cb(g2), row(k_k), row(k_a), hsum, hexp]
    args = [x2, x2, fp] + consts
    specs = [tile, xprev_spec, fp_spec] + [full(c) for c in consts]
    if vres is not None:
        v0, v1, v2 = vres
        extra = [row(v0), cb(v1), cb(v2)]
        args += extra + [v_first.reshape(m, d)]
        specs += [full(c) for c in extra] + [tile]
    cp = pltpu.CompilerParams(dimension_semantics=("parallel",), vmem_limit_bytes=56 * 1024 * 1024)
    r, k, v, lw, kk, a, gg = pl.pallas_call(
        functools.partial(_rwkv_pre_kernel, seq_len=t, has_vres=vres is not None),
        grid=(m // tm,), in_specs=specs, out_specs=[tile] * 7,
        out_shape=[jax.ShapeDtypeStruct((m, d), f32)] * 7, compiler_params=cp, name="rwkv_pre",
    )(*args)
    if vres is None:
        v_first = v.reshape(n, t, d)
    seq = lambda z: z.reshape(n, t, d)
    o, s = wkv_chunked(seq(r), seq(lw), seq(k), seq(v), seq(kk), seq(a), s0.astype(f32))
    o = o.reshape(m, d)
    post_consts = [row(gn_w), row(gn_b), r_k.reshape(1, d).astype(f32), hsum, hexp, cb(wo)]
    x_new = pl.pallas_call(
        _rwkv_post_kernel, grid=(m // tm,),
        in_specs=[tile] * 6 + [full(c) for c in post_consts], out_specs=tile,
        out_shape=jax.ShapeDtypeStruct((m, d), f32), compiler_params=cp, name="rwkv_post",
    )(o, r, k, v, gg, x2, *post_consts)
    h_last = rmsnorm(x[:, -1], norm_g)
    return x_new.reshape(n, t, d), v_first, s, h_last


def kernel(x_prompt, x_sample, cache_nsa_kv, cache_win_kv, state_pool, state_wkv, state_shift,
           page_table, norm_mix, norm_ffn, ab_w_in, ab_w_out, ab_q_norm, ab_k_norm, cmp_pos_w,
           cmp_phi, pool_w, pool_scale, rw_mu, rw_wr, rw_wk, rw_wv, rw_wo, rw_w0, rw_w1, rw_w2,
           rw_a0, rw_a1, rw_a2, rw_v0, rw_v1, rw_v2, rw_g1, rw_g2, rw_kk, rw_ka, rw_rk, rw_gn_w,
           rw_gn_b, moe_wc, moe_bc, moe_wf, moe_bf, moe_wg, moe_wu, moe_wd):
    xp, xs = x_prompt, x_sample
    vf_p, vf_s = None, None
    nsa_p, nsa_s, win_p, win_s, pool_p, pool_s = [], [], [], [], [], []
    wkv_p, wkv_s, sh_p, sh_s = [], [], [], []
    for l in range(DEPTH):
        if l % 2 == 0:
            i = l // 2
            wts = (ab_w_in[i], ab_w_out[i], ab_q_norm[i], ab_k_norm[i], cmp_pos_w[i], cmp_phi[i],
                   pool_w[i], pool_scale[i])
            xp, r_p, w_p, h_p = ab_layer_prompt(xp, norm_mix[l], *wts)
            xs, r_s, w_s, h_s = ab_layer_sample(xs, norm_mix[l], cache_nsa_kv, page_table,
                                                cache_win_kv, i, state_pool[i], *wts)
            nsa_p.append(r_p)
            nsa_s.append(r_s)
            win_p.append(w_p)
            win_s.append(w_s)
            pool_p.append(h_p)
            pool_s.append(h_s)
        else:
            j = l // 2
            vres = None if j == 0 else (rw_v0[j - 1], rw_v1[j - 1], rw_v2[j - 1])
            wts = (rw_mu[j], rw_wr[j], rw_wk[j], rw_wv[j], rw_wo[j], rw_w0[j], rw_w1[j], rw_w2[j],
                   rw_a0[j], rw_a1[j], rw_a2[j], rw_g1[j], rw_g2[j], rw_kk[j], rw_ka[j], rw_rk[j],
                   rw_gn_w[j], rw_gn_b[j])
            bp = xp.shape[0]
            zero_shift = jnp.zeros((bp, D_MODEL), xp.dtype)
            zero_state = jnp.zeros((bp, RWKV_HEADS, RWKV_N, RWKV_N), jnp.float32)
            xp, vf_p, s_p, shp = rwkv_layer_fused(xp, norm_mix[l], zero_shift, zero_state, vf_p, vres, *wts)
            xs, vf_s, s_s, shs = rwkv_layer_fused(xs, norm_mix[l], state_shift[j], state_wkv[j], vf_s, vres, *wts)
            wkv_p.append(s_p)
            wkv_s.append(s_s)
            sh_p.append(shp)
            sh_s.append(shs)
        prep =moe_prep(norm_ffn[l], moe_wc[l], moe_bc[l], moe_wf[l], moe_bf[l], moe_wg[l], moe_wu[l], moe_wd[l])
        xp = moe_residual(xp, prep)
        xs = moe_residual(xs, prep)
    return (xp, xs, jnp.stack(nsa_p), jnp.stack(nsa_s), jnp.stack(win_p), jnp.stack(win_s),
            jnp.stack(pool_p), jnp.stack(pool_s), jnp.stack(wkv_p), jnp.stack(wkv_s),
            jnp.stack(sh_p), jnp.stack(sh_s))
```
